```python
import math
import jax, jax.numpy as jnp
from jax import lax
import numpy as np

D_MODEL = 1024
BATCH = 8
SEQ = 2048
DEPTH = 2

PLE_DIM = 256
HEAD_DIM = 64
MOBA_HEADS = 8
MOBA_BLOCK = 256
MOBA_TOPK = 3
MOBA_Q_CHUNK = 32
NSA_HEADS = 8
NSA_KV_GROUPS = 2
NSA_HPG = NSA_HEADS // NSA_KV_GROUPS
NSA_CMP_LEN = 32
NSA_CMP_STRIDE = 16
NSA_CMP_HIDDEN = 2 * HEAD_DIM
NSA_SEL_BLOCK = 64
NSA_SEL_TOPN = 16
NSA_WINDOW = 512
NSA_Q_CHUNK = 64
WIN_Q_BLOCK = 128
REL_BUCKETS = 32
REL_MAX_DIST = 128
N_ATTN_HEADS = MOBA_HEADS + NSA_HEADS
MOBA_WIDTH = MOBA_HEADS * HEAD_DIM
NSA_WIDTH = NSA_HEADS * HEAD_DIM
NSA_KV_WIDTH = NSA_KV_GROUPS * HEAD_DIM
NSA_GATE_COLS = 3 * NSA_HEADS
MERGE_COLS = 2 * D_MODEL
IN_SIZES = (MOBA_WIDTH, MOBA_WIDTH, MOBA_WIDTH, NSA_WIDTH) + (NSA_KV_WIDTH,) * 6 + (NSA_GATE_COLS, MERGE_COLS)
IN_COLS = 3 * MOBA_WIDTH + NSA_WIDTH + 6 * NSA_KV_WIDTH + NSA_GATE_COLS + MERGE_COLS
D_FF = 128 * (-(-8 * D_MODEL // (3 * 128)))
CONV_W = 3
RMS_EPS = 1e-6
SCALE = HEAD_DIM ** -0.5

kernel_name = 'hybrid_moba_nsa_convffn_ple'


def rms_norm(x, gain):
    x32 = x.astype(jnp.float32)
    y = x32 * lax.rsqrt(jnp.mean(x32 * x32, axis=-1, keepdims=True) + RMS_EPS)
    return y.astype(x.dtype) * gain


def rel_bucket(dist):
    n = jnp.maximum(jnp.asarray(dist, jnp.int32), 0)
    max_exact = REL_BUCKETS // 2
    nf = jnp.maximum(n, 1).astype(jnp.float32)
    large = max_exact + (jnp.log(nf / max_exact) / math.log(REL_MAX_DIST / max_exact)
                         * (REL_BUCKETS - max_exact)).astype(jnp.int32)
    return jnp.where(n < max_exact, n, jnp.minimum(large, REL_BUCKETS - 1))


def masked_softmax_safe(logits, mask):
    z = jnp.where(mask, logits, -jnp.inf)
    m = jnp.max(z, axis=-1, keepdims=True)
    e = jnp.exp(z - jnp.where(jnp.isfinite(m), m, 0.0))
    s = jnp.sum(e, axis=-1, keepdims=True)
    return e / jnp.where(s > 0, s, 1.0)


def to_heads(t, n):
    B, S, _ = t.shape
    return t.reshape(B, S, n, HEAD_DIM).transpose(0, 2, 1, 3)


def causal_dwconv(u, w, b):
    c = u.shape[-1]
    y = lax.conv_general_dilated(u, w[:, None, :], window_strides=(1,), padding=[(CONV_W - 1, 0)],
                                 dimension_numbers=('NWC', 'WIO', 'NWC'), feature_group_count=c)
    return y + b


def compress_blocks(t, pos, w1, w2):
    B, G, S, dh = t.shape
    nc = (S - NSA_CMP_LEN) // NSA_CMP_STRIDE + 1
    idx = np.arange(nc)[:, None] * NSA_CMP_STRIDE + np.arange(NSA_CMP_LEN)[None, :]
    blk = (t[:, :, idx] + pos).reshape(B, G, nc, NSA_CMP_LEN * dh)
    return jax.nn.gelu(blk @ w1) @ w2


def moba_attention(q, k, v, tab):
    B, H, S, dh = q.shape
    L, C = MOBA_BLOCK, MOBA_Q_CHUNK
    nb = -(-S // L)
    sp = nb * L
    padw = ((0, 0), (0, 0), (0, sp - S), (0, 0))
    qp, kp, vp = jnp.pad(q, padw), jnp.pad(k, padw), jnp.pad(v, padw)
    kb = kp.reshape(B, H, nb, L, dh)
    vb = vp.reshape(B, H, nb, L, dh)
    n_real = jnp.asarray(np.minimum(S - np.arange(nb) * L, L), k.dtype)
    kmean = kb.sum(axis=3) / n_real[:, None]
    qblk = jnp.arange(sp) // L
    past = jnp.arange(nb)[None, :] < qblk[:, None]
    gate = jnp.einsum('bhsd,bhnd->bhsn', qp, kmean).astype(jnp.float32)
    gate = jnp.where(past, gate, -jnp.inf)
    ksel = min(MOBA_TOPK, max(nb - 1, 1))
    _, sel = lax.top_k(gate, ksel)
    sel_ok = sel < qblk[:, None]
    nc = sp // C

    def chunks(t):
        return jnp.moveaxis(t.reshape(B, H, nc, C, *t.shape[3:]), 2, 0)

    bi = jnp.arange(B)[:, None, None, None]
    hi = jnp.arange(H)[None, :, None, None]
    hi5 = hi[..., None]

    def query_block(args):
        qc, ic, okc, c0 = args
        tpos = c0 + jnp.arange(C)
        own = c0 // L
        k_own = lax.dynamic_index_in_dim(kb, own, axis=2, keepdims=False)
        v_own = lax.dynamic_index_in_dim(vb, own, axis=2, keepdims=False)
        kpos_own = own * L + jnp.arange(L)
        l_own = (jnp.einsum('bhcd,bhkd->bhck', qc, k_own).astype(jnp.float32) * SCALE
                 + tab[:, rel_bucket(tpos[:, None] - kpos_own[None, :])].astype(jnp.float32))
        l_own = jnp.where(kpos_own[None, :] <= tpos[:, None], l_own, -jnp.inf)
        k_g = kb[bi, hi, ic]
        v_g = vb[bi, hi, ic]
        kpos_g = ic[..., None] * L + jnp.arange(L)
        l_g = (jnp.einsum('bhcd,bhcnkd->bhcnk', qc, k_g).astype(jnp.float32) * SCALE
               + tab[hi5, rel_bucket(tpos[:, None, None] - kpos_g)].astype(jnp.float32))
        l_g = jnp.where(okc[..., None], l_g, -jnp.inf).reshape(B, H, C, ksel * L)
        probs = jax.nn.softmax(jnp.concatenate([l_own, l_g], axis=-1), axis=-1).astype(v.dtype)
        p_own = probs[..., :L]
        p_g = probs[..., L:].reshape(B, H, C, ksel, L)
        return (jnp.einsum('bhck,bhkd->bhcd', p_own, v_own)
                + jnp.einsum('bhcnk,bhcnkd->bhcd', p_g, v_g))

    starts = jnp.arange(nc, dtype=jnp.int32) * C
    out = lax.map(query_block, (chunks(qp), chunks(sel), chunks(sel_ok), starts))
    return jnp.moveaxis(out, 0, 2).reshape(B, H, sp, dh)[:, :, :S]


def nsa_attention(q, k_cmp_raw, v_cmp_raw, k_slc, v_slc, k_win, v_win, gate_logits, k_gain,
                  cmp_k, cmp_v, tab):
    B, G, R, S, dh = q.shape
    f32 = jnp.float32
    tpos = jnp.arange(S)
    kc = rms_norm(compress_blocks(k_cmp_raw, *cmp_k), k_gain[0])
    vc = compress_blocks(v_cmp_raw, *cmp_v)
    n_cmp = kc.shape[2]
    c_end = jnp.arange(n_cmp) * NSA_CMP_STRIDE + NSA_CMP_LEN - 1
    l_cmp = (jnp.einsum('bgrsd,bgnd->bgrsn', q, kc).astype(f32) * SCALE
             + tab[:, :, rel_bucket(tpos[:, None] - c_end[None, :])].astype(f32))
    p_cmp = masked_softmax_safe(l_cmp, c_end[None, :] <= tpos[:, None])
    o_cmp = jnp.einsum('bgrsn,bgnd->bgrsd', p_cmp.astype(vc.dtype), vc)
    SB = NSA_SEL_BLOCK
    n_sel = S // SB
    ci = np.arange(n_cmp)[:, None] * NSA_CMP_STRIDE
    sj = np.arange(n_sel)[None, :] * SB
    overlap = jnp.asarray(((ci < sj + SB) & (ci + NSA_CMP_LEN > sj)).astype(np.float32))
    imp = jnp.einsum('bgrsn,nj->bgsj', p_cmp, overlap)
    j = jnp.arange(n_sel)[None, :]
    own = (tpos // SB)[:, None]
    imp = jnp.where((j == 0) | (j == own) | (j == own - 1), jnp.inf, imp)
    imp = jnp.where(j > own, -jnp.inf, imp)
    ntop = min(NSA_SEL_TOPN, n_sel)
    _, sel = lax.top_k(imp, ntop)
    sel_ok = sel <= own
    ksb = rms_norm(k_slc, k_gain[1]).reshape(B, G, n_sel, SB, dh)
    vsb = v_slc.reshape(B, G, n_sel, SB, dh)
    C = NSA_Q_CHUNK
    nc = S // C
    q_ch = jnp.moveaxis(q.reshape(B, G, R, nc, C, dh), 3, 0)
    sel_ch = jnp.moveaxis(sel.reshape(B, G, nc, C, ntop), 2, 0)
    ok_ch = jnp.moveaxis(sel_ok.reshape(B, G, nc, C, ntop), 2, 0)
    bi = jnp.arange(B)[:, None, None, None]
    gi = jnp.arange(G)[None, :, None, None]
    gi6 = jnp.arange(G)[None, :, None, None, None, None]
    ri6 = jnp.arange(R)[None, None, :, None, None, None]

    def query_block(args):
        qc, ic, okc, c0 = args
        tq = c0 + jnp.arange(C)
        k_g = ksb[bi, gi, ic]
        v_g = vsb[bi, gi, ic]
        kpos = ic[..., None] * SB + jnp.arange(SB)
        mask = okc[..., None] & (kpos <= tq[:, None, None])
        bias = tab[gi6, ri6, rel_bucket(tq[:, None, None] - kpos)[:, :, None]]
        lg = jnp.einsum('bgrcd,bgcnkd->bgrcnk', qc, k_g).astype(f32) * SCALE + bias.astype(f32)
        lg = jnp.where(mask[:, :, None], lg, -jnp.inf).reshape(B, G, R, C, ntop * SB)
        probs = jax.nn.softmax(lg, axis=-1).astype(v_g.dtype).reshape(B, G, R, C, ntop, SB)
        return jnp.einsum('bgrcnk,bgcnkd->bgrcd', probs, v_g)

    o_slc = lax.map(query_block, (q_ch, sel_ch, ok_ch, jnp.arange(nc, dtype=jnp.int32) * C))
    o_slc = jnp.moveaxis(o_slc, 0, 3).reshape(B, G, R, S, dh)
    QB = WIN_Q_BLOCK
    nq = S // QB
    nw = NSA_WINDOW // QB
    kw_len = (nw + 1) * QB
    padw = ((0, 0), (0, 0), (NSA_WINDOW, 0), (0, 0))
    kwp = jnp.pad(rms_norm(k_win, k_gain[2]), padw).reshape(B, G, nq + nw, QB, dh)
    vwp = jnp.pad(v_win, padw).reshape(B, G, nq + nw, QB, dh)
    widx = np.arange(nq)[:, None] + np.arange(nw + 1)[None, :]
    kband = kwp[:, :, widx].reshape(B, G, nq, kw_len, dh)
    vband = vwp[:, :, widx].reshape(B, G, nq, kw_len, dh)
    dist = np.arange(QB)[:, None] + NSA_WINDOW - np.arange(kw_len)[None, :]
    kabs = np.arange(nq)[:, None, None] * QB - NSA_WINDOW + np.arange(kw_len)[None, None, :]
    wmask = (dist >= 0) & (dist < NSA_WINDOW) & (kabs >= 0)
    lw = (jnp.einsum('bgrnqd,bgnkd->bgrnqk', q.reshape(B, G, R, nq, QB, dh), kband).astype(f32) * SCALE
          + tab[:, :, rel_bucket(dist)][:, :, None].astype(f32))
    pw = jax.nn.softmax(jnp.where(wmask, lw, -jnp.inf), axis=-1).astype(vband.dtype)
    o_win = jnp.einsum('bgrnqk,bgnkd->bgrnqd', pw, vband).reshape(B, G, R, S, dh)
    gates = jax.nn.sigmoid(gate_logits).reshape(B, S, 3, G, R).transpose(2, 0, 3, 4, 1)[..., None]
    o = gates[0] * o_cmp + gates[1] * o_slc + gates[2] * o_win
    return o.transpose(0, 3, 1, 2, 4).reshape(B, S, G * R * dh)


def setup_inputs(seed: int = 0) -> dict:
    key = jax.random.key(seed)
    ks = jax.random.split(key, 32)
    f32 = jnp.float32
    L, dh = DEPTH, HEAD_DIM

    def nrm(k, shape, fan_in):
        return jax.random.normal(k, shape, f32) * fan_in ** -0.5

    def gain(k, shape):
        return 1.0 + 0.05 * jax.random.normal(k, shape, f32)

    return {
        'x': jax.random.normal(ks[0], (BATCH, SEQ, D_MODEL), f32),
        'p': jax.random.normal(ks[1], (DEPTH, BATCH, SEQ, PLE_DIM), f32),
        'rel_bias': 0.5 * jax.random.normal(ks[2], (N_ATTN_HEADS, REL_BUCKETS), f32),
        'attn_norm': gain(ks[3], (L, D_MODEL)),
        'w_in': nrm(ks[4], (L, D_MODEL, IN_COLS), D_MODEL),
        'moba_q_gain': gain(ks[5], (L, dh)),
        'moba_k_gain': gain(ks[6], (L, dh)),
        'nsa_q_gain': gain(ks[7], (L, dh)),
        'nsa_k_gain': gain(ks[8], (L, 3, dh)),
        'cmp_pos_k': 0.1 * jax.random.normal(ks[9], (L, NSA_CMP_LEN, dh), f32),
        'cmp_w1_k': nrm(ks[10], (L, NSA_CMP_LEN * dh, NSA_CMP_HIDDEN), NSA_CMP_LEN * dh),
        'cmp_w2_k': nrm(ks[11], (L, NSA_CMP_HIDDEN, dh), NSA_CMP_HIDDEN),
        'cmp_pos_v': 0.1 * jax.random.normal(ks[12], (L, NSA_CMP_LEN, dh), f32),
        'cmp_w1_v': nrm(ks[13], (L, NSA_CMP_LEN * dh, NSA_CMP_HIDDEN), NSA_CMP_LEN * dh),
        'cmp_w2_v': nrm(ks[14], (L, NSA_CMP_HIDDEN, dh), NSA_CMP_HIDDEN),
        'w_br_moba': nrm(ks[15], (L, MOBA_WIDTH, D_MODEL), MOBA_WIDTH),
        'w_br_nsa': nrm(ks[16], (L, NSA_WIDTH, D_MODEL), NSA_WIDTH),
        'w_o': nrm(ks[17], (L, D_MODEL, D_MODEL), D_MODEL),
        'ffn_norm': gain(ks[18], (L, D_MODEL)),
        'w_up': nrm(ks[19], (L, D_MODEL, 2 * D_FF), D_MODEL),
        'conv_w': nrm(ks[20], (L, CONV_W, 2 * D_FF), CONV_W),
        'conv_b': 0.02 * jax.random.normal(ks[21], (L, 2 * D_FF), f32),
        'w_down': nrm(ks[22], (L, D_FF, D_MODEL), D_FF),
        'w_ple_gate': nrm(ks[23], (L, D_MODEL, D_MODEL), D_MODEL),
        'w_ple': nrm(ks[24], (L, PLE_DIM, D_MODEL), PLE_DIM),
    }


def reference(x, p, rel_bias, attn_norm, w_in, moba_q_gain, moba_k_gain, nsa_q_gain, nsa_k_gain,
              cmp_pos_k, cmp_w1_k, cmp_w2_k, cmp_pos_v, cmp_w1_v, cmp_w2_v,
              w_br_moba, w_br_nsa, w_o, ffn_norm, w_up, conv_w, conv_b, w_down, w_ple_gate, w_ple):
    B, S, _ = x.shape
    G, R = NSA_KV_GROUPS, NSA_HPG
    tab_moba = rel_bias[:MOBA_HEADS]
    tab_nsa = rel_bias[MOBA_HEADS:].reshape(G, R, REL_BUCKETS)
    cuts = np.cumsum(IN_SIZES)[:-1].tolist()
    for i in range(DEPTH):
        h = rms_norm(x, attn_norm[i])
        (mq, mk, mv, nsq, kc, vc, ksl, vsl, kwn, vwn, ngate, mgate) = jnp.split(h @ w_in[i], cuts, axis=-1)
        q_a = rms_norm(to_heads(mq, MOBA_HEADS), moba_q_gain[i])
        k_a = rms_norm(to_heads(mk, MOBA_HEADS), moba_k_gain[i])
        y_a = moba_attention(q_a, k_a, to_heads(mv, MOBA_HEADS), tab_moba)
        y_a = y_a.transpose(0, 2, 1, 3).reshape(B, S, MOBA_WIDTH)
        q_b = rms_norm(nsq.reshape(B, S, G, R, HEAD_DIM).transpose(0, 2, 3, 1, 4), nsa_q_gain[i])
        y_b = nsa_attention(q_b, to_heads(kc, G), to_heads(vc, G), to_heads(ksl, G), to_heads(vsl, G),
                            to_heads(kwn, G), to_heads(vwn, G), ngate, nsa_k_gain[i],
                            (cmp_pos_k[i], cmp_w1_k[i], cmp_w2_k[i]),
                            (cmp_pos_v[i], cmp_w1_v[i], cmp_w2_v[i]), tab_nsa)
        g_a, g_b = jnp.split(jax.nn.sigmoid(mgate), 2, axis=-1)
        x = x + (g_a * (y_a @ w_br_moba[i]) + g_b * (y_b @ w_br_nsa[i])) @ w_o[i]
        u = causal_dwconv(rms_norm(x, ffn_norm[i]) @ w_up[i], conv_w[i], conv_b[i])
        u_act, u_lin = jnp.split(u, 2, axis=-1)
        x = x + (jax.nn.gelu(u_act) * u_lin) @ w_down[i]
        x = x + jax.nn.sigmoid(x @ w_ple_gate[i]) * (p[i] @ w_ple[i])
    return x
```

```python
import functools
import math

import numpy as np
import jax
import jax.numpy as jnp
from jax import lax
from jax.experimental import pallas as pl
from jax.experimental.pallas import tpu as pltpu

F32 = jnp.float32
BF16 = jnp.bfloat16

HEAD_DIM = 64
MOBA_HEADS = 8
MOBA_BLOCK = 256
MOBA_TOPK = 3
NSA_HEADS = 8
NSA_KV_GROUPS = 2
NSA_HPG = NSA_HEADS // NSA_KV_GROUPS
NSA_CMP_LEN = 32
NSA_CMP_STRIDE = 16
NSA_CMP_HIDDEN = 2 * HEAD_DIM
NSA_SEL_BLOCK = 64
NSA_SEL_TOPN = 16
NSA_WINDOW = 512
REL_BUCKETS = 32
REL_MAX_DIST = 128
CONV_W = 3
RMS_EPS = 1e-6
SCALE = HEAD_DIM ** -0.5
NEG = -1e30

LANES = 128
VMEM_LIMIT = 56 * 1024 * 1024

C_GA, C_GB, C_MQ, C_MK, C_MV, C_NSQ = 0, 1024, 2048, 2560, 3072, 3584
C_KC, C_VC, C_KSL, C_VSL, C_KWN, C_VWN, C_NG = 4096, 4224, 4352, 4480, 4608, 4736, 4864
PROJ_COLS = 4992

NSA_TQ = 128


def _dot(a, b):
    return jnp.dot(a, b, preferred_element_type=F32)


def _dot_nt(a, b):
    return lax.dot_general(a, b, (((1,), (1,)), ((), ())), preferred_element_type=F32)


def _rms(x, gain):
    return x * lax.rsqrt(jnp.mean(x * x, axis=-1, keepdims=True) + RMS_EPS) * gain


def _split_bf16(x):
    hi = x.astype(BF16)
    return hi, (x - hi.astype(F32)).astype(BF16)


def _lane_placer(width):
    r = lax.broadcasted_iota(jnp.int32, (HEAD_DIM, width), 0)
    c = lax.broadcasted_iota(jnp.int32, (HEAD_DIM, width), 1)
    return (r == c).astype(BF16)


def _softmax_init(s, v):
    m = jnp.max(s, axis=1, keepdims=True)
    p = jnp.exp(s - m)
    return m, jnp.sum(p, axis=1, keepdims=True), _dot(p.astype(BF16), v)


def _softmax_update(carry, s, v):
    m, l, acc = carry
    m_new = jnp.maximum(m, jnp.max(s, axis=1, keepdims=True))
    alpha = jnp.exp(m - m_new)
    p = jnp.exp(s - m_new)
    return (m_new, alpha * l + jnp.sum(p, axis=1, keepdims=True),
            alpha * acc + _dot(p.astype(BF16), v))


def _params(sem):
    return pltpu.CompilerParams(dimension_semantics=sem, vmem_limit_bytes=VMEM_LIMIT)


def _inproj_kernel(x_ref, g_ref, w_ref, o_ref):
    h = _rms(x_ref[...], g_ref[...]).astype(BF16)
    o_ref[...] = _dot(h, w_ref[...])


def _inproj(xf, gain, w):
    n, d = xf.shape
    tm = 256
    return pl.pallas_call(
        _inproj_kernel,
        grid=(n // tm,),
        in_specs=[pl.BlockSpec((tm, d), lambda i: (i, 0)),
                  pl.BlockSpec((1, d), lambda i: (0, 0)),
                  pl.BlockSpec((d, PROJ_COLS), lambda i: (0, 0))],
        out_specs=pl.BlockSpec((tm, PROJ_COLS), lambda i: (i, 0)),
        out_shape=jax.ShapeDtypeStruct((n, PROJ_COLS), F32),
        compiler_params=_params(("parallel",)),
        name="inproj",
    )(xf, gain.reshape(1, d), w)


def _moba_kernel(q_ref, k_ref, v_ref, qg_ref, kg_ref, t0_ref, t1_ref, o_ref,
                 kaug_s, v_s, km_s, *, seq):
    n = pl.program_id(2)
    blk = MOBA_BLOCK
    nb = seq // blk
    placer = _lane_placer(LANES)

    @pl.when(n == 0)
    def _prepare_keys():
        r = lax.broadcasted_iota(jnp.int32, (seq, LANES), 0)
        c = lax.broadcasted_iota(jnp.int32, (seq, LANES), 1)
        onehot = (c - HEAD_DIM == r // blk).astype(F32)
        for hh in range(2):
            kn = _rms(k_ref[:, hh * HEAD_DIM:(hh + 1) * HEAD_DIM], kg_ref[...])
            km = kn.reshape(nb, blk, HEAD_DIM).sum(axis=1) * (1.0 / blk)
            km_s[hh] = jnp.concatenate([km, jnp.zeros((16 - nb, HEAD_DIM), F32)], axis=0)
            kaug_s[hh] = (_dot(kn.astype(BF16), placer) + onehot).astype(BF16)
            v_s[hh] = v_ref[:, hh * HEAD_DIM:(hh + 1) * HEAD_DIM].astype(BF16)

    jidx = lax.broadcasted_iota(jnp.int32, (16, blk), 0)
    past = jidx < n
    ri = lax.broadcasted_iota(jnp.int32, (blk, blk), 0)
    ci = lax.broadcasted_iota(jnp.int32, (blk, blk), 1)
    outs = []
    for hh in range(2):
        qn = _rms(q_ref[:, hh * HEAD_DIM:(hh + 1) * HEAD_DIM], qg_ref[...])
        qh, ql = _split_bf16(qn)
        kmh, kml = _split_bf16(km_s[hh])
        gate = _dot_nt(kmh, qh) + _dot_nt(kml, qh) + _dot_nt(kmh, ql)
        gate = jnp.where(past, gate, NEG)
        cnt = jnp.zeros((16, blk), jnp.int32)
        for i in range(nb):
            gi = gate[i:i + 1, :]
            cnt += ((gi > gate) | ((gi == gate) & (i < jidx))).astype(jnp.int32)
        sel = (past & (cnt < MOBA_TOPK)) | (jidx == n)
        selb = jnp.where(sel, 0.0, NEG)
        selb = jnp.concatenate([jnp.zeros((HEAD_DIM, blk), F32), selb,
                                jnp.zeros((LANES - HEAD_DIM - 16, blk), F32)], axis=0)
        qaug = (_dot((qn * SCALE).astype(BF16), placer) + selb.T).astype(BF16)

        def scores(j, hh=hh, qaug=qaug):
            start = pl.multiple_of(j * blk, blk)
            return _dot_nt(qaug, kaug_s[hh, pl.ds(start, blk), :]), v_s[hh, pl.ds(start, blk), :]

        s, v = scores(n)
        s = jnp.where(ri >= ci, s + t0_ref[hh], NEG)
        carry = _softmax_init(s, v)

        def prev_body(_, c, hh=hh, scores=scores):
            s, v = scores(n - 1)
            return _softmax_update(c, s + t1_ref[hh], v)

        def far_body(j, c, scores=scores):
            s, v = scores(j)
            return _softmax_update(c, s, v)

        carry = lax.fori_loop(0, jnp.minimum(n, 1), prev_body, carry)
        m, l, acc = lax.fori_loop(0, jnp.maximum(n - 1, 0), far_body, carry)
        outs.append(acc * (1.0 / l))
    o_ref[...] = jnp.concatenate(outs, axis=-1)


def _moba(proj, q_gain, k_gain, t0, t1, batch, seq):
    blk = MOBA_BLOCK
    nb = seq // blk
    qc, kc, vc = C_MQ // LANES, C_MK // LANES, C_MV // LANES
    return pl.pallas_call(
        functools.partial(_moba_kernel, seq=seq),
        grid=(batch, MOBA_HEADS // 2, nb),
        in_specs=[pl.BlockSpec((blk, LANES), lambda b, h, n: (b * nb + n, qc + h)),
                  pl.BlockSpec((seq, LANES), lambda b, h, n: (b, kc + h)),
                  pl.BlockSpec((seq, LANES), lambda b, h, n: (b, vc + h)),
                  pl.BlockSpec((1, HEAD_DIM), lambda b, h, n: (0, 0)),
                  pl.BlockSpec((1, HEAD_DIM), lambda b, h, n: (0, 0)),
                  pl.BlockSpec((2, blk, blk), lambda b, h, n: (h, 0, 0)),
                  pl.BlockSpec((2, blk, blk), lambda b, h, n: (h, 0, 0))],
        out_specs=pl.BlockSpec((blk, LANES), lambda b, h, n: (b * nb + n, h)),
        out_shape=jax.ShapeDtypeStruct((batch * seq, MOBA_HEADS * HEAD_DIM), F32),
        scratch_shapes=[pltpu.VMEM((2, seq, LANES), BF16),
                        pltpu.VMEM((2, seq, HEAD_DIM), BF16),
                        pltpu.VMEM((2, 16, HEAD_DIM), F32)],
        compiler_params=_params(("parallel", "parallel", "arbitrary")),
        name="moba",
    )(proj, proj, proj, q_gain.reshape(1, HEAD_DIM), k_gain.reshape(1, HEAD_DIM), t0, t1)


def _compress_kernel(k_ref, v_ref, w1k_ref, w1v_ref, w2k_ref, w2v_ref, pk_ref, pv_ref, kg_ref,
                     ko_ref, vo_ref):
    hid = NSA_CMP_HIDDEN

    def compress(t_ref, w1_ref, w2_ref, pos_ref):
        w1 = w1_ref[...]
        a = _dot(t_ref[0, 0].astype(BF16), w1)
        pw = _dot(pos_ref[...].astype(BF16), w1)
        pos = pw[0:1, :hid] + pw[1:2, hid:]
        nxt = pltpu.roll(a[:, hid:], a.shape[0] - 1, 0)
        h = jax.nn.gelu(a[:, :hid] + nxt + pos)
        return _dot(h.astype(BF16), w2_ref[...])

    kc = compress(k_ref, w1k_ref, w2k_ref, pk_ref)
    ko_ref[0, 0] = _rms(kc, kg_ref[...]).astype(BF16)
    vo_ref[0, 0] = compress(v_ref, w1v_ref, w2v_ref, pv_ref).astype(BF16)


def _compress(k_r, v_r, w1k, w1v, w2k, w2v, pk, pv, kg):
    batch, groups, chunks, width = k_r.shape
    hid = NSA_CMP_HIDDEN
    tok = pl.BlockSpec((1, 1, chunks, width), lambda b, g: (b, g, 0, 0))
    full = lambda shape: pl.BlockSpec(shape, lambda b, g: (0,) * len(shape))
    out = pl.BlockSpec((1, 1, chunks, HEAD_DIM), lambda b, g: (b, g, 0, 0))
    return pl.pallas_call(
        _compress_kernel,
        grid=(batch, groups),
        in_specs=[tok, tok, full((width, 2 * hid)), full((width, 2 * hid)),
                  full((hid, HEAD_DIM)), full((hid, HEAD_DIM)),
                  full((8, width)), full((8, width)), full((1, HEAD_DIM))],
        out_specs=[out, out],
        out_shape=[jax.ShapeDtypeStruct((batch, groups, chunks, HEAD_DIM), BF16)] * 2,
        compiler_params=_params(("parallel", "parallel")),
        name="nsa_compress",
    )(k_r, v_r, w1k, w1v, w2k, w2v, pk, pv, kg)


def _nsa_kernel(q_ref, kc_ref, vc_ref, ksl_ref, vsl_ref, kwn_ref, vwn_ref, gt_ref, qg_ref, kg_ref,
                cb_ref, t0_ref, t1_ref, o_ref, kslaug_s, vsl_s, kwn_s, vwn_s, *, seq):
    qi = pl.program_id(1)
    tq = NSA_TQ
    hpg = NSA_HPG
    sb = NSA_SEL_BLOCK
    nsel = seq // sb
    ncmp = seq // NSA_CMP_STRIDE
    placer = _lane_placer(LANES)
    t_start = qi * tq

    @pl.when(qi == 0)
    def _prepare_keys():
        r = lax.broadcasted_iota(jnp.int32, (seq, LANES), 0)
        c = lax.broadcasted_iota(jnp.int32, (seq, LANES), 1)
        onehot = (c - HEAD_DIM == r // sb).astype(F32)
        for g in range(NSA_KV_GROUPS):
            sl = slice(g * HEAD_DIM, (g + 1) * HEAD_DIM)
            ks = _rms(ksl_ref[:, sl], kg_ref[1:2, :])
            kslaug_s[g] = (_dot(ks.astype(BF16), placer) + onehot).astype(BF16)
            vsl_s[g] = vsl_ref[:, sl].astype(BF16)
            kwn_s[g] = _rms(kwn_ref[:, sl], kg_ref[2:3, :]).astype(BF16)
            vwn_s[g] = vwn_ref[:, sl].astype(BF16)

    stack = lambda parts: jnp.concatenate(parts, axis=0)
    ri = lax.broadcasted_iota(jnp.int32, (tq, tq), 0)
    ci = lax.broadcasted_iota(jnp.int32, (tq, tq), 1)
    causal = stack([ri >= ci] * hpg)
    upper = stack([ci > ri] * hpg)
    cend = lax.broadcasted_iota(jnp.int32, (tq, ncmp), 1) * NSA_CMP_STRIDE + (NSA_CMP_LEN - 1)
    tpos = lax.broadcasted_iota(jnp.int32, (tq, ncmp), 0) + t_start
    cvis = stack([cend <= tpos] * hpg)
    oj = lax.broadcasted_iota(jnp.int32, (nsel, ncmp), 0) * sb
    on = lax.broadcasted_iota(jnp.int32, (nsel, ncmp), 1) * NSA_CMP_STRIDE
    overlap = ((on < oj + sb) & (on + NSA_CMP_LEN > oj) & (on < seq - NSA_CMP_STRIDE)).astype(BF16)
    jidx = lax.broadcasted_iota(jnp.int32, (nsel, tq), 0)
    own = (lax.broadcasted_iota(jnp.int32, (nsel, tq), 1) + t_start) // sb
    gates = jax.nn.sigmoid(gt_ref[...])

    outs = []
    for g in range(NSA_KV_GROUPS):
        heads = [g * hpg + r for r in range(hpg)]
        qs = stack([(_rms(q_ref[:, h * HEAD_DIM:(h + 1) * HEAD_DIM], qg_ref[...]) * SCALE).astype(BF16)
                    for h in heads])
        t0 = stack([t0_ref[h] for h in heads])
        t1 = stack([t1_ref[h] for h in heads])

        s = _dot_nt(qs, kc_ref[0, g]) + stack([cb_ref[h] for h in heads])
        s = jnp.where(cvis, s, NEG)
        m = jnp.max(s, axis=1, keepdims=True)
        e = jnp.where(cvis, jnp.exp(s - m), 0.0)
        den = jnp.sum(e, axis=1, keepdims=True)
        p = e / jnp.where(den > 0, den, 1.0)
        o_cmp = _dot(p.astype(BF16), vc_ref[0, g])
        psum = p[0:tq]
        for r in range(1, hpg):
            psum = psum + p[r * tq:(r + 1) * tq]
        ph, plo = _split_bf16(psum)
        imp = _dot_nt(overlap, ph) + _dot_nt(overlap, plo)
        imp = jnp.where((jidx == 0) | (jidx == own) | (jidx == own - 1), -NEG, imp)
        imp = jnp.where(jidx > own, NEG, imp)
        cnt = jnp.zeros((nsel, tq), jnp.int32)
        for i in range(nsel):
            vi = imp[i:i + 1, :]
            cnt += ((vi > imp) | ((vi == imp) & (i < jidx))).astype(jnp.int32)
        selb = jnp.where((cnt < NSA_SEL_TOPN) & (jidx <= own), 0.0, NEG)
        selb = jnp.concatenate([jnp.zeros((HEAD_DIM, tq), F32), selb,
                                jnp.zeros((LANES - HEAD_DIM - nsel, tq), F32)], axis=0).T
        qaug = (_dot(qs, placer) + stack([selb] * hpg)).astype(BF16)

        def slc_tile(j, g=g, qaug=qaug):
            start = pl.multiple_of(j * tq, tq)
            return _dot_nt(qaug, kslaug_s[g, pl.ds(start, tq), :]), vsl_s[g, pl.ds(start, tq), :]

        s, v = slc_tile(qi)
        carry = _softmax_init(jnp.where(causal, s + t0, NEG), v)

        def slc_prev(_, c, slc_tile=slc_tile, t1=t1):
            s, v = slc_tile(qi - 1)
            return _softmax_update(c, s + t1, v)

        def slc_far(j, c, slc_tile=slc_tile):
            s, v = slc_tile(j)
            return _softmax_update(c, s, v)

        carry = lax.fori_loop(0, jnp.minimum(qi, 1), slc_prev, carry)
        m, l, acc = lax.fori_loop(0, jnp.maximum(qi - 1, 0), slc_far, carry)
        o_slc = acc * (1.0 / l)

        def win_tile(j, g=g, qs=qs):
            start = pl.multiple_of(j * tq, tq)
            return _dot_nt(qs, kwn_s[g, pl.ds(start, tq), :]), vwn_s[g, pl.ds(start, tq), :]

        s, v = win_tile(qi)
        carry = _softmax_init(jnp.where(causal, s + t0, NEG), v)

        def win_prev(_, c, win_tile=win_tile, t1=t1):
            s, v = win_tile(qi - 1)
            return _softmax_update(c, s + t1, v)

        def win_mid(j, c, win_tile=win_tile):
            s, v = win_tile(j)
            return _softmax_update(c, s, v)

        def win_edge(_, c, win_tile=win_tile):
            s, v = win_tile(qi - NSA_WINDOW // tq)
            return _softmax_update(c, jnp.where(upper, s, NEG), v)

        carry = lax.fori_loop(0, jnp.minimum(qi, 1), win_prev, carry)
        carry = lax.fori_loop(jnp.maximum(qi - NSA_WINDOW // tq + 1, 0), jnp.maximum(qi - 1, 0),
                              win_mid, carry)
        m, l, acc = lax.fori_loop(0, (qi >= NSA_WINDOW // tq).astype(jnp.int32), win_edge, carry)
        o_win = acc * (1.0 / l)

        for r, h in enumerate(heads):
            rows = slice(r * tq, (r + 1) * tq)
            outs.append(gates[:, h:h + 1] * o_cmp[rows]
                        + gates[:, NSA_HEADS + h:NSA_HEADS + h + 1] * o_slc[rows]
                        + gates[:, 2 * NSA_HEADS + h:2 * NSA_HEADS + h + 1] * o_win[rows])
    o_ref[...] = jnp.concatenate(outs, axis=-1)


def _nsa(proj, kcn, vcm, q_gain, k_gain, cb, t0, t1, batch, seq):
    tq = NSA_TQ
    nq = seq // tq
    ncmp = seq // NSA_CMP_STRIDE
    width = NSA_HEADS * HEAD_DIM
    kv = lambda col: pl.BlockSpec((seq, LANES), lambda b, i: (b, col // LANES))
    cmp_spec = pl.BlockSpec((1, NSA_KV_GROUPS, ncmp, HEAD_DIM), lambda b, i: (b, 0, 0, 0))
    return pl.pallas_call(
        functools.partial(_nsa_kernel, seq=seq),
        grid=(batch, nq),
        in_specs=[pl.BlockSpec((tq, width), lambda b, i: (b * nq + i, C_NSQ // width)),
                  cmp_spec, cmp_spec, kv(C_KSL), kv(C_VSL), kv(C_KWN), kv(C_VWN),
                  pl.BlockSpec((tq, LANES), lambda b, i: (b * nq + i, C_NG // LANES)),
                  pl.BlockSpec((1, HEAD_DIM), lambda b, i: (0, 0)),
                  pl.BlockSpec((3, HEAD_DIM), lambda b, i: (0, 0)),
                  pl.BlockSpec((NSA_HEADS, tq, ncmp), lambda b, i: (0, i, 0)),
                  pl.BlockSpec((NSA_HEADS, tq, tq), lambda b, i: (0, 0, 0)),
                  pl.BlockSpec((NSA_HEADS, tq, tq), lambda b, i: (0, 0, 0))],
        out_specs=pl.BlockSpec((tq, width), lambda b, i: (b * nq + i, 0)),
        out_shape=jax.ShapeDtypeStruct((batch * seq, width), F32),
        scratch_shapes=[pltpu.VMEM((NSA_KV_GROUPS, seq, LANES), BF16),
                        pltpu.VMEM((NSA_KV_GROUPS, seq, HEAD_DIM), BF16),
                        pltpu.VMEM((NSA_KV_GROUPS, seq, HEAD_DIM), BF16),
                        pltpu.VMEM((NSA_KV_GROUPS, seq, HEAD_DIM), BF16)],
        compiler_params=_params(("parallel", "arbitrary")),
        name="nsa",
    )(proj, kcn, vcm, proj, proj, proj, proj, proj, q_gain.reshape(1, HEAD_DIM), k_gain, cb, t0, t1)


def _merge_kernel(x_ref, ya_ref, yb_ref, ga_ref, gb_ref, wa_ref, wb_ref, wo_ref, o_ref):
    a = _dot(ya_ref[...].astype(BF16), wa_ref[...])
    b = _dot(yb_ref[...].astype(BF16), wb_ref[...])
    z = jax.nn.sigmoid(ga_ref[...]) * a + jax.nn.sigmoid(gb_ref[...]) * b
    o_ref[...] = x_ref[...] + _dot(z.astype(BF16), wo_ref[...])


def _merge(xf, ya, yb, proj, wa, wb, wo):
    n, d = xf.shape
    tm = 256
    row = lambda w, col=0: pl.BlockSpec((tm, w), lambda i: (i, col))
    full = lambda a: pl.BlockSpec(a.shape, lambda i: (0, 0))
    return pl.pallas_call(
        _merge_kernel,
        grid=(n // tm,),
        in_specs=[row(d), row(ya.shape[1]), row(yb.shape[1]), row(d, C_GA // d), row(d, C_GB // d),
                  full(wa), full(wb), full(wo)],
        out_specs=row(d),
        out_shape=jax.ShapeDtypeStruct((n, d), F32),
        compiler_params=_params(("parallel",)),
        name="merge",
    )(xf, ya, yb, proj, proj, wa, wb, wo)


FFN_HALO = 16


def _ffn_kernel(x_ref, xh_ref, g_ref, wa_ref, wb_ref, cwa_ref, cwb_ref, cba_ref, cbb_ref, wd_ref,
                o_ref, hn_s, acc_s, *, seq, tm):
    i = pl.program_id(0)
    j = pl.program_id(1)

    @pl.when(j == 0)
    def _normalise():
        halo = _rms(xh_ref[...], g_ref[...])
        at_start = (i * tm) % seq == 0
        hn_s[0:FFN_HALO, :] = jnp.where(at_start, 0.0, halo).astype(BF16)
        hn_s[FFN_HALO:, :] = _rms(x_ref[...], g_ref[...]).astype(BF16)
        acc_s[...] = jnp.zeros_like(acc_s)

    hn = hn_s[...]

    def conv(w_ref, cw_ref, cb_ref):
        u = _dot(hn, w_ref[...])
        u1 = pltpu.roll(u, 1, 0)[FFN_HALO:]
        u2 = pltpu.roll(u, 2, 0)[FFN_HALO:]
        cw = cw_ref[...]
        return cw[0:1] * u2 + cw[1:2] * u1 + cw[2:3] * u[FFN_HALO:] + cb_ref[...]

    act = jax.nn.gelu(conv(wa_ref, cwa_ref, cba_ref)) * conv(wb_ref, cwb_ref, cbb_ref)
    acc_s[...] += _dot(act.astype(BF16), wd_ref[...])

    @pl.when(j == pl.num_programs(1) - 1)
    def _finish():
        o_ref[...] = x_ref[...] + acc_s[...]


def _ffn(xf, gain, w_up, conv_w, conv_b, w_down, seq):
    n, d = xf.shape
    d_ff = w_down.shape[0]
    tm, tf = 1024, 256
    nf = d_ff // tf
    hb = tm // FFN_HALO
    return pl.pallas_call(
        functools.partial(_ffn_kernel, seq=seq, tm=tm),
        grid=(n // tm, nf),
        in_specs=[pl.BlockSpec((tm, d), lambda i, j: (i, 0)),
                  pl.BlockSpec((FFN_HALO, d), lambda i, j: (jnp.maximum(i * hb - 1, 0), 0)),
                  pl.BlockSpec((1, d), lambda i, j: (0, 0)),
                  pl.BlockSpec((d, tf), lambda i, j: (0, j)),
                  pl.BlockSpec((d, tf), lambda i, j: (0, nf + j)),
                  pl.BlockSpec((CONV_W, tf), lambda i, j: (0, j)),
                  pl.BlockSpec((CONV_W, tf), lambda i, j: (0, nf + j)),
                  pl.BlockSpec((1, tf), lambda i, j: (0, j)),
                  pl.BlockSpec((1, tf), lambda i, j: (0, nf + j)),
                  pl.BlockSpec((tf, d), lambda i, j: (j, 0))],
        out_specs=pl.BlockSpec((tm, d), lambda i, j: (i, 0)),
        out_shape=jax.ShapeDtypeStruct((n, d), F32),
        scratch_shapes=[pltpu.VMEM((FFN_HALO + tm, d), BF16), pltpu.VMEM((tm, d), F32)],
        compiler_params=_params(("parallel", "arbitrary")),
        name="conv_ffn",
    )(xf, xf, gain.reshape(1, d), w_up, w_up, conv_w, conv_w, conv_b.reshape(1, -1),
      conv_b.reshape(1, -1), w_down)


def _ple_kernel(x_ref, p_ref, wg_ref, wp_ref, o_ref):
    x = x_ref[...]
    gate = jax.nn.sigmoid(_dot(x.astype(BF16), wg_ref[...]))
    o_ref[...] = x + gate * _dot(p_ref[...].astype(BF16), wp_ref[...])


def _ple(xf, pf, wg, wp):
    n, d = xf.shape
    tm = 512
    return pl.pallas_call(
        _ple_kernel,
        grid=(n // tm,),
        in_specs=[pl.BlockSpec((tm, d), lambda i: (i, 0)),
                  pl.BlockSpec((tm, pf.shape[1]), lambda i: (i, 0)),
                  pl.BlockSpec(wg.shape, lambda i: (0, 0)),
                  pl.BlockSpec(wp.shape, lambda i: (0, 0))],
        out_specs=pl.BlockSpec((tm, d), lambda i: (i, 0)),
        out_shape=jax.ShapeDtypeStruct((n, d), F32),
        compiler_params=_params(("parallel",)),
        name="ple",
    )(xf, pf, wg, wp)


def _rel_bucket_np(dist):
    n = np.maximum(dist, 0)
    max_exact = REL_BUCKETS // 2
    nf = np.maximum(n, 1).astype(np.float32)
    large = max_exact + (np.log(nf / np.float32(max_exact)) / np.float32(math.log(REL_MAX_DIST / max_exact))
                         * np.float32(REL_BUCKETS - max_exact)).astype(np.int32)
    return np.where(n < max_exact, n, np.minimum(large, REL_BUCKETS - 1))


def _bias_tables(rel_bias, seq):
    shifted = rel_bias - rel_bias[:, REL_BUCKETS - 1:]
    by_dist = shifted[:, _rel_bucket_np(np.arange(seq + MOBA_BLOCK))]

    def toeplitz(tab, size, offset):
        d = offset + np.arange(size)[:, None] - np.arange(size)[None, :]
        return jnp.where(jnp.asarray(d >= 0), tab[:, np.maximum(d, 0)], 0.0)

    moba, nsa = by_dist[:MOBA_HEADS], by_dist[MOBA_HEADS:]
    ncmp = seq // NSA_CMP_STRIDE
    dc = np.arange(seq)[:, None] - (np.arange(ncmp)[None, :] * NSA_CMP_STRIDE + NSA_CMP_LEN - 1)
    cmp_bias = jnp.where(jnp.asarray(dc >= 0), nsa[:, np.maximum(dc, 0)], 0.0)
    return (toeplitz(moba, MOBA_BLOCK, 0), toeplitz(moba, MOBA_BLOCK, MOBA_BLOCK),
            toeplitz(nsa, NSA_TQ, 0), toeplitz(nsa, NSA_TQ, NSA_TQ), cmp_bias)


def _reorder_in_proj(w):
    attn = 3 * MOBA_HEADS * HEAD_DIM + NSA_HEADS * HEAD_DIM + 6 * NSA_KV_GROUPS * HEAD_DIM
    ng = 3 * NSA_HEADS
    pad = jnp.zeros((w.shape[0], PROJ_COLS - w.shape[1]), w.dtype)
    return jnp.concatenate([w[:, attn + ng:], w[:, :attn], w[:, attn:attn + ng], pad], axis=1)


def _cmp_weights(w1, pos):
    half = NSA_CMP_STRIDE * HEAD_DIM
    w1cat = jnp.concatenate([w1[:half], w1[half:]], axis=1).astype(BF16)
    posr = jnp.zeros((8, half), F32).at[0:2].set(pos.reshape(2, half))
    return w1cat, posr


def kernel(x, p, rel_bias, attn_norm, w_in, moba_q_gain, moba_k_gain, nsa_q_gain, nsa_k_gain,
           cmp_pos_k, cmp_w1_k, cmp_w2_k, cmp_pos_v, cmp_w1_v, cmp_w2_v,
           w_br_moba, w_br_nsa, w_o, ffn_norm, w_up, conv_w, conv_b, w_down, w_ple_gate, w_ple):
    batch, seq, d = x.shape
    n = batch * seq
    depth = w_in.shape[0]
    groups = NSA_KV_GROUPS
    chunks = seq // NSA_CMP_STRIDE
    t0m, t1m, t0n, t1n, cmp_bias = _bias_tables(rel_bias, seq)
    xf = x.reshape(n, d)
    for i in range(depth):
        proj = _inproj(xf, attn_norm[i], _reorder_in_proj(w_in[i]).astype(BF16))
        ya = _moba(proj, moba_q_gain[i], moba_k_gain[i], t0m, t1m, batch, seq)

        def chunked(col):
            t = proj[:, col:col + groups * HEAD_DIM].reshape(batch, seq, groups, HEAD_DIM)
            return t.transpose(0, 2, 1, 3).reshape(batch, groups, chunks, NSA_CMP_STRIDE * HEAD_DIM)

        w1k, pk = _cmp_weights(cmp_w1_k[i], cmp_pos_k[i])
        w1v, pv = _cmp_weights(cmp_w1_v[i], cmp_pos_v[i])
        kcn, vcm = _compress(chunked(C_KC), chunked(C_VC), w1k, w1v, cmp_w2_k[i].astype(BF16),
                             cmp_w2_v[i].astype(BF16), pk, pv, nsa_k_gain[i, 0:1])
        yb = _nsa(proj, kcn, vcm, nsa_q_gain[i], nsa_k_gain[i], cmp_bias, t0n, t1n, batch, seq)
        xf = _merge(xf, ya, yb, proj, w_br_moba[i].astype(BF16), w_br_nsa[i].astype(BF16),
                    w_o[i].astype(BF16))
        xf = _ffn(xf, ffn_norm[i], w_up[i].astype(BF16), conv_w[i], conv_b[i],
                  w_down[i].astype(BF16), seq)
        xf = _ple(xf, p[i].reshape(n, -1), w_ple_gate[i].astype(BF16), w_ple[i].astype(BF16))
    return xf.reshape(batch, seq, d)
```

```python
import functools
import math

import numpy as np
import jax
import jax.numpy as jnp
from jax import lax
from jax.experimental import pallas as pl
from jax.experimental.pallas import tpu as pltpu

F32 = jnp.float32
BF16 = jnp.bfloat16

HEAD_DIM = 64
MOBA_HEADS = 8
MOBA_BLOCK = 256
MOBA_TOPK = 3
NSA_HEADS = 8
NSA_KV_GROUPS = 2
NSA_HPG = NSA_HEADS // NSA_KV_GROUPS
NSA_CMP_LEN = 32
NSA_CMP_STRIDE = 16
NSA_CMP_HIDDEN = 2 * HEAD_DIM
NSA_SEL_BLOCK = 64
NSA_SEL_TOPN = 16
NSA_WINDOW = 512
REL_BUCKETS = 32
REL_MAX_DIST = 128
CONV_W = 3
RMS_EPS = 1e-6
SCALE = HEAD_DIM ** -0.5
NEG = -1e30

LANES = 128
VMEM_LIMIT = 56 * 1024 * 1024

C_GA, C_GB, C_MQ, C_MK, C_MV, C_NSQ = 0, 1024, 2048, 2560, 3072, 3584
C_KC, C_VC, C_KSL, C_VSL, C_KWN, C_VWN, C_NG = 4096, 4224, 4352, 4480, 4608, 4736, 4864
PROJ_COLS = 4992

NSA_TQ = 128
CMP_WIN = 16
CMP_PAD = 8


def _dot(a, b):
    return jnp.dot(a, b, preferred_element_type=F32)


def _dot_nt(a, b):
    return lax.dot_general(a, b, (((1,), (1,)), ((), ())), preferred_element_type=F32)


def _rms(x, gain):
    return x * lax.rsqrt(jnp.mean(x * x, axis=-1, keepdims=True) + RMS_EPS) * gain


def _split_bf16(x):
    hi = x.astype(BF16)
    return hi, (x - hi.astype(F32)).astype(BF16)


def _eye(rows, cols):
    r = lax.broadcasted_iota(jnp.int32, (rows, cols), 0)
    c = lax.broadcasted_iota(jnp.int32, (rows, cols), 1)
    return (r == c).astype(BF16)


def _transpose_bf16(x, rows):
    return _dot_nt(_eye(rows, x.shape[1]), x)


def _softmax_init(s, vt):
    m = jnp.max(s, axis=0, keepdims=True)
    p = jnp.exp(s - m)
    return m, jnp.sum(p, axis=0, keepdims=True), _dot(vt, p.astype(BF16))


def _softmax_joint(tiles):
    m = functools.reduce(jnp.maximum, [jnp.max(s, axis=0, keepdims=True) for s, _ in tiles])
    ps = [jnp.exp(s - m) for s, _ in tiles]
    l = functools.reduce(jnp.add, [jnp.sum(p, axis=0, keepdims=True) for p in ps])
    acc = functools.reduce(jnp.add, [_dot(vt, p.astype(BF16)) for p, (_, vt) in zip(ps, tiles)])
    return m, l, acc


def _softmax_update(carry, s, vt):
    m, l, acc = carry
    m_new = jnp.maximum(m, jnp.max(s, axis=0, keepdims=True))
    alpha = jnp.exp(m - m_new)
    p = jnp.exp(s - m_new)
    return (m_new, alpha * l + jnp.sum(p, axis=0, keepdims=True),
            alpha * acc + _dot(vt, p.astype(BF16)))


def _rank_select(score, idx, count):
    beaten = jnp.zeros(score.shape, jnp.int32)
    for i in range(score.shape[0]):
        si = score[i:i + 1, :]
        beaten += ((si > score) | ((si == score) & (i < idx))).astype(jnp.int32)
    return beaten < count


def _params(sem):
    return pltpu.CompilerParams(dimension_semantics=sem, vmem_limit_bytes=VMEM_LIMIT)


def _rel_bucket_np(dist):
    n = np.maximum(dist, 0)
    max_exact = REL_BUCKETS // 2
    nf = np.maximum(n, 1).astype(np.float32)
    large = max_exact + (np.log(nf / np.float32(max_exact)) / np.float32(math.log(REL_MAX_DIST / max_exact))
                         * np.float32(REL_BUCKETS - max_exact)).astype(np.int32)
    return np.where(n < max_exact, n, np.minimum(large, REL_BUCKETS - 1))


def _bucket_starts():
    buckets = _rel_bucket_np(np.arange(4 * REL_MAX_DIST))
    return [int(np.argmax(buckets >= k)) for k in range(REL_BUCKETS)]


BUCKET_START = _bucket_starts()
BIAS_REACH = BUCKET_START[-1]
assert BIAS_REACH <= MOBA_BLOCK and BIAS_REACH <= NSA_TQ - NSA_CMP_LEN + 1 + NSA_CMP_STRIDE


def _tables_kernel(tab_ref, t0m_ref, t1m_ref, t0n_ref, t1n_ref, wc_ref):
    h = pl.program_id(0)

    def bias(dist, head):
        last = tab_ref[head, REL_BUCKETS - 1]
        val = jnp.zeros(dist.shape, F32)
        for k in range(REL_BUCKETS - 2, -1, -1):
            val = jnp.where(dist < BUCKET_START[k + 1], tab_ref[head, k] - last, val)
        return val

    def toeplitz(size, offset, head):
        key = lax.broadcasted_iota(jnp.int32, (size, size), 0)
        qry = lax.broadcasted_iota(jnp.int32, (size, size), 1)
        return bias(offset + qry - key, head)

    t0m_ref[0] = toeplitz(MOBA_BLOCK, 0, h)
    t1m_ref[0] = toeplitz(MOBA_BLOCK, MOBA_BLOCK, h)
    t0n_ref[0] = toeplitz(NSA_TQ, 0, MOBA_HEADS + h)
    t1n_ref[0] = toeplitz(NSA_TQ, NSA_TQ, MOBA_HEADS + h)
    a = lax.broadcasted_iota(jnp.int32, (CMP_WIN, NSA_TQ), 0)
    i = lax.broadcasted_iota(jnp.int32, (CMP_WIN, NSA_TQ), 1)
    wc_ref[0] = bias(i + (NSA_TQ - NSA_CMP_LEN + 1) - NSA_CMP_STRIDE * a, MOBA_HEADS + h)


def _bias_tables(rel_bias):
    blk, tq = MOBA_BLOCK, NSA_TQ
    shapes = [(blk, blk), (blk, blk), (tq, tq), (tq, tq), (CMP_WIN, tq)]
    return pl.pallas_call(
        _tables_kernel,
        grid=(MOBA_HEADS,),
        in_specs=[pl.BlockSpec(memory_space=pltpu.SMEM)],
        out_specs=[pl.BlockSpec((1,) + s, lambda h: (h, 0, 0)) for s in shapes],
        out_shape=[jax.ShapeDtypeStruct((MOBA_HEADS,) + s, F32) for s in shapes],
        compiler_params=_params(("arbitrary",)),
        name="bias_tables",
    )(rel_bias)


def _inproj_kernel(x_ref, g_ref, w_ref, o_ref):
    h = _rms(x_ref[...], g_ref[...]).astype(BF16)
    o_ref[...] = _dot(h, w_ref[...])


def _inproj(xf, gain, w):
    n, d = xf.shape
    tm = 256
    return pl.pallas_call(
        _inproj_kernel,
        grid=(n // tm,),
        in_specs=[pl.BlockSpec((tm, d), lambda i: (i, 0)),
                  pl.BlockSpec((1, d), lambda i: (0, 0)),
                  pl.BlockSpec((d, PROJ_COLS), lambda i: (0, 0))],
        out_specs=pl.BlockSpec((tm, PROJ_COLS), lambda i: (i, 0)),
        out_shape=jax.ShapeDtypeStruct((n, PROJ_COLS), F32),
        compiler_params=_params(("parallel",)),
        name="inproj",
    )(xf, gain.reshape(1, d), w)


MOBA_HPS = 4


def _moba_kernel(q_ref, k_ref, v_ref, qg_ref, kg_ref, t0_ref, t1_ref, o_ref,
                 kaug_s, vt_s, km_s, *, seq):
    n = pl.program_id(2)
    blk = MOBA_BLOCK
    nb = seq // blk
    head_cols = lambda hh: slice(hh * HEAD_DIM, (hh + 1) * HEAD_DIM)

    @pl.when(n == 0)
    def _prepare_keys():
        r = lax.broadcasted_iota(jnp.int32, (seq, LANES), 0)
        c = lax.broadcasted_iota(jnp.int32, (seq, LANES), 1)
        onehot = (c - HEAD_DIM == r // blk).astype(F32)
        place = _eye(HEAD_DIM, LANES)
        for hh in range(MOBA_HPS):
            kn = _rms(k_ref[:, head_cols(hh)], kg_ref[...])
            km = kn.reshape(nb, blk, HEAD_DIM).sum(axis=1) * (1.0 / blk)
            km_s[hh] = jnp.concatenate([km, jnp.zeros((16 - nb, HEAD_DIM), F32)], axis=0)
            kaug_s[hh] = (_dot(kn.astype(BF16), place) + onehot).astype(BF16)
            for j in range(nb):
                vj = v_ref[j * blk:(j + 1) * blk, head_cols(hh)].astype(BF16)
                vt_s[hh, j] = _transpose_bf16(vj, HEAD_DIM).astype(BF16)

    jidx = lax.broadcasted_iota(jnp.int32, (16, blk), 0)
    past = jidx < n
    key = lax.broadcasted_iota(jnp.int32, (blk, blk), 0)
    qry = lax.broadcasted_iota(jnp.int32, (blk, blk), 1)
    prev = jnp.maximum(n - 1, 0)
    qaugs, state = [], []
    for hh in range(MOBA_HPS):
        qn = _rms(q_ref[:, head_cols(hh)], qg_ref[...])
        qh, ql = _split_bf16(qn)
        kmh, kml = _split_bf16(km_s[hh])
        gate = _dot_nt(kmh, qh) + _dot_nt(kml, qh) + _dot_nt(kmh, ql)
        gate = jnp.where(past, gate, NEG)
        sel = (past & _rank_select(gate, jidx, MOBA_TOPK)) | (jidx == n)
        selb = jnp.where(sel, 0.0, NEG)
        qaug = _transpose_bf16((qn * SCALE).astype(BF16), LANES)
        qaugs.append((qaug + jnp.concatenate([jnp.zeros((HEAD_DIM, blk), F32), selb,
                                              jnp.zeros((LANES - HEAD_DIM - 16, blk), F32)], axis=0)
                      ).astype(BF16))

    def scores(hh, j):
        start = pl.multiple_of(j * blk, blk)
        return _dot(kaug_s[hh, pl.ds(start, blk), :], qaugs[hh]), vt_s[hh, j]

    near = [(scores(hh, n), scores(hh, prev)) for hh in range(MOBA_HPS)]
    for hh, ((s0, vt0), (s1, vt1)) in enumerate(near):
        s0 = jnp.where(key <= qry, s0 + t0_ref[hh], NEG)
        s1 = jnp.where(n >= 1, s1 + t1_ref[hh], NEG)
        state.append(_softmax_joint([(s0, vt0), (s1, vt1)]))

    def far_body(j, carry):
        tiles = [scores(hh, j) for hh in range(MOBA_HPS)]
        return tuple(_softmax_update(carry[hh], *tiles[hh]) for hh in range(MOBA_HPS))

    state = lax.fori_loop(0, jnp.maximum(n - 1, 0), far_body, tuple(state))
    o_ref[...] = jnp.concatenate([acc * (1.0 / l) for _, l, acc in state], axis=0).T


def _moba(proj, q_gain, k_gain, t0, t1, batch, seq):
    blk = MOBA_BLOCK
    nb = seq // blk
    width = MOBA_HPS * HEAD_DIM
    qc, kc, vc = C_MQ // width, C_MK // width, C_MV // width
    return pl.pallas_call(
        functools.partial(_moba_kernel, seq=seq),
        grid=(batch, MOBA_HEADS // MOBA_HPS, nb),
        in_specs=[pl.BlockSpec((blk, width), lambda b, h, n: (b * nb + n, qc + h)),
                  pl.BlockSpec((seq, width), lambda b, h, n: (b, kc + h)),
                  pl.BlockSpec((seq, width), lambda b, h, n: (b, vc + h)),
                  pl.BlockSpec((1, HEAD_DIM), lambda b, h, n: (0, 0)),
                  pl.BlockSpec((1, HEAD_DIM), lambda b, h, n: (0, 0)),
                  pl.BlockSpec((MOBA_HPS, blk, blk), lambda b, h, n: (h, 0, 0)),
                  pl.BlockSpec((MOBA_HPS, blk, blk), lambda b, h, n: (h, 0, 0))],
        out_specs=pl.BlockSpec((blk, width), lambda b, h, n: (b * nb + n, h)),
        out_shape=jax.ShapeDtypeStruct((batch * seq, MOBA_HEADS * HEAD_DIM), F32),
        scratch_shapes=[pltpu.VMEM((MOBA_HPS, seq, LANES), BF16),
                        pltpu.VMEM((MOBA_HPS, nb, HEAD_DIM, blk), BF16),
                        pltpu.VMEM((MOBA_HPS, 16, HEAD_DIM), F32)],
        compiler_params=_params(("parallel", "parallel", "arbitrary")),
        name="moba",
    )(proj, proj, proj, q_gain.reshape(1, HEAD_DIM), k_gain.reshape(1, HEAD_DIM), t0, t1)


def _compress_kernel(k_ref, v_ref, w1k_ref, w1v_ref, w2k_ref, w2v_ref, pk_ref, pv_ref, kg_ref,
                     ko_ref, vo_ref):
    hid = NSA_CMP_HIDDEN

    def compress(t_ref, w1_ref, w2_ref, pos_ref):
        w1 = w1_ref[...]
        a = _dot(t_ref[0, 0].astype(BF16), w1)
        pw = _dot(pos_ref[...].astype(BF16), w1)
        pos = pw[0:1, :hid] + pw[1:2, hid:]
        nxt = pltpu.roll(a[:, hid:], a.shape[0] - 1, 0)
        h = jax.nn.gelu(a[:, :hid] + nxt + pos)
        return _dot(h.astype(BF16), w2_ref[...])

    kc = compress(k_ref, w1k_ref, w2k_ref, pk_ref)
    ko_ref[0, 0] = _rms(kc, kg_ref[...]).astype(BF16)
    vc = compress(v_ref, w1v_ref, w2v_ref, pv_ref).astype(BF16)
    vo_ref[0, 0] = _transpose_bf16(vc, HEAD_DIM).astype(BF16)


def _compress(k_r, v_r, w1k, w1v, w2k, w2v, pk, pv, kg):
    batch, groups, chunks, width = k_r.shape
    hid = NSA_CMP_HIDDEN
    tok = pl.BlockSpec((1, 1, chunks, width), lambda b, g: (b, g, 0, 0))
    full = lambda shape: pl.BlockSpec(shape, lambda b, g: (0,) * len(shape))
    return pl.pallas_call(
        _compress_kernel,
        grid=(batch, groups),
        in_specs=[tok, tok, full((width, 2 * hid)), full((width, 2 * hid)),
                  full((hid, HEAD_DIM)), full((hid, HEAD_DIM)),
                  full((8, width)), full((8, width)), full((1, HEAD_DIM))],
        out_specs=[pl.BlockSpec((1, 1, chunks, HEAD_DIM), lambda b, g: (b, g, 0, 0)),
                   pl.BlockSpec((1, 1, HEAD_DIM, chunks), lambda b, g: (b, g, 0, 0))],
        out_shape=[jax.ShapeDtypeStruct((batch, groups, chunks, HEAD_DIM), BF16),
                   jax.ShapeDtypeStruct((batch, groups, HEAD_DIM, chunks), BF16)],
        compiler_params=_params(("parallel", "parallel")),
        name="nsa_compress",
    )(k_r, v_r, w1k, w1v, w2k, w2v, pk, pv, kg)


def _nsa_kernel(q_ref, kc_ref, vc_ref, ksl_ref, vsl_ref, kwn_ref, vwn_ref, gt_ref, qg_ref, kg_ref,
                wc_ref, t0_ref, t1_ref, o_ref, kslaug_s, vslt_s, kwn_s, vwnt_s, sc_s, *, seq):
    qi = pl.program_id(1)
    tq = NSA_TQ
    hpg = NSA_HPG
    sb = NSA_SEL_BLOCK
    nsel = seq // sb
    ncmp = seq // NSA_CMP_STRIDE
    nwin = NSA_WINDOW // tq
    t_start = qi * tq

    @pl.when(qi == 0)
    def _prepare_keys():
        r = lax.broadcasted_iota(jnp.int32, (seq, LANES), 0)
        c = lax.broadcasted_iota(jnp.int32, (seq, LANES), 1)
        onehot = (c - HEAD_DIM == r // sb).astype(F32)
        place = _eye(HEAD_DIM, LANES)
        sc_s[:, 0:CMP_PAD, :] = jnp.zeros((NSA_KV_GROUPS, CMP_PAD, hpg * tq), F32)
        for g in range(NSA_KV_GROUPS):
            sl = slice(g * HEAD_DIM, (g + 1) * HEAD_DIM)
            ks = _rms(ksl_ref[:, sl], kg_ref[1:2, :])
            kslaug_s[g] = (_dot(ks.astype(BF16), place) + onehot).astype(BF16)
            kwn_s[g] = _rms(kwn_ref[:, sl], kg_ref[2:3, :]).astype(BF16)
            for j in range(seq // tq):
                rows = slice(j * tq, (j + 1) * tq)
                vslt_s[g, j] = _transpose_bf16(vsl_ref[rows, sl].astype(BF16), HEAD_DIM).astype(BF16)
                vwnt_s[g, j] = _transpose_bf16(vwn_ref[rows, sl].astype(BF16), HEAD_DIM).astype(BF16)

    lanes = lambda parts: jnp.concatenate(parts, axis=1)
    key = lax.broadcasted_iota(jnp.int32, (tq, tq), 0)
    qry = lax.broadcasted_iota(jnp.int32, (tq, tq), 1)
    causal = lanes([key <= qry] * hpg)
    upper = lanes([key > qry] * hpg)
    cend = lax.broadcasted_iota(jnp.int32, (ncmp, tq), 0) * NSA_CMP_STRIDE + (NSA_CMP_LEN - 1)
    tpos = lax.broadcasted_iota(jnp.int32, (ncmp, tq), 1) + t_start
    cvis = lanes([cend <= tpos] * hpg)
    oj = lax.broadcasted_iota(jnp.int32, (nsel, ncmp), 0) * sb
    on = lax.broadcasted_iota(jnp.int32, (nsel, ncmp), 1) * NSA_CMP_STRIDE
    overlap = ((on < oj + sb) & (on + NSA_CMP_LEN > oj) & (on < seq - NSA_CMP_STRIDE)).astype(BF16)
    jidx = lax.broadcasted_iota(jnp.int32, (nsel, tq), 0)
    own = (lax.broadcasted_iota(jnp.int32, (nsel, tq), 1) + t_start) // sb
    gates = jax.nn.sigmoid(gt_ref[...]).T
    tile_rows = lambda j: pl.ds(pl.multiple_of(j * tq, tq), tq)
    back = lambda k: jnp.maximum(qi - k, 0)

    groups = range(NSA_KV_GROUPS)
    heads = [[g * hpg + r for r in range(hpg)] for g in groups]
    t0 = [lanes([t0_ref[h] for h in heads[g]]) for g in groups]
    t1 = [lanes([t1_ref[h] for h in heads[g]]) for g in groups]

    qt, qplain, win_scores = [], [], []
    for g in groups:
        qt.append(lanes([_transpose_bf16((_rms(q_ref[:, h * HEAD_DIM:(h + 1) * HEAD_DIM], qg_ref[...])
                                          * SCALE).astype(BF16), LANES) for h in heads[g]]))
        qplain.append(qt[g][0:HEAD_DIM].astype(BF16))
        sc_s[g, CMP_PAD:, :] = _dot(kc_ref[0, g], qplain[g])
        win_scores.append([_dot(kwn_s[g, tile_rows(back(k)), :], qplain[g]) for k in range(nwin + 1)])

    qaugs, o_cmps, slc_near = [], [], []
    for g in groups:
        win = pl.ds(pl.multiple_of(qi * (tq // NSA_CMP_STRIDE), 8), CMP_WIN)
        sc_s[g, win, :] += lanes([wc_ref[h] for h in heads[g]])
        s = jnp.where(cvis, sc_s[g, CMP_PAD:, :], NEG)
        m = jnp.max(s, axis=0, keepdims=True)
        e = jnp.where(cvis, jnp.exp(s - m), 0.0)
        den = jnp.sum(e, axis=0, keepdims=True)
        p = e * (1.0 / jnp.where(den > 0, den, 1.0))
        o_cmps.append(_dot(vc_ref[0, g], p.astype(BF16)))
        psum = p[:, 0:tq]
        for r in range(1, hpg):
            psum = psum + p[:, r * tq:(r + 1) * tq]
        ph, plo = _split_bf16(psum)
        imp = _dot(overlap, ph) + _dot(overlap, plo)
        imp = jnp.where((jidx == 0) | (jidx == own) | (jidx == own - 1), -NEG, imp)
        imp = jnp.where(jidx > own, NEG, imp)
        sel = _rank_select(imp, jidx, NSA_SEL_TOPN) & (jidx <= own)
        selb = jnp.concatenate([jnp.zeros((HEAD_DIM, tq), F32), jnp.where(sel, 0.0, NEG),
                                jnp.zeros((LANES - HEAD_DIM - nsel, tq), F32)], axis=0)
        qaugs.append((qt[g] + lanes([selb] * hpg)).astype(BF16))
        slc_near.append((_dot(kslaug_s[g, tile_rows(qi), :], qaugs[g]),
                         _dot(kslaug_s[g, tile_rows(back(1)), :], qaugs[g])))

    o_wins = []
    for g in groups:
        tiles = []
        for k, s in enumerate(win_scores[g]):
            if k == 0:
                s = jnp.where(causal, s + t0[g], NEG)
            elif k == 1:
                s = jnp.where(qi >= 1, s + t1[g], NEG)
            elif k < nwin:
                s = jnp.where(qi >= k, s, NEG)
            else:
                s = jnp.where(upper & (qi >= k), s, NEG)
            tiles.append((s, vwnt_s[g, back(k)]))
        m, l, acc = _softmax_joint(tiles)
        o_wins.append(acc * (1.0 / l))

    slc_state = tuple(_softmax_joint([
        (jnp.where(causal, slc_near[g][0] + t0[g], NEG), vslt_s[g, qi]),
        (jnp.where(qi >= 1, slc_near[g][1] + t1[g], NEG), vslt_s[g, back(1)])]) for g in groups)

    def slc_far(j, carry):
        scores = [_dot(kslaug_s[g, tile_rows(j), :], qaugs[g]) for g in groups]
        return tuple(_softmax_update(carry[g], scores[g], vslt_s[g, j]) for g in groups)

    slc_state = lax.fori_loop(0, jnp.maximum(qi - 1, 0), slc_far, slc_state)

    outs = []
    for g in range(NSA_KV_GROUPS):
        m, l, acc = slc_state[g]
        o_slc = acc * (1.0 / l)
        for r in range(hpg):
            h = g * hpg + r
            cols = slice(r * tq, (r + 1) * tq)
            outs.append(gates[h:h + 1, :] * o_cmps[g][:, cols]
                        + gates[NSA_HEADS + h:NSA_HEADS + h + 1, :] * o_slc[:, cols]
                        + gates[2 * NSA_HEADS + h:2 * NSA_HEADS + h + 1, :] * o_wins[g][:, cols])
    o_ref[...] = jnp.concatenate(outs, axis=0).T


def _nsa(proj, kcn, vct, q_gain, k_gain, wc, t0, t1, batch, seq):
    tq = NSA_TQ
    nq = seq // tq
    ncmp = seq // NSA_CMP_STRIDE
    width = NSA_HEADS * HEAD_DIM
    groups = NSA_KV_GROUPS
    kv = lambda col: pl.BlockSpec((seq, LANES), lambda b, i: (b, col // LANES))
    const = lambda shape: pl.BlockSpec(shape, lambda b, i: (0,) * len(shape))
    return pl.pallas_call(
        functools.partial(_nsa_kernel, seq=seq),
        grid=(batch, nq),
        in_specs=[pl.BlockSpec((tq, width), lambda b, i: (b * nq + i, C_NSQ // width)),
                  pl.BlockSpec((1, groups, ncmp, HEAD_DIM), lambda b, i: (b, 0, 0, 0)),
                  pl.BlockSpec((1, groups, HEAD_DIM, ncmp), lambda b, i: (b, 0, 0, 0)),
                  kv(C_KSL), kv(C_VSL), kv(C_KWN), kv(C_VWN),
                  pl.BlockSpec((tq, LANES), lambda b, i: (b * nq + i, C_NG // LANES)),
                  const((1, HEAD_DIM)), const((3, HEAD_DIM)),
                  const((NSA_HEADS, CMP_WIN, tq)), const((NSA_HEADS, tq, tq)),
                  const((NSA_HEADS, tq, tq))],
        out_specs=pl.BlockSpec((tq, width), lambda b, i: (b * nq + i, 0)),
        out_shape=jax.ShapeDtypeStruct((batch * seq, width), F32),
        scratch_shapes=[pltpu.VMEM((groups, seq, LANES), BF16),
                        pltpu.VMEM((groups, nq, HEAD_DIM, tq), BF16),
                        pltpu.VMEM((groups, seq, HEAD_DIM), BF16),
                        pltpu.VMEM((groups, nq, HEAD_DIM, tq), BF16),
                        pltpu.VMEM((groups, CMP_PAD + ncmp, NSA_HPG * tq), F32)],
        compiler_params=_params(("parallel", "arbitrary")),
        name="nsa",
    )(proj, kcn, vct, proj, proj, proj, proj, proj, q_gain.reshape(1, HEAD_DIM), k_gain, wc, t0, t1)


def _merge_kernel(x_ref, ya_ref, yb_ref, ga_ref, gb_ref, wa_ref, wb_ref, wo_ref, o_ref):
    a = _dot(ya_ref[...].astype(BF16), wa_ref[...])
    b = _dot(yb_ref[...].astype(BF16), wb_ref[...])
    z = jax.nn.sigmoid(ga_ref[...]) * a + jax.nn.sigmoid(gb_ref[...]) * b
    o_ref[...] = x_ref[...] + _dot(z.astype(BF16), wo_ref[...])


def _merge(xf, ya, yb, proj, wa, wb, wo):
    n, d = xf.shape
    tm = 256
    row = lambda w, col=0: pl.BlockSpec((tm, w), lambda i: (i, col))
    full = lambda a: pl.BlockSpec(a.shape, lambda i: (0, 0))
    return pl.pallas_call(
        _merge_kernel,
        grid=(n // tm,),
        in_specs=[row(d), row(ya.shape[1]), row(yb.shape[1]), row(d, C_GA // d), row(d, C_GB // d),
                  full(wa), full(wb), full(wo)],
        out_specs=row(d),
        out_shape=jax.ShapeDtypeStruct((n, d), F32),
        compiler_params=_params(("parallel",)),
        name="merge",
    )(xf, ya, yb, proj, proj, wa, wb, wo)


FFN_HALO = 16


def _ffn_kernel(x_ref, xh_ref, g_ref, wa_ref, wb_ref, cwa_ref, cwb_ref, cba_ref, cbb_ref, wd_ref,
                o_ref, hn_s, acc_s, *, seq, tm):
    i = pl.program_id(0)
    j = pl.program_id(1)

    @pl.when(j == 0)
    def _normalise():
        halo = _rms(xh_ref[...], g_ref[...])
        at_start = (i * tm) % seq == 0
        hn_s[0:FFN_HALO, :] = jnp.where(at_start, 0.0, halo).astype(BF16)
        hn_s[FFN_HALO:, :] = _rms(x_ref[...], g_ref[...]).astype(BF16)
        acc_s[...] = jnp.zeros_like(acc_s)

    hn = hn_s[...]

    def conv(w_ref, cw_ref, cb_ref):
        u = _dot(hn, w_ref[...])
        u1 = pltpu.roll(u, 1, 0)[FFN_HALO:]
        u2 = pltpu.roll(u, 2, 0)[FFN_HALO:]
        cw = cw_ref[...]
        return cw[0:1] * u2 + cw[1:2] * u1 + cw[2:3] * u[FFN_HALO:] + cb_ref[...]

    act = jax.nn.gelu(conv(wa_ref, cwa_ref, cba_ref)) * conv(wb_ref, cwb_ref, cbb_ref)
    acc_s[...] += _dot(act.astype(BF16), wd_ref[...])

    @pl.when(j == pl.num_programs(1) - 1)
    def _finish():
        o_ref[...] = x_ref[...] + acc_s[...]


def _ffn(xf, gain, w_up, conv_w, conv_b, w_down, seq):
    n, d = xf.shape
    d_ff = w_down.shape[0]
    tm, tf = 1024, 256
    nf = d_ff // tf
    hb = tm // FFN_HALO
    return pl.pallas_call(
        functools.partial(_ffn_kernel, seq=seq, tm=tm),
        grid=(n // tm, nf),
        in_specs=[pl.BlockSpec((tm, d), lambda i, j: (i, 0)),
                  pl.BlockSpec((FFN_HALO, d), lambda i, j: (jnp.maximum(i * hb - 1, 0), 0)),
                  pl.BlockSpec((1, d), lambda i, j: (0, 0)),
                  pl.BlockSpec((d, tf), lambda i, j: (0, j)),
                  pl.BlockSpec((d, tf), lambda i, j: (0, nf + j)),
                  pl.BlockSpec((CONV_W, tf), lambda i, j: (0, j)),
                  pl.BlockSpec((CONV_W, tf), lambda i, j: (0, nf + j)),
                  pl.BlockSpec((1, tf), lambda i, j: (0, j)),
                  pl.BlockSpec((1, tf), lambda i, j: (0, nf + j)),
                  pl.BlockSpec((tf, d), lambda i, j: (j, 0))],
        out_specs=pl.BlockSpec((tm, d), lambda i, j: (i, 0)),
        out_shape=jax.ShapeDtypeStruct((n, d), F32),
        scratch_shapes=[pltpu.VMEM((FFN_HALO + tm, d), BF16), pltpu.VMEM((tm, d), F32)],
        compiler_params=_params(("parallel", "arbitrary")),
        name="conv_ffn",
    )(xf, xf, gain.reshape(1, d), w_up, w_up, conv_w, conv_w, conv_b.reshape(1, -1),
      conv_b.reshape(1, -1), w_down)


def _ple_kernel(x_ref, p_ref, wg_ref, wp_ref, o_ref):
    x = x_ref[...]
    gate = jax.nn.sigmoid(_dot(x.astype(BF16), wg_ref[...]))
    o_ref[...] = x + gate * _dot(p_ref[...].astype(BF16), wp_ref[...])


def _ple(xf, pf, wg, wp):
    n, d = xf.shape
    tm = 512
    return pl.pallas_call(
        _ple_kernel,
        grid=(n // tm,),
        in_specs=[pl.BlockSpec((tm, d), lambda i: (i, 0)),
                  pl.BlockSpec((tm, pf.shape[1]), lambda i: (i, 0)),
                  pl.BlockSpec(wg.shape, lambda i: (0, 0)),
                  pl.BlockSpec(wp.shape, lambda i: (0, 0))],
        out_specs=pl.BlockSpec((tm, d), lambda i: (i, 0)),
        out_shape=jax.ShapeDtypeStruct((n, d), F32),
        compiler_params=_params(("parallel",)),
        name="ple",
    )(xf, pf, wg, wp)


def _reorder_in_proj(w):
    attn = 3 * MOBA_HEADS * HEAD_DIM + NSA_HEADS * HEAD_DIM + 6 * NSA_KV_GROUPS * HEAD_DIM
    ng = 3 * NSA_HEADS
    pad = jnp.zeros((w.shape[0], PROJ_COLS - w.shape[1]), w.dtype)
    return jnp.concatenate([w[:, attn + ng:], w[:, :attn], w[:, attn:attn + ng], pad], axis=1)


def _cmp_weights(w1, pos):
    half = NSA_CMP_STRIDE * HEAD_DIM
    w1cat = jnp.concatenate([w1[:half], w1[half:]], axis=1).astype(BF16)
    posr = jnp.zeros((8, half), F32).at[0:2].set(pos.reshape(2, half))
    return w1cat, posr


def kernel(x, p, rel_bias, attn_norm, w_in, moba_q_gain, moba_k_gain, nsa_q_gain, nsa_k_gain,
           cmp_pos_k, cmp_w1_k, cmp_w2_k, cmp_pos_v, cmp_w1_v, cmp_w2_v,
           w_br_moba, w_br_nsa, w_o, ffn_norm, w_up, conv_w, conv_b, w_down, w_ple_gate, w_ple):
    batch, seq, d = x.shape
    n = batch * seq
    depth = w_in.shape[0]
    groups = NSA_KV_GROUPS
    chunks = seq // NSA_CMP_STRIDE
    t0m, t1m, t0n, t1n, wc = _bias_tables(rel_bias)
    xf = x.reshape(n, d)
    for i in range(depth):
        proj = _inproj(xf, attn_norm[i], _reorder_in_proj(w_in[i]).astype(BF16))
        ya = _moba(proj, moba_q_gain[i], moba_k_gain[i], t0m, t1m, batch, seq)

        def chunked(col):
            t = proj[:, col:col + groups * HEAD_DIM].reshape(batch, seq, groups, HEAD_DIM)
            return t.transpose(0, 2, 1, 3).reshape(batch, groups, chunks, NSA_CMP_STRIDE * HEAD_DIM)

        w1k, pk = _cmp_weights(cmp_w1_k[i], cmp_pos_k[i])
        w1v, pv = _cmp_weights(cmp_w1_v[i], cmp_pos_v[i])
        kcn, vct = _compress(chunked(C_KC), chunked(C_VC), w1k, w1v, cmp_w2_k[i].astype(BF16),
                             cmp_w2_v[i].astype(BF16), pk, pv, nsa_k_gain[i, 0:1])
        yb = _nsa(proj, kcn, vct, nsa_q_gain[i], nsa_k_gain[i], wc, t0n, t1n, batch, seq)
        xf = _merge(xf, ya, yb, proj, w_br_moba[i].astype(BF16), w_br_nsa[i].astype(BF16),
                    w_o[i].astype(BF16))
        xf = _ffn(xf, ffn_norm[i], w_up[i].astype(BF16), conv_w[i], conv_b[i],
                  w_down[i].astype(BF16), seq)
        xf = _ple(xf, p[i].reshape(n, -1), w_ple_gate[i].astype(BF16), w_ple[i].astype(BF16))
    return xf.reshape(batch, seq, d)
```

```python
import functools
import math

import numpy as np
import jax
import jax.numpy as jnp
from jax import lax
from jax.experimental import pallas as pl
from jax.experimental.pallas import tpu as pltpu

F32 = jnp.float32
BF16 = jnp.bfloat16

HEAD_DIM = 64
MOBA_HEADS = 8
MOBA_BLOCK = 256
MOBA_TOPK = 3
NSA_HEADS = 8
NSA_KV_GROUPS = 2
NSA_HPG = NSA_HEADS // NSA_KV_GROUPS
NSA_CMP_LEN = 32
NSA_CMP_STRIDE = 16
NSA_CMP_HIDDEN = 2 * HEAD_DIM
NSA_SEL_BLOCK = 64
NSA_SEL_TOPN = 16
NSA_WINDOW = 512
REL_BUCKETS = 32
REL_MAX_DIST = 128
CONV_W = 3
RMS_EPS = 1e-6
SCALE = HEAD_DIM ** -0.5
NEG = -1e30

LANES = 128
VMEM_LIMIT = 56 * 1024 * 1024

C_GA, C_GB, C_MQ, C_MK, C_MV, C_NSQ = 0, 1024, 2048, 2560, 3072, 3584
C_KC, C_VC, C_KSL, C_VSL, C_KWN, C_VWN, C_NG = 4096, 4224, 4352, 4480, 4608, 4736, 4864
PROJ_COLS = 4992

NSA_TQ = 128
CMP_WIN = 16
CMP_PAD = 8


def _dot(a, b):
    return jnp.dot(a, b, preferred_element_type=F32)


def _dot_nt(a, b):
    return lax.dot_general(a, b, (((1,), (1,)), ((), ())), preferred_element_type=F32)


def _rms(x, gain):
    return x * lax.rsqrt(jnp.mean(x * x, axis=-1, keepdims=True) + RMS_EPS) * gain


def _split_bf16(x):
    hi = x.astype(BF16)
    return hi, (x - hi.astype(F32)).astype(BF16)


def _eye(rows, cols):
    r = lax.broadcasted_iota(jnp.int32, (rows, cols), 0)
    c = lax.broadcasted_iota(jnp.int32, (rows, cols), 1)
    return (r == c).astype(BF16)


def _transpose_bf16(x, rows):
    return _dot_nt(_eye(rows, x.shape[1]), x)


def _softmax_init(s, vt):
    m = jnp.max(s, axis=0, keepdims=True)
    p = jnp.exp(s - m)
    return m, jnp.sum(p, axis=0, keepdims=True), _dot(vt, p.astype(BF16))


def _softmax_joint(tiles):
    m = functools.reduce(jnp.maximum, [jnp.max(s, axis=0, keepdims=True) for s, _ in tiles])
    ps = [jnp.exp(s - m) for s, _ in tiles]
    l = functools.reduce(jnp.add, [jnp.sum(p, axis=0, keepdims=True) for p in ps])
    acc = functools.reduce(jnp.add, [_dot(vt, p.astype(BF16)) for p, (_, vt) in zip(ps, tiles)])
    return m, l, acc


def _softmax_update(carry, s, vts):
    m, l, acc = carry
    m_new = jnp.maximum(m, jnp.max(s, axis=0, keepdims=True))
    alpha = jnp.exp(m - m_new)
    p = jnp.exp(s - m_new)
    rows = s.shape[0] // len(vts)
    pv = functools.reduce(jnp.add, [_dot(vt, p[i * rows:(i + 1) * rows].astype(BF16))
                                    for i, vt in enumerate(vts)])
    return m_new, alpha * l + jnp.sum(p, axis=0, keepdims=True), alpha * acc + pv


def _rank_select(score, idx, count):
    beaten = jnp.zeros(score.shape, jnp.int32)
    for i in range(score.shape[0]):
        si = score[i:i + 1, :]
        beaten += ((si > score) | ((si == score) & (i < idx))).astype(jnp.int32)
    return beaten < count


def _params(sem):
    return pltpu.CompilerParams(dimension_semantics=sem, vmem_limit_bytes=VMEM_LIMIT)


def _rel_bucket_np(dist):
    n = np.maximum(dist, 0)
    max_exact = REL_BUCKETS // 2
    nf = np.maximum(n, 1).astype(np.float32)
    large = max_exact + (np.log(nf / np.float32(max_exact)) / np.float32(math.log(REL_MAX_DIST / max_exact))
                         * np.float32(REL_BUCKETS - max_exact)).astype(np.int32)
    return np.where(n < max_exact, n, np.minimum(large, REL_BUCKETS - 1))


def _bucket_starts():
    buckets = _rel_bucket_np(np.arange(4 * REL_MAX_DIST))
    return [int(np.argmax(buckets >= k)) for k in range(REL_BUCKETS)]


BUCKET_START = _bucket_starts()
BIAS_REACH = BUCKET_START[-1]
assert BIAS_REACH <= MOBA_BLOCK and BIAS_REACH <= NSA_TQ - NSA_CMP_LEN + 1 + NSA_CMP_STRIDE


def _tables_kernel(tab_ref, t0m_ref, t1m_ref, t0n_ref, t1n_ref, wc_ref):
    h = pl.program_id(0)

    def bias(dist, head):
        last = tab_ref[head, REL_BUCKETS - 1]
        val = jnp.zeros(dist.shape, F32)
        for k in range(REL_BUCKETS - 2, -1, -1):
            val = jnp.where(dist < BUCKET_START[k + 1], tab_ref[head, k] - last, val)
        return val

    def toeplitz(size, offset, head):
        key = lax.broadcasted_iota(jnp.int32, (size, size), 0)
        qry = lax.broadcasted_iota(jnp.int32, (size, size), 1)
        return bias(offset + qry - key, head)

    t0m_ref[0] = toeplitz(MOBA_BLOCK, 0, h)
    t1m_ref[0] = toeplitz(MOBA_BLOCK, MOBA_BLOCK, h)
    t0n_ref[0] = toeplitz(NSA_TQ, 0, MOBA_HEADS + h)
    t1n_ref[0] = toeplitz(NSA_TQ, NSA_TQ, MOBA_HEADS + h)
    a = lax.broadcasted_iota(jnp.int32, (CMP_WIN, NSA_TQ), 0)
    i = lax.broadcasted_iota(jnp.int32, (CMP_WIN, NSA_TQ), 1)
    wc_ref[0] = bias(i + (NSA_TQ - NSA_CMP_LEN + 1) - NSA_CMP_STRIDE * a, MOBA_HEADS + h)


def _bias_tables(rel_bias):
    blk, tq = MOBA_BLOCK, NSA_TQ
    shapes = [(blk, blk), (blk, blk), (tq, tq), (tq, tq), (CMP_WIN, tq)]
    return pl.pallas_call(
        _tables_kernel,
        grid=(MOBA_HEADS,),
        in_specs=[pl.BlockSpec(memory_space=pltpu.SMEM)],
        out_specs=[pl.BlockSpec((1,) + s, lambda h: (h, 0, 0)) for s in shapes],
        out_shape=[jax.ShapeDtypeStruct((MOBA_HEADS,) + s, F32) for s in shapes],
        compiler_params=_params(("arbitrary",)),
        name="bias_tables",
    )(rel_bias)


def _inproj_kernel(x_ref, g_ref, w_ref, o_ref):
    h = _rms(x_ref[...], g_ref[...]).astype(BF16)
    o_ref[...] = _dot(h, w_ref[...])


def _inproj(xf, gain, w):
    n, d = xf.shape
    tm = 256
    return pl.pallas_call(
        _inproj_kernel,
        grid=(n // tm,),
        in_specs=[pl.BlockSpec((tm, d), lambda i: (i, 0)),
                  pl.BlockSpec((1, d), lambda i: (0, 0)),
                  pl.BlockSpec((d, PROJ_COLS), lambda i: (0, 0))],
        out_specs=pl.BlockSpec((tm, PROJ_COLS), lambda i: (i, 0)),
        out_shape=jax.ShapeDtypeStruct((n, PROJ_COLS), F32),
        compiler_params=_params(("parallel",)),
        name="inproj",
    )(xf, gain.reshape(1, d), w)


MOBA_HPS = 8


def _moba_kernel(q_ref, k_ref, v_ref, qg_ref, kg_ref, t0_ref, t1_ref, o_ref,
                 kaug_s, vt_s, km_s, *, seq):
    n = pl.program_id(2)
    blk = MOBA_BLOCK
    nb = seq // blk
    head_cols = lambda hh: slice(hh * HEAD_DIM, (hh + 1) * HEAD_DIM)

    @pl.when(n == 0)
    def _prepare_keys():
        r = lax.broadcasted_iota(jnp.int32, (seq, LANES), 0)
        c = lax.broadcasted_iota(jnp.int32, (seq, LANES), 1)
        onehot = (c - HEAD_DIM == r // blk).astype(F32)
        place = _eye(HEAD_DIM, LANES)
        for hh in range(MOBA_HPS):
            kn = _rms(k_ref[:, head_cols(hh)], kg_ref[...])
            km = kn.reshape(nb, blk, HEAD_DIM).sum(axis=1) * (1.0 / blk)
            km_s[hh] = jnp.concatenate([km, jnp.zeros((16 - nb, HEAD_DIM), F32)], axis=0)
            kaug_s[hh] = (_dot(kn.astype(BF16), place) + onehot).astype(BF16)
            for j in range(nb):
                vj = v_ref[j * blk:(j + 1) * blk, head_cols(hh)].astype(BF16)
                vt_s[hh, j] = _transpose_bf16(vj, HEAD_DIM).astype(BF16)

    jidx = lax.broadcasted_iota(jnp.int32, (16, blk), 0)
    past = jidx < n
    key = lax.broadcasted_iota(jnp.int32, (blk, blk), 0)
    qry = lax.broadcasted_iota(jnp.int32, (blk, blk), 1)
    prev = jnp.maximum(n - 1, 0)
    heads = range(MOBA_HPS)
    qn = [_rms(q_ref[:, head_cols(hh)], qg_ref[...]) for hh in heads]
    gates, qts = [], []
    for hh in heads:
        qh, ql = _split_bf16(qn[hh])
        kmh, kml = _split_bf16(km_s[hh])
        gates.append(_dot_nt(kmh, qh) + _dot_nt(kml, qh) + _dot_nt(kmh, ql))
        qts.append(_transpose_bf16((qn[hh] * SCALE).astype(BF16), LANES))
    qaugs, state = [], []
    for hh in heads:
        gate = jnp.where(past, gates[hh], NEG)
        sel = (past & _rank_select(gate, jidx, MOBA_TOPK)) | (jidx == n)
        selb = jnp.where(sel, 0.0, NEG)
        qaugs.append((qts[hh] + jnp.concatenate([jnp.zeros((HEAD_DIM, blk), F32), selb,
                                                 jnp.zeros((LANES - HEAD_DIM - 16, blk), F32)], axis=0)
                      ).astype(BF16))

    def scores(hh, j):
        start = pl.multiple_of(j * blk, blk)
        return _dot(kaug_s[hh, pl.ds(start, blk), :], qaugs[hh]), vt_s[hh, j]

    near = [(scores(hh, n), scores(hh, prev)) for hh in range(MOBA_HPS)]
    for hh, ((s0, vt0), (s1, vt1)) in enumerate(near):
        s0 = jnp.where(key <= qry, s0 + t0_ref[hh], NEG)
        s1 = jnp.where(n >= 1, s1 + t1_ref[hh], NEG)
        state.append(_softmax_joint([(s0, vt0), (s1, vt1)]))

    def far_body(j, carry):
        tiles = [scores(hh, j) for hh in range(MOBA_HPS)]
        return tuple(_softmax_update(carry[hh], tiles[hh][0], [tiles[hh][1]]) for hh in range(MOBA_HPS))

    state = lax.fori_loop(0, jnp.maximum(n - 1, 0), far_body, tuple(state))
    o_ref[...] = jnp.concatenate([acc * (1.0 / l) for _, l, acc in state], axis=0).T


def _moba(proj, q_gain, k_gain, t0, t1, batch, seq):
    blk = MOBA_BLOCK
    nb = seq // blk
    width = MOBA_HPS * HEAD_DIM
    qc, kc, vc = C_MQ // width, C_MK // width, C_MV // width
    return pl.pallas_call(
        functools.partial(_moba_kernel, seq=seq),
        grid=(batch, MOBA_HEADS // MOBA_HPS, nb),
        in_specs=[pl.BlockSpec((blk, width), lambda b, h, n: (b * nb + n, qc + h)),
                  pl.BlockSpec((seq, width), lambda b, h, n: (b, kc + h)),
                  pl.BlockSpec((seq, width), lambda b, h, n: (b, vc + h)),
                  pl.BlockSpec((1, HEAD_DIM), lambda b, h, n: (0, 0)),
                  pl.BlockSpec((1, HEAD_DIM), lambda b, h, n: (0, 0)),
                  pl.BlockSpec((MOBA_HPS, blk, blk), lambda b, h, n: (h, 0, 0)),
                  pl.BlockSpec((MOBA_HPS, blk, blk), lambda b, h, n: (h, 0, 0))],
        out_specs=pl.BlockSpec((blk, width), lambda b, h, n: (b * nb + n, h)),
        out_shape=jax.ShapeDtypeStruct((batch * seq, MOBA_HEADS * HEAD_DIM), F32),
        scratch_shapes=[pltpu.VMEM((MOBA_HPS, seq, LANES), BF16),
                        pltpu.VMEM((MOBA_HPS, nb, HEAD_DIM, blk), BF16),
                        pltpu.VMEM((MOBA_HPS, 16, HEAD_DIM), F32)],
        compiler_params=_params(("parallel", "parallel", "arbitrary")),
        name="moba",
    )(proj, proj, proj, q_gain.reshape(1, HEAD_DIM), k_gain.reshape(1, HEAD_DIM), t0, t1)


def _compress_kernel(k_ref, v_ref, w1k_ref, w1v_ref, w2k_ref, w2v_ref, pk_ref, pv_ref, kg_ref,
                     ko_ref, vo_ref):
    hid = NSA_CMP_HIDDEN

    def compress(t_ref, w1_ref, w2_ref, pos_ref):
        w1 = w1_ref[...]
        a = _dot(t_ref[0, 0].astype(BF16), w1)
        pw = _dot(pos_ref[...].astype(BF16), w1)
        pos = pw[0:1, :hid] + pw[1:2, hid:]
        nxt = pltpu.roll(a[:, hid:], a.shape[0] - 1, 0)
        h = jax.nn.gelu(a[:, :hid] + nxt + pos)
        return _dot(h.astype(BF16), w2_ref[...])

    kc = compress(k_ref, w1k_ref, w2k_ref, pk_ref)
    ko_ref[0, 0] = _rms(kc, kg_ref[...]).astype(BF16)
    vc = compress(v_ref, w1v_ref, w2v_ref, pv_ref).astype(BF16)
    vo_ref[0, 0] = _transpose_bf16(vc, HEAD_DIM).astype(BF16)


def _compress(k_r, v_r, w1k, w1v, w2k, w2v, pk, pv, kg):
    batch, groups, chunks, width = k_r.shape
    hid = NSA_CMP_HIDDEN
    tok = pl.BlockSpec((1, 1, chunks, width), lambda b, g: (b, g, 0, 0))
    full = lambda shape: pl.BlockSpec(shape, lambda b, g: (0,) * len(shape))
    return pl.pallas_call(
        _compress_kernel,
        grid=(batch, groups),
        in_specs=[tok, tok, full((width, 2 * hid)), full((width, 2 * hid)),
                  full((hid, HEAD_DIM)), full((hid, HEAD_DIM)),
                  full((8, width)), full((8, width)), full((1, HEAD_DIM))],
        out_specs=[pl.BlockSpec((1, 1, chunks, HEAD_DIM), lambda b, g: (b, g, 0, 0)),
                   pl.BlockSpec((1, 1, HEAD_DIM, chunks), lambda b, g: (b, g, 0, 0))],
        out_shape=[jax.ShapeDtypeStruct((batch, groups, chunks, HEAD_DIM), BF16),
                   jax.ShapeDtypeStruct((batch, groups, HEAD_DIM, chunks), BF16)],
        compiler_params=_params(("parallel", "parallel")),
        name="nsa_compress",
    )(k_r, v_r, w1k, w1v, w2k, w2v, pk, pv, kg)


def _nsa_kernel(q_ref, kc_ref, vc_ref, ksl_ref, vsl_ref, kwn_ref, vwn_ref, gt_ref, qg_ref, kg_ref,
                wc_ref, t0_ref, t1_ref, o_ref, kslaug_s, vslt_s, kwn_s, vwnt_s, sc_s, *, seq):
    qi = pl.program_id(1)
    tq = NSA_TQ
    hpg = NSA_HPG
    sb = NSA_SEL_BLOCK
    nsel = seq // sb
    ncmp = seq // NSA_CMP_STRIDE
    nwin = NSA_WINDOW // tq
    t_start = qi * tq

    @pl.when(qi == 0)
    def _prepare_keys():
        r = lax.broadcasted_iota(jnp.int32, (seq, LANES), 0)
        c = lax.broadcasted_iota(jnp.int32, (seq, LANES), 1)
        onehot = (c - HEAD_DIM == r // sb).astype(F32)
        place = _eye(HEAD_DIM, LANES)
        sc_s[:, 0:CMP_PAD, :] = jnp.zeros((NSA_KV_GROUPS, CMP_PAD, hpg * tq), F32)
        for g in range(NSA_KV_GROUPS):
            sl = slice(g * HEAD_DIM, (g + 1) * HEAD_DIM)
            ks = _rms(ksl_ref[:, sl], kg_ref[1:2, :])
            kslaug_s[g] = (_dot(ks.astype(BF16), place) + onehot).astype(BF16)
            kwn_s[g] = _rms(kwn_ref[:, sl], kg_ref[2:3, :]).astype(BF16)
            for j in range(seq // tq):
                rows = slice(j * tq, (j + 1) * tq)
                vslt_s[g, j] = _transpose_bf16(vsl_ref[rows, sl].astype(BF16), HEAD_DIM).astype(BF16)
                vwnt_s[g, j] = _transpose_bf16(vwn_ref[rows, sl].astype(BF16), HEAD_DIM).astype(BF16)

    lanes = lambda parts: jnp.concatenate(parts, axis=1)
    key = lax.broadcasted_iota(jnp.int32, (tq, tq), 0)
    qry = lax.broadcasted_iota(jnp.int32, (tq, tq), 1)
    causal = lanes([key <= qry] * hpg)
    upper = lanes([key > qry] * hpg)
    cend = lax.broadcasted_iota(jnp.int32, (ncmp, tq), 0) * NSA_CMP_STRIDE + (NSA_CMP_LEN - 1)
    tpos = lax.broadcasted_iota(jnp.int32, (ncmp, tq), 1) + t_start
    cvis = lanes([cend <= tpos] * hpg)
    oj = lax.broadcasted_iota(jnp.int32, (nsel, ncmp), 0) * sb
    on = lax.broadcasted_iota(jnp.int32, (nsel, ncmp), 1) * NSA_CMP_STRIDE
    overlap = ((on < oj + sb) & (on + NSA_CMP_LEN > oj) & (on < seq - NSA_CMP_STRIDE)).astype(BF16)
    jidx = lax.broadcasted_iota(jnp.int32, (nsel, tq), 0)
    own = (lax.broadcasted_iota(jnp.int32, (nsel, tq), 1) + t_start) // sb
    gates = jax.nn.sigmoid(gt_ref[...]).T
    tile_rows = lambda j: pl.ds(pl.multiple_of(j * tq, tq), tq)
    back = lambda k: jnp.maximum(qi - k, 0)

    groups = range(NSA_KV_GROUPS)
    heads = [[g * hpg + r for r in range(hpg)] for g in groups]
    t0 = [lanes([t0_ref[h] for h in heads[g]]) for g in groups]
    t1 = [lanes([t1_ref[h] for h in heads[g]]) for g in groups]

    qt, qplain, win_scores = [], [], []
    for g in groups:
        qt.append(lanes([_transpose_bf16((_rms(q_ref[:, h * HEAD_DIM:(h + 1) * HEAD_DIM], qg_ref[...])
                                          * SCALE).astype(BF16), LANES) for h in heads[g]]))
        qplain.append(qt[g][0:HEAD_DIM].astype(BF16))
        sc_s[g, CMP_PAD:, :] = _dot(kc_ref[0, g], qplain[g])
        win_scores.append([_dot(kwn_s[g, tile_rows(back(k)), :], qplain[g]) for k in range(nwin + 1)])

    o_cmps, imps = [], []
    for g in groups:
        win = pl.ds(pl.multiple_of(qi * (tq // NSA_CMP_STRIDE), 8), CMP_WIN)
        sc_s[g, win, :] += lanes([wc_ref[h] for h in heads[g]])
        s = jnp.where(cvis, sc_s[g, CMP_PAD:, :], NEG)
        m = jnp.max(s, axis=0, keepdims=True)
        e = jnp.where(cvis, jnp.exp(s - m), 0.0)
        den = jnp.sum(e, axis=0, keepdims=True)
        p = e * (1.0 / jnp.where(den > 0, den, 1.0))
        o_cmps.append(_dot(vc_ref[0, g], p.astype(BF16)))
        psum = p[:, 0:tq]
        for r in range(1, hpg):
            psum = psum + p[:, r * tq:(r + 1) * tq]
        ph, plo = _split_bf16(psum)
        imps.append(_dot(overlap, ph) + _dot(overlap, plo))

    o_wins = []
    for g in groups:
        tiles = []
        for k, s in enumerate(win_scores[g]):
            if k == 0:
                s = jnp.where(causal, s + t0[g], NEG)
            elif k == 1:
                s = jnp.where(qi >= 1, s + t1[g], NEG)
            elif k < nwin:
                s = jnp.where(qi >= k, s, NEG)
            else:
                s = jnp.where(upper & (qi >= k), s, NEG)
            tiles.append((s, vwnt_s[g, back(k)]))
        m, l, acc = _softmax_joint(tiles)
        o_wins.append(acc * (1.0 / l))

    qaugs = []
    for g in groups:
        imp = jnp.where((jidx == 0) | (jidx == own) | (jidx == own - 1), -NEG, imps[g])
        imp = jnp.where(jidx > own, NEG, imp)
        sel = _rank_select(imp, jidx, NSA_SEL_TOPN) & (jidx <= own)
        selb = jnp.concatenate([jnp.zeros((HEAD_DIM, tq), F32), jnp.where(sel, 0.0, NEG),
                                jnp.zeros((LANES - HEAD_DIM - nsel, tq), F32)], axis=0)
        qaugs.append((qt[g] + lanes([selb] * hpg)).astype(BF16))

    odd = (qi >= 2) & (qi % 2 == 0)
    slc_near = [[_dot(kslaug_s[g, tile_rows(back(k)), :], qaugs[g]) for k in range(3)] for g in groups]
    slc_state = tuple(_softmax_joint([
        (jnp.where(causal, slc_near[g][0] + t0[g], NEG), vslt_s[g, qi]),
        (jnp.where(qi >= 1, slc_near[g][1] + t1[g], NEG), vslt_s[g, back(1)]),
        (jnp.where(odd, slc_near[g][2], NEG), vslt_s[g, back(2)])]) for g in groups)

    def slc_far(i, carry):
        slab = pl.ds(pl.multiple_of(i * (2 * tq), 2 * tq), 2 * tq)
        scores = [_dot(kslaug_s[g, slab, :], qaugs[g]) for g in groups]
        return tuple(_softmax_update(carry[g], scores[g], [vslt_s[g, 2 * i], vslt_s[g, 2 * i + 1]])
                     for g in groups)

    slc_state = lax.fori_loop(0, jnp.maximum(qi - 1, 0) // 2, slc_far, slc_state)

    outs = []
    for g in range(NSA_KV_GROUPS):
        m, l, acc = slc_state[g]
        o_slc = acc * (1.0 / l)
        for r in range(hpg):
            h = g * hpg + r
            cols = slice(r * tq, (r + 1) * tq)
            outs.append(gates[h:h + 1, :] * o_cmps[g][:, cols]
                        + gates[NSA_HEADS + h:NSA_HEADS + h + 1, :] * o_slc[:, cols]
                        + gates[2 * NSA_HEADS + h:2 * NSA_HEADS + h + 1, :] * o_wins[g][:, cols])
    o_ref[...] = jnp.concatenate(outs, axis=0).T


def _nsa(proj, kcn, vct, q_gain, k_gain, wc, t0, t1, batch, seq):
    tq = NSA_TQ
    nq = seq // tq
    ncmp = seq // NSA_CMP_STRIDE
    width = NSA_HEADS * HEAD_DIM
    groups = NSA_KV_GROUPS
    kv = lambda col: pl.BlockSpec((seq, LANES), lambda b, i: (b, col // LANES))
    const = lambda shape: pl.BlockSpec(shape, lambda b, i: (0,) * len(shape))
    return pl.pallas_call(
        functools.partial(_nsa_kernel, seq=seq),
        grid=(batch, nq),
        in_specs=[pl.BlockSpec((tq, width), lambda b, i: (b * nq + i, C_NSQ // width)),
                  pl.BlockSpec((1, groups, ncmp, HEAD_DIM), lambda b, i: (b, 0, 0, 0)),
                  pl.BlockSpec((1, groups, HEAD_DIM, ncmp), lambda b, i: (b, 0, 0, 0)),
                  kv(C_KSL), kv(C_VSL), kv(C_KWN), kv(C_VWN),
                  pl.BlockSpec((tq, LANES), lambda b, i: (b * nq + i, C_NG // LANES)),
                  const((1, HEAD_DIM)), const((3, HEAD_DIM)),
                  const((NSA_HEADS, CMP_WIN, tq)), const((NSA_HEADS, tq, tq)),
                  const((NSA_HEADS, tq, tq))],
        out_specs=pl.BlockSpec((tq, width), lambda b, i: (b * nq + i, 0)),
        out_shape=jax.ShapeDtypeStruct((batch * seq, width), F32),
        scratch_shapes=[pltpu.VMEM((groups, seq, LANES), BF16),
                        pltpu.VMEM((groups, nq, HEAD_DIM, tq), BF16),
                        pltpu.VMEM((groups, seq, HEAD_DIM), BF16),
                        pltpu.VMEM((groups, nq, HEAD_DIM, tq), BF16),
                        pltpu.VMEM((groups, CMP_PAD + ncmp, NSA_HPG * tq), F32)],
        compiler_params=_params(("parallel", "arbitrary")),
        name="nsa",
    )(proj, kcn, vct, proj, proj, proj, proj, proj, q_gain.reshape(1, HEAD_DIM), k_gain, wc, t0, t1)


def _merge_kernel(x_ref, ya_ref, yb_ref, ga_ref, gb_ref, wa_ref, wb_ref, wo_ref, o_ref):
    a = _dot(ya_ref[...].astype(BF16), wa_ref[...])
    b = _dot(yb_ref[...].astype(BF16), wb_ref[...])
    z = jax.nn.sigmoid(ga_ref[...]) * a + jax.nn.sigmoid(gb_ref[...]) * b
    o_ref[...] = x_ref[...] + _dot(z.astype(BF16), wo_ref[...])


def _merge(xf, ya, yb, proj, wa, wb, wo):
    n, d = xf.shape
    tm = 256
    row = lambda w, col=0: pl.BlockSpec((tm, w), lambda i: (i, col))
    full = lambda a: pl.BlockSpec(a.shape, lambda i: (0, 0))
    return pl.pallas_call(
        _merge_kernel,
        grid=(n // tm,),
        in_specs=[row(d), row(ya.shape[1]), row(yb.shape[1]), row(d, C_GA // d), row(d, C_GB // d),
                  full(wa), full(wb), full(wo)],
        out_specs=row(d),
        out_shape=jax.ShapeDtypeStruct((n, d), F32),
        compiler_params=_params(("parallel",)),
        name="merge",
    )(xf, ya, yb, proj, proj, wa, wb, wo)


FFN_HALO = 16
FFN_TM = 512
FFN_TF = 256


def _ffn_kernel(x_ref, xh_ref, g_ref, wu_ref, cw_ref, cb_ref, wd_ref, p_ref, wg_ref, wp_ref,
                o_ref, act_s, *, seq, d_ff):
    i = pl.program_id(0)
    tm = x_ref.shape[0]
    x = x_ref[...]
    at_start = (i * tm) % seq == 0
    halo = jnp.where(at_start, 0.0, _rms(xh_ref[...], g_ref[...]))
    hn = jnp.concatenate([halo.astype(BF16), _rms(x, g_ref[...]).astype(BF16)], axis=0)

    def conv(cols):
        u = _dot(hn, wu_ref[:, cols])
        u1 = pltpu.roll(u, 1, 0)[FFN_HALO:]
        u2 = pltpu.roll(u, 2, 0)[FFN_HALO:]
        cw = cw_ref[:, cols]
        return cw[0:1] * u2 + cw[1:2] * u1 + cw[2:3] * u[FFN_HALO:] + cb_ref[:, cols]

    for c in range(d_ff // FFN_TF):
        lo = c * FFN_TF
        act = jax.nn.gelu(conv(slice(lo, lo + FFN_TF))) * conv(slice(d_ff + lo, d_ff + lo + FFN_TF))
        act_s[:, lo:lo + FFN_TF] = act.astype(BF16)

    x = x + _dot(act_s[...], wd_ref[...])
    gate = jax.nn.sigmoid(_dot(x.astype(BF16), wg_ref[...]))
    o_ref[...] = x + gate * _dot(p_ref[...].astype(BF16), wp_ref[...])


def _ffn_ple(xf, gain, w_up, conv_w, conv_b, w_down, pf, wg, wp, seq):
    n, d = xf.shape
    d_ff = w_down.shape[0]
    tm = FFN_TM
    hb = tm // FFN_HALO
    resident = lambda a: pl.BlockSpec(a.shape, lambda i: (0, 0), pipeline_mode=pl.Buffered(1))
    gain = gain.reshape(1, d)
    conv_b = conv_b.reshape(1, -1)
    return pl.pallas_call(
        functools.partial(_ffn_kernel, seq=seq, d_ff=d_ff),
        grid=(n // tm,),
        in_specs=[pl.BlockSpec((tm, d), lambda i: (i, 0)),
                  pl.BlockSpec((FFN_HALO, d), lambda i: (jnp.maximum(i * hb - 1, 0), 0)),
                  resident(gain), resident(w_up), resident(conv_w), resident(conv_b), resident(w_down),
                  pl.BlockSpec((tm, pf.shape[1]), lambda i: (i, 0)),
                  resident(wg), resident(wp)],
        out_specs=pl.BlockSpec((tm, d), lambda i: (i, 0)),
        out_shape=jax.ShapeDtypeStruct((n, d), F32),
        scratch_shapes=[pltpu.VMEM((tm, d_ff), BF16)],
        compiler_params=_params(("parallel",)),
        name="conv_ffn_ple",
    )(xf, xf, gain, w_up, conv_w, conv_b, w_down, pf, wg, wp)


def _reorder_in_proj(w):
    attn = 3 * MOBA_HEADS * HEAD_DIM + NSA_HEADS * HEAD_DIM + 6 * NSA_KV_GROUPS * HEAD_DIM
    ng = 3 * NSA_HEADS
    pad = jnp.zeros((w.shape[0], PROJ_COLS - w.shape[1]), w.dtype)
    return jnp.concatenate([w[:, attn + ng:], w[:, :attn], w[:, attn:attn + ng], pad], axis=1)


def _cmp_weights(w1, pos):
    half = NSA_CMP_STRIDE * HEAD_DIM
    w1cat = jnp.concatenate([w1[:half], w1[half:]], axis=1).astype(BF16)
    posr = jnp.zeros((8, half), F32).at[0:2].set(pos.reshape(2, half))
    return w1cat, posr


def kernel(x, p, rel_bias, attn_norm, w_in, moba_q_gain, moba_k_gain, nsa_q_gain, nsa_k_gain,
           cmp_pos_k, cmp_w1_k, cmp_w2_k, cmp_pos_v, cmp_w1_v, cmp_w2_v,
           w_br_moba, w_br_nsa, w_o, ffn_norm, w_up, conv_w, conv_b, w_down, w_ple_gate, w_ple):
    batch, seq, d = x.shape
    n = batch * seq
    depth = w_in.shape[0]
    groups = NSA_KV_GROUPS
    chunks = seq // NSA_CMP_STRIDE
    t0m, t1m, t0n, t1n, wc = _bias_tables(rel_bias)
    xf = x.reshape(n, d)
    for i in range(depth):
        proj = _inproj(xf, attn_norm[i], _reorder_in_proj(w_in[i]).astype(BF16))
        ya = _moba(proj, moba_q_gain[i], moba_k_gain[i], t0m, t1m, batch, seq)

        def chunked(col):
            t = proj[:, col:col + groups * HEAD_DIM].reshape(batch, seq, groups, HEAD_DIM)
            return t.transpose(0, 2, 1, 3).reshape(batch, groups, chunks, NSA_CMP_STRIDE * HEAD_DIM)

        w1k, pk = _cmp_weights(cmp_w1_k[i], cmp_pos_k[i])
        w1v, pv = _cmp_weights(cmp_w1_v[i], cmp_pos_v[i])
        kcn, vct = _compress(chunked(C_KC), chunked(C_VC), w1k, w1v, cmp_w2_k[i].astype(BF16),
                             cmp_w2_v[i].astype(BF16), pk, pv, nsa_k_gain[i, 0:1])
        yb = _nsa(proj, kcn, vct, nsa_q_gain[i], nsa_k_gain[i], wc, t0n, t1n, batch, seq)
        xf = _merge(xf, ya, yb, proj, w_br_moba[i].astype(BF16), w_br_nsa[i].astype(BF16),
                    w_o[i].astype(BF16))
        xf = _ffn_ple(xf, ffn_norm[i], w_up[i].astype(BF16), conv_w[i], conv_b[i], w_down[i].astype(BF16),
                      p[i].reshape(n, -1), w_ple_gate[i].astype(BF16), w_ple[i].astype(BF16), seq)
    return xf.reshape(batch, seq, d)
```

```python
import functools
import math

import numpy as np
import jax
import jax.numpy as jnp
from jax import lax
from jax.experimental import pallas as pl
from jax.experimental.pallas import tpu as pltpu

F32 = jnp.float32
BF16 = jnp.bfloat16

HEAD_DIM = 64
MOBA_HEADS = 8
MOBA_BLOCK = 256
MOBA_TOPK = 3
NSA_HEADS = 8
NSA_KV_GROUPS = 2
NSA_HPG = NSA_HEADS // NSA_KV_GROUPS
NSA_CMP_LEN = 32
NSA_CMP_STRIDE = 16
NSA_CMP_HIDDEN = 2 * HEAD_DIM
NSA_SEL_BLOCK = 64
NSA_SEL_TOPN = 16
NSA_WINDOW = 512
REL_BUCKETS = 32
REL_MAX_DIST = 128
CONV_W = 3
RMS_EPS = 1e-6
SCALE = HEAD_DIM ** -0.5
NEG = -1e30

LANES = 128
VMEM_LIMIT = 56 * 1024 * 1024

C_GA, C_GB, C_MQ, C_MK, C_MV, C_NSQ = 0, 1024, 2048, 2560, 3072, 3584
C_KC, C_VC, C_KSL, C_VSL, C_KWN, C_VWN, C_NG = 4096, 4224, 4352, 4480, 4608, 4736, 4864
PROJ_COLS = 4992

NSA_TQ = 128
CMP_WIN = 16
CMP_PAD = 8


def _dot(a, b):
    return jnp.dot(a, b, preferred_element_type=F32)


def _dot_nt(a, b):
    return lax.dot_general(a, b, (((1,), (1,)), ((), ())), preferred_element_type=F32)


def _rms(x, gain):
    return x * lax.rsqrt(jnp.mean(x * x, axis=-1, keepdims=True) + RMS_EPS) * gain


def _split_bf16(x):
    hi = x.astype(BF16)
    return hi, (x - hi.astype(F32)).astype(BF16)


def _eye(rows, cols):
    r = lax.broadcasted_iota(jnp.int32, (rows, cols), 0)
    c = lax.broadcasted_iota(jnp.int32, (rows, cols), 1)
    return (r == c).astype(BF16)


def _transpose_bf16(x, rows):
    return _dot_nt(_eye(rows, x.shape[1]), x)


def _softmax_joint(tiles):
    m = functools.reduce(jnp.maximum, [jnp.max(s, axis=0, keepdims=True) for s, _ in tiles])
    ps = [jnp.exp(s - m) for s, _ in tiles]
    l = functools.reduce(jnp.add, [jnp.sum(p, axis=0, keepdims=True) for p in ps])
    acc = functools.reduce(jnp.add, [_dot(vt, p.astype(BF16)) for p, (_, vt) in zip(ps, tiles)])
    return m, l, acc


def _softmax_update(carry, s, vts):
    m, l, acc = carry
    m_new = jnp.maximum(m, jnp.max(s, axis=0, keepdims=True))
    alpha = jnp.exp(m - m_new)
    p = jnp.exp(s - m_new)
    rows = s.shape[0] // len(vts)
    pv = functools.reduce(jnp.add, [_dot(vt, p[i * rows:(i + 1) * rows].astype(BF16))
                                    for i, vt in enumerate(vts)])
    return m_new, alpha * l + jnp.sum(p, axis=0, keepdims=True), alpha * acc + pv


def _rank_select(score, idx, count):
    beaten = jnp.zeros(score.shape, jnp.int32)
    for i in range(score.shape[0]):
        si = score[i:i + 1, :]
        beaten += ((si > score) | ((si == score) & (i < idx))).astype(jnp.int32)
    return beaten < count


def _params(sem):
    return pltpu.CompilerParams(dimension_semantics=sem, vmem_limit_bytes=VMEM_LIMIT)


def _rel_bucket_np(dist):
    n = np.maximum(dist, 0)
    max_exact = REL_BUCKETS // 2
    nf = np.maximum(n, 1).astype(np.float32)
    large = max_exact + (np.log(nf / np.float32(max_exact)) / np.float32(math.log(REL_MAX_DIST / max_exact))
                         * np.float32(REL_BUCKETS - max_exact)).astype(np.int32)
    return np.where(n < max_exact, n, np.minimum(large, REL_BUCKETS - 1))


def _bucket_starts():
    buckets = _rel_bucket_np(np.arange(4 * REL_MAX_DIST))
    return [int(np.argmax(buckets >= k)) for k in range(REL_BUCKETS)]


BUCKET_START = _bucket_starts()
BIAS_REACH = BUCKET_START[-1]
assert BIAS_REACH <= MOBA_BLOCK and BIAS_REACH <= NSA_TQ - NSA_CMP_LEN + 1 + NSA_CMP_STRIDE


def _tables_kernel(tab_ref, t0m_ref, t1m_ref, t0n_ref, t1n_ref, wc_ref):
    h = pl.program_id(0)

    def bias(dist, head):
        last = tab_ref[head, REL_BUCKETS - 1]
        val = jnp.zeros(dist.shape, F32)
        for k in range(REL_BUCKETS - 2, -1, -1):
            val = jnp.where(dist < BUCKET_START[k + 1], tab_ref[head, k] - last, val)
        return val

    def toeplitz(size, offset, head):
        key = lax.broadcasted_iota(jnp.int32, (size, size), 0)
        qry = lax.broadcasted_iota(jnp.int32, (size, size), 1)
        return bias(offset + qry - key, head)

    t0m_ref[0] = toeplitz(MOBA_BLOCK, 0, h)
    t1m_ref[0] = toeplitz(MOBA_BLOCK, MOBA_BLOCK, h)
    t0n_ref[0] = toeplitz(NSA_TQ, 0, MOBA_HEADS + h)
    t1n_ref[0] = toeplitz(NSA_TQ, NSA_TQ, MOBA_HEADS + h)
    a = lax.broadcasted_iota(jnp.int32, (CMP_WIN, NSA_TQ), 0)
    i = lax.broadcasted_iota(jnp.int32, (CMP_WIN, NSA_TQ), 1)
    wc_ref[0] = bias(i + (NSA_TQ - NSA_CMP_LEN + 1) - NSA_CMP_STRIDE * a, MOBA_HEADS + h)


def _bias_tables(rel_bias):
    blk, tq = MOBA_BLOCK, NSA_TQ
    shapes = [(blk, blk), (blk, blk), (tq, tq), (tq, tq), (CMP_WIN, tq)]
    return pl.pallas_call(
        _tables_kernel,
        grid=(MOBA_HEADS,),
        in_specs=[pl.BlockSpec(memory_space=pltpu.SMEM)],
        out_specs=[pl.BlockSpec((1,) + s, lambda h: (h, 0, 0)) for s in shapes],
        out_shape=[jax.ShapeDtypeStruct((MOBA_HEADS,) + s, F32) for s in shapes],
        compiler_params=_params(("arbitrary",)),
        name="bias_tables",
    )(rel_bias)


NORM_SLABS = (list(range(C_MQ, C_MV, LANES)) + list(range(C_NSQ, C_KC, LANES)) + [C_KSL, C_KWN])


def _inproj_kernel(x_ref, g_ref, w_ref, hg_ref, o_ref):
    h = _rms(x_ref[...], g_ref[...]).astype(BF16)
    y = _dot(h, w_ref[...])
    first = lax.broadcasted_iota(jnp.int32, (1, LANES), 1) < HEAD_DIM
    edges = sorted(set([0, PROJ_COLS] + NORM_SLABS + [c + LANES for c in NORM_SLABS]))
    for lo, hi in zip(edges[:-1], edges[1:]):
        t = y[:, lo:hi]
        if lo in NORM_SLABS:
            sq = t * t
            s0 = jnp.sum(jnp.where(first, sq, 0.0), axis=-1, keepdims=True)
            s1 = jnp.sum(jnp.where(first, 0.0, sq), axis=-1, keepdims=True)
            ms = jnp.where(first, s0, s1) * (1.0 / HEAD_DIM)
            t = t * lax.rsqrt(ms + RMS_EPS) * hg_ref[:, lo:hi]
        o_ref[:, lo:hi] = t


def _inproj(xf, gain, w, head_gain):
    n, d = xf.shape
    tm = 256
    resident = lambda a: pl.BlockSpec(a.shape, lambda i: (0, 0), pipeline_mode=pl.Buffered(1))
    gain = gain.reshape(1, d)
    return pl.pallas_call(
        _inproj_kernel,
        grid=(n // tm,),
        in_specs=[pl.BlockSpec((tm, d), lambda i: (i, 0)), resident(gain), resident(w),
                  resident(head_gain)],
        out_specs=pl.BlockSpec((tm, PROJ_COLS), lambda i: (i, 0)),
        out_shape=jax.ShapeDtypeStruct((n, PROJ_COLS), F32),
        compiler_params=_params(("parallel",)),
        name="inproj",
    )(xf, gain, w, head_gain)


MOBA_HPS = 8


def _moba_kernel(q_ref, k_ref, v_ref, t0_ref, t1_ref, o_ref, kaug_s, vt_s, km_s, *, seq):
    n = pl.program_id(2)
    blk = MOBA_BLOCK
    nb = seq // blk
    head_cols = lambda hh: slice(hh * HEAD_DIM, (hh + 1) * HEAD_DIM)

    @pl.when(n == 0)
    def _prepare_keys():
        r = lax.broadcasted_iota(jnp.int32, (seq, LANES), 0)
        c = lax.broadcasted_iota(jnp.int32, (seq, LANES), 1)
        onehot = (c - HEAD_DIM == r // blk).astype(F32)
        place = _eye(HEAD_DIM, LANES)
        for hh in range(MOBA_HPS):
            kn = k_ref[:, head_cols(hh)]
            km = kn.reshape(nb, blk, HEAD_DIM).sum(axis=1) * (1.0 / blk)
            km_s[hh] = jnp.concatenate([km, jnp.zeros((16 - nb, HEAD_DIM), F32)], axis=0)
            kaug_s[hh] = (_dot(kn.astype(BF16), place) + onehot).astype(BF16)
            for j in range(nb):
                vj = v_ref[j * blk:(j + 1) * blk, head_cols(hh)].astype(BF16)
                vt_s[hh, j] = _transpose_bf16(vj, HEAD_DIM).astype(BF16)

    jidx = lax.broadcasted_iota(jnp.int32, (16, blk), 0)
    past = jidx < n
    key = lax.broadcasted_iota(jnp.int32, (blk, blk), 0)
    qry = lax.broadcasted_iota(jnp.int32, (blk, blk), 1)
    prev = jnp.maximum(n - 1, 0)
    heads = range(MOBA_HPS)
    qts = []
    for pair in range(MOBA_HPS // 2):
        qp = _transpose_bf16(q_ref[:, pair * LANES:(pair + 1) * LANES].astype(BF16), LANES)
        qts += [qp[0:HEAD_DIM].astype(BF16), qp[HEAD_DIM:].astype(BF16)]
    gates = []
    for hh in heads:
        kmh, kml = _split_bf16(km_s[hh])
        gates.append(_dot(kmh, qts[hh]) + _dot(kml, qts[hh]))
    qaugs, state = [], []
    for hh in heads:
        gate = jnp.where(past, gates[hh], NEG)
        sel = (past & _rank_select(gate, jidx, MOBA_TOPK)) | (jidx == n)
        selb = jnp.where(sel, 0.0, NEG).astype(BF16)
        qaugs.append(jnp.concatenate([qts[hh], selb, jnp.zeros((LANES - HEAD_DIM - 16, blk), BF16)],
                                     axis=0))

    def scores(hh, j):
        start = pl.multiple_of(j * blk, blk)
        return _dot(kaug_s[hh, pl.ds(start, blk), :], qaugs[hh]), vt_s[hh, j]

    near = [(scores(hh, n), scores(hh, prev)) for hh in range(MOBA_HPS)]
    for hh, ((s0, vt0), (s1, vt1)) in enumerate(near):
        s0 = jnp.where(key <= qry, s0 + t0_ref[hh], NEG)
        s1 = jnp.where(n >= 1, s1 + t1_ref[hh], NEG)
        state.append(_softmax_joint([(s0, vt0), (s1, vt1)]))

    def far_body(j, carry):
        tiles = [scores(hh, j) for hh in range(MOBA_HPS)]
        return tuple(_softmax_update(carry[hh], tiles[hh][0], [tiles[hh][1]]) for hh in range(MOBA_HPS))

    state = lax.fori_loop(0, jnp.maximum(n - 1, 0), far_body, tuple(state))
    o_ref[...] = jnp.concatenate([acc * (1.0 / l) for _, l, acc in state], axis=0).T


def _moba(proj, t0, t1, batch, seq):
    blk = MOBA_BLOCK
    nb = seq // blk
    width = MOBA_HPS * HEAD_DIM
    qc, kc, vc = C_MQ // width, C_MK // width, C_MV // width
    return pl.pallas_call(
        functools.partial(_moba_kernel, seq=seq),
        grid=(batch, MOBA_HEADS // MOBA_HPS, nb),
        in_specs=[pl.BlockSpec((blk, width), lambda b, h, n: (b * nb + n, qc + h)),
                  pl.BlockSpec((seq, width), lambda b, h, n: (b, kc + h)),
                  pl.BlockSpec((seq, width), lambda b, h, n: (b, vc + h)),
                  pl.BlockSpec((MOBA_HPS, blk, blk), lambda b, h, n: (h, 0, 0)),
                  pl.BlockSpec((MOBA_HPS, blk, blk), lambda b, h, n: (h, 0, 0))],
        out_specs=pl.BlockSpec((blk, width), lambda b, h, n: (b * nb + n, h)),
        out_shape=jax.ShapeDtypeStruct((batch * seq, MOBA_HEADS * HEAD_DIM), F32),
        scratch_shapes=[pltpu.VMEM((MOBA_HPS, seq, LANES), BF16),
                        pltpu.VMEM((MOBA_HPS, nb, HEAD_DIM, blk), BF16),
                        pltpu.VMEM((MOBA_HPS, 16, HEAD_DIM), F32)],
        compiler_params=_params(("parallel", "parallel", "arbitrary")),
        name="moba",
    )(proj, proj, proj, t0, t1)


def _compress_kernel(k_ref, v_ref, w1k_ref, w1v_ref, w2k_ref, w2v_ref, pk_ref, pv_ref, kg_ref,
                     ko_ref, vo_ref):
    hid = NSA_CMP_HIDDEN

    def compress(t_ref, w1_ref, w2_ref, pos_ref):
        w1 = w1_ref[...]
        a = _dot(t_ref[0, 0].astype(BF16), w1)
        pw = _dot(pos_ref[...].astype(BF16), w1)
        pos = pw[0:1, :hid] + pw[1:2, hid:]
        nxt = pltpu.roll(a[:, hid:], a.shape[0] - 1, 0)
        h = jax.nn.gelu(a[:, :hid] + nxt + pos)
        return _dot(h.astype(BF16), w2_ref[...])

    kc = compress(k_ref, w1k_ref, w2k_ref, pk_ref)
    ko_ref[0, 0] = _rms(kc, kg_ref[...]).astype(BF16)
    vc = compress(v_ref, w1v_ref, w2v_ref, pv_ref).astype(BF16)
    vo_ref[0, 0] = _transpose_bf16(vc, HEAD_DIM).astype(BF16)


def _compress(k_r, v_r, w1k, w1v, w2k, w2v, pk, pv, kg):
    batch, groups, chunks, width = k_r.shape
    hid = NSA_CMP_HIDDEN
    tok = pl.BlockSpec((1, 1, chunks, width), lambda b, g: (b, g, 0, 0))
    full = lambda shape: pl.BlockSpec(shape, lambda b, g: (0,) * len(shape))
    return pl.pallas_call(
        _compress_kernel,
        grid=(batch, groups),
        in_specs=[tok, tok, full((width, 2 * hid)), full((width, 2 * hid)),
                  full((hid, HEAD_DIM)), full((hid, HEAD_DIM)),
                  full((8, width)), full((8, width)), full((1, HEAD_DIM))],
        out_specs=[pl.BlockSpec((1, 1, chunks, HEAD_DIM), lambda b, g: (b, g, 0, 0)),
                   pl.BlockSpec((1, 1, HEAD_DIM, chunks), lambda b, g: (b, g, 0, 0))],
        out_shape=[jax.ShapeDtypeStruct((batch, groups, chunks, HEAD_DIM), BF16),
                   jax.ShapeDtypeStruct((batch, groups, HEAD_DIM, chunks), BF16)],
        compiler_params=_params(("parallel", "parallel")),
        name="nsa_compress",
    )(k_r, v_r, w1k, w1v, w2k, w2v, pk, pv, kg)


def _nsa_kernel(q_ref, kc_ref, vc_ref, ksl_ref, vsl_ref, kwn_ref, vwn_ref, gt_ref,
                wc_ref, t0_ref, t1_ref, o_ref, kslaug_s, vslt_s, kwn_s, vwnt_s, sc_s, *, seq):
    qi = pl.program_id(1)
    tq = NSA_TQ
    hpg = NSA_HPG
    sb = NSA_SEL_BLOCK
    nsel = seq // sb
    ncmp = seq // NSA_CMP_STRIDE
    nwin = NSA_WINDOW // tq
    t_start = qi * tq

    @pl.when(qi == 0)
    def _prepare_keys():
        r = lax.broadcasted_iota(jnp.int32, (seq, LANES), 0)
        c = lax.broadcasted_iota(jnp.int32, (seq, LANES), 1)
        onehot = (c - HEAD_DIM == r // sb).astype(F32)
        place = _eye(HEAD_DIM, LANES)
        sc_s[:, 0:CMP_PAD, :] = jnp.zeros((NSA_KV_GROUPS, CMP_PAD, hpg * tq), F32)
        for g in range(NSA_KV_GROUPS):
            sl = slice(g * HEAD_DIM, (g + 1) * HEAD_DIM)
            kslaug_s[g] = (_dot(ksl_ref[:, sl].astype(BF16), place) + onehot).astype(BF16)
            kwn_s[g] = kwn_ref[:, sl].astype(BF16)
            for j in range(seq // tq):
                rows = slice(j * tq, (j + 1) * tq)
                vslt_s[g, j] = _transpose_bf16(vsl_ref[rows, sl].astype(BF16), HEAD_DIM).astype(BF16)
                vwnt_s[g, j] = _transpose_bf16(vwn_ref[rows, sl].astype(BF16), HEAD_DIM).astype(BF16)

    lanes = lambda parts: jnp.concatenate(parts, axis=1)
    key = lax.broadcasted_iota(jnp.int32, (tq, tq), 0)
    qry = lax.broadcasted_iota(jnp.int32, (tq, tq), 1)
    causal = lanes([key <= qry] * hpg)
    upper = lanes([key > qry] * hpg)
    cend = lax.broadcasted_iota(jnp.int32, (ncmp, tq), 0) * NSA_CMP_STRIDE + (NSA_CMP_LEN - 1)
    tpos = lax.broadcasted_iota(jnp.int32, (ncmp, tq), 1) + t_start
    cvis = lanes([cend <= tpos] * hpg)
    oj = lax.broadcasted_iota(jnp.int32, (nsel, ncmp), 0) * sb
    on = lax.broadcasted_iota(jnp.int32, (nsel, ncmp), 1) * NSA_CMP_STRIDE
    overlap = ((on < oj + sb) & (on + NSA_CMP_LEN > oj) & (on < seq - NSA_CMP_STRIDE)).astype(BF16)
    jidx = lax.broadcasted_iota(jnp.int32, (nsel, tq), 0)
    own = (lax.broadcasted_iota(jnp.int32, (nsel, tq), 1) + t_start) // sb
    gates = jax.nn.sigmoid(gt_ref[...]).T
    tile_rows = lambda j: pl.ds(pl.multiple_of(j * tq, tq), tq)
    back = lambda k: jnp.maximum(qi - k, 0)

    groups = range(NSA_KV_GROUPS)
    heads = [[g * hpg + r for r in range(hpg)] for g in groups]
    t0 = [lanes([t0_ref[h] for h in heads[g]]) for g in groups]
    t1 = [lanes([t1_ref[h] for h in heads[g]]) for g in groups]

    qplain, win_scores = [], []
    for g in groups:
        width = hpg * HEAD_DIM
        qg = _transpose_bf16(q_ref[:, g * width:(g + 1) * width].astype(BF16), width)
        qplain.append(lanes([qg[r * HEAD_DIM:(r + 1) * HEAD_DIM] for r in range(hpg)]).astype(BF16))
        sc_s[g, CMP_PAD:, :] = _dot(kc_ref[0, g], qplain[g])
        win_scores.append([_dot(kwn_s[g, tile_rows(back(k)), :], qplain[g]) for k in range(nwin + 1)])

    o_cmps, imps = [], []
    for g in groups:
        win = pl.ds(pl.multiple_of(qi * (tq // NSA_CMP_STRIDE), 8), CMP_WIN)
        sc_s[g, win, :] += lanes([wc_ref[h] for h in heads[g]])
        s = jnp.where(cvis, sc_s[g, CMP_PAD:, :], NEG)
        m = jnp.max(s, axis=0, keepdims=True)
        e = jnp.where(cvis, jnp.exp(s - m), 0.0)
        den = jnp.sum(e, axis=0, keepdims=True)
        p = e * (1.0 / jnp.where(den > 0, den, 1.0))
        o_cmps.append(_dot(vc_ref[0, g], p.astype(BF16)))
        psum = p[:, 0:tq]
        for r in range(1, hpg):
            psum = psum + p[:, r * tq:(r + 1) * tq]
        ph, plo = _split_bf16(psum)
        imps.append(_dot(overlap, ph) + _dot(overlap, plo))

    o_wins = []
    for g in groups:
        tiles = []
        for k, s in enumerate(win_scores[g]):
            if k == 0:
                s = jnp.where(causal, s + t0[g], NEG)
            elif k == 1:
                s = jnp.where(qi >= 1, s + t1[g], NEG)
            elif k < nwin:
                s = jnp.where(qi >= k, s, NEG)
            else:
                s = jnp.where(upper & (qi >= k), s, NEG)
            tiles.append((s, vwnt_s[g, back(k)]))
        m, l, acc = _softmax_joint(tiles)
        o_wins.append(acc * (1.0 / l))

    qaugs = []
    for g in groups:
        imp = jnp.where((jidx == 0) | (jidx == own) | (jidx == own - 1), -NEG, imps[g])
        imp = jnp.where(jidx > own, NEG, imp)
        sel = _rank_select(imp, jidx, NSA_SEL_TOPN) & (jidx <= own)
        selb = jnp.where(sel, 0.0, NEG).astype(BF16)
        qaugs.append(jnp.concatenate([qplain[g], lanes([selb] * hpg),
                                      jnp.zeros((LANES - HEAD_DIM - nsel, hpg * tq), BF16)], axis=0))

    odd = (qi >= 2) & (qi % 2 == 0)
    slc_near = [[_dot(kslaug_s[g, tile_rows(back(k)), :], qaugs[g]) for k in range(3)] for g in groups]
    slc_state = tuple(_softmax_joint([
        (jnp.where(causal, slc_near[g][0] + t0[g], NEG), vslt_s[g, qi]),
        (jnp.where(qi >= 1, slc_near[g][1] + t1[g], NEG), vslt_s[g, back(1)]),
        (jnp.where(odd, slc_near[g][2], NEG), vslt_s[g, back(2)])]) for g in groups)

    def slc_far(i, carry):
        slab = pl.ds(pl.multiple_of(i * (2 * tq), 2 * tq), 2 * tq)
        scores = [_dot(kslaug_s[g, slab, :], qaugs[g]) for g in groups]
        return tuple(_softmax_update(carry[g], scores[g], [vslt_s[g, 2 * i], vslt_s[g, 2 * i + 1]])
                     for g in groups)

    slc_state = lax.fori_loop(0, jnp.maximum(qi - 1, 0) // 2, slc_far, slc_state)

    outs = []
    for g in range(NSA_KV_GROUPS):
        m, l, acc = slc_state[g]
        o_slc = acc * (1.0 / l)
        for r in range(hpg):
            h = g * hpg + r
            cols = slice(r * tq, (r + 1) * tq)
            outs.append(gates[h:h + 1, :] * o_cmps[g][:, cols]
                        + gates[NSA_HEADS + h:NSA_HEADS + h + 1, :] * o_slc[:, cols]
                        + gates[2 * NSA_HEADS + h:2 * NSA_HEADS + h + 1, :] * o_wins[g][:, cols])
    o_ref[...] = jnp.concatenate(outs, axis=0).T


def _nsa(proj, kcn, vct, wc, t0, t1, batch, seq):
    tq = NSA_TQ
    nq = seq // tq
    ncmp = seq // NSA_CMP_STRIDE
    width = NSA_HEADS * HEAD_DIM
    groups = NSA_KV_GROUPS
    kv = lambda col: pl.BlockSpec((seq, LANES), lambda b, i: (b, col // LANES))
    const = lambda shape: pl.BlockSpec(shape, lambda b, i: (0,) * len(shape))
    return pl.pallas_call(
        functools.partial(_nsa_kernel, seq=seq),
        grid=(batch, nq),
        in_specs=[pl.BlockSpec((tq, width), lambda b, i: (b * nq + i, C_NSQ // width)),
                  pl.BlockSpec((1, groups, ncmp, HEAD_DIM), lambda b, i: (b, 0, 0, 0)),
                  pl.BlockSpec((1, groups, HEAD_DIM, ncmp), lambda b, i: (b, 0, 0, 0)),
                  kv(C_KSL), kv(C_VSL), kv(C_KWN), kv(C_VWN),
                  pl.BlockSpec((tq, LANES), lambda b, i: (b * nq + i, C_NG // LANES)),
                  const((NSA_HEADS, CMP_WIN, tq)), const((NSA_HEADS, tq, tq)),
                  const((NSA_HEADS, tq, tq))],
        out_specs=pl.BlockSpec((tq, width), lambda b, i: (b * nq + i, 0)),
        out_shape=jax.ShapeDtypeStruct((batch * seq, width), F32),
        scratch_shapes=[pltpu.VMEM((groups, seq, LANES), BF16),
                        pltpu.VMEM((groups, nq, HEAD_DIM, tq), BF16),
                        pltpu.VMEM((groups, seq, HEAD_DIM), BF16),
                        pltpu.VMEM((groups, nq, HEAD_DIM, tq), BF16),
                        pltpu.VMEM((groups, CMP_PAD + ncmp, NSA_HPG * tq), F32)],
        compiler_params=_params(("parallel", "arbitrary")),
        name="nsa",
    )(proj, kcn, vct, proj, proj, proj, proj, proj, wc, t0, t1)


def _merge_kernel(x_ref, ya_ref, yb_ref, ga_ref, gb_ref, wa_ref, wb_ref, wo_ref, o_ref):
    a = _dot(ya_ref[...].astype(BF16), wa_ref[...])
    b = _dot(yb_ref[...].astype(BF16), wb_ref[...])
    z = jax.nn.sigmoid(ga_ref[...]) * a + jax.nn.sigmoid(gb_ref[...]) * b
    o_ref[...] = x_ref[...] + _dot(z.astype(BF16), wo_ref[...])


def _merge(xf, ya, yb, proj, wa, wb, wo):
    n, d = xf.shape
    tm = 256
    row = lambda w, col=0: pl.BlockSpec((tm, w), lambda i: (i, col))
    full = lambda a: pl.BlockSpec(a.shape, lambda i: (0, 0))
    return pl.pallas_call(
        _merge_kernel,
        grid=(n // tm,),
        in_specs=[row(d), row(ya.shape[1]), row(yb.shape[1]), row(d, C_GA // d), row(d, C_GB // d),
                  full(wa), full(wb), full(wo)],
        out_specs=row(d),
        out_shape=jax.ShapeDtypeStruct((n, d), F32),
        compiler_params=_params(("parallel",)),
        name="merge",
    )(xf, ya, yb, proj, proj, wa, wb, wo)


FFN_HALO = 16
FFN_TM = 512
FFN_TF = 256


def _ffn_kernel(x_ref, xh_ref, g_ref, wu_ref, cw_ref, cb_ref, wd_ref, p_ref, wg_ref, wp_ref,
                o_ref, act_s, *, seq, d_ff):
    i = pl.program_id(0)
    tm = x_ref.shape[0]
    x = x_ref[...]
    at_start = (i * tm) % seq == 0
    halo = jnp.where(at_start, 0.0, _rms(xh_ref[...], g_ref[...]))
    hn = jnp.concatenate([halo.astype(BF16), _rms(x, g_ref[...]).astype(BF16)], axis=0)

    def conv(cols):
        u = _dot(hn, wu_ref[:, cols])
        u1 = pltpu.roll(u, 1, 0)[FFN_HALO:]
        u2 = pltpu.roll(u, 2, 0)[FFN_HALO:]
        cw = cw_ref[:, cols]
        return cw[0:1] * u2 + cw[1:2] * u1 + cw[2:3] * u[FFN_HALO:] + cb_ref[:, cols]

    for c in range(d_ff // FFN_TF):
        lo = c * FFN_TF
        act = jax.nn.gelu(conv(slice(lo, lo + FFN_TF))) * conv(slice(d_ff + lo, d_ff + lo + FFN_TF))
        act_s[:, lo:lo + FFN_TF] = act.astype(BF16)

    x = x + _dot(act_s[...], wd_ref[...])
    gate = jax.nn.sigmoid(_dot(x.astype(BF16), wg_ref[...]))
    o_ref[...] = x + gate * _dot(p_ref[...].astype(BF16), wp_ref[...])


def _ffn_ple(xf, gain, w_up, conv_w, conv_b, w_down, pf, wg, wp, seq):
    n, d = xf.shape
    d_ff = w_down.shape[0]
    tm = FFN_TM
    hb = tm // FFN_HALO
    resident = lambda a: pl.BlockSpec(a.shape, lambda i: (0, 0), pipeline_mode=pl.Buffered(1))
    gain = gain.reshape(1, d)
    conv_b = conv_b.reshape(1, -1)
    return pl.pallas_call(
        functools.partial(_ffn_kernel, seq=seq, d_ff=d_ff),
        grid=(n // tm,),
        in_specs=[pl.BlockSpec((tm, d), lambda i: (i, 0)),
                  pl.BlockSpec((FFN_HALO, d), lambda i: (jnp.maximum(i * hb - 1, 0), 0)),
                  resident(gain), resident(w_up), resident(conv_w), resident(conv_b), resident(w_down),
                  pl.BlockSpec((tm, pf.shape[1]), lambda i: (i, 0)),
                  resident(wg), resident(wp)],
        out_specs=pl.BlockSpec((tm, d), lambda i: (i, 0)),
        out_shape=jax.ShapeDtypeStruct((n, d), F32),
        scratch_shapes=[pltpu.VMEM((tm, d_ff), BF16)],
        compiler_params=_params(("parallel",)),
        name="conv_ffn_ple",
    )(xf, xf, gain, w_up, conv_w, conv_b, w_down, pf, wg, wp)


def _reorder_in_proj(w):
    attn = 3 * MOBA_HEADS * HEAD_DIM + NSA_HEADS * HEAD_DIM + 6 * NSA_KV_GROUPS * HEAD_DIM
    ng = 3 * NSA_HEADS
    pad = jnp.zeros((w.shape[0], PROJ_COLS - w.shape[1]), w.dtype)
    return jnp.concatenate([w[:, attn + ng:], w[:, :attn], w[:, attn:attn + ng], pad], axis=1)


def _head_gain_row(moba_q, moba_k, nsa_q, nsa_k):
    row = jnp.zeros((1, PROJ_COLS), F32)
    for col, gain, heads in ((C_MQ, moba_q * SCALE, MOBA_HEADS), (C_MK, moba_k, MOBA_HEADS),
                             (C_NSQ, nsa_q * SCALE, NSA_HEADS), (C_KSL, nsa_k[1], NSA_KV_GROUPS),
                             (C_KWN, nsa_k[2], NSA_KV_GROUPS)):
        row = row.at[0, col:col + heads * HEAD_DIM].set(jnp.tile(gain, heads))
    return row


def _cmp_weights(w1, pos):
    half = NSA_CMP_STRIDE * HEAD_DIM
    w1cat = jnp.concatenate([w1[:half], w1[half:]], axis=1).astype(BF16)
    posr = jnp.zeros((8, half), F32).at[0:2].set(pos.reshape(2, half))
    return w1cat, posr


def kernel(x, p, rel_bias, attn_norm, w_in, moba_q_gain, moba_k_gain, nsa_q_gain, nsa_k_gain,
           cmp_pos_k, cmp_w1_k, cmp_w2_k, cmp_pos_v, cmp_w1_v, cmp_w2_v,
           w_br_moba, w_br_nsa, w_o, ffn_norm, w_up, conv_w, conv_b, w_down, w_ple_gate, w_ple):
    batch, seq, d = x.shape
    n = batch * seq
    depth = w_in.shape[0]
    groups = NSA_KV_GROUPS
    chunks = seq // NSA_CMP_STRIDE
    t0m, t1m, t0n, t1n, wc = _bias_tables(rel_bias)
    xf = x.reshape(n, d)
    for i in range(depth):
        head_gain = _head_gain_row(moba_q_gain[i], moba_k_gain[i], nsa_q_gain[i], nsa_k_gain[i])
        proj = _inproj(xf, attn_norm[i], _reorder_in_proj(w_in[i]).astype(BF16), head_gain)
        ya = _moba(proj, t0m, t1m, batch, seq)

        def chunked(col):
            t = proj[:, col:col + groups * HEAD_DIM].reshape(batch, seq, groups, HEAD_DIM)
            return t.transpose(0, 2, 1, 3).reshape(batch, groups, chunks, NSA_CMP_STRIDE * HEAD_DIM)

        w1k, pk = _cmp_weights(cmp_w1_k[i], cmp_pos_k[i])
        w1v, pv = _cmp_weights(cmp_w1_v[i], cmp_pos_v[i])
        kcn, vct = _compress(chunked(C_KC), chunked(C_VC), w1k, w1v, cmp_w2_k[i].astype(BF16),
                             cmp_w2_v[i].astype(BF16), pk, pv, nsa_k_gain[i, 0:1])
        yb = _nsa(proj, kcn, vct, wc, t0n, t1n, batch, seq)
        xf = _merge(xf, ya, yb, proj, w_br_moba[i].astype(BF16), w_br_nsa[i].astype(BF16),
                    w_o[i].astype(BF16))
        xf = _ffn_ple(xf, ffn_norm[i], w_up[i].astype(BF16), conv_w[i], conv_b[i], w_down[i].astype(BF16),
                      p[i].reshape(n, -1), w_ple_gate[i].astype(BF16), w_ple[i].astype(BF16), seq)
    return xf.reshape(batch, seq, d)
```

```python
import functools
import math

import numpy as np
import jax
import jax.numpy as jnp
from jax import lax
from jax.experimental import pallas as pl
from jax.experimental.pallas import tpu as pltpu

F32 = jnp.float32
BF16 = jnp.bfloat16

HEAD_DIM = 64
MOBA_HEADS = 8
MOBA_BLOCK = 256
MOBA_TOPK = 3
NSA_HEADS = 8
NSA_KV_GROUPS = 2
NSA_HPG = NSA_HEADS // NSA_KV_GROUPS
NSA_CMP_LEN = 32
NSA_CMP_STRIDE = 16
NSA_CMP_HIDDEN = 2 * HEAD_DIM
NSA_SEL_BLOCK = 64
NSA_SEL_TOPN = 16
NSA_WINDOW = 512
REL_BUCKETS = 32
REL_MAX_DIST = 128
CONV_W = 3
RMS_EPS = 1e-6
SCALE = HEAD_DIM ** -0.5
NEG = -1e30

LANES = 128
VMEM_LIMIT = 56 * 1024 * 1024

C_GA, C_GB, C_MQ, C_MK, C_MV, C_NSQ = 0, 1024, 2048, 2560, 3072, 3584
C_KC, C_VC, C_KSL, C_VSL, C_KWN, C_VWN, C_NG = 4096, 4224, 4352, 4480, 4608, 4736, 4864
PROJ_COLS = 4992

NSA_TQ = 128
CMP_WIN = 16
CMP_PAD = 8


def _dot(a, b):
    return jnp.dot(a, b, preferred_element_type=F32)


def _dot_nt(a, b):
    return lax.dot_general(a, b, (((1,), (1,)), ((), ())), preferred_element_type=F32)


def _rms(x, gain):
    return x * lax.rsqrt(jnp.mean(x * x, axis=-1, keepdims=True) + RMS_EPS) * gain


def _split_bf16(x):
    hi = x.astype(BF16)
    return hi, (x - hi.astype(F32)).astype(BF16)


def _eye(rows, cols):
    r = lax.broadcasted_iota(jnp.int32, (rows, cols), 0)
    c = lax.broadcasted_iota(jnp.int32, (rows, cols), 1)
    return (r == c).astype(BF16)


def _transpose_bf16(x, rows):
    return _dot_nt(_eye(rows, x.shape[1]), x)


def _softmax_joint(tiles):
    m = functools.reduce(jnp.maximum, [jnp.max(s, axis=0, keepdims=True) for s, _ in tiles])
    ps = [jnp.exp(s - m) for s, _ in tiles]
    l = functools.reduce(jnp.add, [jnp.sum(p, axis=0, keepdims=True) for p in ps])
    acc = functools.reduce(jnp.add, [_dot(vt, p.astype(BF16)) for p, (_, vt) in zip(ps, tiles)])
    return m, l, acc


def _softmax_update(carry, s, vts):
    m, l, acc = carry
    m_new = jnp.maximum(m, jnp.max(s, axis=0, keepdims=True))
    alpha = jnp.exp(m - m_new)
    p = jnp.exp(s - m_new)
    rows = s.shape[0] // len(vts)
    pv = functools.reduce(jnp.add, [_dot(vt, p[i * rows:(i + 1) * rows].astype(BF16))
                                    for i, vt in enumerate(vts)])
    return m_new, alpha * l + jnp.sum(p, axis=0, keepdims=True), alpha * acc + pv


def _rank_select(score, idx, count):
    beaten = jnp.zeros(score.shape, jnp.int32)
    for i in range(score.shape[0]):
        si = score[i:i + 1, :]
        beaten += ((si > score) | ((si == score) & (i < idx))).astype(jnp.int32)
    return beaten < count


def _params(sem):
    return pltpu.CompilerParams(dimension_semantics=sem, vmem_limit_bytes=VMEM_LIMIT)


def _layer_spec(a, layer):
    return pl.BlockSpec((None,) + a.shape[1:], lambda *_: (layer,) + (0,) * (a.ndim - 1),
                        pipeline_mode=pl.Buffered(1))


def _rel_bucket_np(dist):
    n = np.maximum(dist, 0)
    max_exact = REL_BUCKETS // 2
    nf = np.maximum(n, 1).astype(np.float32)
    large = max_exact + (np.log(nf / np.float32(max_exact)) / np.float32(math.log(REL_MAX_DIST / max_exact))
                         * np.float32(REL_BUCKETS - max_exact)).astype(np.int32)
    return np.where(n < max_exact, n, np.minimum(large, REL_BUCKETS - 1))


def _bucket_starts():
    buckets = _rel_bucket_np(np.arange(4 * REL_MAX_DIST))
    return [int(np.argmax(buckets >= k)) for k in range(REL_BUCKETS)]


BUCKET_START = _bucket_starts()
BIAS_REACH = BUCKET_START[-1]
assert BIAS_REACH <= MOBA_BLOCK and BIAS_REACH <= NSA_TQ - NSA_CMP_LEN + 1 + NSA_CMP_STRIDE


def _tables_kernel(tab_ref, t0m_ref, t1m_ref, t0n_ref, t1n_ref, wc_ref):
    h = pl.program_id(0)

    def bias(dist, head):
        last = tab_ref[head, REL_BUCKETS - 1]
        val = jnp.zeros(dist.shape, F32)
        for k in range(REL_BUCKETS - 2, -1, -1):
            val = jnp.where(dist < BUCKET_START[k + 1], tab_ref[head, k] - last, val)
        return val

    def toeplitz(size, offset, head):
        key = lax.broadcasted_iota(jnp.int32, (size, size), 0)
        qry = lax.broadcasted_iota(jnp.int32, (size, size), 1)
        return bias(offset + qry - key, head)

    t0m_ref[0] = toeplitz(MOBA_BLOCK, 0, h)
    t1m_ref[0] = toeplitz(MOBA_BLOCK, MOBA_BLOCK, h)
    t0n_ref[0] = toeplitz(NSA_TQ, 0, MOBA_HEADS + h)
    t1n_ref[0] = toeplitz(NSA_TQ, NSA_TQ, MOBA_HEADS + h)
    a = lax.broadcasted_iota(jnp.int32, (CMP_WIN, NSA_TQ), 0)
    i = lax.broadcasted_iota(jnp.int32, (CMP_WIN, NSA_TQ), 1)
    wc_ref[0] = bias(i + (NSA_TQ - NSA_CMP_LEN + 1) - NSA_CMP_STRIDE * a, MOBA_HEADS + h)


def _bias_tables(rel_bias):
    blk, tq = MOBA_BLOCK, NSA_TQ
    shapes = [(blk, blk), (blk, blk), (tq, tq), (tq, tq), (CMP_WIN, tq)]
    return pl.pallas_call(
        _tables_kernel,
        grid=(MOBA_HEADS,),
        in_specs=[pl.BlockSpec(memory_space=pltpu.SMEM)],
        out_specs=[pl.BlockSpec((1,) + s, lambda h: (h, 0, 0)) for s in shapes],
        out_shape=[jax.ShapeDtypeStruct((MOBA_HEADS,) + s, F32) for s in shapes],
        compiler_params=_params(("arbitrary",)),
        name="bias_tables",
    )(rel_bias)


NORM_SLABS = (list(range(C_MQ, C_MV, LANES)) + list(range(C_NSQ, C_KC, LANES)) + [C_KSL, C_KWN])


def _inproj_kernel(x_ref, g_ref, w_ref, hg_ref, o_ref):
    h = _rms(x_ref[...], g_ref[...]).astype(BF16)
    y = _dot(h, w_ref[...])
    first = lax.broadcasted_iota(jnp.int32, (1, LANES), 1) < HEAD_DIM
    edges = sorted(set([0, PROJ_COLS] + NORM_SLABS + [c + LANES for c in NORM_SLABS]))
    for lo, hi in zip(edges[:-1], edges[1:]):
        t = y[:, lo:hi]
        if lo in NORM_SLABS:
            sq = t * t
            s0 = jnp.sum(jnp.where(first, sq, 0.0), axis=-1, keepdims=True)
            s1 = jnp.sum(jnp.where(first, 0.0, sq), axis=-1, keepdims=True)
            ms = jnp.where(first, s0, s1) * (1.0 / HEAD_DIM)
            t = t * lax.rsqrt(ms + RMS_EPS) * hg_ref[:, lo:hi]
        o_ref[:, lo:hi] = t


def _inproj(xf, gain, w, head_gain, layer):
    n, d = xf.shape
    tm = 256
    return pl.pallas_call(
        _inproj_kernel,
        grid=(n // tm,),
        in_specs=[pl.BlockSpec((tm, d), lambda i: (i, 0)), _layer_spec(gain, layer),
                  _layer_spec(w, layer), _layer_spec(head_gain, layer)],
        out_specs=pl.BlockSpec((tm, PROJ_COLS), lambda i: (i, 0)),
        out_shape=jax.ShapeDtypeStruct((n, PROJ_COLS), F32),
        compiler_params=_params(("parallel",)),
        name="inproj",
    )(xf, gain, w, head_gain)


MOBA_HPS = 8


def _moba_kernel(q_ref, k_ref, v_ref, t0_ref, t1_ref, o_ref, kaug_s, vt_s, km_s, *, seq):
    n = pl.program_id(2)
    blk = MOBA_BLOCK
    nb = seq // blk
    head_cols = lambda hh: slice(hh * HEAD_DIM, (hh + 1) * HEAD_DIM)

    @pl.when(n == 0)
    def _prepare_keys():
        r = lax.broadcasted_iota(jnp.int32, (seq, LANES), 0)
        c = lax.broadcasted_iota(jnp.int32, (seq, LANES), 1)
        onehot = (c - HEAD_DIM == r // blk).astype(F32)
        place = _eye(HEAD_DIM, LANES)
        for hh in range(MOBA_HPS):
            kn = k_ref[:, head_cols(hh)]
            km = kn.reshape(nb, blk, HEAD_DIM).sum(axis=1) * (1.0 / blk)
            km_s[hh] = jnp.concatenate([km, jnp.zeros((16 - nb, HEAD_DIM), F32)], axis=0)
            kaug_s[hh] = (_dot(kn.astype(BF16), place) + onehot).astype(BF16)
            for j in range(nb):
                vj = v_ref[j * blk:(j + 1) * blk, head_cols(hh)].astype(BF16)
                vt_s[hh, j] = _transpose_bf16(vj, HEAD_DIM).astype(BF16)

    jidx = lax.broadcasted_iota(jnp.int32, (16, blk), 0)
    past = jidx < n
    key = lax.broadcasted_iota(jnp.int32, (blk, blk), 0)
    qry = lax.broadcasted_iota(jnp.int32, (blk, blk), 1)
    prev = jnp.maximum(n - 1, 0)
    heads = range(MOBA_HPS)
    qts = []
    for pair in range(MOBA_HPS // 2):
        qp = _transpose_bf16(q_ref[:, pair * LANES:(pair + 1) * LANES].astype(BF16), LANES)
        qts += [qp[0:HEAD_DIM].astype(BF16), qp[HEAD_DIM:].astype(BF16)]
    gates = []
    for hh in heads:
        kmh, kml = _split_bf16(km_s[hh])
        gates.append(_dot(kmh, qts[hh]) + _dot(kml, qts[hh]))
    qaugs, state = [], []
    for hh in heads:
        gate = jnp.where(past, gates[hh], NEG)
        sel = (past & _rank_select(gate, jidx, MOBA_TOPK)) | (jidx == n)
        selb = jnp.where(sel, 0.0, NEG).astype(BF16)
        qaugs.append(jnp.concatenate([qts[hh], selb, jnp.zeros((LANES - HEAD_DIM - 16, blk), BF16)],
                                     axis=0))

    def scores(hh, j):
        start = pl.multiple_of(j * blk, blk)
        return _dot(kaug_s[hh, pl.ds(start, blk), :], qaugs[hh]), vt_s[hh, j]

    near = [(scores(hh, n), scores(hh, prev)) for hh in range(MOBA_HPS)]
    for hh, ((s0, vt0), (s1, vt1)) in enumerate(near):
        s0 = jnp.where(key <= qry, s0 + t0_ref[hh], NEG)
        s1 = jnp.where(n >= 1, s1 + t1_ref[hh], NEG)
        state.append(_softmax_joint([(s0, vt0), (s1, vt1)]))

    def far_body(j, carry):
        tiles = [scores(hh, j) for hh in range(MOBA_HPS)]
        return tuple(_softmax_update(carry[hh], tiles[hh][0], [tiles[hh][1]]) for hh in range(MOBA_HPS))

    state = lax.fori_loop(0, jnp.maximum(n - 1, 0), far_body, tuple(state))
    o_ref[...] = jnp.concatenate([acc * (1.0 / l) for _, l, acc in state], axis=0).T.astype(BF16)


def _moba(proj, t0, t1, batch, seq):
    blk = MOBA_BLOCK
    nb = seq // blk
    width = MOBA_HPS * HEAD_DIM
    qc, kc, vc = C_MQ // width, C_MK // width, C_MV // width
    return pl.pallas_call(
        functools.partial(_moba_kernel, seq=seq),
        grid=(batch, MOBA_HEADS // MOBA_HPS, nb),
        in_specs=[pl.BlockSpec((blk, width), lambda b, h, n: (b * nb + n, qc + h)),
                  pl.BlockSpec((seq, width), lambda b, h, n: (b, kc + h)),
                  pl.BlockSpec((seq, width), lambda b, h, n: (b, vc + h)),
                  pl.BlockSpec((MOBA_HPS, blk, blk), lambda b, h, n: (h, 0, 0)),
                  pl.BlockSpec((MOBA_HPS, blk, blk), lambda b, h, n: (h, 0, 0))],
        out_specs=pl.BlockSpec((blk, width), lambda b, h, n: (b * nb + n, h)),
        out_shape=jax.ShapeDtypeStruct((batch * seq, MOBA_HEADS * HEAD_DIM), BF16),
        scratch_shapes=[pltpu.VMEM((MOBA_HPS, seq, LANES), BF16),
                        pltpu.VMEM((MOBA_HPS, nb, HEAD_DIM, blk), BF16),
                        pltpu.VMEM((MOBA_HPS, 16, HEAD_DIM), F32)],
        compiler_params=_params(("parallel", "parallel", "arbitrary")),
        name="moba",
    )(proj, proj, proj, t0, t1)


def _compress_kernel(k_ref, v_ref, w1k_ref, w1v_ref, wtk_ref, wtv_ref, w2k_ref, w2v_ref, pk_ref, pv_ref,
                     kg_ref, ko_ref, vo_ref):
    hid = NSA_CMP_HIDDEN
    chunks = k_ref.shape[0] // NSA_CMP_STRIDE

    def compress(t_ref, w1_ref, wt_ref, w2_ref, pos_ref, g):
        a = functools.reduce(jnp.add, [
            _dot(t_ref[pl.ds(l, chunks, stride=NSA_CMP_STRIDE), :].astype(BF16), wt_ref[g, l])
            for l in range(NSA_CMP_STRIDE)])
        pw = _dot(pos_ref[...].astype(BF16), w1_ref[...])
        pos = pw[0:1, :hid] + pw[1:2, hid:]
        nxt = pltpu.roll(a[:, hid:], chunks - 1, 0)
        h = jax.nn.gelu(a[:, :hid] + nxt + pos)
        return _dot(h.astype(BF16), w2_ref[...])

    for g in range(NSA_KV_GROUPS):
        kc = compress(k_ref, w1k_ref, wtk_ref, w2k_ref, pk_ref, g)
        ko_ref[0, g] = _rms(kc, kg_ref[...]).astype(BF16)
        vc = compress(v_ref, w1v_ref, wtv_ref, w2v_ref, pv_ref, g).astype(BF16)
        vo_ref[0, g] = _transpose_bf16(vc, HEAD_DIM).astype(BF16)


def _compress(proj, w1k, w1v, wtk, wtv, w2k, w2v, pk, pv, kg, layer, batch, seq):
    groups = NSA_KV_GROUPS
    chunks = seq // NSA_CMP_STRIDE
    tok = lambda col: pl.BlockSpec((seq, LANES), lambda b: (b, col // LANES))
    return pl.pallas_call(
        _compress_kernel,
        grid=(batch,),
        in_specs=[tok(C_KC), tok(C_VC)] + [_layer_spec(a, layer)
                                           for a in (w1k, w1v, wtk, wtv, w2k, w2v, pk, pv, kg)],
        out_specs=[pl.BlockSpec((1, groups, chunks, HEAD_DIM), lambda b: (b, 0, 0, 0)),
                   pl.BlockSpec((1, groups, HEAD_DIM, chunks), lambda b: (b, 0, 0, 0))],
        out_shape=[jax.ShapeDtypeStruct((batch, groups, chunks, HEAD_DIM), BF16),
                   jax.ShapeDtypeStruct((batch, groups, HEAD_DIM, chunks), BF16)],
        compiler_params=_params(("parallel",)),
        name="nsa_compress",
    )(proj, proj, w1k, w1v, wtk, wtv, w2k, w2v, pk, pv, kg)


def _nsa_kernel(q_ref, kc_ref, vc_ref, ksl_ref, vsl_ref, kwn_ref, vwn_ref, gt_ref,
                wc_ref, t0_ref, t1_ref, o_ref, kslaug_s, vslt_s, kwn_s, vwnt_s, sc_s, *, seq):
    qi = pl.program_id(1)
    tq = NSA_TQ
    hpg = NSA_HPG
    sb = NSA_SEL_BLOCK
    nsel = seq // sb
    ncmp = seq // NSA_CMP_STRIDE
    nwin = NSA_WINDOW // tq
    t_start = qi * tq

    @pl.when(qi == 0)
    def _prepare_keys():
        r = lax.broadcasted_iota(jnp.int32, (seq, LANES), 0)
        c = lax.broadcasted_iota(jnp.int32, (seq, LANES), 1)
        onehot = (c - HEAD_DIM == r // sb).astype(F32)
        place = _eye(HEAD_DIM, LANES)
        sc_s[:, 0:CMP_PAD, :] = jnp.zeros((NSA_KV_GROUPS, CMP_PAD, hpg * tq), F32)
        for g in range(NSA_KV_GROUPS):
            sl = slice(g * HEAD_DIM, (g + 1) * HEAD_DIM)
            kslaug_s[g] = (_dot(ksl_ref[:, sl].astype(BF16), place) + onehot).astype(BF16)
            kwn_s[g] = kwn_ref[:, sl].astype(BF16)
            for j in range(seq // tq):
                rows = slice(j * tq, (j + 1) * tq)
                vslt_s[g, j] = _transpose_bf16(vsl_ref[rows, sl].astype(BF16), HEAD_DIM).astype(BF16)
                vwnt_s[g, j] = _transpose_bf16(vwn_ref[rows, sl].astype(BF16), HEAD_DIM).astype(BF16)

    lanes = lambda parts: jnp.concatenate(parts, axis=1)
    key = lax.broadcasted_iota(jnp.int32, (tq, tq), 0)
    qry = lax.broadcasted_iota(jnp.int32, (tq, tq), 1)
    causal = lanes([key <= qry] * hpg)
    upper = lanes([key > qry] * hpg)
    cend = lax.broadcasted_iota(jnp.int32, (ncmp, tq), 0) * NSA_CMP_STRIDE + (NSA_CMP_LEN - 1)
    tpos = lax.broadcasted_iota(jnp.int32, (ncmp, tq), 1) + t_start
    cvis = lanes([cend <= tpos] * hpg)
    oj = lax.broadcasted_iota(jnp.int32, (nsel, ncmp), 0) * sb
    on = lax.broadcasted_iota(jnp.int32, (nsel, ncmp), 1) * NSA_CMP_STRIDE
    overlap = ((on < oj + sb) & (on + NSA_CMP_LEN > oj) & (on < seq - NSA_CMP_STRIDE)).astype(BF16)
    jidx = lax.broadcasted_iota(jnp.int32, (nsel, tq), 0)
    own = (lax.broadcasted_iota(jnp.int32, (nsel, tq), 1) + t_start) // sb
    gates = jax.nn.sigmoid(gt_ref[...]).T
    tile_rows = lambda j: pl.ds(pl.multiple_of(j * tq, tq), tq)
    back = lambda k: jnp.maximum(qi - k, 0)

    groups = range(NSA_KV_GROUPS)
    heads = [[g * hpg + r for r in range(hpg)] for g in groups]
    t0 = [lanes([t0_ref[h] for h in heads[g]]) for g in groups]
    t1 = [lanes([t1_ref[h] for h in heads[g]]) for g in groups]

    qplain, win_scores = [], []
    for g in groups:
        width = hpg * HEAD_DIM
        qg = _transpose_bf16(q_ref[:, g * width:(g + 1) * width].astype(BF16), width)
        qplain.append(lanes([qg[r * HEAD_DIM:(r + 1) * HEAD_DIM] for r in range(hpg)]).astype(BF16))
        sc_s[g, CMP_PAD:, :] = _dot(kc_ref[0, g], qplain[g])
        win_scores.append([_dot(kwn_s[g, tile_rows(back(k)), :], qplain[g]) for k in range(nwin + 1)])

    o_cmps, imps = [], []
    for g in groups:
        win = pl.ds(pl.multiple_of(qi * (tq // NSA_CMP_STRIDE), 8), CMP_WIN)
        sc_s[g, win, :] += lanes([wc_ref[h] for h in heads[g]])
        s = jnp.where(cvis, sc_s[g, CMP_PAD:, :], NEG)
        m = jnp.max(s, axis=0, keepdims=True)
        e = jnp.where(cvis, jnp.exp(s - m), 0.0)
        den = jnp.sum(e, axis=0, keepdims=True)
        p = e * (1.0 / jnp.where(den > 0, den, 1.0))
        o_cmps.append(_dot(vc_ref[0, g], p.astype(BF16)))
        psum = p[:, 0:tq]
        for r in range(1, hpg):
            psum = psum + p[:, r * tq:(r + 1) * tq]
        ph, plo = _split_bf16(psum)
        imps.append(_dot(overlap, ph) + _dot(overlap, plo))

    o_wins = []
    for g in groups:
        tiles = []
        for k, s in enumerate(win_scores[g]):
            if k == 0:
                s = jnp.where(causal, s + t0[g], NEG)
            elif k == 1:
                s = jnp.where(qi >= 1, s + t1[g], NEG)
            elif k < nwin:
                s = jnp.where(qi >= k, s, NEG)
            else:
                s = jnp.where(upper & (qi >= k), s, NEG)
            tiles.append((s, vwnt_s[g, back(k)]))
        m, l, acc = _softmax_joint(tiles)
        o_wins.append(acc * (1.0 / l))

    qaugs = []
    for g in groups:
        imp = jnp.where((jidx == 0) | (jidx == own) | (jidx == own - 1), -NEG, imps[g])
        imp = jnp.where(jidx > own, NEG, imp)
        sel = _rank_select(imp, jidx, NSA_SEL_TOPN) & (jidx <= own)
        selb = jnp.where(sel, 0.0, NEG).astype(BF16)
        qaugs.append(jnp.concatenate([qplain[g], lanes([selb] * hpg),
                                      jnp.zeros((LANES - HEAD_DIM - nsel, hpg * tq), BF16)], axis=0))

    odd = (qi >= 2) & (qi % 2 == 0)
    slc_near = [[_dot(kslaug_s[g, tile_rows(back(k)), :], qaugs[g]) for k in range(3)] for g in groups]
    slc_state = tuple(_softmax_joint([
        (jnp.where(causal, slc_near[g][0] + t0[g], NEG), vslt_s[g, qi]),
        (jnp.where(qi >= 1, slc_near[g][1] + t1[g], NEG), vslt_s[g, back(1)]),
        (jnp.where(odd, slc_near[g][2], NEG), vslt_s[g, back(2)])]) for g in groups)

    def slc_far(i, carry):
        slab = pl.ds(pl.multiple_of(i * (2 * tq), 2 * tq), 2 * tq)
        scores = [_dot(kslaug_s[g, slab, :], qaugs[g]) for g in groups]
        return tuple(_softmax_update(carry[g], scores[g], [vslt_s[g, 2 * i], vslt_s[g, 2 * i + 1]])
                     for g in groups)

    slc_state = lax.fori_loop(0, jnp.maximum(qi - 1, 0) // 2, slc_far, slc_state)

    outs = []
    for g in range(NSA_KV_GROUPS):
        m, l, acc = slc_state[g]
        o_slc = acc * (1.0 / l)
        for r in range(hpg):
            h = g * hpg + r
            cols = slice(r * tq, (r + 1) * tq)
            outs.append(gates[h:h + 1, :] * o_cmps[g][:, cols]
                        + gates[NSA_HEADS + h:NSA_HEADS + h + 1, :] * o_slc[:, cols]
                        + gates[2 * NSA_HEADS + h:2 * NSA_HEADS + h + 1, :] * o_wins[g][:, cols])
    o_ref[...] = jnp.concatenate(outs, axis=0).T.astype(BF16)


def _nsa(proj, kcn, vct, wc, t0, t1, batch, seq):
    tq = NSA_TQ
    nq = seq // tq
    ncmp = seq // NSA_CMP_STRIDE
    width = NSA_HEADS * HEAD_DIM
    groups = NSA_KV_GROUPS
    kv = lambda col: pl.BlockSpec((seq, LANES), lambda b, i: (b, col // LANES))
    const = lambda shape: pl.BlockSpec(shape, lambda b, i: (0,) * len(shape))
    return pl.pallas_call(
        functools.partial(_nsa_kernel, seq=seq),
        grid=(batch, nq),
        in_specs=[pl.BlockSpec((tq, width), lambda b, i: (b * nq + i, C_NSQ // width)),
                  pl.BlockSpec((1, groups, ncmp, HEAD_DIM), lambda b, i: (b, 0, 0, 0)),
                  pl.BlockSpec((1, groups, HEAD_DIM, ncmp), lambda b, i: (b, 0, 0, 0)),
                  kv(C_KSL), kv(C_VSL), kv(C_KWN), kv(C_VWN),
                  pl.BlockSpec((tq, LANES), lambda b, i: (b * nq + i, C_NG // LANES)),
                  const((NSA_HEADS, CMP_WIN, tq)), const((NSA_HEADS, tq, tq)),
                  const((NSA_HEADS, tq, tq))],
        out_specs=pl.BlockSpec((tq, width), lambda b, i: (b * nq + i, 0)),
        out_shape=jax.ShapeDtypeStruct((batch * seq, width), BF16),
        scratch_shapes=[pltpu.VMEM((groups, seq, LANES), BF16),
                        pltpu.VMEM((groups, nq, HEAD_DIM, tq), BF16),
                        pltpu.VMEM((groups, seq, HEAD_DIM), BF16),
                        pltpu.VMEM((groups, nq, HEAD_DIM, tq), BF16),
                        pltpu.VMEM((groups, CMP_PAD + ncmp, NSA_HPG * tq), F32)],
        compiler_params=_params(("parallel", "arbitrary")),
        name="nsa",
    )(proj, kcn, vct, proj, proj, proj, proj, proj, wc, t0, t1)


def _merge_kernel(x_ref, ya_ref, yb_ref, ga_ref, gb_ref, wa_ref, wb_ref, wo_ref, o_ref):
    a = _dot(ya_ref[...], wa_ref[...])
    b = _dot(yb_ref[...], wb_ref[...])
    z = jax.nn.sigmoid(ga_ref[...]) * a + jax.nn.sigmoid(gb_ref[...]) * b
    o_ref[...] = x_ref[...] + _dot(z.astype(BF16), wo_ref[...])


def _merge(xf, ya, yb, proj, wa, wb, wo, layer):
    n, d = xf.shape
    tm = 512
    row = lambda w, col=0: pl.BlockSpec((tm, w), lambda i: (i, col))
    return pl.pallas_call(
        _merge_kernel,
        grid=(n // tm,),
        in_specs=[row(d), row(ya.shape[1]), row(yb.shape[1]), row(d, C_GA // d), row(d, C_GB // d),
                  _layer_spec(wa, layer), _layer_spec(wb, layer), _layer_spec(wo, layer)],
        out_specs=row(d),
        out_shape=jax.ShapeDtypeStruct((n, d), F32),
        compiler_params=_params(("parallel",)),
        name="merge",
    )(xf, ya, yb, proj, proj, wa, wb, wo)


FFN_HALO = 16
FFN_TM = 512
FFN_TF = 256


def _ffn_kernel(x_ref, xh_ref, g_ref, wu_ref, cw_ref, cb_ref, wd_ref, p_ref, wg_ref, wp_ref,
                o_ref, act_s, *, seq, d_ff):
    i = pl.program_id(0)
    tm = x_ref.shape[0]
    x = x_ref[...]
    at_start = (i * tm) % seq == 0
    halo = jnp.where(at_start, 0.0, _rms(xh_ref[...], g_ref[...]))
    hn = jnp.concatenate([halo.astype(BF16), _rms(x, g_ref[...]).astype(BF16)], axis=0)

    def conv(cols):
        u = _dot(hn, wu_ref[:, cols])
        u1 = pltpu.roll(u, 1, 0)[FFN_HALO:]
        u2 = pltpu.roll(u, 2, 0)[FFN_HALO:]
        cw = cw_ref[:, cols]
        return cw[0:1] * u2 + cw[1:2] * u1 + cw[2:3] * u[FFN_HALO:] + cb_ref[:, cols]

    for c in range(d_ff // FFN_TF):
        lo = c * FFN_TF
        act = jax.nn.gelu(conv(slice(lo, lo + FFN_TF))) * conv(slice(d_ff + lo, d_ff + lo + FFN_TF))
        act_s[:, lo:lo + FFN_TF] = act.astype(BF16)

    x = x + _dot(act_s[...], wd_ref[...])
    gate = jax.nn.sigmoid(_dot(x.astype(BF16), wg_ref[...]))
    o_ref[...] = x + gate * _dot(p_ref[...].astype(BF16), wp_ref[...])


def _ffn_ple(xf, gain, w_up, conv_w, conv_b, w_down, pf, wg, wp, layer, seq):
    n, d = xf.shape
    d_ff = w_down.shape[1]
    tm = FFN_TM
    hb = tm // FFN_HALO
    resident = lambda a: _layer_spec(a, layer)
    return pl.pallas_call(
        functools.partial(_ffn_kernel, seq=seq, d_ff=d_ff),
        grid=(n // tm,),
        in_specs=[pl.BlockSpec((tm, d), lambda i: (i, 0)),
                  pl.BlockSpec((FFN_HALO, d), lambda i: (jnp.maximum(i * hb - 1, 0), 0)),
                  resident(gain), resident(w_up), resident(conv_w), resident(conv_b), resident(w_down),
                  pl.BlockSpec((None, tm, pf.shape[2]), lambda i: (layer, i, 0)),
                  resident(wg), resident(wp)],
        out_specs=pl.BlockSpec((tm, d), lambda i: (i, 0)),
        out_shape=jax.ShapeDtypeStruct((n, d), F32),
        scratch_shapes=[pltpu.VMEM((tm, d_ff), BF16)],
        compiler_params=_params(("parallel",)),
        name="conv_ffn_ple",
    )(xf, xf, gain, w_up, conv_w, conv_b, w_down, pf, wg, wp)


def _reorder_in_proj(w):
    attn = 3 * MOBA_HEADS * HEAD_DIM + NSA_HEADS * HEAD_DIM + 6 * NSA_KV_GROUPS * HEAD_DIM
    ng = 3 * NSA_HEADS
    pad = jnp.zeros(w.shape[:-1] + (PROJ_COLS - w.shape[-1],), w.dtype)
    return jnp.concatenate([w[..., attn + ng:], w[..., :attn], w[..., attn:attn + ng], pad], axis=-1)


def _head_gain_rows(moba_q, moba_k, nsa_q, nsa_k):
    rows = jnp.zeros((moba_q.shape[0], 1, PROJ_COLS), F32)
    for col, gain, heads in ((C_MQ, moba_q * SCALE, MOBA_HEADS), (C_MK, moba_k, MOBA_HEADS),
                             (C_NSQ, nsa_q * SCALE, NSA_HEADS), (C_KSL, nsa_k[:, 1], NSA_KV_GROUPS),
                             (C_KWN, nsa_k[:, 2], NSA_KV_GROUPS)):
        rows = rows.at[:, 0, col:col + heads * HEAD_DIM].set(jnp.tile(gain, (1, heads)))
    return rows


def _cmp_weights(w1, pos):
    depth = w1.shape[0]
    half = NSA_CMP_STRIDE * HEAD_DIM
    w1cat = jnp.concatenate([w1[:, :half], w1[:, half:]], axis=2).astype(BF16)
    tok = w1cat.reshape(depth, NSA_CMP_STRIDE, HEAD_DIM, -1)
    zero = jnp.zeros_like(tok)
    placed = jnp.stack([jnp.concatenate([tok, zero], axis=2), jnp.concatenate([zero, tok], axis=2)], axis=1)
    posr = jnp.zeros((depth, 8, half), F32).at[:, 0:2].set(pos.reshape(depth, 2, half))
    return w1cat, placed, posr


def kernel(x, p, rel_bias, attn_norm, w_in, moba_q_gain, moba_k_gain, nsa_q_gain, nsa_k_gain,
           cmp_pos_k, cmp_w1_k, cmp_w2_k, cmp_pos_v, cmp_w1_v, cmp_w2_v,
           w_br_moba, w_br_nsa, w_o, ffn_norm, w_up, conv_w, conv_b, w_down, w_ple_gate, w_ple):
    batch, seq, d = x.shape
    n = batch * seq
    depth = w_in.shape[0]
    bf = lambda a: a.astype(BF16)
    w_in_r = bf(_reorder_in_proj(w_in))
    head_gain = _head_gain_rows(moba_q_gain, moba_k_gain, nsa_q_gain, nsa_k_gain)
    attn_gain, ffn_gain = attn_norm[:, None, :], ffn_norm[:, None, :]
    w1k, wtk, pk = _cmp_weights(cmp_w1_k, cmp_pos_k)
    w1v, wtv, pv = _cmp_weights(cmp_w1_v, cmp_pos_v)
    w2k, w2v, kg0 = bf(cmp_w2_k), bf(cmp_w2_v), nsa_k_gain[:, 0:1]
    wa, wb, wo = bf(w_br_moba), bf(w_br_nsa), bf(w_o)
    wu, wd, wg, wp = bf(w_up), bf(w_down), bf(w_ple_gate), bf(w_ple)
    conv_b = conv_b[:, None, :]
    pf = p.reshape(depth, n, -1)

    t0m, t1m, t0n, t1n, wc = _bias_tables(rel_bias)
    xf = x.reshape(n, d)
    for i in range(depth):
        proj = _inproj(xf, attn_gain, w_in_r, head_gain, i)
        ya = _moba(proj, t0m, t1m, batch, seq)
        kcn, vct = _compress(proj, w1k, w1v, wtk, wtv, w2k, w2v, pk, pv, kg0, i, batch, seq)
        yb = _nsa(proj, kcn, vct, wc, t0n, t1n, batch, seq)
        xf = _merge(xf, ya, yb, proj, wa, wb, wo, i)
        xf = _ffn_ple(xf, ffn_gain, wu, conv_w, conv_b, wd, pf, wg, wp, i, seq)
    return xf.reshape(batch, seq, d)
```

```python
import functools
import math

import numpy as np
import jax
import jax.numpy as jnp
from jax import lax
from jax.experimental import pallas as pl
from jax.experimental.pallas import tpu as pltpu

F32 = jnp.float32
BF16 = jnp.bfloat16

HEAD_DIM = 64
MOBA_HEADS = 8
MOBA_BLOCK = 256
MOBA_TOPK = 3
NSA_HEADS = 8
NSA_KV_GROUPS = 2
NSA_HPG = NSA_HEADS // NSA_KV_GROUPS
NSA_CMP_LEN = 32
NSA_CMP_STRIDE = 16
NSA_CMP_HIDDEN = 2 * HEAD_DIM
NSA_SEL_BLOCK = 64
NSA_SEL_TOPN = 16
NSA_WINDOW = 512
REL_BUCKETS = 32
REL_MAX_DIST = 128
CONV_W = 3
RMS_EPS = 1e-6
SCALE = HEAD_DIM ** -0.5
LOG2E = math.log2(math.e)
QSCALE = SCALE * LOG2E
NEG = -1e30

LANES = 128
VMEM_LIMIT = 56 * 1024 * 1024

C_GA, C_GB, C_MQ, C_MK, C_MV, C_NSQ = 0, 1024, 2048, 2560, 3072, 3584
C_KC, C_VC, C_KSL, C_VSL, C_KWN, C_VWN, C_NG = 4096, 4224, 4352, 4480, 4608, 4736, 4864
PROJ_COLS = 4992

NSA_TQ = 128
CMP_WIN = 16
CMP_PAD = 8


def _dot(a, b):
    return jnp.dot(a, b, preferred_element_type=F32)


def _dot_nt(a, b):
    return lax.dot_general(a, b, (((1,), (1,)), ((), ())), preferred_element_type=F32)


def _rms(x, gain):
    return x * lax.rsqrt(jnp.mean(x * x, axis=-1, keepdims=True) + RMS_EPS) * gain


def _split_bf16(x):
    hi = x.astype(BF16)
    return hi, (x - hi.astype(F32)).astype(BF16)


def _eye(rows, cols):
    r = lax.broadcasted_iota(jnp.int32, (rows, cols), 0)
    c = lax.broadcasted_iota(jnp.int32, (rows, cols), 1)
    return (r == c).astype(BF16)


def _transpose_bf16(x, rows):
    return _dot_nt(_eye(rows, x.shape[1]), x)


VROWS = 80


def _value_tile(v):
    row = lax.broadcasted_iota(jnp.int32, (VROWS, v.shape[0]), 0)
    return jnp.where(row == HEAD_DIM, 1.0, _transpose_bf16(v, VROWS)).astype(BF16)


def _softmax_joint(tiles):
    m = functools.reduce(jnp.maximum, [jnp.max(s, axis=0, keepdims=True) for s, _ in tiles])
    acc = functools.reduce(jnp.add, [_dot(vt, jnp.exp2(s - m).astype(BF16)) for s, vt in tiles])
    return m, acc


def _softmax_update(carry, s, vts):
    m, acc = carry
    m_new = jnp.maximum(m, jnp.max(s, axis=0, keepdims=True))
    p = jnp.exp2(s - m_new).astype(BF16)
    rows = s.shape[0] // len(vts)
    pv = functools.reduce(jnp.add, [_dot(vt, p[i * rows:(i + 1) * rows]) for i, vt in enumerate(vts)])
    return m_new, jnp.exp2(m - m_new) * acc + pv


def _softmax_finish(carry):
    _, acc = carry
    return acc[0:HEAD_DIM] * (1.0 / acc[HEAD_DIM:HEAD_DIM + 1])


def _rank_select(score, idx, count):
    beaten = jnp.zeros(score.shape, jnp.int32)
    for i in range(score.shape[0]):
        si = score[i:i + 1, :]
        beaten += ((si > score) | ((si == score) & (i < idx))).astype(jnp.int32)
    return beaten < count


def _params(sem):
    return pltpu.CompilerParams(dimension_semantics=sem, vmem_limit_bytes=VMEM_LIMIT)


def _layer_spec(a, layer):
    return pl.BlockSpec((None,) + a.shape[1:], lambda *_: (layer,) + (0,) * (a.ndim - 1),
                        pipeline_mode=pl.Buffered(1))


def _rel_bucket_np(dist):
    n = np.maximum(dist, 0)
    max_exact = REL_BUCKETS // 2
    nf = np.maximum(n, 1).astype(np.float32)
    large = max_exact + (np.log(nf / np.float32(max_exact)) / np.float32(math.log(REL_MAX_DIST / max_exact))
                         * np.float32(REL_BUCKETS - max_exact)).astype(np.int32)
    return np.where(n < max_exact, n, np.minimum(large, REL_BUCKETS - 1))


def _bucket_starts():
    buckets = _rel_bucket_np(np.arange(4 * REL_MAX_DIST))
    return [int(np.argmax(buckets >= k)) for k in range(REL_BUCKETS)]


BUCKET_START = _bucket_starts()
BIAS_REACH = BUCKET_START[-1]
assert BIAS_REACH <= MOBA_BLOCK and BIAS_REACH <= NSA_TQ - NSA_CMP_LEN + 1 + NSA_CMP_STRIDE


def _tables_kernel(tab_ref, t0m_ref, t1m_ref, t0n_ref, t1n_ref, wc_ref):
    h = pl.program_id(0)

    def bias(dist, head):
        last = tab_ref[head, REL_BUCKETS - 1]
        val = jnp.zeros(dist.shape, F32)
        for k in range(REL_BUCKETS - 2, -1, -1):
            val = jnp.where(dist < BUCKET_START[k + 1], (tab_ref[head, k] - last) * LOG2E, val)
        return val

    def toeplitz(size, offset, head):
        key = lax.broadcasted_iota(jnp.int32, (size, size), 0)
        qry = lax.broadcasted_iota(jnp.int32, (size, size), 1)
        return bias(offset + qry - key, head)

    t0m_ref[0] = toeplitz(MOBA_BLOCK, 0, h)
    t1m_ref[0] = toeplitz(MOBA_BLOCK, MOBA_BLOCK, h)
    t0n_ref[0] = toeplitz(NSA_TQ, 0, MOBA_HEADS + h)
    t1n_ref[0] = toeplitz(NSA_TQ, NSA_TQ, MOBA_HEADS + h)
    a = lax.broadcasted_iota(jnp.int32, (CMP_WIN, NSA_TQ), 0)
    i = lax.broadcasted_iota(jnp.int32, (CMP_WIN, NSA_TQ), 1)
    wc_ref[0] = bias(i + (NSA_TQ - NSA_CMP_LEN + 1) - NSA_CMP_STRIDE * a, MOBA_HEADS + h)


def _bias_tables(rel_bias):
    blk, tq = MOBA_BLOCK, NSA_TQ
    shapes = [(blk, blk), (blk, blk), (tq, tq), (tq, tq), (CMP_WIN, tq)]
    return pl.pallas_call(
        _tables_kernel,
        grid=(MOBA_HEADS,),
        in_specs=[pl.BlockSpec(memory_space=pltpu.SMEM)],
        out_specs=[pl.BlockSpec((1,) + s, lambda h: (h, 0, 0)) for s in shapes],
        out_shape=[jax.ShapeDtypeStruct((MOBA_HEADS,) + s, F32) for s in shapes],
        compiler_params=_params(("arbitrary",)),
        name="bias_tables",
    )(rel_bias)


NORM_SLABS = (list(range(C_MQ, C_MV, LANES)) + list(range(C_NSQ, C_KC, LANES)) + [C_KSL, C_KWN])


def _inproj_kernel(x_ref, g_ref, w_ref, hg_ref, o_ref):
    h = _rms(x_ref[...], g_ref[...]).astype(BF16)
    y = _dot(h, w_ref[...])
    first = lax.broadcasted_iota(jnp.int32, (1, LANES), 1) < HEAD_DIM
    edges = sorted(set([0, PROJ_COLS] + NORM_SLABS + [c + LANES for c in NORM_SLABS]))
    for lo, hi in zip(edges[:-1], edges[1:]):
        t = y[:, lo:hi]
        if lo in NORM_SLABS:
            sq = t * t
            s0 = jnp.sum(jnp.where(first, sq, 0.0), axis=-1, keepdims=True)
            s1 = jnp.sum(jnp.where(first, 0.0, sq), axis=-1, keepdims=True)
            ms = jnp.where(first, s0, s1) * (1.0 / HEAD_DIM)
            t = t * lax.rsqrt(ms + RMS_EPS) * hg_ref[:, lo:hi]
        o_ref[:, lo:hi] = t


def _inproj(xf, gain, w, head_gain, layer):
    n, d = xf.shape
    tm = 256
    return pl.pallas_call(
        _inproj_kernel,
        grid=(n // tm,),
        in_specs=[pl.BlockSpec((tm, d), lambda i: (i, 0)), _layer_spec(gain, layer),
                  _layer_spec(w, layer), _layer_spec(head_gain, layer)],
        out_specs=pl.BlockSpec((tm, PROJ_COLS), lambda i: (i, 0)),
        out_shape=jax.ShapeDtypeStruct((n, PROJ_COLS), F32),
        compiler_params=_params(("parallel",)),
        name="inproj",
    )(xf, gain, w, head_gain)


MOBA_HPS = 8


def _moba_kernel(q_ref, k_ref, v_ref, t0_ref, t1_ref, o_ref, kaug_s, vt_s, km_s, *, seq):
    n = pl.program_id(2)
    blk = MOBA_BLOCK
    nb = seq // blk
    head_cols = lambda hh: slice(hh * HEAD_DIM, (hh + 1) * HEAD_DIM)

    @pl.when(n == 0)
    def _prepare_keys():
        r = lax.broadcasted_iota(jnp.int32, (seq, LANES), 0)
        c = lax.broadcasted_iota(jnp.int32, (seq, LANES), 1)
        onehot = (c - HEAD_DIM == r // blk).astype(F32)
        place = _eye(HEAD_DIM, LANES)
        for hh in range(MOBA_HPS):
            kn = k_ref[:, head_cols(hh)]
            km = kn.reshape(nb, blk, HEAD_DIM).sum(axis=1) * (1.0 / blk)
            km_s[hh] = jnp.concatenate([km, jnp.zeros((16 - nb, HEAD_DIM), F32)], axis=0)
            kaug_s[hh] = (_dot(kn.astype(BF16), place) + onehot).astype(BF16)
            for j in range(nb):
                vj = v_ref[j * blk:(j + 1) * blk, head_cols(hh)].astype(BF16)
                vt_s[hh, j] = _value_tile(vj)

    jidx = lax.broadcasted_iota(jnp.int32, (16, blk), 0)
    past = jidx < n
    key = lax.broadcasted_iota(jnp.int32, (blk, blk), 0)
    qry = lax.broadcasted_iota(jnp.int32, (blk, blk), 1)
    prev = jnp.maximum(n - 1, 0)
    heads = range(MOBA_HPS)
    qts = []
    for pair in range(MOBA_HPS // 2):
        qp = _transpose_bf16(q_ref[:, pair * LANES:(pair + 1) * LANES].astype(BF16), LANES)
        qts += [qp[0:HEAD_DIM].astype(BF16), qp[HEAD_DIM:].astype(BF16)]
    gates = []
    for hh in heads:
        kmh, kml = _split_bf16(km_s[hh])
        gates.append(_dot(kmh, qts[hh]) + _dot(kml, qts[hh]))
    qaugs, state = [], []
    for hh in heads:
        gate = jnp.where(past, gates[hh], NEG)
        sel = (past & _rank_select(gate, jidx, MOBA_TOPK)) | (jidx == n)
        selb = jnp.where(sel, 0.0, NEG).astype(BF16)
        qaugs.append(jnp.concatenate([qts[hh], selb, jnp.zeros((LANES - HEAD_DIM - 16, blk), BF16)],
                                     axis=0))

    def scores(hh, j):
        start = pl.multiple_of(j * blk, blk)
        return _dot(kaug_s[hh, pl.ds(start, blk), :], qaugs[hh]), vt_s[hh, j]

    near = [(scores(hh, n), scores(hh, prev)) for hh in range(MOBA_HPS)]
    for hh, ((s0, vt0), (s1, vt1)) in enumerate(near):
        s0 = jnp.where(key <= qry, s0 + t0_ref[hh], NEG)
        s1 = jnp.where(n >= 1, s1 + t1_ref[hh], NEG)
        state.append(_softmax_joint([(s0, vt0), (s1, vt1)]))

    def far_body(j, carry):
        tiles = [scores(hh, j) for hh in range(MOBA_HPS)]
        return tuple(_softmax_update(carry[hh], tiles[hh][0], [tiles[hh][1]]) for hh in range(MOBA_HPS))

    state = lax.fori_loop(0, jnp.maximum(n - 1, 0), far_body, tuple(state))
    o_ref[...] = jnp.concatenate([_softmax_finish(c) for c in state], axis=0).T.astype(BF16)


def _moba(proj, t0, t1, batch, seq):
    blk = MOBA_BLOCK
    nb = seq // blk
    width = MOBA_HPS * HEAD_DIM
    qc, kc, vc = C_MQ // width, C_MK // width, C_MV // width
    return pl.pallas_call(
        functools.partial(_moba_kernel, seq=seq),
        grid=(batch, MOBA_HEADS // MOBA_HPS, nb),
        in_specs=[pl.BlockSpec((blk, width), lambda b, h, n: (b * nb + n, qc + h)),
                  pl.BlockSpec((seq, width), lambda b, h, n: (b, kc + h)),
                  pl.BlockSpec((seq, width), lambda b, h, n: (b, vc + h)),
                  pl.BlockSpec((MOBA_HPS, blk, blk), lambda b, h, n: (h, 0, 0)),
                  pl.BlockSpec((MOBA_HPS, blk, blk), lambda b, h, n: (h, 0, 0))],
        out_specs=pl.BlockSpec((blk, width), lambda b, h, n: (b * nb + n, h)),
        out_shape=jax.ShapeDtypeStruct((batch * seq, MOBA_HEADS * HEAD_DIM), BF16),
        scratch_shapes=[pltpu.VMEM((MOBA_HPS, seq, LANES), BF16),
                        pltpu.VMEM((MOBA_HPS, nb, VROWS, blk), BF16),
                        pltpu.VMEM((MOBA_HPS, 16, HEAD_DIM), F32)],
        compiler_params=_params(("parallel", "parallel", "arbitrary")),
        name="moba",
    )(proj, proj, proj, t0, t1)


def _compress_kernel(k_ref, v_ref, w1k_ref, w1v_ref, wtk_ref, wtv_ref, w2k_ref, w2v_ref, pk_ref, pv_ref,
                     kg_ref, ko_ref, vo_ref):
    hid = NSA_CMP_HIDDEN
    chunks = k_ref.shape[0] // NSA_CMP_STRIDE

    def compress(t_ref, w1_ref, wt_ref, w2_ref, pos_ref, g):
        a = functools.reduce(jnp.add, [
            _dot(t_ref[pl.ds(l, chunks, stride=NSA_CMP_STRIDE), :].astype(BF16), wt_ref[g, l])
            for l in range(NSA_CMP_STRIDE)])
        pw = _dot(pos_ref[...].astype(BF16), w1_ref[...])
        pos = pw[0:1, :hid] + pw[1:2, hid:]
        nxt = pltpu.roll(a[:, hid:], chunks - 1, 0)
        h = jax.nn.gelu(a[:, :hid] + nxt + pos)
        return _dot(h.astype(BF16), w2_ref[...])

    for g in range(NSA_KV_GROUPS):
        kc = compress(k_ref, w1k_ref, wtk_ref, w2k_ref, pk_ref, g)
        ko_ref[0, g] = _rms(kc, kg_ref[...]).astype(BF16)
        vc = compress(v_ref, w1v_ref, wtv_ref, w2v_ref, pv_ref, g).astype(BF16)
        vo_ref[0, g] = _transpose_bf16(vc, HEAD_DIM).astype(BF16)


def _compress(proj, w1k, w1v, wtk, wtv, w2k, w2v, pk, pv, kg, layer, batch, seq):
    groups = NSA_KV_GROUPS
    chunks = seq // NSA_CMP_STRIDE
    tok = lambda col: pl.BlockSpec((seq, LANES), lambda b: (b, col // LANES))
    return pl.pallas_call(
        _compress_kernel,
        grid=(batch,),
        in_specs=[tok(C_KC), tok(C_VC)] + [_layer_spec(a, layer)
                                           for a in (w1k, w1v, wtk, wtv, w2k, w2v, pk, pv, kg)],
        out_specs=[pl.BlockSpec((1, groups, chunks, HEAD_DIM), lambda b: (b, 0, 0, 0)),
                   pl.BlockSpec((1, groups, HEAD_DIM, chunks), lambda b: (b, 0, 0, 0))],
        out_shape=[jax.ShapeDtypeStruct((batch, groups, chunks, HEAD_DIM), BF16),
                   jax.ShapeDtypeStruct((batch, groups, HEAD_DIM, chunks), BF16)],
        compiler_params=_params(("parallel",)),
        name="nsa_compress",
    )(proj, proj, w1k, w1v, wtk, wtv, w2k, w2v, pk, pv, kg)


def _nsa_kernel(q_ref, kc_ref, vc_ref, ksl_ref, vsl_ref, kwn_ref, vwn_ref, gt_ref,
                wc_ref, t0_ref, t1_ref, o_ref, kslaug_s, vslt_s, kwn_s, vwnt_s, sc_s, *, seq):
    qi = pl.program_id(1)
    tq = NSA_TQ
    hpg = NSA_HPG
    sb = NSA_SEL_BLOCK
    nsel = seq // sb
    ncmp = seq // NSA_CMP_STRIDE
    nwin = NSA_WINDOW // tq
    t_start = qi * tq

    @pl.when(qi == 0)
    def _prepare_keys():
        r = lax.broadcasted_iota(jnp.int32, (seq, LANES), 0)
        c = lax.broadcasted_iota(jnp.int32, (seq, LANES), 1)
        onehot = (c - HEAD_DIM == r // sb).astype(F32)
        place = _eye(HEAD_DIM, LANES)
        sc_s[:, 0:CMP_PAD, :] = jnp.zeros((NSA_KV_GROUPS, CMP_PAD, hpg * tq), F32)
        for g in range(NSA_KV_GROUPS):
            sl = slice(g * HEAD_DIM, (g + 1) * HEAD_DIM)
            kslaug_s[g] = (_dot(ksl_ref[:, sl].astype(BF16), place) + onehot).astype(BF16)
            kwn_s[g] = kwn_ref[:, sl].astype(BF16)
            for j in range(seq // tq):
                rows = slice(j * tq, (j + 1) * tq)
                vslt_s[g, j] = _value_tile(vsl_ref[rows, sl].astype(BF16))
                vwnt_s[g, j] = _value_tile(vwn_ref[rows, sl].astype(BF16))

    lanes = lambda parts: jnp.concatenate(parts, axis=1)
    key = lax.broadcasted_iota(jnp.int32, (tq, tq), 0)
    qry = lax.broadcasted_iota(jnp.int32, (tq, tq), 1)
    causal = lanes([key <= qry] * hpg)
    upper = lanes([key > qry] * hpg)
    cend = lax.broadcasted_iota(jnp.int32, (ncmp, tq), 0) * NSA_CMP_STRIDE + (NSA_CMP_LEN - 1)
    tpos = lax.broadcasted_iota(jnp.int32, (ncmp, tq), 1) + t_start
    cvis = lanes([cend <= tpos] * hpg)
    oj = lax.broadcasted_iota(jnp.int32, (nsel, ncmp), 0) * sb
    on = lax.broadcasted_iota(jnp.int32, (nsel, ncmp), 1) * NSA_CMP_STRIDE
    overlap = ((on < oj + sb) & (on + NSA_CMP_LEN > oj) & (on < seq - NSA_CMP_STRIDE)).astype(BF16)
    jidx = lax.broadcasted_iota(jnp.int32, (nsel, tq), 0)
    own = (lax.broadcasted_iota(jnp.int32, (nsel, tq), 1) + t_start) // sb
    gates = jax.nn.sigmoid(gt_ref[...]).T
    tile_rows = lambda j: pl.ds(pl.multiple_of(j * tq, tq), tq)
    back = lambda k: jnp.maximum(qi - k, 0)

    groups = range(NSA_KV_GROUPS)
    heads = [[g * hpg + r for r in range(hpg)] for g in groups]
    t0 = [lanes([t0_ref[h] for h in heads[g]]) for g in groups]
    t1 = [lanes([t1_ref[h] for h in heads[g]]) for g in groups]

    qplain, win_scores = [], []
    for g in groups:
        width = hpg * HEAD_DIM
        qg = _transpose_bf16(q_ref[:, g * width:(g + 1) * width].astype(BF16), width)
        qplain.append(lanes([qg[r * HEAD_DIM:(r + 1) * HEAD_DIM] for r in range(hpg)]).astype(BF16))
        sc_s[g, CMP_PAD:, :] = _dot(kc_ref[0, g], qplain[g])
        win_scores.append([_dot(kwn_s[g, tile_rows(back(k)), :], qplain[g]) for k in range(nwin + 1)])

    o_cmps, imps = [], []
    for g in groups:
        win = pl.ds(pl.multiple_of(qi * (tq // NSA_CMP_STRIDE), 8), CMP_WIN)
        sc_s[g, win, :] += lanes([wc_ref[h] for h in heads[g]])
        s = jnp.where(cvis, sc_s[g, CMP_PAD:, :], NEG)
        m = jnp.max(s, axis=0, keepdims=True)
        e = jnp.where(cvis, jnp.exp2(s - m), 0.0)
        den = jnp.sum(e, axis=0, keepdims=True)
        p = e * (1.0 / jnp.where(den > 0, den, 1.0))
        o_cmps.append(_dot(vc_ref[0, g], p.astype(BF16)))
        psum = p[:, 0:tq]
        for r in range(1, hpg):
            psum = psum + p[:, r * tq:(r + 1) * tq]
        ph, plo = _split_bf16(psum)
        imps.append(_dot(overlap, ph) + _dot(overlap, plo))

    o_wins = []
    for g in groups:
        tiles = []
        for k, s in enumerate(win_scores[g]):
            if k == 0:
                s = jnp.where(causal, s + t0[g], NEG)
            elif k == 1:
                s = jnp.where(qi >= 1, s + t1[g], NEG)
            elif k < nwin:
                s = jnp.where(qi >= k, s, NEG)
            else:
                s = jnp.where(upper & (qi >= k), s, NEG)
            tiles.append((s, vwnt_s[g, back(k)]))
        o_wins.append(_softmax_finish(_softmax_joint(tiles)))

    qaugs = []
    for g in groups:
        imp = jnp.where((jidx == 0) | (jidx == own) | (jidx == own - 1), -NEG, imps[g])
        imp = jnp.where(jidx > own, NEG, imp)
        sel = _rank_select(imp, jidx, NSA_SEL_TOPN) & (jidx <= own)
        selb = jnp.where(sel, 0.0, NEG).astype(BF16)
        qaugs.append(jnp.concatenate([qplain[g], lanes([selb] * hpg),
                                      jnp.zeros((LANES - HEAD_DIM - nsel, hpg * tq), BF16)], axis=0))

    odd = (qi >= 2) & (qi % 2 == 0)
    slc_near = [[_dot(kslaug_s[g, tile_rows(back(k)), :], qaugs[g]) for k in range(3)] for g in groups]
    slc_state = tuple(_softmax_joint([
        (jnp.where(causal, slc_near[g][0] + t0[g], NEG), vslt_s[g, qi]),
        (jnp.where(qi >= 1, slc_near[g][1] + t1[g], NEG), vslt_s[g, back(1)]),
        (jnp.where(odd, slc_near[g][2], NEG), vslt_s[g, back(2)])]) for g in groups)

    def slc_far(i, carry):
        slab = pl.ds(pl.multiple_of(i * (2 * tq), 2 * tq), 2 * tq)
        scores = [_dot(kslaug_s[g, slab, :], qaugs[g]) for g in groups]
        return tuple(_softmax_update(carry[g], scores[g], [vslt_s[g, 2 * i], vslt_s[g, 2 * i + 1]])
                     for g in groups)

    slc_state = lax.fori_loop(0, jnp.maximum(qi - 1, 0) // 2, slc_far, slc_state)

    outs = []
    for g in range(NSA_KV_GROUPS):
        o_slc = _softmax_finish(slc_state[g])
        for r in range(hpg):
            h = g * hpg + r
            cols = slice(r * tq, (r + 1) * tq)
            outs.append(gates[h:h + 1, :] * o_cmps[g][:, cols]
                        + gates[NSA_HEADS + h:NSA_HEADS + h + 1, :] * o_slc[:, cols]
                        + gates[2 * NSA_HEADS + h:2 * NSA_HEADS + h + 1, :] * o_wins[g][:, cols])
    o_ref[...] = jnp.concatenate(outs, axis=0).T.astype(BF16)


def _nsa(proj, kcn, vct, wc, t0, t1, batch, seq):
    tq = NSA_TQ
    nq = seq // tq
    ncmp = seq // NSA_CMP_STRIDE
    width = NSA_HEADS * HEAD_DIM
    groups = NSA_KV_GROUPS
    kv = lambda col: pl.BlockSpec((seq, LANES), lambda b, i: (b, col // LANES))
    const = lambda shape: pl.BlockSpec(shape, lambda b, i: (0,) * len(shape))
    return pl.pallas_call(
        functools.partial(_nsa_kernel, seq=seq),
        grid=(batch, nq),
        in_specs=[pl.BlockSpec((tq, width), lambda b, i: (b * nq + i, C_NSQ // width)),
                  pl.BlockSpec((1, groups, ncmp, HEAD_DIM), lambda b, i: (b, 0, 0, 0)),
                  pl.BlockSpec((1, groups, HEAD_DIM, ncmp), lambda b, i: (b, 0, 0, 0)),
                  kv(C_KSL), kv(C_VSL), kv(C_KWN), kv(C_VWN),
                  pl.BlockSpec((tq, LANES), lambda b, i: (b * nq + i, C_NG // LANES)),
                  const((NSA_HEADS, CMP_WIN, tq)), const((NSA_HEADS, tq, tq)),
                  const((NSA_HEADS, tq, tq))],
        out_specs=pl.BlockSpec((tq, width), lambda b, i: (b * nq + i, 0)),
        out_shape=jax.ShapeDtypeStruct((batch * seq, width), BF16),
        scratch_shapes=[pltpu.VMEM((groups, seq, LANES), BF16),
                        pltpu.VMEM((groups, nq, VROWS, tq), BF16),
                        pltpu.VMEM((groups, seq, HEAD_DIM), BF16),
                        pltpu.VMEM((groups, nq, VROWS, tq), BF16),
                        pltpu.VMEM((groups, CMP_PAD + ncmp, NSA_HPG * tq), F32)],
        compiler_params=_params(("parallel", "arbitrary")),
        name="nsa",
    )(proj, kcn, vct, proj, proj, proj, proj, proj, wc, t0, t1)


def _merge_kernel(x_ref, ya_ref, yb_ref, ga_ref, gb_ref, wa_ref, wb_ref, wo_ref, o_ref):
    a = _dot(ya_ref[...], wa_ref[...])
    b = _dot(yb_ref[...], wb_ref[...])
    z = jax.nn.sigmoid(ga_ref[...]) * a + jax.nn.sigmoid(gb_ref[...]) * b
    o_ref[...] = x_ref[...] + _dot(z.astype(BF16), wo_ref[...])


def _merge(xf, ya, yb, proj, wa, wb, wo, layer):
    n, d = xf.shape
    tm = 512
    row = lambda w, col=0: pl.BlockSpec((tm, w), lambda i: (i, col))
    return pl.pallas_call(
        _merge_kernel,
        grid=(n // tm,),
        in_specs=[row(d), row(ya.shape[1]), row(yb.shape[1]), row(d, C_GA // d), row(d, C_GB // d),
                  _layer_spec(wa, layer), _layer_spec(wb, layer), _layer_spec(wo, layer)],
        out_specs=row(d),
        out_shape=jax.ShapeDtypeStruct((n, d), F32),
        compiler_params=_params(("parallel",)),
        name="merge",
    )(xf, ya, yb, proj, proj, wa, wb, wo)


FFN_HALO = 16
FFN_TM = 512
FFN_TF = 256


def _ffn_kernel(x_ref, xh_ref, g_ref, wu_ref, cw_ref, cb_ref, wd_ref, p_ref, wg_ref, wp_ref,
                o_ref, act_s, *, seq, d_ff):
    i = pl.program_id(0)
    tm = x_ref.shape[0]
    x = x_ref[...]
    at_start = (i * tm) % seq == 0
    halo = jnp.where(at_start, 0.0, _rms(xh_ref[...], g_ref[...]))
    hn = jnp.concatenate([halo.astype(BF16), _rms(x, g_ref[...]).astype(BF16)], axis=0)

    def conv(cols):
        u = _dot(hn, wu_ref[:, cols])
        u1 = pltpu.roll(u, 1, 0)[FFN_HALO:]
        u2 = pltpu.roll(u, 2, 0)[FFN_HALO:]
        cw = cw_ref[:, cols]
        return cw[0:1] * u2 + cw[1:2] * u1 + cw[2:3] * u[FFN_HALO:] + cb_ref[:, cols]

    for c in range(d_ff // FFN_TF):
        lo = c * FFN_TF
        act = jax.nn.gelu(conv(slice(lo, lo + FFN_TF))) * conv(slice(d_ff + lo, d_ff + lo + FFN_TF))
        act_s[:, lo:lo + FFN_TF] = act.astype(BF16)

    x = x + _dot(act_s[...], wd_ref[...])
    gate = jax.nn.sigmoid(_dot(x.astype(BF16), wg_ref[...]))
    o_ref[...] = x + gate * _dot(p_ref[...].astype(BF16), wp_ref[...])


def _ffn_ple(xf, gain, w_up, conv_w, conv_b, w_down, pf, wg, wp, layer, seq):
    n, d = xf.shape
    d_ff = w_down.shape[1]
    tm = FFN_TM
    hb = tm // FFN_HALO
    resident = lambda a: _layer_spec(a, layer)
    return pl.pallas_call(
        functools.partial(_ffn_kernel, seq=seq, d_ff=d_ff),
        grid=(n // tm,),
        in_specs=[pl.BlockSpec((tm, d), lambda i: (i, 0)),
                  pl.BlockSpec((FFN_HALO, d), lambda i: (jnp.maximum(i * hb - 1, 0), 0)),
                  resident(gain), resident(w_up), resident(conv_w), resident(conv_b), resident(w_down),
                  pl.BlockSpec((None, tm, pf.shape[2]), lambda i: (layer, i, 0)),
                  resident(wg), resident(wp)],
        out_specs=pl.BlockSpec((tm, d), lambda i: (i, 0)),
        out_shape=jax.ShapeDtypeStruct((n, d), F32),
        scratch_shapes=[pltpu.VMEM((tm, d_ff), BF16)],
        compiler_params=_params(("parallel",)),
        name="conv_ffn_ple",
    )(xf, xf, gain, w_up, conv_w, conv_b, w_down, pf, wg, wp)


def _reorder_in_proj(w):
    attn = 3 * MOBA_HEADS * HEAD_DIM + NSA_HEADS * HEAD_DIM + 6 * NSA_KV_GROUPS * HEAD_DIM
    ng = 3 * NSA_HEADS
    pad = jnp.zeros(w.shape[:-1] + (PROJ_COLS - w.shape[-1],), w.dtype)
    return jnp.concatenate([w[..., attn + ng:], w[..., :attn], w[..., attn:attn + ng], pad], axis=-1)


def _head_gain_rows(moba_q, moba_k, nsa_q, nsa_k):
    rows = jnp.zeros((moba_q.shape[0], 1, PROJ_COLS), F32)
    for col, gain, heads in ((C_MQ, moba_q * QSCALE, MOBA_HEADS), (C_MK, moba_k, MOBA_HEADS),
                             (C_NSQ, nsa_q * QSCALE, NSA_HEADS), (C_KSL, nsa_k[:, 1], NSA_KV_GROUPS),
                             (C_KWN, nsa_k[:, 2], NSA_KV_GROUPS)):
        rows = rows.at[:, 0, col:col + heads * HEAD_DIM].set(jnp.tile(gain, (1, heads)))
    return rows


def _cmp_weights(w1, pos):
    depth = w1.shape[0]
    half = NSA_CMP_STRIDE * HEAD_DIM
    w1cat = jnp.concatenate([w1[:, :half], w1[:, half:]], axis=2).astype(BF16)
    tok = w1cat.reshape(depth, NSA_CMP_STRIDE, HEAD_DIM, -1)
    zero = jnp.zeros_like(tok)
    placed = jnp.stack([jnp.concatenate([tok, zero], axis=2), jnp.concatenate([zero, tok], axis=2)], axis=1)
    posr = jnp.zeros((depth, 8, half), F32).at[:, 0:2].set(pos.reshape(depth, 2, half))
    return w1cat, placed, posr


def kernel(x, p, rel_bias, attn_norm, w_in, moba_q_gain, moba_k_gain, nsa_q_gain, nsa_k_gain,
           cmp_pos_k, cmp_w1_k, cmp_w2_k, cmp_pos_v, cmp_w1_v, cmp_w2_v,
           w_br_moba, w_br_nsa, w_o, ffn_norm, w_up, conv_w, conv_b, w_down, w_ple_gate, w_ple):
    batch, seq, d = x.shape
    n = batch * seq
    depth = w_in.shape[0]
    bf = lambda a: a.astype(BF16)
    w_in_r = bf(_reorder_in_proj(w_in))
    head_gain = _head_gain_rows(moba_q_gain, moba_k_gain, nsa_q_gain, nsa_k_gain)
    attn_gain, ffn_gain = attn_norm[:, None, :], ffn_norm[:, None, :]
    w1k, wtk, pk = _cmp_weights(cmp_w1_k, cmp_pos_k)
    w1v, wtv, pv = _cmp_weights(cmp_w1_v, cmp_pos_v)
    w2k, w2v, kg0 = bf(cmp_w2_k), bf(cmp_w2_v), nsa_k_gain[:, 0:1]
    wa, wb, wo = bf(w_br_moba), bf(w_br_nsa), bf(w_o)
    wu, wd, wg, wp = bf(w_up), bf(w_down), bf(w_ple_gate), bf(w_ple)
    conv_b = conv_b[:, None, :]
    pf = p.reshape(depth, n, -1)

    t0m, t1m, t0n, t1n, wc = _bias_tables(rel_bias)
    xf = x.reshape(n, d)
    for i in range(depth):
        proj = _inproj(xf, attn_gain, w_in_r, head_gain, i)
        ya = _moba(proj, t0m, t1m, batch, seq)
        kcn, vct = _compress(proj, w1k, w1v, wtk, wtv, w2k, w2v, pk, pv, kg0, i, batch, seq)
        yb = _nsa(proj, kcn, vct, wc, t0n, t1n, batch, seq)
        xf = _merge(xf, ya, yb, proj, wa, wb, wo, i)
        xf = _ffn_ple(xf, ffn_gain, wu, conv_w, conv_b, wd, pf, wg, wp, i, seq)
    return xf.reshape(batch, seq, d)
```

```python
import functools
import math

import numpy as np
import jax
import jax.numpy as jnp
from jax import lax
from jax.experimental import pallas as pl
from jax.experimental.pallas import tpu as pltpu

F32 = jnp.float32
BF16 = jnp.bfloat16

HEAD_DIM = 64
MOBA_HEADS = 8
MOBA_BLOCK = 256
MOBA_TOPK = 3
NSA_HEADS = 8
NSA_KV_GROUPS = 2
NSA_HPG = NSA_HEADS // NSA_KV_GROUPS
NSA_CMP_LEN = 32
NSA_CMP_STRIDE = 16
NSA_CMP_HIDDEN = 2 * HEAD_DIM
NSA_SEL_BLOCK = 64
NSA_SEL_TOPN = 16
NSA_WINDOW = 512
REL_BUCKETS = 32
REL_MAX_DIST = 128
CONV_W = 3
RMS_EPS = 1e-6
SCALE = HEAD_DIM ** -0.5
LOG2E = math.log2(math.e)
QSCALE = SCALE * LOG2E
NEG = -1e30

LANES = 128
VMEM_LIMIT = 56 * 1024 * 1024

C_GA, C_GB, C_MQ, C_MK, C_MV, C_NSQ = 0, 1024, 2048, 2560, 3072, 3584
C_KC, C_VC, C_KSL, C_VSL, C_KWN, C_VWN, C_NG = 4096, 4224, 4352, 4480, 4608, 4736, 4864
PROJ_COLS = 4992

NSA_TQ = 128
CMP_WIN = 16
CMP_PAD = 8


def _dot(a, b):
    return jnp.dot(a, b, preferred_element_type=F32)


def _dot_nt(a, b):
    return lax.dot_general(a, b, (((1,), (1,)), ((), ())), preferred_element_type=F32)


def _rms(x, gain):
    return x * lax.rsqrt(jnp.mean(x * x, axis=-1, keepdims=True) + RMS_EPS) * gain


def _split_bf16(x):
    hi = x.astype(BF16)
    return hi, (x - hi.astype(F32)).astype(BF16)


def _eye(rows, cols):
    r = lax.broadcasted_iota(jnp.int32, (rows, cols), 0)
    c = lax.broadcasted_iota(jnp.int32, (rows, cols), 1)
    return (r == c).astype(BF16)


def _transpose_bf16(x, rows):
    return _dot_nt(_eye(rows, x.shape[1]), x)


VROWS = 80


def _value_tile(v):
    row = lax.broadcasted_iota(jnp.int32, (VROWS, v.shape[0]), 0)
    return jnp.where(row == HEAD_DIM, 1.0, _transpose_bf16(v, VROWS)).astype(BF16)


def _softmax_joint(tiles):
    m = functools.reduce(jnp.maximum, [jnp.max(s, axis=0, keepdims=True) for s, _ in tiles])
    acc = functools.reduce(jnp.add, [_dot(vt, jnp.exp2(s - m).astype(BF16)) for s, vt in tiles])
    return m, acc


def _softmax_update(carry, s, vts):
    m, acc = carry
    m_new = jnp.maximum(m, jnp.max(s, axis=0, keepdims=True))
    p = jnp.exp2(s - m_new).astype(BF16)
    rows = s.shape[0] // len(vts)
    pv = functools.reduce(jnp.add, [_dot(vt, p[i * rows:(i + 1) * rows]) for i, vt in enumerate(vts)])
    return m_new, jnp.exp2(m - m_new) * acc + pv


def _softmax_finish(carry):
    _, acc = carry
    return acc[0:HEAD_DIM] * (1.0 / acc[HEAD_DIM:HEAD_DIM + 1])


def _rank_select(score, idx, count):
    beaten = jnp.zeros(score.shape, jnp.int32)
    for i in range(score.shape[0]):
        si = score[i:i + 1, :]
        beaten += ((si > score) | ((si == score) & (i < idx))).astype(jnp.int32)
    return beaten < count


def _params(sem):
    return pltpu.CompilerParams(dimension_semantics=sem, vmem_limit_bytes=VMEM_LIMIT)


def _layer_spec(a, layer):
    return pl.BlockSpec((None,) + a.shape[1:], lambda *_: (layer,) + (0,) * (a.ndim - 1),
                        pipeline_mode=pl.Buffered(1))


def _rel_bucket_np(dist):
    n = np.maximum(dist, 0)
    max_exact = REL_BUCKETS // 2
    nf = np.maximum(n, 1).astype(np.float32)
    large = max_exact + (np.log(nf / np.float32(max_exact)) / np.float32(math.log(REL_MAX_DIST / max_exact))
                         * np.float32(REL_BUCKETS - max_exact)).astype(np.int32)
    return np.where(n < max_exact, n, np.minimum(large, REL_BUCKETS - 1))


def _bucket_starts():
    buckets = _rel_bucket_np(np.arange(4 * REL_MAX_DIST))
    return [int(np.argmax(buckets >= k)) for k in range(REL_BUCKETS)]


BUCKET_START = _bucket_starts()
BIAS_REACH = BUCKET_START[-1]
assert BIAS_REACH <= MOBA_BLOCK and BIAS_REACH <= NSA_TQ - NSA_CMP_LEN + 1 + NSA_CMP_STRIDE


def _tables_kernel(tab_ref, t0m_ref, t1m_ref, t0n_ref, t1n_ref, wc_ref):
    h = pl.program_id(0)

    def bias(dist, head):
        last = tab_ref[head, REL_BUCKETS - 1]
        val = jnp.zeros(dist.shape, F32)
        for k in range(REL_BUCKETS - 2, -1, -1):
            val = jnp.where(dist < BUCKET_START[k + 1], (tab_ref[head, k] - last) * LOG2E, val)
        return val

    def toeplitz(size, offset, head):
        key = lax.broadcasted_iota(jnp.int32, (size, size), 0)
        qry = lax.broadcasted_iota(jnp.int32, (size, size), 1)
        return bias(offset + qry - key, head)

    t0m_ref[0] = toeplitz(MOBA_BLOCK, 0, h)
    t1m_ref[0] = toeplitz(MOBA_BLOCK, MOBA_BLOCK, h)
    t0n_ref[0] = toeplitz(NSA_TQ, 0, MOBA_HEADS + h)
    t1n_ref[0] = toeplitz(NSA_TQ, NSA_TQ, MOBA_HEADS + h)
    a = lax.broadcasted_iota(jnp.int32, (CMP_WIN, NSA_TQ), 0)
    i = lax.broadcasted_iota(jnp.int32, (CMP_WIN, NSA_TQ), 1)
    wc_ref[0] = bias(i + (NSA_TQ - NSA_CMP_LEN + 1) - NSA_CMP_STRIDE * a, MOBA_HEADS + h)


def _bias_tables(rel_bias):
    blk, tq = MOBA_BLOCK, NSA_TQ
    shapes = [(blk, blk), (blk, blk), (tq, tq), (tq, tq), (CMP_WIN, tq)]
    return pl.pallas_call(
        _tables_kernel,
        grid=(MOBA_HEADS,),
        in_specs=[pl.BlockSpec(memory_space=pltpu.SMEM)],
        out_specs=[pl.BlockSpec((1,) + s, lambda h: (h, 0, 0)) for s in shapes],
        out_shape=[jax.ShapeDtypeStruct((MOBA_HEADS,) + s, F32) for s in shapes],
        compiler_params=_params(("arbitrary",)),
        name="bias_tables",
    )(rel_bias)


NORM_SLABS = (list(range(C_MQ, C_MV, LANES)) + list(range(C_NSQ, C_KC, LANES)) + [C_KSL, C_KWN])


def _inproj_kernel(x_ref, g_ref, w_ref, hg_ref, o_ref):
    h = _rms(x_ref[...], g_ref[...]).astype(BF16)
    y = _dot(h, w_ref[...])
    first = lax.broadcasted_iota(jnp.int32, (1, LANES), 1) < HEAD_DIM
    edges = sorted(set([0, PROJ_COLS] + NORM_SLABS + [c + LANES for c in NORM_SLABS]))
    for lo, hi in zip(edges[:-1], edges[1:]):
        t = y[:, lo:hi]
        if lo in NORM_SLABS:
            sq = t * t
            s0 = jnp.sum(jnp.where(first, sq, 0.0), axis=-1, keepdims=True)
            s1 = jnp.sum(jnp.where(first, 0.0, sq), axis=-1, keepdims=True)
            ms = jnp.where(first, s0, s1) * (1.0 / HEAD_DIM)
            t = t * lax.rsqrt(ms + RMS_EPS) * hg_ref[:, lo:hi]
        o_ref[:, lo:hi] = t


def _inproj(xf, gain, w, head_gain, layer):
    n, d = xf.shape
    tm = 256
    return pl.pallas_call(
        _inproj_kernel,
        grid=(n // tm,),
        in_specs=[pl.BlockSpec((tm, d), lambda i: (i, 0)), _layer_spec(gain, layer),
                  _layer_spec(w, layer), _layer_spec(head_gain, layer)],
        out_specs=pl.BlockSpec((tm, PROJ_COLS), lambda i: (i, 0)),
        out_shape=jax.ShapeDtypeStruct((n, PROJ_COLS), F32),
        compiler_params=_params(("parallel",)),
        name="inproj",
    )(xf, gain, w, head_gain)


MOBA_HPS = 8


def _moba_kernel(q_ref, k_ref, v_ref, t0_ref, t1_ref, o_ref, kaug_s, vt_s, km_s, *, seq):
    n = pl.program_id(2)
    blk = MOBA_BLOCK
    nb = seq // blk
    head_cols = lambda hh: slice(hh * HEAD_DIM, (hh + 1) * HEAD_DIM)

    @pl.when(n == 0)
    def _prepare_keys():
        r = lax.broadcasted_iota(jnp.int32, (seq, LANES), 0)
        c = lax.broadcasted_iota(jnp.int32, (seq, LANES), 1)
        onehot = (c - HEAD_DIM == r // blk).astype(F32)
        place = _eye(HEAD_DIM, LANES)
        for hh in range(MOBA_HPS):
            kn = k_ref[:, head_cols(hh)]
            km = kn.reshape(nb, blk, HEAD_DIM).sum(axis=1) * (1.0 / blk)
            km_s[hh] = jnp.concatenate([km, jnp.zeros((16 - nb, HEAD_DIM), F32)], axis=0)
            kaug_s[hh] = (_dot(kn.astype(BF16), place) + onehot).astype(BF16)
            for j in range(nb):
                vj = v_ref[j * blk:(j + 1) * blk, head_cols(hh)].astype(BF16)
                vt_s[hh, j] = _value_tile(vj)

    jidx = lax.broadcasted_iota(jnp.int32, (16, blk), 0)
    past = jidx < n
    key = lax.broadcasted_iota(jnp.int32, (blk, blk), 0)
    qry = lax.broadcasted_iota(jnp.int32, (blk, blk), 1)
    prev = jnp.maximum(n - 1, 0)
    heads = range(MOBA_HPS)
    qts = []
    for pair in range(MOBA_HPS // 2):
        qp = _transpose_bf16(q_ref[:, pair * LANES:(pair + 1) * LANES].astype(BF16), LANES)
        qts += [qp[0:HEAD_DIM].astype(BF16), qp[HEAD_DIM:].astype(BF16)]
    gates = []
    for hh in heads:
        kmh, kml = _split_bf16(km_s[hh])
        gates.append(_dot(kmh, qts[hh]) + _dot(kml, qts[hh]))
    qaugs, state = [], []
    for hh in heads:
        gate = jnp.where(past, gates[hh], NEG)
        sel = (past & _rank_select(gate, jidx, MOBA_TOPK)) | (jidx == n)
        selb = jnp.where(sel, 0.0, NEG).astype(BF16)
        qaugs.append(jnp.concatenate([qts[hh], selb, jnp.zeros((LANES - HEAD_DIM - 16, blk), BF16)],
                                     axis=0))

    def scores(hh, j):
        start = pl.multiple_of(j * blk, blk)
        return _dot(kaug_s[hh, pl.ds(start, blk), :], qaugs[hh]), vt_s[hh, j]

    near = [(scores(hh, n), scores(hh, prev)) for hh in range(MOBA_HPS)]
    for hh, ((s0, vt0), (s1, vt1)) in enumerate(near):
        s0 = jnp.where(key <= qry, s0, NEG) + t0_ref[hh]
        s1 = jnp.where(n >= 1, s1, NEG) + t1_ref[hh]
        state.append(_softmax_joint([(s0, vt0), (s1, vt1)]))

    def far_body(j, carry):
        tiles = [scores(hh, j) for hh in range(MOBA_HPS)]
        return tuple(_softmax_update(carry[hh], tiles[hh][0], [tiles[hh][1]]) for hh in range(MOBA_HPS))

    state = lax.fori_loop(0, jnp.maximum(n - 1, 0), far_body, tuple(state))
    o_ref[...] = jnp.concatenate([_softmax_finish(c) for c in state], axis=0).T.astype(BF16)


def _moba(proj, t0, t1, batch, seq):
    blk = MOBA_BLOCK
    nb = seq // blk
    width = MOBA_HPS * HEAD_DIM
    qc, kc, vc = C_MQ // width, C_MK // width, C_MV // width
    return pl.pallas_call(
        functools.partial(_moba_kernel, seq=seq),
        grid=(batch, MOBA_HEADS // MOBA_HPS, nb),
        in_specs=[pl.BlockSpec((blk, width), lambda b, h, n: (b * nb + n, qc + h)),
                  pl.BlockSpec((seq, width), lambda b, h, n: (b, kc + h)),
                  pl.BlockSpec((seq, width), lambda b, h, n: (b, vc + h)),
                  pl.BlockSpec((MOBA_HPS, blk, blk), lambda b, h, n: (h, 0, 0)),
                  pl.BlockSpec((MOBA_HPS, blk, blk), lambda b, h, n: (h, 0, 0))],
        out_specs=pl.BlockSpec((blk, width), lambda b, h, n: (b * nb + n, h)),
        out_shape=jax.ShapeDtypeStruct((batch * seq, MOBA_HEADS * HEAD_DIM), BF16),
        scratch_shapes=[pltpu.VMEM((MOBA_HPS, seq, LANES), BF16),
                        pltpu.VMEM((MOBA_HPS, nb, VROWS, blk), BF16),
                        pltpu.VMEM((MOBA_HPS, 16, HEAD_DIM), F32)],
        compiler_params=_params(("parallel", "parallel", "arbitrary")),
        name="moba",
    )(proj, proj, proj, t0, t1)


def _compress_kernel(k_ref, v_ref, w1k_ref, w1v_ref, wtk_ref, wtv_ref, w2k_ref, w2v_ref, pk_ref, pv_ref,
                     kg_ref, ko_ref, vo_ref):
    hid = NSA_CMP_HIDDEN
    chunks = k_ref.shape[0] // NSA_CMP_STRIDE

    def compress(t_ref, w1_ref, wt_ref, w2_ref, pos_ref, g):
        a = functools.reduce(jnp.add, [
            _dot(t_ref[pl.ds(l, chunks, stride=NSA_CMP_STRIDE), :].astype(BF16), wt_ref[g, l])
            for l in range(NSA_CMP_STRIDE)])
        pw = _dot(pos_ref[...].astype(BF16), w1_ref[...])
        pos = pw[0:1, :hid] + pw[1:2, hid:]
        nxt = pltpu.roll(a[:, hid:], chunks - 1, 0)
        h = jax.nn.gelu(a[:, :hid] + nxt + pos)
        return _dot(h.astype(BF16), w2_ref[...])

    for g in range(NSA_KV_GROUPS):
        kc = compress(k_ref, w1k_ref, wtk_ref, w2k_ref, pk_ref, g)
        ko_ref[0, g] = _rms(kc, kg_ref[...]).astype(BF16)
        vc = compress(v_ref, w1v_ref, wtv_ref, w2v_ref, pv_ref, g).astype(BF16)
        vo_ref[0, g] = _transpose_bf16(vc, HEAD_DIM).astype(BF16)


def _compress(proj, w1k, w1v, wtk, wtv, w2k, w2v, pk, pv, kg, layer, batch, seq):
    groups = NSA_KV_GROUPS
    chunks = seq // NSA_CMP_STRIDE
    tok = lambda col: pl.BlockSpec((seq, LANES), lambda b: (b, col // LANES))
    return pl.pallas_call(
        _compress_kernel,
        grid=(batch,),
        in_specs=[tok(C_KC), tok(C_VC)] + [_layer_spec(a, layer)
                                           for a in (w1k, w1v, wtk, wtv, w2k, w2v, pk, pv, kg)],
        out_specs=[pl.BlockSpec((1, groups, chunks, HEAD_DIM), lambda b: (b, 0, 0, 0)),
                   pl.BlockSpec((1, groups, HEAD_DIM, chunks), lambda b: (b, 0, 0, 0))],
        out_shape=[jax.ShapeDtypeStruct((batch, groups, chunks, HEAD_DIM), BF16),
                   jax.ShapeDtypeStruct((batch, groups, HEAD_DIM, chunks), BF16)],
        compiler_params=_params(("parallel",)),
        name="nsa_compress",
    )(proj, proj, w1k, w1v, wtk, wtv, w2k, w2v, pk, pv, kg)


def _nsa_kernel(q_ref, kc_ref, vc_ref, ksl_ref, vsl_ref, kwn_ref, vwn_ref, gt_ref,
                wc_ref, t0_ref, t1_ref, o_ref, kslaug_s, vslt_s, kwn_s, vwnt_s, sc_s, *, seq):
    qi = pl.program_id(1)
    tq = NSA_TQ
    hpg = NSA_HPG
    sb = NSA_SEL_BLOCK
    nsel = seq // sb
    ncmp = seq // NSA_CMP_STRIDE
    nwin = NSA_WINDOW // tq
    t_start = qi * tq

    @pl.when(qi == 0)
    def _prepare_keys():
        r = lax.broadcasted_iota(jnp.int32, (seq, LANES), 0)
        c = lax.broadcasted_iota(jnp.int32, (seq, LANES), 1)
        onehot = (c - HEAD_DIM == r // sb).astype(F32)
        place = _eye(HEAD_DIM, LANES)
        sc_s[:, 0:CMP_PAD, :] = jnp.zeros((NSA_KV_GROUPS, CMP_PAD, hpg * tq), F32)
        for g in range(NSA_KV_GROUPS):
            sl = slice(g * HEAD_DIM, (g + 1) * HEAD_DIM)
            kslaug_s[g] = (_dot(ksl_ref[:, sl].astype(BF16), place) + onehot).astype(BF16)
            kwn_s[g] = kwn_ref[:, sl].astype(BF16)
            for j in range(seq // tq):
                rows = slice(j * tq, (j + 1) * tq)
                vslt_s[g, j] = _value_tile(vsl_ref[rows, sl].astype(BF16))
                vwnt_s[g, j] = _value_tile(vwn_ref[rows, sl].astype(BF16))

    lanes = lambda parts: jnp.concatenate(parts, axis=1)
    key = lax.broadcasted_iota(jnp.int32, (tq, tq), 0)
    qry = lax.broadcasted_iota(jnp.int32, (tq, tq), 1)
    causal = lanes([key <= qry] * hpg)
    upper = lanes([key > qry] * hpg)
    cend = lax.broadcasted_iota(jnp.int32, (ncmp, tq), 0) * NSA_CMP_STRIDE + (NSA_CMP_LEN - 1)
    tpos = lax.broadcasted_iota(jnp.int32, (ncmp, tq), 1) + t_start
    cvis = lanes([cend <= tpos] * hpg)
    oj = lax.broadcasted_iota(jnp.int32, (nsel, ncmp), 0) * sb
    on = lax.broadcasted_iota(jnp.int32, (nsel, ncmp), 1) * NSA_CMP_STRIDE
    overlap = ((on < oj + sb) & (on + NSA_CMP_LEN > oj) & (on < seq - NSA_CMP_STRIDE)).astype(BF16)
    jidx = lax.broadcasted_iota(jnp.int32, (nsel, tq), 0)
    own = (lax.broadcasted_iota(jnp.int32, (nsel, tq), 1) + t_start) // sb
    gates = jax.nn.sigmoid(gt_ref[...]).T
    tile_rows = lambda j: pl.ds(pl.multiple_of(j * tq, tq), tq)
    back = lambda k: jnp.maximum(qi - k, 0)

    groups = range(NSA_KV_GROUPS)
    heads = [[g * hpg + r for r in range(hpg)] for g in groups]
    t0 = [lanes([t0_ref[h] for h in heads[g]]) for g in groups]
    t1 = [lanes([t1_ref[h] for h in heads[g]]) for g in groups]

    qplain, win_scores = [], []
    for g in groups:
        width = hpg * HEAD_DIM
        qg = _transpose_bf16(q_ref[:, g * width:(g + 1) * width].astype(BF16), width)
        qplain.append(lanes([qg[r * HEAD_DIM:(r + 1) * HEAD_DIM] for r in range(hpg)]).astype(BF16))
        sc_s[g, CMP_PAD:, :] = _dot(kc_ref[0, g], qplain[g])
        win_scores.append([_dot(kwn_s[g, tile_rows(back(k)), :], qplain[g]) for k in range(nwin + 1)])

    o_cmps, imps = [], []
    for g in groups:
        win = pl.ds(pl.multiple_of(qi * (tq // NSA_CMP_STRIDE), 8), CMP_WIN)
        sc_s[g, win, :] += lanes([wc_ref[h] for h in heads[g]])
        s = jnp.where(cvis, sc_s[g, CMP_PAD:, :], NEG)
        m = jnp.max(s, axis=0, keepdims=True)
        e = jnp.where(cvis, jnp.exp2(s - m), 0.0)
        den = jnp.sum(e, axis=0, keepdims=True)
        p = e * (1.0 / jnp.where(den > 0, den, 1.0))
        o_cmps.append(_dot(vc_ref[0, g], p.astype(BF16)))
        psum = p[:, 0:tq]
        for r in range(1, hpg):
            psum = psum + p[:, r * tq:(r + 1) * tq]
        ph, plo = _split_bf16(psum)
        imps.append(_dot(overlap, ph) + _dot(overlap, plo))

    o_wins = []
    for g in groups:
        tiles = []
        for k, s in enumerate(win_scores[g]):
            if k == 0:
                s = jnp.where(causal, s, NEG) + t0[g]
            elif k == 1:
                s = jnp.where(qi >= 1, s, NEG) + t1[g]
            elif k < nwin:
                s = jnp.where(qi >= k, s, NEG)
            else:
                s = jnp.where(upper & (qi >= k), s, NEG)
            tiles.append((s, vwnt_s[g, back(k)]))
        o_wins.append(_softmax_finish(_softmax_joint(tiles)))

    qaugs = []
    for g in groups:
        imp = jnp.where((jidx == 0) | (jidx == own) | (jidx == own - 1), -NEG, imps[g])
        imp = jnp.where(jidx > own, NEG, imp)
        sel = _rank_select(imp, jidx, NSA_SEL_TOPN) & (jidx <= own)
        selb = jnp.where(sel, 0.0, NEG).astype(BF16)
        qaugs.append(jnp.concatenate([qplain[g], lanes([selb] * hpg),
                                      jnp.zeros((LANES - HEAD_DIM - nsel, hpg * tq), BF16)], axis=0))

    odd = (qi >= 2) & (qi % 2 == 0)
    slc_near = [[_dot(kslaug_s[g, tile_rows(back(k)), :], qaugs[g]) for k in range(3)] for g in groups]
    slc_state = tuple(_softmax_joint([
        (jnp.where(causal, slc_near[g][0], NEG) + t0[g], vslt_s[g, qi]),
        (jnp.where(qi >= 1, slc_near[g][1], NEG) + t1[g], vslt_s[g, back(1)]),
        (jnp.where(odd, slc_near[g][2], NEG), vslt_s[g, back(2)])]) for g in groups)

    def slc_far(i, carry):
        slab = pl.ds(pl.multiple_of(i * (2 * tq), 2 * tq), 2 * tq)
        scores = [_dot(kslaug_s[g, slab, :], qaugs[g]) for g in groups]
        return tuple(_softmax_update(carry[g], scores[g], [vslt_s[g, 2 * i], vslt_s[g, 2 * i + 1]])
                     for g in groups)

    slc_state = lax.fori_loop(0, jnp.maximum(qi - 1, 0) // 2, slc_far, slc_state)

    outs = []
    for g in range(NSA_KV_GROUPS):
        o_slc = _softmax_finish(slc_state[g])
        for r in range(hpg):
            h = g * hpg + r
            cols = slice(r * tq, (r + 1) * tq)
            outs.append(gates[h:h + 1, :] * o_cmps[g][:, cols]
                        + gates[NSA_HEADS + h:NSA_HEADS + h + 1, :] * o_slc[:, cols]
                        + gates[2 * NSA_HEADS + h:2 * NSA_HEADS + h + 1, :] * o_wins[g][:, cols])
    o_ref[...] = jnp.concatenate(outs, axis=0).T.astype(BF16)


def _nsa(proj, kcn, vct, wc, t0, t1, batch, seq):
    tq = NSA_TQ
    nq = seq // tq
    ncmp = seq // NSA_CMP_STRIDE
    width = NSA_HEADS * HEAD_DIM
    groups = NSA_KV_GROUPS
    kv = lambda col: pl.BlockSpec((seq, LANES), lambda b, i: (b, col // LANES))
    const = lambda shape: pl.BlockSpec(shape, lambda b, i: (0,) * len(shape))
    return pl.pallas_call(
        functools.partial(_nsa_kernel, seq=seq),
        grid=(batch, nq),
        in_specs=[pl.BlockSpec((tq, width), lambda b, i: (b * nq + i, C_NSQ // width)),
                  pl.BlockSpec((1, groups, ncmp, HEAD_DIM), lambda b, i: (b, 0, 0, 0)),
                  pl.BlockSpec((1, groups, HEAD_DIM, ncmp), lambda b, i: (b, 0, 0, 0)),
                  kv(C_KSL), kv(C_VSL), kv(C_KWN), kv(C_VWN),
                  pl.BlockSpec((tq, LANES), lambda b, i: (b * nq + i, C_NG // LANES)),
                  const((NSA_HEADS, CMP_WIN, tq)), const((NSA_HEADS, tq, tq)),
                  const((NSA_HEADS, tq, tq))],
        out_specs=pl.BlockSpec((tq, width), lambda b, i: (b * nq + i, 0)),
        out_shape=jax.ShapeDtypeStruct((batch * seq, width), BF16),
        scratch_shapes=[pltpu.VMEM((groups, seq, LANES), BF16),
                        pltpu.VMEM((groups, nq, VROWS, tq), BF16),
                        pltpu.VMEM((groups, seq, HEAD_DIM), BF16),
                        pltpu.VMEM((groups, nq, VROWS, tq), BF16),
                        pltpu.VMEM((groups, CMP_PAD + ncmp, NSA_HPG * tq), F32)],
        compiler_params=_params(("parallel", "arbitrary")),
        name="nsa",
    )(proj, kcn, vct, proj, proj, proj, proj, proj, wc, t0, t1)


def _merge_kernel(x_ref, ya_ref, yb_ref, ga_ref, gb_ref, wa_ref, wb_ref, wo_ref, o_ref):
    a = _dot(ya_ref[...], wa_ref[...])
    b = _dot(yb_ref[...], wb_ref[...])
    z = jax.nn.sigmoid(ga_ref[...]) * a + jax.nn.sigmoid(gb_ref[...]) * b
    o_ref[...] = x_ref[...] + _dot(z.astype(BF16), wo_ref[...])


def _merge(xf, ya, yb, proj, wa, wb, wo, layer):
    n, d = xf.shape
    tm = 512
    row = lambda w, col=0: pl.BlockSpec((tm, w), lambda i: (i, col))
    return pl.pallas_call(
        _merge_kernel,
        grid=(n // tm,),
        in_specs=[row(d), row(ya.shape[1]), row(yb.shape[1]), row(d, C_GA // d), row(d, C_GB // d),
                  _layer_spec(wa, layer), _layer_spec(wb, layer), _layer_spec(wo, layer)],
        out_specs=row(d),
        out_shape=jax.ShapeDtypeStruct((n, d), F32),
        compiler_params=_params(("parallel",)),
        name="merge",
    )(xf, ya, yb, proj, proj, wa, wb, wo)


FFN_HALO = 16
FFN_TM = 512
FFN_TF = 256


def _ffn_kernel(x_ref, xh_ref, g_ref, wu_ref, cw_ref, cb_ref, wd_ref, p_ref, wg_ref, wp_ref,
                o_ref, act_s, *, seq, d_ff):
    i = pl.program_id(0)
    tm = x_ref.shape[0]
    x = x_ref[...]
    at_start = (i * tm) % seq == 0
    halo = jnp.where(at_start, 0.0, _rms(xh_ref[...], g_ref[...]))
    hn = jnp.concatenate([halo.astype(BF16), _rms(x, g_ref[...]).astype(BF16)], axis=0)

    def conv(cols):
        u = _dot(hn, wu_ref[:, cols])
        u1 = pltpu.roll(u, 1, 0)[FFN_HALO:]
        u2 = pltpu.roll(u, 2, 0)[FFN_HALO:]
        cw = cw_ref[:, cols]
        return cw[0:1] * u2 + cw[1:2] * u1 + cw[2:3] * u[FFN_HALO:] + cb_ref[:, cols]

    for c in range(d_ff // FFN_TF):
        lo = c * FFN_TF
        act = jax.nn.gelu(conv(slice(lo, lo + FFN_TF))) * conv(slice(d_ff + lo, d_ff + lo + FFN_TF))
        act_s[:, lo:lo + FFN_TF] = act.astype(BF16)

    x = x + _dot(act_s[...], wd_ref[...])
    gate = jax.nn.sigmoid(_dot(x.astype(BF16), wg_ref[...]))
    o_ref[...] = x + gate * _dot(p_ref[...].astype(BF16), wp_ref[...])


def _ffn_ple(xf, gain, w_up, conv_w, conv_b, w_down, pf, wg, wp, layer, seq):
    n, d = xf.shape
    d_ff = w_down.shape[1]
    tm = FFN_TM
    hb = tm // FFN_HALO
    resident = lambda a: _layer_spec(a, layer)
    return pl.pallas_call(
        functools.partial(_ffn_kernel, seq=seq, d_ff=d_ff),
        grid=(n // tm,),
        in_specs=[pl.BlockSpec((tm, d), lambda i: (i, 0)),
                  pl.BlockSpec((FFN_HALO, d), lambda i: (jnp.maximum(i * hb - 1, 0), 0)),
                  resident(gain), resident(w_up), resident(conv_w), resident(conv_b), resident(w_down),
                  pl.BlockSpec((None, tm, pf.shape[2]), lambda i: (layer, i, 0)),
                  resident(wg), resident(wp)],
        out_specs=pl.BlockSpec((tm, d), lambda i: (i, 0)),
        out_shape=jax.ShapeDtypeStruct((n, d), F32),
        scratch_shapes=[pltpu.VMEM((tm, d_ff), BF16)],
        compiler_params=_params(("parallel",)),
        name="conv_ffn_ple",
    )(xf, xf, gain, w_up, conv_w, conv_b, w_down, pf, wg, wp)


def _reorder_in_proj(w):
    attn = 3 * MOBA_HEADS * HEAD_DIM + NSA_HEADS * HEAD_DIM + 6 * NSA_KV_GROUPS * HEAD_DIM
    ng = 3 * NSA_HEADS
    pad = jnp.zeros(w.shape[:-1] + (PROJ_COLS - w.shape[-1],), w.dtype)
    return jnp.concatenate([w[..., attn + ng:], w[..., :attn], w[..., attn:attn + ng], pad], axis=-1)


def _head_gain_rows(moba_q, moba_k, nsa_q, nsa_k):
    rows = jnp.zeros((moba_q.shape[0], 1, PROJ_COLS), F32)
    for col, gain, heads in ((C_MQ, moba_q * QSCALE, MOBA_HEADS), (C_MK, moba_k, MOBA_HEADS),
                             (C_NSQ, nsa_q * QSCALE, NSA_HEADS), (C_KSL, nsa_k[:, 1], NSA_KV_GROUPS),
                             (C_KWN, nsa_k[:, 2], NSA_KV_GROUPS)):
        rows = rows.at[:, 0, col:col + heads * HEAD_DIM].set(jnp.tile(gain, (1, heads)))
    return rows


def _cmp_weights(w1, pos):
    depth = w1.shape[0]
    half = NSA_CMP_STRIDE * HEAD_DIM
    w1cat = jnp.concatenate([w1[:, :half], w1[:, half:]], axis=2).astype(BF16)
    tok = w1cat.reshape(depth, NSA_CMP_STRIDE, HEAD_DIM, -1)
    zero = jnp.zeros_like(tok)
    placed = jnp.stack([jnp.concatenate([tok, zero], axis=2), jnp.concatenate([zero, tok], axis=2)], axis=1)
    posr = jnp.zeros((depth, 8, half), F32).at[:, 0:2].set(pos.reshape(depth, 2, half))
    return w1cat, placed, posr


def kernel(x, p, rel_bias, attn_norm, w_in, moba_q_gain, moba_k_gain, nsa_q_gain, nsa_k_gain,
           cmp_pos_k, cmp_w1_k, cmp_w2_k, cmp_pos_v, cmp_w1_v, cmp_w2_v,
           w_br_moba, w_br_nsa, w_o, ffn_norm, w_up, conv_w, conv_b, w_down, w_ple_gate, w_ple):
    batch, seq, d = x.shape
    n = batch * seq
    depth = w_in.shape[0]
    bf = lambda a: a.astype(BF16)
    w_in_r = bf(_reorder_in_proj(w_in))
    head_gain = _head_gain_rows(moba_q_gain, moba_k_gain, nsa_q_gain, nsa_k_gain)
    attn_gain, ffn_gain = attn_norm[:, None, :], ffn_norm[:, None, :]
    w1k, wtk, pk = _cmp_weights(cmp_w1_k, cmp_pos_k)
    w1v, wtv, pv = _cmp_weights(cmp_w1_v, cmp_pos_v)
    w2k, w2v, kg0 = bf(cmp_w2_k), bf(cmp_w2_v), nsa_k_gain[:, 0:1]
    wa, wb, wo = bf(w_br_moba), bf(w_br_nsa), bf(w_o)
    wu, wd, wg, wp = bf(w_up), bf(w_down), bf(w_ple_gate), bf(w_ple)
    conv_b = conv_b[:, None, :]
    pf = p.reshape(depth, n, -1)

    t0m, t1m, t0n, t1n, wc = _bias_tables(rel_bias)
    xf = x.reshape(n, d)
    for i in range(depth):
        proj = _inproj(xf, attn_gain, w_in_r, head_gain, i)
        ya = _moba(proj, t0m, t1m, batch, seq)
        kcn, vct = _compress(proj, w1k, w1v, wtk, wtv, w2k, w2v, pk, pv, kg0, i, batch, seq)
        yb = _nsa(proj, kcn, vct, wc, t0n, t1n, batch, seq)
        xf = _merge(xf, ya, yb, proj, wa, wb, wo, i)
        xf = _ffn_ple(xf, ffn_gain, wu, conv_w, conv_b, wd, pf, wg, wp, i, seq)
    return xf.reshape(batch, seq, d)
```

```python
import functools
import math

import numpy as np
import jax
import jax.numpy as jnp
from jax import lax
from jax.experimental import pallas as pl
from jax.experimental.pallas import tpu as pltpu

F32 = jnp.float32
BF16 = jnp.bfloat16

HEAD_DIM = 64
MOBA_HEADS = 8
MOBA_BLOCK = 256
MOBA_TOPK = 3
NSA_HEADS = 8
NSA_KV_GROUPS = 2
NSA_HPG = NSA_HEADS // NSA_KV_GROUPS
NSA_CMP_LEN = 32
NSA_CMP_STRIDE = 16
NSA_CMP_HIDDEN = 2 * HEAD_DIM
NSA_SEL_BLOCK = 64
NSA_SEL_TOPN = 16
NSA_WINDOW = 512
REL_BUCKETS = 32
REL_MAX_DIST = 128
CONV_W = 3
RMS_EPS = 1e-6
SCALE = HEAD_DIM ** -0.5
LOG2E = math.log2(math.e)
QSCALE = SCALE * LOG2E
NEG = -1e30

LANES = 128
VMEM_LIMIT = 56 * 1024 * 1024

C_GA, C_GB, C_MQ, C_MK, C_MV, C_NSQ = 0, 1024, 2048, 2560, 3072, 3584
C_KC, C_VC, C_KSL, C_VSL, C_KWN, C_VWN, C_NG = 4096, 4224, 4352, 4480, 4608, 4736, 4864
PROJ_COLS = 4992

NSA_TQ = 128
CMP_WIN = 16
CMP_PAD = 8


def _dot(a, b):
    return jnp.dot(a, b, preferred_element_type=F32)


def _dot_nt(a, b):
    return lax.dot_general(a, b, (((1,), (1,)), ((), ())), preferred_element_type=F32)


def _rms(x, gain):
    return x * lax.rsqrt(jnp.mean(x * x, axis=-1, keepdims=True) + RMS_EPS) * gain


def _split_bf16(x):
    hi = x.astype(BF16)
    return hi, (x - hi.astype(F32)).astype(BF16)


def _eye(rows, cols):
    r = lax.broadcasted_iota(jnp.int32, (rows, cols), 0)
    c = lax.broadcasted_iota(jnp.int32, (rows, cols), 1)
    return (r == c).astype(BF16)


def _transpose_bf16(x, rows):
    return _dot_nt(_eye(rows, x.shape[1]), x)


VROWS = 80


def _value_tile(v):
    row = lax.broadcasted_iota(jnp.int32, (VROWS, v.shape[0]), 0)
    return jnp.where(row == HEAD_DIM, 1.0, _transpose_bf16(v, VROWS)).astype(BF16)


def _softmax_joint(tiles):
    m = functools.reduce(jnp.maximum, [jnp.max(s, axis=0, keepdims=True) for s, _ in tiles])
    acc = functools.reduce(jnp.add, [_dot(vt, jnp.exp2(s - m).astype(BF16)) for s, vt in tiles])
    return m, acc


def _softmax_update(carry, s, vts):
    m, acc = carry
    m_new = jnp.maximum(m, jnp.max(s, axis=0, keepdims=True))
    p = jnp.exp2(s - m_new).astype(BF16)
    rows = s.shape[0] // len(vts)
    pv = functools.reduce(jnp.add, [_dot(vt, p[i * rows:(i + 1) * rows]) for i, vt in enumerate(vts)])
    return m_new, jnp.exp2(m - m_new) * acc + pv


def _softmax_finish(carry):
    _, acc = carry
    return acc[0:HEAD_DIM] * (1.0 / acc[HEAD_DIM:HEAD_DIM + 1])


def _rank_select(score, idx, count):
    beaten = jnp.zeros(score.shape, jnp.int32)
    for i in range(score.shape[0]):
        si = score[i:i + 1, :]
        beaten += ((si > score) | ((si == score) & (i < idx))).astype(jnp.int32)
    return beaten < count


def _params(sem):
    return pltpu.CompilerParams(dimension_semantics=sem, vmem_limit_bytes=VMEM_LIMIT)


def _layer_spec(a, layer):
    return pl.BlockSpec((None,) + a.shape[1:], lambda *_: (layer,) + (0,) * (a.ndim - 1),
                        pipeline_mode=pl.Buffered(1))


def _rel_bucket_np(dist):
    n = np.maximum(dist, 0)
    max_exact = REL_BUCKETS // 2
    nf = np.maximum(n, 1).astype(np.float32)
    large = max_exact + (np.log(nf / np.float32(max_exact)) / np.float32(math.log(REL_MAX_DIST / max_exact))
                         * np.float32(REL_BUCKETS - max_exact)).astype(np.int32)
    return np.where(n < max_exact, n, np.minimum(large, REL_BUCKETS - 1))


def _bucket_starts():
    buckets = _rel_bucket_np(np.arange(4 * REL_MAX_DIST))
    return [int(np.argmax(buckets >= k)) for k in range(REL_BUCKETS)]


BUCKET_START = _bucket_starts()
BIAS_REACH = BUCKET_START[-1]
assert BIAS_REACH <= MOBA_BLOCK and BIAS_REACH <= NSA_TQ - NSA_CMP_LEN + 1 + NSA_CMP_STRIDE


def _tables_kernel(tab_ref, t0m_ref, t1m_ref, t0n_ref, t1n_ref, wc_ref):
    h = pl.program_id(0)

    def bias(dist, head):
        last = tab_ref[head, REL_BUCKETS - 1]
        val = jnp.zeros(dist.shape, F32)
        for k in range(REL_BUCKETS - 2, -1, -1):
            val = jnp.where(dist < BUCKET_START[k + 1], (tab_ref[head, k] - last) * LOG2E, val)
        return val

    def toeplitz(size, offset, head):
        key = lax.broadcasted_iota(jnp.int32, (size, size), 0)
        qry = lax.broadcasted_iota(jnp.int32, (size, size), 1)
        return bias(offset + qry - key, head)

    t0m_ref[0] = toeplitz(MOBA_BLOCK, 0, h)
    t1m_ref[0] = toeplitz(MOBA_BLOCK, MOBA_BLOCK, h)
    t0n_ref[0] = toeplitz(NSA_TQ, 0, MOBA_HEADS + h)
    t1n_ref[0] = toeplitz(NSA_TQ, NSA_TQ, MOBA_HEADS + h)
    a = lax.broadcasted_iota(jnp.int32, (CMP_WIN, NSA_TQ), 0)
    i = lax.broadcasted_iota(jnp.int32, (CMP_WIN, NSA_TQ), 1)
    wc_ref[0] = bias(i + (NSA_TQ - NSA_CMP_LEN + 1) - NSA_CMP_STRIDE * a, MOBA_HEADS + h)


def _bias_tables(rel_bias):
    blk, tq = MOBA_BLOCK, NSA_TQ
    shapes = [(blk, blk), (blk, blk), (tq, tq), (tq, tq), (CMP_WIN, tq)]
    return pl.pallas_call(
        _tables_kernel,
        grid=(MOBA_HEADS,),
        in_specs=[pl.BlockSpec(memory_space=pltpu.SMEM)],
        out_specs=[pl.BlockSpec((1,) + s, lambda h: (h, 0, 0)) for s in shapes],
        out_shape=[jax.ShapeDtypeStruct((MOBA_HEADS,) + s, F32) for s in shapes],
        compiler_params=_params(("arbitrary",)),
        name="bias_tables",
    )(rel_bias)


NORM_SLABS = (list(range(C_MQ, C_MV, LANES)) + list(range(C_NSQ, C_KC, LANES)) + [C_KSL, C_KWN])


def _inproj_kernel(x_ref, g_ref, w_ref, hg_ref, o_ref):
    h = _rms(x_ref[...], g_ref[...]).astype(BF16)
    y = _dot(h, w_ref[...])
    first = lax.broadcasted_iota(jnp.int32, (1, LANES), 1) < HEAD_DIM
    edges = sorted(set([0, PROJ_COLS] + NORM_SLABS + [c + LANES for c in NORM_SLABS]))
    for lo, hi in zip(edges[:-1], edges[1:]):
        t = y[:, lo:hi]
        if lo in NORM_SLABS:
            sq = t * t
            s0 = jnp.sum(jnp.where(first, sq, 0.0), axis=-1, keepdims=True)
            s1 = jnp.sum(jnp.where(first, 0.0, sq), axis=-1, keepdims=True)
            ms = jnp.where(first, s0, s1) * (1.0 / HEAD_DIM)
            t = t * lax.rsqrt(ms + RMS_EPS) * hg_ref[:, lo:hi]
        o_ref[:, lo:hi] = t


def _inproj(xf, gain, w, head_gain, layer):
    n, d = xf.shape
    tm = 512
    return pl.pallas_call(
        _inproj_kernel,
        grid=(n // tm,),
        in_specs=[pl.BlockSpec((tm, d), lambda i: (i, 0)), _layer_spec(gain, layer),
                  _layer_spec(w, layer), _layer_spec(head_gain, layer)],
        out_specs=pl.BlockSpec((tm, PROJ_COLS), lambda i: (i, 0)),
        out_shape=jax.ShapeDtypeStruct((n, PROJ_COLS), F32),
        compiler_params=_params(("parallel",)),
        name="inproj",
    )(xf, gain, w, head_gain)


MOBA_HPS = 8


def _moba_kernel(q_ref, k_ref, v_ref, t0_ref, t1_ref, o_ref, kaug_s, vt_s, km_s, *, seq):
    n = pl.program_id(2)
    blk = MOBA_BLOCK
    nb = seq // blk
    head_cols = lambda hh: slice(hh * HEAD_DIM, (hh + 1) * HEAD_DIM)

    @pl.when(n == 0)
    def _prepare_keys():
        r = lax.broadcasted_iota(jnp.int32, (seq, LANES), 0)
        c = lax.broadcasted_iota(jnp.int32, (seq, LANES), 1)
        onehot = (c - HEAD_DIM == r // blk).astype(F32)
        place = _eye(HEAD_DIM, LANES)
        for hh in range(MOBA_HPS):
            kn = k_ref[:, head_cols(hh)]
            km = kn.reshape(nb, blk, HEAD_DIM).sum(axis=1) * (1.0 / blk)
            km_s[hh] = jnp.concatenate([km, jnp.zeros((16 - nb, HEAD_DIM), F32)], axis=0)
            kaug_s[hh] = (_dot(kn.astype(BF16), place) + onehot).astype(BF16)
            for j in range(nb):
                vj = v_ref[j * blk:(j + 1) * blk, head_cols(hh)].astype(BF16)
                vt_s[hh, j] = _value_tile(vj)

    jidx = lax.broadcasted_iota(jnp.int32, (16, blk), 0)
    past = jidx < n
    key = lax.broadcasted_iota(jnp.int32, (blk, blk), 0)
    qry = lax.broadcasted_iota(jnp.int32, (blk, blk), 1)
    prev = jnp.maximum(n - 1, 0)
    heads = range(MOBA_HPS)
    qts = []
    for pair in range(MOBA_HPS // 2):
        qp = _transpose_bf16(q_ref[:, pair * LANES:(pair + 1) * LANES].astype(BF16), LANES)
        qts += [qp[0:HEAD_DIM].astype(BF16), qp[HEAD_DIM:].astype(BF16)]
    gates = []
    for hh in heads:
        kmh, kml = _split_bf16(km_s[hh])
        gates.append(_dot(kmh, qts[hh]) + _dot(kml, qts[hh]))
    qaugs, state = [], []
    for hh in heads:
        gate = jnp.where(past, gates[hh], NEG)
        sel = (past & _rank_select(gate, jidx, MOBA_TOPK)) | (jidx == n)
        selb = jnp.where(sel, 0.0, NEG).astype(BF16)
        qaugs.append(jnp.concatenate([qts[hh], selb, jnp.zeros((LANES - HEAD_DIM - 16, blk), BF16)],
                                     axis=0))

    def scores(hh, j):
        start = pl.multiple_of(j * blk, blk)
        return _dot(kaug_s[hh, pl.ds(start, blk), :], qaugs[hh]), vt_s[hh, j]

    near = [(scores(hh, n), scores(hh, prev)) for hh in range(MOBA_HPS)]
    for hh, ((s0, vt0), (s1, vt1)) in enumerate(near):
        s0 = jnp.where(key <= qry, s0, NEG) + t0_ref[hh]
        s1 = jnp.where(n >= 1, s1, NEG) + t1_ref[hh]
        state.append(_softmax_joint([(s0, vt0), (s1, vt1)]))

    def far_body(j, carry):
        tiles = [scores(hh, j) for hh in range(MOBA_HPS)]
        return tuple(_softmax_update(carry[hh], tiles[hh][0], [tiles[hh][1]]) for hh in range(MOBA_HPS))

    state = lax.fori_loop(0, jnp.maximum(n - 1, 0), far_body, tuple(state))
    o_ref[...] = jnp.concatenate([_softmax_finish(c) for c in state], axis=0).T.astype(BF16)


def _moba(proj, t0, t1, batch, seq):
    blk = MOBA_BLOCK
    nb = seq // blk
    width = MOBA_HPS * HEAD_DIM
    qc, kc, vc = C_MQ // width, C_MK // width, C_MV // width
    return pl.pallas_call(
        functools.partial(_moba_kernel, seq=seq),
        grid=(batch, MOBA_HEADS // MOBA_HPS, nb),
        in_specs=[pl.BlockSpec((blk, width), lambda b, h, n: (b * nb + n, qc + h)),
                  pl.BlockSpec((seq, width), lambda b, h, n: (b, kc + h)),
                  pl.BlockSpec((seq, width), lambda b, h, n: (b, vc + h)),
                  pl.BlockSpec((MOBA_HPS, blk, blk), lambda b, h, n: (h, 0, 0)),
                  pl.BlockSpec((MOBA_HPS, blk, blk), lambda b, h, n: (h, 0, 0))],
        out_specs=pl.BlockSpec((blk, width), lambda b, h, n: (b * nb + n, h)),
        out_shape=jax.ShapeDtypeStruct((batch * seq, MOBA_HEADS * HEAD_DIM), BF16),
        scratch_shapes=[pltpu.VMEM((MOBA_HPS, seq, LANES), BF16),
                        pltpu.VMEM((MOBA_HPS, nb, VROWS, blk), BF16),
                        pltpu.VMEM((MOBA_HPS, 16, HEAD_DIM), F32)],
        compiler_params=_params(("parallel", "parallel", "arbitrary")),
        name="moba",
    )(proj, proj, proj, t0, t1)


def _compress_kernel(k_ref, v_ref, w1k_ref, w1v_ref, wtk_ref, wtv_ref, w2k_ref, w2v_ref, pk_ref, pv_ref,
                     kg_ref, ko_ref, vo_ref):
    hid = NSA_CMP_HIDDEN
    chunks = k_ref.shape[0] // NSA_CMP_STRIDE

    def compress(t_ref, w1_ref, wt_ref, w2_ref, pos_ref, g):
        a = functools.reduce(jnp.add, [
            _dot(t_ref[pl.ds(l, chunks, stride=NSA_CMP_STRIDE), :].astype(BF16), wt_ref[g, l])
            for l in range(NSA_CMP_STRIDE)])
        pw = _dot(pos_ref[...].astype(BF16), w1_ref[...])
        pos = pw[0:1, :hid] + pw[1:2, hid:]
        nxt = pltpu.roll(a[:, hid:], chunks - 1, 0)
        h = jax.nn.gelu(a[:, :hid] + nxt + pos)
        return _dot(h.astype(BF16), w2_ref[...])

    for g in range(NSA_KV_GROUPS):
        kc = compress(k_ref, w1k_ref, wtk_ref, w2k_ref, pk_ref, g)
        ko_ref[0, g] = _rms(kc, kg_ref[...]).astype(BF16)
        vc = compress(v_ref, w1v_ref, wtv_ref, w2v_ref, pv_ref, g).astype(BF16)
        vo_ref[0, g] = _transpose_bf16(vc, HEAD_DIM).astype(BF16)


def _compress(proj, w1k, w1v, wtk, wtv, w2k, w2v, pk, pv, kg, layer, batch, seq):
    groups = NSA_KV_GROUPS
    chunks = seq // NSA_CMP_STRIDE
    tok = lambda col: pl.BlockSpec((seq, LANES), lambda b: (b, col // LANES))
    return pl.pallas_call(
        _compress_kernel,
        grid=(batch,),
        in_specs=[tok(C_KC), tok(C_VC)] + [_layer_spec(a, layer)
                                           for a in (w1k, w1v, wtk, wtv, w2k, w2v, pk, pv, kg)],
        out_specs=[pl.BlockSpec((1, groups, chunks, HEAD_DIM), lambda b: (b, 0, 0, 0)),
                   pl.BlockSpec((1, groups, HEAD_DIM, chunks), lambda b: (b, 0, 0, 0))],
        out_shape=[jax.ShapeDtypeStruct((batch, groups, chunks, HEAD_DIM), BF16),
                   jax.ShapeDtypeStruct((batch, groups, HEAD_DIM, chunks), BF16)],
        compiler_params=_params(("parallel",)),
        name="nsa_compress",
    )(proj, proj, w1k, w1v, wtk, wtv, w2k, w2v, pk, pv, kg)


def _nsa_kernel(q_ref, kc_ref, vc_ref, ksl_ref, vsl_ref, kwn_ref, vwn_ref, gt_ref,
                wc_ref, t0_ref, t1_ref, o_ref, kslaug_s, vslt_s, kwn_s, vwnt_s, sc_s, *, seq):
    qi = pl.program_id(1)
    tq = NSA_TQ
    hpg = NSA_HPG
    sb = NSA_SEL_BLOCK
    nsel = seq // sb
    ncmp = seq // NSA_CMP_STRIDE
    nwin = NSA_WINDOW // tq
    t_start = qi * tq

    @pl.when(qi == 0)
    def _prepare_keys():
        r = lax.broadcasted_iota(jnp.int32, (seq, LANES), 0)
        c = lax.broadcasted_iota(jnp.int32, (seq, LANES), 1)
        onehot = (c - HEAD_DIM == r // sb).astype(F32)
        place = _eye(HEAD_DIM, LANES)
        sc_s[:, 0:CMP_PAD, :] = jnp.zeros((NSA_KV_GROUPS, CMP_PAD, hpg * tq), F32)
        for g in range(NSA_KV_GROUPS):
            sl = slice(g * HEAD_DIM, (g + 1) * HEAD_DIM)
            kslaug_s[g] = (_dot(ksl_ref[:, sl].astype(BF16), place) + onehot).astype(BF16)
            kwn_s[g] = kwn_ref[:, sl].astype(BF16)
            for j in range(seq // tq):
                rows = slice(j * tq, (j + 1) * tq)
                vslt_s[g, j] = _value_tile(vsl_ref[rows, sl].astype(BF16))
                vwnt_s[g, j] = _value_tile(vwn_ref[rows, sl].astype(BF16))

    lanes = lambda parts: jnp.concatenate(parts, axis=1)
    key = lax.broadcasted_iota(jnp.int32, (tq, tq), 0)
    qry = lax.broadcasted_iota(jnp.int32, (tq, tq), 1)
    causal = lanes([key <= qry] * hpg)
    upper = lanes([key > qry] * hpg)
    cend = lax.broadcasted_iota(jnp.int32, (ncmp, tq), 0) * NSA_CMP_STRIDE + (NSA_CMP_LEN - 1)
    tpos = lax.broadcasted_iota(jnp.int32, (ncmp, tq), 1) + t_start
    cvis = lanes([cend <= tpos] * hpg)
    oj = lax.broadcasted_iota(jnp.int32, (nsel, ncmp), 0) * sb
    on = lax.broadcasted_iota(jnp.int32, (nsel, ncmp), 1) * NSA_CMP_STRIDE
    overlap = ((on < oj + sb) & (on + NSA_CMP_LEN > oj) & (on < seq - NSA_CMP_STRIDE)).astype(BF16)
    jidx = lax.broadcasted_iota(jnp.int32, (nsel, tq), 0)
    own = (lax.broadcasted_iota(jnp.int32, (nsel, tq), 1) + t_start) // sb
    gates = jax.nn.sigmoid(gt_ref[...]).T
    tile_rows = lambda j: pl.ds(pl.multiple_of(j * tq, tq), tq)
    back = lambda k: jnp.maximum(qi - k, 0)

    groups = range(NSA_KV_GROUPS)
    heads = [[g * hpg + r for r in range(hpg)] for g in groups]
    t0 = [lanes([t0_ref[h] for h in heads[g]]) for g in groups]
    t1 = [lanes([t1_ref[h] for h in heads[g]]) for g in groups]

    qplain, win_scores = [], []
    for g in groups:
        width = hpg * HEAD_DIM
        qg = _transpose_bf16(q_ref[:, g * width:(g + 1) * width].astype(BF16), width)
        qplain.append(lanes([qg[r * HEAD_DIM:(r + 1) * HEAD_DIM] for r in range(hpg)]).astype(BF16))
        sc_s[g, CMP_PAD:, :] = _dot(kc_ref[0, g], qplain[g])
        win_scores.append([_dot(kwn_s[g, tile_rows(back(k)), :], qplain[g]) for k in range(nwin + 1)])

    o_cmps, imps = [], []
    for g in groups:
        win = pl.ds(pl.multiple_of(qi * (tq // NSA_CMP_STRIDE), 8), CMP_WIN)
        sc_s[g, win, :] += lanes([wc_ref[h] for h in heads[g]])
        s = jnp.where(cvis, sc_s[g, CMP_PAD:, :], NEG)
        m = jnp.max(s, axis=0, keepdims=True)
        e = jnp.where(cvis, jnp.exp2(s - m), 0.0)
        den = jnp.sum(e, axis=0, keepdims=True)
        p = e * (1.0 / jnp.where(den > 0, den, 1.0))
        o_cmps.append(_dot(vc_ref[0, g], p.astype(BF16)))
        psum = p[:, 0:tq]
        for r in range(1, hpg):
            psum = psum + p[:, r * tq:(r + 1) * tq]
        ph, plo = _split_bf16(psum)
        imps.append(_dot(overlap, ph) + _dot(overlap, plo))

    o_wins = []
    for g in groups:
        tiles = []
        for k, s in enumerate(win_scores[g]):
            if k == 0:
                s = jnp.where(causal, s, NEG) + t0[g]
            elif k == 1:
                s = jnp.where(qi >= 1, s, NEG) + t1[g]
            elif k < nwin:
                s = jnp.where(qi >= k, s, NEG)
            else:
                s = jnp.where(upper & (qi >= k), s, NEG)
            tiles.append((s, vwnt_s[g, back(k)]))
        o_wins.append(_softmax_finish(_softmax_joint(tiles)))

    qaugs = []
    for g in groups:
        imp = jnp.where((jidx == 0) | (jidx == own) | (jidx == own - 1), -NEG, imps[g])
        imp = jnp.where(jidx > own, NEG, imp)
        sel = _rank_select(imp, jidx, NSA_SEL_TOPN) & (jidx <= own)
        selb = jnp.where(sel, 0.0, NEG).astype(BF16)
        qaugs.append(jnp.concatenate([qplain[g], lanes([selb] * hpg),
                                      jnp.zeros((LANES - HEAD_DIM - nsel, hpg * tq), BF16)], axis=0))

    odd = (qi >= 2) & (qi % 2 == 0)
    slc_near = [[_dot(kslaug_s[g, tile_rows(back(k)), :], qaugs[g]) for k in range(3)] for g in groups]
    slc_state = tuple(_softmax_joint([
        (jnp.where(causal, slc_near[g][0], NEG) + t0[g], vslt_s[g, qi]),
        (jnp.where(qi >= 1, slc_near[g][1], NEG) + t1[g], vslt_s[g, back(1)]),
        (jnp.where(odd, slc_near[g][2], NEG), vslt_s[g, back(2)])]) for g in groups)

    def slc_far(i, carry):
        slab = pl.ds(pl.multiple_of(i * (2 * tq), 2 * tq), 2 * tq)
        scores = [_dot(kslaug_s[g, slab, :], qaugs[g]) for g in groups]
        return tuple(_softmax_update(carry[g], scores[g], [vslt_s[g, 2 * i], vslt_s[g, 2 * i + 1]])
                     for g in groups)

    slc_state = lax.fori_loop(0, jnp.maximum(qi - 1, 0) // 2, slc_far, slc_state)

    outs = []
    for g in range(NSA_KV_GROUPS):
        o_slc = _softmax_finish(slc_state[g])
        for r in range(hpg):
            h = g * hpg + r
            cols = slice(r * tq, (r + 1) * tq)
            outs.append(gates[h:h + 1, :] * o_cmps[g][:, cols]
                        + gates[NSA_HEADS + h:NSA_HEADS + h + 1, :] * o_slc[:, cols]
                        + gates[2 * NSA_HEADS + h:2 * NSA_HEADS + h + 1, :] * o_wins[g][:, cols])
    o_ref[...] = jnp.concatenate(outs, axis=0).T.astype(BF16)


def _nsa(proj, kcn, vct, wc, t0, t1, batch, seq):
    tq = NSA_TQ
    nq = seq // tq
    ncmp = seq // NSA_CMP_STRIDE
    width = NSA_HEADS * HEAD_DIM
    groups = NSA_KV_GROUPS
    kv = lambda col: pl.BlockSpec((seq, LANES), lambda b, i: (b, col // LANES))
    const = lambda shape: pl.BlockSpec(shape, lambda b, i: (0,) * len(shape))
    return pl.pallas_call(
        functools.partial(_nsa_kernel, seq=seq),
        grid=(batch, nq),
        in_specs=[pl.BlockSpec((tq, width), lambda b, i: (b * nq + i, C_NSQ // width)),
                  pl.BlockSpec((1, groups, ncmp, HEAD_DIM), lambda b, i: (b, 0, 0, 0)),
                  pl.BlockSpec((1, groups, HEAD_DIM, ncmp), lambda b, i: (b, 0, 0, 0)),
                  kv(C_KSL), kv(C_VSL), kv(C_KWN), kv(C_VWN),
                  pl.BlockSpec((tq, LANES), lambda b, i: (b * nq + i, C_NG // LANES)),
                  const((NSA_HEADS, CMP_WIN, tq)), const((NSA_HEADS, tq, tq)),
                  const((NSA_HEADS, tq, tq))],
        out_specs=pl.BlockSpec((tq, width), lambda b, i: (b * nq + i, 0)),
        out_shape=jax.ShapeDtypeStruct((batch * seq, width), BF16),
        scratch_shapes=[pltpu.VMEM((groups, seq, LANES), BF16),
                        pltpu.VMEM((groups, nq, VROWS, tq), BF16),
                        pltpu.VMEM((groups, seq, HEAD_DIM), BF16),
                        pltpu.VMEM((groups, nq, VROWS, tq), BF16),
                        pltpu.VMEM((groups, CMP_PAD + ncmp, NSA_HPG * tq), F32)],
        compiler_params=_params(("parallel", "arbitrary")),
        name="nsa",
    )(proj, kcn, vct, proj, proj, proj, proj, proj, wc, t0, t1)


def _merge_kernel(x_ref, ya_ref, yb_ref, ga_ref, gb_ref, wa_ref, wb_ref, wo_ref, o_ref):
    a = _dot(ya_ref[...], wa_ref[...])
    b = _dot(yb_ref[...], wb_ref[...])
    z = jax.nn.sigmoid(ga_ref[...]) * a + jax.nn.sigmoid(gb_ref[...]) * b
    o_ref[...] = x_ref[...] + _dot(z.astype(BF16), wo_ref[...])


def _merge(xf, ya, yb, proj, wa, wb, wo, layer):
    n, d = xf.shape
    tm = 512
    row = lambda w, col=0: pl.BlockSpec((tm, w), lambda i: (i, col))
    return pl.pallas_call(
        _merge_kernel,
        grid=(n // tm,),
        in_specs=[row(d), row(ya.shape[1]), row(yb.shape[1]), row(d, C_GA // d), row(d, C_GB // d),
                  _layer_spec(wa, layer), _layer_spec(wb, layer), _layer_spec(wo, layer)],
        out_specs=row(d),
        out_shape=jax.ShapeDtypeStruct((n, d), F32),
        compiler_params=_params(("parallel",)),
        name="merge",
    )(xf, ya, yb, proj, proj, wa, wb, wo)


FFN_HALO = 16
FFN_TM = 1024
FFN_TF = 256


def _ffn_kernel(x_ref, xh_ref, g_ref, wu_ref, cw_ref, cb_ref, wd_ref, p_ref, wg_ref, wp_ref,
                o_ref, act_s, *, seq, d_ff):
    i = pl.program_id(0)
    tm = x_ref.shape[0]
    x = x_ref[...]
    at_start = (i * tm) % seq == 0
    halo = jnp.where(at_start, 0.0, _rms(xh_ref[...], g_ref[...]))
    hn = jnp.concatenate([halo.astype(BF16), _rms(x, g_ref[...]).astype(BF16)], axis=0)

    def conv(cols):
        u = _dot(hn, wu_ref[:, cols])
        u1 = pltpu.roll(u, 1, 0)[FFN_HALO:]
        u2 = pltpu.roll(u, 2, 0)[FFN_HALO:]
        cw = cw_ref[:, cols]
        return cw[0:1] * u2 + cw[1:2] * u1 + cw[2:3] * u[FFN_HALO:] + cb_ref[:, cols]

    for c in range(d_ff // FFN_TF):
        lo = c * FFN_TF
        act = jax.nn.gelu(conv(slice(lo, lo + FFN_TF))) * conv(slice(d_ff + lo, d_ff + lo + FFN_TF))
        act_s[:, lo:lo + FFN_TF] = act.astype(BF16)

    x = x + _dot(act_s[...], wd_ref[...])
    gate = jax.nn.sigmoid(_dot(x.astype(BF16), wg_ref[...]))
    o_ref[...] = x + gate * _dot(p_ref[...].astype(BF16), wp_ref[...])


def _ffn_ple(xf, gain, w_up, conv_w, conv_b, w_down, pf, wg, wp, layer, seq):
    n, d = xf.shape
    d_ff = w_down.shape[1]
    tm = FFN_TM
    hb = tm // FFN_HALO
    resident = lambda a: _layer_spec(a, layer)
    return pl.pallas_call(
        functools.partial(_ffn_kernel, seq=seq, d_ff=d_ff),
        grid=(n // tm,),
        in_specs=[pl.BlockSpec((tm, d), lambda i: (i, 0)),
                  pl.BlockSpec((FFN_HALO, d), lambda i: (jnp.maximum(i * hb - 1, 0), 0)),
                  resident(gain), resident(w_up), resident(conv_w), resident(conv_b), resident(w_down),
                  pl.BlockSpec((None, tm, pf.shape[2]), lambda i: (layer, i, 0)),
                  resident(wg), resident(wp)],
        out_specs=pl.BlockSpec((tm, d), lambda i: (i, 0)),
        out_shape=jax.ShapeDtypeStruct((n, d), F32),
        scratch_shapes=[pltpu.VMEM((tm, d_ff), BF16)],
        compiler_params=_params(("parallel",)),
        name="conv_ffn_ple",
    )(xf, xf, gain, w_up, conv_w, conv_b, w_down, pf, wg, wp)


def _reorder_in_proj(w):
    attn = 3 * MOBA_HEADS * HEAD_DIM + NSA_HEADS * HEAD_DIM + 6 * NSA_KV_GROUPS * HEAD_DIM
    ng = 3 * NSA_HEADS
    pad = jnp.zeros(w.shape[:-1] + (PROJ_COLS - w.shape[-1],), w.dtype)
    return jnp.concatenate([w[..., attn + ng:], w[..., :attn], w[..., attn:attn + ng], pad], axis=-1)


def _head_gain_rows(moba_q, moba_k, nsa_q, nsa_k):
    rows = jnp.zeros((moba_q.shape[0], 1, PROJ_COLS), F32)
    for col, gain, heads in ((C_MQ, moba_q * QSCALE, MOBA_HEADS), (C_MK, moba_k, MOBA_HEADS),
                             (C_NSQ, nsa_q * QSCALE, NSA_HEADS), (C_KSL, nsa_k[:, 1], NSA_KV_GROUPS),
                             (C_KWN, nsa_k[:, 2], NSA_KV_GROUPS)):
        rows = rows.at[:, 0, col:col + heads * HEAD_DIM].set(jnp.tile(gain, (1, heads)))
    return rows


def _cmp_weights(w1, pos):
    depth = w1.shape[0]
    half = NSA_CMP_STRIDE * HEAD_DIM
    w1cat = jnp.concatenate([w1[:, :half], w1[:, half:]], axis=2).astype(BF16)
    tok = w1cat.reshape(depth, NSA_CMP_STRIDE, HEAD_DIM, -1)
    zero = jnp.zeros_like(tok)
    placed = jnp.stack([jnp.concatenate([tok, zero], axis=2), jnp.concatenate([zero, tok], axis=2)], axis=1)
    posr = jnp.zeros((depth, 8, half), F32).at[:, 0:2].set(pos.reshape(depth, 2, half))
    return w1cat, placed, posr


def kernel(x, p, rel_bias, attn_norm, w_in, moba_q_gain, moba_k_gain, nsa_q_gain, nsa_k_gain,
           cmp_pos_k, cmp_w1_k, cmp_w2_k, cmp_pos_v, cmp_w1_v, cmp_w2_v,
           w_br_moba, w_br_nsa, w_o, ffn_norm, w_up, conv_w, conv_b, w_down, w_ple_gate, w_ple):
    batch, seq, d = x.shape
    n = batch * seq
    depth = w_in.shape[0]
    bf = lambda a: a.astype(BF16)
    w_in_r = _reorder_in_proj(bf(w_in))
    head_gain = _head_gain_rows(moba_q_gain, moba_k_gain, nsa_q_gain, nsa_k_gain)
    attn_gain, ffn_gain = attn_norm[:, None, :], ffn_norm[:, None, :]
    w1k, wtk, pk = _cmp_weights(cmp_w1_k, cmp_pos_k)
    w1v, wtv, pv = _cmp_weights(cmp_w1_v, cmp_pos_v)
    w2k, w2v, kg0 = bf(cmp_w2_k), bf(cmp_w2_v), nsa_k_gain[:, 0:1]
    wa, wb, wo = bf(w_br_moba), bf(w_br_nsa), bf(w_o)
    wu, wd, wg, wp = bf(w_up), bf(w_down), bf(w_ple_gate), bf(w_ple)
    conv_b = conv_b[:, None, :]
    pf = p.reshape(depth, n, -1)

    t0m, t1m, t0n, t1n, wc = _bias_tables(rel_bias)
    xf = x.reshape(n, d)
    for i in range(depth):
        proj = _inproj(xf, attn_gain, w_in_r, head_gain, i)
        ya = _moba(proj, t0m, t1m, batch, seq)
        kcn, vct = _compress(proj, w1k, w1v, wtk, wtv, w2k, w2v, pk, pv, kg0, i, batch, seq)
        yb = _nsa(proj, kcn, vct, wc, t0n, t1n, batch, seq)
        xf = _merge(xf, ya, yb, proj, wa, wb, wo, i)
        xf = _ffn_ple(xf, ffn_gain, wu, conv_w, conv_b, wd, pf, wg, wp, i, seq)
    return xf.reshape(batch, seq, d)
```

```python
import functools
import math

import numpy as np
import jax
import jax.numpy as jnp
from jax import lax
from jax.experimental import pallas as pl
from jax.experimental.pallas import tpu as pltpu

F32 = jnp.float32
BF16 = jnp.bfloat16

HEAD_DIM = 64
MOBA_HEADS = 8
MOBA_BLOCK = 256
MOBA_TOPK = 3
NSA_HEADS = 8
NSA_KV_GROUPS = 2
NSA_HPG = NSA_HEADS // NSA_KV_GROUPS
NSA_CMP_LEN = 32
NSA_CMP_STRIDE = 16
NSA_CMP_HIDDEN = 2 * HEAD_DIM
NSA_SEL_BLOCK = 64
NSA_SEL_TOPN = 16
NSA_WINDOW = 512
REL_BUCKETS = 32
REL_MAX_DIST = 128
RMS_EPS = 1e-6
SCALE = HEAD_DIM ** -0.5
LOG2E = math.log2(math.e)
QSCALE = SCALE * LOG2E
NEG = -1e30

LANES = 128
V7X_VMEM_BYTES = 64 * 1024 * 1024
VMEM_LIMIT = V7X_VMEM_BYTES * 7 // 8

INPROJ_TM = 512
MERGE_TM = 512

C_GA, C_GB, C_MQ, C_MK, C_MV, C_NSQ = 0, 1024, 2048, 2560, 3072, 3584
C_KC, C_VC, C_KSL, C_VSL, C_KWN, C_VWN, C_NG = 4096, 4224, 4352, 4480, 4608, 4736, 4864
PROJ_COLS = 4992

NSA_TQ = 128
CMP_WIN = 16
CMP_PAD = 8


def _dot(a, b):
    return jnp.dot(a, b, preferred_element_type=F32)


def _dot_nt(a, b):
    return lax.dot_general(a, b, (((1,), (1,)), ((), ())), preferred_element_type=F32)


def _rms(x, gain):
    return x * lax.rsqrt(jnp.mean(x * x, axis=-1, keepdims=True) + RMS_EPS) * gain


def _split_bf16(x):
    hi = x.astype(BF16)
    return hi, (x - hi.astype(F32)).astype(BF16)


def _eye(rows, cols):
    r = lax.broadcasted_iota(jnp.int32, (rows, cols), 0)
    c = lax.broadcasted_iota(jnp.int32, (rows, cols), 1)
    return (r == c).astype(BF16)


def _transpose_bf16(x, rows):
    return _dot_nt(_eye(rows, x.shape[1]), x)


VROWS = 80


def _value_tile(v):
    row = lax.broadcasted_iota(jnp.int32, (VROWS, v.shape[0]), 0)
    return jnp.where(row == HEAD_DIM, 1.0, _transpose_bf16(v, VROWS)).astype(BF16)


def _softmax_joint(tiles):
    m = functools.reduce(jnp.maximum, [jnp.max(s, axis=0, keepdims=True) for s, _ in tiles])
    acc = functools.reduce(jnp.add, [_dot(vt, jnp.exp2(s - m).astype(BF16)) for s, vt in tiles])
    return m, acc


def _softmax_update(carry, s, vts):
    m, acc = carry
    m_new = jnp.maximum(m, jnp.max(s, axis=0, keepdims=True))
    p = jnp.exp2(s - m_new).astype(BF16)
    rows = s.shape[0] // len(vts)
    pv = functools.reduce(jnp.add, [_dot(vt, p[i * rows:(i + 1) * rows]) for i, vt in enumerate(vts)])
    return m_new, jnp.exp2(m - m_new) * acc + pv


def _softmax_finish(carry):
    _, acc = carry
    return acc[0:HEAD_DIM] * (1.0 / acc[HEAD_DIM:HEAD_DIM + 1])


def _rank_select(score, idx, count):
    beaten = jnp.zeros(score.shape, jnp.int32)
    for i in range(score.shape[0]):
        si = score[i:i + 1, :]
        beaten += ((si > score) | ((si == score) & (i < idx))).astype(jnp.int32)
    return beaten < count


def _params(sem):
    return pltpu.CompilerParams(dimension_semantics=sem, vmem_limit_bytes=VMEM_LIMIT)


def _layer_spec(a, layer):
    return pl.BlockSpec((None,) + a.shape[1:], lambda *_: (layer,) + (0,) * (a.ndim - 1),
                        pipeline_mode=pl.Buffered(1))


def _rel_bucket_np(dist):
    n = np.maximum(dist, 0)
    max_exact = REL_BUCKETS // 2
    nf = np.maximum(n, 1).astype(np.float32)
    large = max_exact + (np.log(nf / np.float32(max_exact)) / np.float32(math.log(REL_MAX_DIST / max_exact))
                         * np.float32(REL_BUCKETS - max_exact)).astype(np.int32)
    return np.where(n < max_exact, n, np.minimum(large, REL_BUCKETS - 1))


def _bucket_starts():
    buckets = _rel_bucket_np(np.arange(4 * REL_MAX_DIST))
    return [int(np.argmax(buckets >= k)) for k in range(REL_BUCKETS)]


BUCKET_START = _bucket_starts()
BIAS_REACH = BUCKET_START[-1]
assert BIAS_REACH <= MOBA_BLOCK and BIAS_REACH <= NSA_TQ - NSA_CMP_LEN + 1 + NSA_CMP_STRIDE


def _tables_kernel(tab_ref, t0m_ref, t1m_ref, t0n_ref, t1n_ref, wc_ref):
    h = pl.program_id(0)

    def bias(dist, head):
        last = tab_ref[head, REL_BUCKETS - 1]
        val = jnp.zeros(dist.shape, F32)
        for k in range(REL_BUCKETS - 2, -1, -1):
            val = jnp.where(dist < BUCKET_START[k + 1], (tab_ref[head, k] - last) * LOG2E, val)
        return val

    def toeplitz(size, offset, head):
        key = lax.broadcasted_iota(jnp.int32, (size, size), 0)
        qry = lax.broadcasted_iota(jnp.int32, (size, size), 1)
        return bias(offset + qry - key, head)

    t0m_ref[0] = toeplitz(MOBA_BLOCK, 0, h)
    t1m_ref[0] = toeplitz(MOBA_BLOCK, MOBA_BLOCK, h)
    t0n_ref[0] = toeplitz(NSA_TQ, 0, MOBA_HEADS + h)
    t1n_ref[0] = toeplitz(NSA_TQ, NSA_TQ, MOBA_HEADS + h)
    a = lax.broadcasted_iota(jnp.int32, (CMP_WIN, NSA_TQ), 0)
    i = lax.broadcasted_iota(jnp.int32, (CMP_WIN, NSA_TQ), 1)
    wc_ref[0] = bias(i + (NSA_TQ - NSA_CMP_LEN + 1) - NSA_CMP_STRIDE * a, MOBA_HEADS + h)


def _bias_tables(rel_bias):
    blk, tq = MOBA_BLOCK, NSA_TQ
    shapes = [(blk, blk), (blk, blk), (tq, tq), (tq, tq), (CMP_WIN, tq)]
    return pl.pallas_call(
        _tables_kernel,
        grid=(MOBA_HEADS,),
        in_specs=[pl.BlockSpec(memory_space=pltpu.SMEM)],
        out_specs=[pl.BlockSpec((1,) + s, lambda h: (h, 0, 0)) for s in shapes],
        out_shape=[jax.ShapeDtypeStruct((MOBA_HEADS,) + s, F32) for s in shapes],
        compiler_params=_params(("arbitrary",)),
        name="bias_tables",
    )(rel_bias)


NORM_SLABS = (list(range(C_MQ, C_MV, LANES)) + list(range(C_NSQ, C_KC, LANES)) + [C_KSL, C_KWN])


def _inproj_kernel(x_ref, g_ref, w_ref, hg_ref, o_ref):
    h = _rms(x_ref[...], g_ref[...]).astype(BF16)
    y = _dot(h, w_ref[...])
    first = lax.broadcasted_iota(jnp.int32, (1, LANES), 1) < HEAD_DIM
    edges = sorted(set([0, PROJ_COLS] + NORM_SLABS + [c + LANES for c in NORM_SLABS]))
    for lo, hi in zip(edges[:-1], edges[1:]):
        t = y[:, lo:hi]
        if lo in NORM_SLABS:
            sq = t * t
            s0 = jnp.sum(jnp.where(first, sq, 0.0), axis=-1, keepdims=True)
            s1 = jnp.sum(jnp.where(first, 0.0, sq), axis=-1, keepdims=True)
            ms = jnp.where(first, s0, s1) * (1.0 / HEAD_DIM)
            t = t * lax.rsqrt(ms + RMS_EPS) * hg_ref[:, lo:hi]
        o_ref[:, lo:hi] = t


def _inproj(xf, gain, w, head_gain, layer):
    n, d = xf.shape
    tm = INPROJ_TM
    return pl.pallas_call(
        _inproj_kernel,
        grid=(n // tm,),
        in_specs=[pl.BlockSpec((tm, d), lambda i: (i, 0)), _layer_spec(gain, layer),
                  _layer_spec(w, layer), _layer_spec(head_gain, layer)],
        out_specs=pl.BlockSpec((tm, PROJ_COLS), lambda i: (i, 0)),
        out_shape=jax.ShapeDtypeStruct((n, PROJ_COLS), F32),
        compiler_params=_params(("parallel",)),
        name="inproj",
    )(xf, gain, w, head_gain)


MOBA_HPS = 8


def _moba_kernel(q_ref, k_ref, v_ref, t0_ref, t1_ref, o_ref, kaug_s, vt_s, km_s, *, seq):
    n = pl.program_id(2)
    blk = MOBA_BLOCK
    nb = seq // blk
    head_cols = lambda hh: slice(hh * HEAD_DIM, (hh + 1) * HEAD_DIM)

    @pl.when(n == 0)
    def _prepare_keys():
        r = lax.broadcasted_iota(jnp.int32, (seq, LANES), 0)
        c = lax.broadcasted_iota(jnp.int32, (seq, LANES), 1)
        onehot = (c - HEAD_DIM == r // blk).astype(F32)
        place = _eye(HEAD_DIM, LANES)
        for hh in range(MOBA_HPS):
            kn = k_ref[:, head_cols(hh)]
            km = kn.reshape(nb, blk, HEAD_DIM).sum(axis=1) * (1.0 / blk)
            km_s[hh] = jnp.concatenate([km, jnp.zeros((16 - nb, HEAD_DIM), F32)], axis=0)
            kaug_s[hh] = (_dot(kn.astype(BF16), place) + onehot).astype(BF16)
            for j in range(nb):
                vj = v_ref[j * blk:(j + 1) * blk, head_cols(hh)].astype(BF16)
                vt_s[hh, j] = _value_tile(vj)

    jidx = lax.broadcasted_iota(jnp.int32, (16, blk), 0)
    past = jidx < n
    key = lax.broadcasted_iota(jnp.int32, (blk, blk), 0)
    qry = lax.broadcasted_iota(jnp.int32, (blk, blk), 1)
    prev = jnp.maximum(n - 1, 0)
    heads = range(MOBA_HPS)
    qts = []
    for pair in range(MOBA_HPS // 2):
        qp = _transpose_bf16(q_ref[:, pair * LANES:(pair + 1) * LANES].astype(BF16), LANES)
        qts += [qp[0:HEAD_DIM].astype(BF16), qp[HEAD_DIM:].astype(BF16)]
    gates = []
    for hh in heads:
        kmh, kml = _split_bf16(km_s[hh])
        gates.append(_dot(kmh, qts[hh]) + _dot(kml, qts[hh]))
    qaugs, state = [], []
    for hh in heads:
        gate = jnp.where(past, gates[hh], NEG)
        sel = (past & _rank_select(gate, jidx, MOBA_TOPK)) | (jidx == n)
        selb = jnp.where(sel, 0.0, NEG).astype(BF16)
        qaugs.append(jnp.concatenate([qts[hh], selb, jnp.zeros((LANES - HEAD_DIM - 16, blk), BF16)],
                                     axis=0))

    def scores(hh, j):
        start = pl.multiple_of(j * blk, blk)
        return _dot(kaug_s[hh, pl.ds(start, blk), :], qaugs[hh]), vt_s[hh, j]

    near = [(scores(hh, n), scores(hh, prev)) for hh in range(MOBA_HPS)]
    for hh, ((s0, vt0), (s1, vt1)) in enumerate(near):
        s0 = jnp.where(key <= qry, s0, NEG) + t0_ref[hh]
        s1 = jnp.where(n >= 1, s1, NEG) + t1_ref[hh]
        state.append(_softmax_joint([(s0, vt0), (s1, vt1)]))

    def far_body(j, carry):
        tiles = [scores(hh, j) for hh in range(MOBA_HPS)]
        return tuple(_softmax_update(carry[hh], tiles[hh][0], [tiles[hh][1]]) for hh in range(MOBA_HPS))

    state = lax.fori_loop(0, jnp.maximum(n - 1, 0), far_body, tuple(state))
    o_ref[...] = jnp.concatenate([_softmax_finish(c) for c in state], axis=0).T.astype(BF16)


def _moba(proj, t0, t1, batch, seq):
    blk = MOBA_BLOCK
    nb = seq // blk
    width = MOBA_HPS * HEAD_DIM
    qc, kc, vc = C_MQ // width, C_MK // width, C_MV // width
    return pl.pallas_call(
        functools.partial(_moba_kernel, seq=seq),
        grid=(batch, MOBA_HEADS // MOBA_HPS, nb),
        in_specs=[pl.BlockSpec((blk, width), lambda b, h, n: (b * nb + n, qc + h)),
                  pl.BlockSpec((seq, width), lambda b, h, n: (b, kc + h)),
                  pl.BlockSpec((seq, width), lambda b, h, n: (b, vc + h)),
                  pl.BlockSpec((MOBA_HPS, blk, blk), lambda b, h, n: (h, 0, 0)),
                  pl.BlockSpec((MOBA_HPS, blk, blk), lambda b, h, n: (h, 0, 0))],
        out_specs=pl.BlockSpec((blk, width), lambda b, h, n: (b * nb + n, h)),
        out_shape=jax.ShapeDtypeStruct((batch * seq, MOBA_HEADS * HEAD_DIM), BF16),
        scratch_shapes=[pltpu.VMEM((MOBA_HPS, seq, LANES), BF16),
                        pltpu.VMEM((MOBA_HPS, nb, VROWS, blk), BF16),
                        pltpu.VMEM((MOBA_HPS, 16, HEAD_DIM), F32)],
        compiler_params=_params(("parallel", "parallel", "arbitrary")),
        name="moba",
    )(proj, proj, proj, t0, t1)


def _compress_kernel(k_ref, v_ref, w1k_ref, w1v_ref, wtk_ref, wtv_ref, w2k_ref, w2v_ref, pk_ref, pv_ref,
                     kg_ref, ko_ref, vo_ref):
    hid = NSA_CMP_HIDDEN
    chunks = k_ref.shape[0] // NSA_CMP_STRIDE

    def compress(t_ref, w1_ref, wt_ref, w2_ref, pos_ref, g):
        a = functools.reduce(jnp.add, [
            _dot(t_ref[pl.ds(l, chunks, stride=NSA_CMP_STRIDE), :].astype(BF16), wt_ref[g, l])
            for l in range(NSA_CMP_STRIDE)])
        pw = _dot(pos_ref[...].astype(BF16), w1_ref[...])
        pos = pw[0:1, :hid] + pw[1:2, hid:]
        nxt = pltpu.roll(a[:, hid:], chunks - 1, 0)
        h = jax.nn.gelu(a[:, :hid] + nxt + pos)
        return _dot(h.astype(BF16), w2_ref[...])

    for g in range(NSA_KV_GROUPS):
        kc = compress(k_ref, w1k_ref, wtk_ref, w2k_ref, pk_ref, g)
        ko_ref[0, g] = _rms(kc, kg_ref[...]).astype(BF16)
        vc = compress(v_ref, w1v_ref, wtv_ref, w2v_ref, pv_ref, g).astype(BF16)
        vo_ref[0, g] = _transpose_bf16(vc, HEAD_DIM).astype(BF16)


def _compress(proj, w1k, w1v, wtk, wtv, w2k, w2v, pk, pv, kg, layer, batch, seq):
    groups = NSA_KV_GROUPS
    chunks = seq // NSA_CMP_STRIDE
    tok = lambda col: pl.BlockSpec((seq, LANES), lambda b: (b, col // LANES))
    return pl.pallas_call(
        _compress_kernel,
        grid=(batch,),
        in_specs=[tok(C_KC), tok(C_VC)] + [_layer_spec(a, layer)
                                           for a in (w1k, w1v, wtk, wtv, w2k, w2v, pk, pv, kg)],
        out_specs=[pl.BlockSpec((1, groups, chunks, HEAD_DIM), lambda b: (b, 0, 0, 0)),
                   pl.BlockSpec((1, groups, HEAD_DIM, chunks), lambda b: (b, 0, 0, 0))],
        out_shape=[jax.ShapeDtypeStruct((batch, groups, chunks, HEAD_DIM), BF16),
                   jax.ShapeDtypeStruct((batch, groups, HEAD_DIM, chunks), BF16)],
        compiler_params=_params(("parallel",)),
        name="nsa_compress",
    )(proj, proj, w1k, w1v, wtk, wtv, w2k, w2v, pk, pv, kg)


def _nsa_kernel(q_ref, kc_ref, vc_ref, ksl_ref, vsl_ref, kwn_ref, vwn_ref, gt_ref,
                wc_ref, t0_ref, t1_ref, o_ref, kslaug_s, vslt_s, kwn_s, vwnt_s, sc_s, *, seq):
    qi = pl.program_id(1)
    tq = NSA_TQ
    hpg = NSA_HPG
    sb = NSA_SEL_BLOCK
    nsel = seq // sb
    ncmp = seq // NSA_CMP_STRIDE
    nwin = NSA_WINDOW // tq
    t_start = qi * tq

    @pl.when(qi == 0)
    def _prepare_keys():
        r = lax.broadcasted_iota(jnp.int32, (seq, LANES), 0)
        c = lax.broadcasted_iota(jnp.int32, (seq, LANES), 1)
        onehot = (c - HEAD_DIM == r // sb).astype(F32)
        place = _eye(HEAD_DIM, LANES)
        sc_s[:, 0:CMP_PAD, :] = jnp.zeros((NSA_KV_GROUPS, CMP_PAD, hpg * tq), F32)
        for g in range(NSA_KV_GROUPS):
            sl = slice(g * HEAD_DIM, (g + 1) * HEAD_DIM)
            kslaug_s[g] = (_dot(ksl_ref[:, sl].astype(BF16), place) + onehot).astype(BF16)
            kwn_s[g] = kwn_ref[:, sl].astype(BF16)
            for j in range(seq // tq):
                rows = slice(j * tq, (j + 1) * tq)
                vslt_s[g, j] = _value_tile(vsl_ref[rows, sl].astype(BF16))
                vwnt_s[g, j] = _value_tile(vwn_ref[rows, sl].astype(BF16))

    lanes = lambda parts: jnp.concatenate(parts, axis=1)
    key = lax.broadcasted_iota(jnp.int32, (tq, tq), 0)
    qry = lax.broadcasted_iota(jnp.int32, (tq, tq), 1)
    causal = lanes([key <= qry] * hpg)
    upper = lanes([key > qry] * hpg)
    cend = lax.broadcasted_iota(jnp.int32, (ncmp, tq), 0) * NSA_CMP_STRIDE + (NSA_CMP_LEN - 1)
    tpos = lax.broadcasted_iota(jnp.int32, (ncmp, tq), 1) + t_start
    cvis = lanes([cend <= tpos] * hpg)
    oj = lax.broadcasted_iota(jnp.int32, (nsel, ncmp), 0) * sb
    on = lax.broadcasted_iota(jnp.int32, (nsel, ncmp), 1) * NSA_CMP_STRIDE
    overlap = ((on < oj + sb) & (on + NSA_CMP_LEN > oj) & (on < seq - NSA_CMP_STRIDE)).astype(BF16)
    jidx = lax.broadcasted_iota(jnp.int32, (nsel, tq), 0)
    own = (lax.broadcasted_iota(jnp.int32, (nsel, tq), 1) + t_start) // sb
    gates = jax.nn.sigmoid(gt_ref[...]).T
    tile_rows = lambda j: pl.ds(pl.multiple_of(j * tq, tq), tq)
    back = lambda k: jnp.maximum(qi - k, 0)

    groups = range(NSA_KV_GROUPS)
    heads = [[g * hpg + r for r in range(hpg)] for g in groups]
    t0 = [lanes([t0_ref[h] for h in heads[g]]) for g in groups]
    t1 = [lanes([t1_ref[h] for h in heads[g]]) for g in groups]

    qplain, win_scores = [], []
    for g in groups:
        width = hpg * HEAD_DIM
        qg = _transpose_bf16(q_ref[:, g * width:(g + 1) * width].astype(BF16), width)
        qplain.append(lanes([qg[r * HEAD_DIM:(r + 1) * HEAD_DIM] for r in range(hpg)]).astype(BF16))
        sc_s[g, CMP_PAD:, :] = _dot(kc_ref[0, g], qplain[g])
        win_scores.append([_dot(kwn_s[g, tile_rows(back(k)), :], qplain[g]) for k in range(nwin + 1)])

    o_cmps, imps = [], []
    for g in groups:
        win = pl.ds(pl.multiple_of(qi * (tq // NSA_CMP_STRIDE), 8), CMP_WIN)
        sc_s[g, win, :] += lanes([wc_ref[h] for h in heads[g]])
        s = jnp.where(cvis, sc_s[g, CMP_PAD:, :], NEG)
        m = jnp.max(s, axis=0, keepdims=True)
        e = jnp.where(cvis, jnp.exp2(s - m), 0.0)
        den = jnp.sum(e, axis=0, keepdims=True)
        p = e * (1.0 / jnp.where(den > 0, den, 1.0))
        o_cmps.append(_dot(vc_ref[0, g], p.astype(BF16)))
        psum = p[:, 0:tq]
        for r in range(1, hpg):
            psum = psum + p[:, r * tq:(r + 1) * tq]
        ph, plo = _split_bf16(psum)
        imps.append(_dot(overlap, ph) + _dot(overlap, plo))

    o_wins = []
    for g in groups:
        tiles = []
        for k, s in enumerate(win_scores[g]):
            if k == 0:
                s = jnp.where(causal, s, NEG) + t0[g]
            elif k == 1:
                s = jnp.where(qi >= 1, s, NEG) + t1[g]
            elif k < nwin:
                s = jnp.where(qi >= k, s, NEG)
            else:
                s = jnp.where(upper & (qi >= k), s, NEG)
            tiles.append((s, vwnt_s[g, back(k)]))
        o_wins.append(_softmax_finish(_softmax_joint(tiles)))

    qaugs = []
    for g in groups:
        imp = jnp.where((jidx == 0) | (jidx == own) | (jidx == own - 1), -NEG, imps[g])
        imp = jnp.where(jidx > own, NEG, imp)
        sel = _rank_select(imp, jidx, NSA_SEL_TOPN) & (jidx <= own)
        selb = jnp.where(sel, 0.0, NEG).astype(BF16)
        qaugs.append(jnp.concatenate([qplain[g], lanes([selb] * hpg),
                                      jnp.zeros((LANES - HEAD_DIM - nsel, hpg * tq), BF16)], axis=0))

    odd = (qi >= 2) & (qi % 2 == 0)
    slc_near = [[_dot(kslaug_s[g, tile_rows(back(k)), :], qaugs[g]) for k in range(3)] for g in groups]
    slc_state = tuple(_softmax_joint([
        (jnp.where(causal, slc_near[g][0], NEG) + t0[g], vslt_s[g, qi]),
        (jnp.where(qi >= 1, slc_near[g][1], NEG) + t1[g], vslt_s[g, back(1)]),
        (jnp.where(odd, slc_near[g][2], NEG), vslt_s[g, back(2)])]) for g in groups)

    def slc_far(i, carry):
        slab = pl.ds(pl.multiple_of(i * (2 * tq), 2 * tq), 2 * tq)
        scores = [_dot(kslaug_s[g, slab, :], qaugs[g]) for g in groups]
        return tuple(_softmax_update(carry[g], scores[g], [vslt_s[g, 2 * i], vslt_s[g, 2 * i + 1]])
                     for g in groups)

    slc_state = lax.fori_loop(0, jnp.maximum(qi - 1, 0) // 2, slc_far, slc_state)

    outs = []
    for g in range(NSA_KV_GROUPS):
        o_slc = _softmax_finish(slc_state[g])
        for r in range(hpg):
            h = g * hpg + r
            cols = slice(r * tq, (r + 1) * tq)
            outs.append(gates[h:h + 1, :] * o_cmps[g][:, cols]
                        + gates[NSA_HEADS + h:NSA_HEADS + h + 1, :] * o_slc[:, cols]
                        + gates[2 * NSA_HEADS + h:2 * NSA_HEADS + h + 1, :] * o_wins[g][:, cols])
    o_ref[...] = jnp.concatenate(outs, axis=0).T.astype(BF16)


def _nsa(proj, kcn, vct, wc, t0, t1, batch, seq):
    tq = NSA_TQ
    nq = seq // tq
    ncmp = seq // NSA_CMP_STRIDE
    width = NSA_HEADS * HEAD_DIM
    groups = NSA_KV_GROUPS
    kv = lambda col: pl.BlockSpec((seq, LANES), lambda b, i: (b, col // LANES))
    const = lambda shape: pl.BlockSpec(shape, lambda b, i: (0,) * len(shape))
    return pl.pallas_call(
        functools.partial(_nsa_kernel, seq=seq),
        grid=(batch, nq),
        in_specs=[pl.BlockSpec((tq, width), lambda b, i: (b * nq + i, C_NSQ // width)),
                  pl.BlockSpec((1, groups, ncmp, HEAD_DIM), lambda b, i: (b, 0, 0, 0)),
                  pl.BlockSpec((1, groups, HEAD_DIM, ncmp), lambda b, i: (b, 0, 0, 0)),
                  kv(C_KSL), kv(C_VSL), kv(C_KWN), kv(C_VWN),
                  pl.BlockSpec((tq, LANES), lambda b, i: (b * nq + i, C_NG // LANES)),
                  const((NSA_HEADS, CMP_WIN, tq)), const((NSA_HEADS, tq, tq)),
                  const((NSA_HEADS, tq, tq))],
        out_specs=pl.BlockSpec((tq, width), lambda b, i: (b * nq + i, 0)),
        out_shape=jax.ShapeDtypeStruct((batch * seq, width), BF16),
        scratch_shapes=[pltpu.VMEM((groups, seq, LANES), BF16),
                        pltpu.VMEM((groups, nq, VROWS, tq), BF16),
                        pltpu.VMEM((groups, seq, HEAD_DIM), BF16),
                        pltpu.VMEM((groups, nq, VROWS, tq), BF16),
                        pltpu.VMEM((groups, CMP_PAD + ncmp, NSA_HPG * tq), F32)],
        compiler_params=_params(("parallel", "arbitrary")),
        name="nsa",
    )(proj, kcn, vct, proj, proj, proj, proj, proj, wc, t0, t1)


def _merge_kernel(x_ref, ya_ref, yb_ref, ga_ref, gb_ref, wa_ref, wb_ref, wo_ref, o_ref):
    a = _dot(ya_ref[...], wa_ref[...])
    b = _dot(yb_ref[...], wb_ref[...])
    z = jax.nn.sigmoid(ga_ref[...]) * a + jax.nn.sigmoid(gb_ref[...]) * b
    o_ref[...] = x_ref[...] + _dot(z.astype(BF16), wo_ref[...])


def _merge(xf, ya, yb, proj, wa, wb, wo, layer):
    n, d = xf.shape
    tm = MERGE_TM
    row = lambda w, col=0: pl.BlockSpec((tm, w), lambda i: (i, col))
    return pl.pallas_call(
        _merge_kernel,
        grid=(n // tm,),
        in_specs=[row(d), row(ya.shape[1]), row(yb.shape[1]), row(d, C_GA // d), row(d, C_GB // d),
                  _layer_spec(wa, layer), _layer_spec(wb, layer), _layer_spec(wo, layer)],
        out_specs=row(d),
        out_shape=jax.ShapeDtypeStruct((n, d), F32),
        compiler_params=_params(("parallel",)),
        name="merge",
    )(xf, ya, yb, proj, proj, wa, wb, wo)


FFN_HALO = 16
FFN_TM = 1024
FFN_TF = 256


def _ffn_kernel(x_ref, xh_ref, g_ref, wu_ref, cw_ref, cb_ref, wd_ref, p_ref, wg_ref, wp_ref,
                o_ref, act_s, *, seq, d_ff):
    i = pl.program_id(0)
    tm = x_ref.shape[0]
    x = x_ref[...]
    at_start = (i * tm) % seq == 0
    halo = jnp.where(at_start, 0.0, _rms(xh_ref[...], g_ref[...]))
    hn = jnp.concatenate([halo.astype(BF16), _rms(x, g_ref[...]).astype(BF16)], axis=0)

    def conv(cols):
        u = _dot(hn, wu_ref[:, cols])
        u1 = pltpu.roll(u, 1, 0)[FFN_HALO:]
        u2 = pltpu.roll(u, 2, 0)[FFN_HALO:]
        cw = cw_ref[:, cols]
        return cw[0:1] * u2 + cw[1:2] * u1 + cw[2:3] * u[FFN_HALO:] + cb_ref[:, cols]

    for c in range(d_ff // FFN_TF):
        lo = c * FFN_TF
        act = jax.nn.gelu(conv(slice(lo, lo + FFN_TF))) * conv(slice(d_ff + lo, d_ff + lo + FFN_TF))
        act_s[:, lo:lo + FFN_TF] = act.astype(BF16)

    x = x + _dot(act_s[...], wd_ref[...])
    gate = jax.nn.sigmoid(_dot(x.astype(BF16), wg_ref[...]))
    o_ref[...] = x + gate * _dot(p_ref[...].astype(BF16), wp_ref[...])


def _ffn_ple(xf, gain, w_up, conv_w, conv_b, w_down, pf, wg, wp, layer, seq):
    n, d = xf.shape
    d_ff = w_down.shape[1]
    tm = FFN_TM
    hb = tm // FFN_HALO
    resident = lambda a: _layer_spec(a, layer)
    return pl.pallas_call(
        functools.partial(_ffn_kernel, seq=seq, d_ff=d_ff),
        grid=(n // tm,),
        in_specs=[pl.BlockSpec((tm, d), lambda i: (i, 0)),
                  pl.BlockSpec((FFN_HALO, d), lambda i: (jnp.maximum(i * hb - 1, 0), 0)),
                  resident(gain), resident(w_up), resident(conv_w), resident(conv_b), resident(w_down),
                  pl.BlockSpec((None, tm, pf.shape[2]), lambda i: (layer, i, 0)),
                  resident(wg), resident(wp)],
        out_specs=pl.BlockSpec((tm, d), lambda i: (i, 0)),
        out_shape=jax.ShapeDtypeStruct((n, d), F32),
        scratch_shapes=[pltpu.VMEM((tm, d_ff), BF16)],
        compiler_params=_params(("parallel",)),
        name="conv_ffn_ple",
    )(xf, xf, gain, w_up, conv_w, conv_b, w_down, pf, wg, wp)


def _reorder_in_proj(w):
    attn = 3 * MOBA_HEADS * HEAD_DIM + NSA_HEADS * HEAD_DIM + 6 * NSA_KV_GROUPS * HEAD_DIM
    ng = 3 * NSA_HEADS
    pad = jnp.zeros(w.shape[:-1] + (PROJ_COLS - w.shape[-1],), w.dtype)
    return jnp.concatenate([w[..., attn + ng:], w[..., :attn], w[..., attn:attn + ng], pad], axis=-1)


def _head_gain_rows(moba_q, moba_k, nsa_q, nsa_k):
    rows = jnp.zeros((moba_q.shape[0], 1, PROJ_COLS), F32)
    for col, gain, heads in ((C_MQ, moba_q * QSCALE, MOBA_HEADS), (C_MK, moba_k, MOBA_HEADS),
                             (C_NSQ, nsa_q * QSCALE, NSA_HEADS), (C_KSL, nsa_k[:, 1], NSA_KV_GROUPS),
                             (C_KWN, nsa_k[:, 2], NSA_KV_GROUPS)):
        rows = rows.at[:, 0, col:col + heads * HEAD_DIM].set(jnp.tile(gain, (1, heads)))
    return rows


def _cmp_weights(w1, pos):
    depth = w1.shape[0]
    half = NSA_CMP_STRIDE * HEAD_DIM
    w1cat = jnp.concatenate([w1[:, :half], w1[:, half:]], axis=2).astype(BF16)
    tok = w1cat.reshape(depth, NSA_CMP_STRIDE, HEAD_DIM, -1)
    zero = jnp.zeros_like(tok)
    placed = jnp.stack([jnp.concatenate([tok, zero], axis=2), jnp.concatenate([zero, tok], axis=2)], axis=1)
    posr = jnp.zeros((depth, 8, half), F32).at[:, 0:2].set(pos.reshape(depth, 2, half))
    return w1cat, placed, posr


def kernel(x, p, rel_bias, attn_norm, w_in, moba_q_gain, moba_k_gain, nsa_q_gain, nsa_k_gain,
           cmp_pos_k, cmp_w1_k, cmp_w2_k, cmp_pos_v, cmp_w1_v, cmp_w2_v,
           w_br_moba, w_br_nsa, w_o, ffn_norm, w_up, conv_w, conv_b, w_down, w_ple_gate, w_ple):
    batch, seq, d = x.shape
    n = batch * seq
    depth = w_in.shape[0]
    bf = lambda a: a.astype(BF16)
    w_in_r = _reorder_in_proj(bf(w_in))
    head_gain = _head_gain_rows(moba_q_gain, moba_k_gain, nsa_q_gain, nsa_k_gain)
    attn_gain, ffn_gain = attn_norm[:, None, :], ffn_norm[:, None, :]
    w1k, wtk, pk = _cmp_weights(cmp_w1_k, cmp_pos_k)
    w1v, wtv, pv = _cmp_weights(cmp_w1_v, cmp_pos_v)
    w2k, w2v, kg0 = bf(cmp_w2_k), bf(cmp_w2_v), nsa_k_gain[:, 0:1]
    wa, wb, wo = bf(w_br_moba), bf(w_br_nsa), bf(w_o)
    wu, wd, wg, wp = bf(w_up), bf(w_down), bf(w_ple_gate), bf(w_ple)
    conv_b = conv_b[:, None, :]
    pf = p.reshape(depth, n, -1)

    t0m, t1m, t0n, t1n, wc = _bias_tables(rel_bias)
    xf = x.reshape(n, d)
    for i in range(depth):
        proj = _inproj(xf, attn_gain, w_in_r, head_gain, i)
        ya = _moba(proj, t0m, t1m, batch, seq)
        kcn, vct = _compress(proj, w1k, w1v, wtk, wtv, w2k, w2v, pk, pv, kg0, i, batch, seq)
        yb = _nsa(proj, kcn, vct, wc, t0n, t1n, batch, seq)
        xf = _merge(xf, ya, yb, proj, wa, wb, wo, i)
        xf = _ffn_ple(xf, ffn_gain, wu, conv_w, conv_b, wd, pf, wg, wp, i, seq)
    return xf.reshape(batch, seq, d)
```

```python
import functools
import math

import numpy as np
import jax
import jax.numpy as jnp
from jax import lax
from jax.experimental import pallas as pl
from jax.experimental.pallas import tpu as pltpu

F32 = jnp.float32
BF16 = jnp.bfloat16

HEAD_DIM = 64
MOBA_HEADS = 8
MOBA_BLOCK = 256
MOBA_TOPK = 3
NSA_HEADS = 8
NSA_KV_GROUPS = 2
NSA_HPG = NSA_HEADS // NSA_KV_GROUPS
NSA_CMP_LEN = 32
NSA_CMP_STRIDE = 16
NSA_CMP_HIDDEN = 2 * HEAD_DIM
NSA_SEL_BLOCK = 64
NSA_SEL_TOPN = 16
NSA_WINDOW = 512
REL_BUCKETS = 32
REL_MAX_DIST = 128
RMS_EPS = 1e-6
SCALE = HEAD_DIM ** -0.5
LOG2E = math.log2(math.e)
QSCALE = SCALE * LOG2E
NEG = -1e30

LANES = 128
V7X_VMEM_BYTES = 64 * 1024 * 1024
VMEM_LIMIT = V7X_VMEM_BYTES * 7 // 8

INPROJ_TM = 512
MERGE_TM = 512

C_GA, C_GB, C_MQ, C_MK, C_MV, C_NSQ = 0, 1024, 2048, 2560, 3072, 3584
C_KC, C_VC, C_KSL, C_VSL, C_KWN, C_VWN, C_NG = 4096, 4224, 4352, 4480, 4608, 4736, 4864
PROJ_COLS = 4992

NSA_TQ = 128
NSA_TPS = 2
CMP_WIN = 16
CMP_PAD = 8


def _dot(a, b):
    return jnp.dot(a, b, preferred_element_type=F32)


def _dot_nt(a, b):
    return lax.dot_general(a, b, (((1,), (1,)), ((), ())), preferred_element_type=F32)


def _rms(x, gain):
    return x * lax.rsqrt(jnp.mean(x * x, axis=-1, keepdims=True) + RMS_EPS) * gain


def _split_bf16(x):
    hi = x.astype(BF16)
    return hi, (x - hi.astype(F32)).astype(BF16)


def _eye(rows, cols):
    r = lax.broadcasted_iota(jnp.int32, (rows, cols), 0)
    c = lax.broadcasted_iota(jnp.int32, (rows, cols), 1)
    return (r == c).astype(BF16)


def _transpose_bf16(x, rows):
    return _dot_nt(_eye(rows, x.shape[1]), x)


VROWS = 80


def _value_tile(v):
    row = lax.broadcasted_iota(jnp.int32, (VROWS, v.shape[0]), 0)
    return jnp.where(row == HEAD_DIM, 1.0, _transpose_bf16(v, VROWS)).astype(BF16)


def _softmax_joint(tiles):
    m = functools.reduce(jnp.maximum, [jnp.max(s, axis=0, keepdims=True) for s, _ in tiles])
    acc = functools.reduce(jnp.add, [_dot(vt, jnp.exp2(s - m).astype(BF16)) for s, vt in tiles])
    return m, acc


def _softmax_update(carry, s, vts):
    m, acc = carry
    m_new = jnp.maximum(m, jnp.max(s, axis=0, keepdims=True))
    p = jnp.exp2(s - m_new).astype(BF16)
    rows = s.shape[0] // len(vts)
    pv = functools.reduce(jnp.add, [_dot(vt, p[i * rows:(i + 1) * rows]) for i, vt in enumerate(vts)])
    return m_new, jnp.exp2(m - m_new) * acc + pv


def _softmax_finish(carry):
    _, acc = carry
    return acc[0:HEAD_DIM] * (1.0 / acc[HEAD_DIM:HEAD_DIM + 1])


def _rank_select(score, idx, count):
    beaten = jnp.zeros(score.shape, jnp.int32)
    for i in range(score.shape[0]):
        si = score[i:i + 1, :]
        beaten += ((si > score) | ((si == score) & (i < idx))).astype(jnp.int32)
    return beaten < count


def _params(sem):
    return pltpu.CompilerParams(dimension_semantics=sem, vmem_limit_bytes=VMEM_LIMIT)


def _layer_spec(a, layer):
    return pl.BlockSpec((None,) + a.shape[1:], lambda *_: (layer,) + (0,) * (a.ndim - 1),
                        pipeline_mode=pl.Buffered(1))


def _rel_bucket_np(dist):
    n = np.maximum(dist, 0)
    max_exact = REL_BUCKETS // 2
    nf = np.maximum(n, 1).astype(np.float32)
    large = max_exact + (np.log(nf / np.float32(max_exact)) / np.float32(math.log(REL_MAX_DIST / max_exact))
                         * np.float32(REL_BUCKETS - max_exact)).astype(np.int32)
    return np.where(n < max_exact, n, np.minimum(large, REL_BUCKETS - 1))


def _bucket_starts():
    buckets = _rel_bucket_np(np.arange(4 * REL_MAX_DIST))
    return [int(np.argmax(buckets >= k)) for k in range(REL_BUCKETS)]


BUCKET_START = _bucket_starts()
BIAS_REACH = BUCKET_START[-1]
assert BIAS_REACH <= MOBA_BLOCK and BIAS_REACH <= NSA_TQ - NSA_CMP_LEN + 1 + NSA_CMP_STRIDE


def _tables_kernel(tab_ref, t0m_ref, t1m_ref, t0n_ref, t1n_ref, wc_ref):
    h = pl.program_id(0)

    def bias(dist, head):
        last = tab_ref[head, REL_BUCKETS - 1]
        val = jnp.zeros(dist.shape, F32)
        for k in range(REL_BUCKETS - 2, -1, -1):
            val = jnp.where(dist < BUCKET_START[k + 1], (tab_ref[head, k] - last) * LOG2E, val)
        return val

    def toeplitz(size, offset, head):
        key = lax.broadcasted_iota(jnp.int32, (size, size), 0)
        qry = lax.broadcasted_iota(jnp.int32, (size, size), 1)
        return bias(offset + qry - key, head)

    t0m_ref[0] = toeplitz(MOBA_BLOCK, 0, h)
    t1m_ref[0] = toeplitz(MOBA_BLOCK, MOBA_BLOCK, h)
    t0n_ref[0] = toeplitz(NSA_TQ, 0, MOBA_HEADS + h)
    t1n_ref[0] = toeplitz(NSA_TQ, NSA_TQ, MOBA_HEADS + h)
    a = lax.broadcasted_iota(jnp.int32, (CMP_WIN, NSA_TQ), 0)
    i = lax.broadcasted_iota(jnp.int32, (CMP_WIN, NSA_TQ), 1)
    wc_ref[0] = bias(i + (NSA_TQ - NSA_CMP_LEN + 1) - NSA_CMP_STRIDE * a, MOBA_HEADS + h)


def _bias_tables(rel_bias):
    blk, tq = MOBA_BLOCK, NSA_TQ
    shapes = [(blk, blk), (blk, blk), (tq, tq), (tq, tq), (CMP_WIN, tq)]
    return pl.pallas_call(
        _tables_kernel,
        grid=(MOBA_HEADS,),
        in_specs=[pl.BlockSpec(memory_space=pltpu.SMEM)],
        out_specs=[pl.BlockSpec((1,) + s, lambda h: (h, 0, 0)) for s in shapes],
        out_shape=[jax.ShapeDtypeStruct((MOBA_HEADS,) + s, F32) for s in shapes],
        compiler_params=_params(("arbitrary",)),
        name="bias_tables",
    )(rel_bias)


NORM_SLABS = (list(range(C_MQ, C_MV, LANES)) + list(range(C_NSQ, C_KC, LANES)) + [C_KSL, C_KWN])


def _inproj_kernel(x_ref, g_ref, w_ref, hg_ref, o_ref):
    h = _rms(x_ref[...], g_ref[...]).astype(BF16)
    y = _dot(h, w_ref[...])
    first = lax.broadcasted_iota(jnp.int32, (1, LANES), 1) < HEAD_DIM
    edges = sorted(set([0, PROJ_COLS] + NORM_SLABS + [c + LANES for c in NORM_SLABS]))
    for lo, hi in zip(edges[:-1], edges[1:]):
        t = y[:, lo:hi]
        if lo in NORM_SLABS:
            sq = t * t
            s0 = jnp.sum(jnp.where(first, sq, 0.0), axis=-1, keepdims=True)
            s1 = jnp.sum(jnp.where(first, 0.0, sq), axis=-1, keepdims=True)
            ms = jnp.where(first, s0, s1) * (1.0 / HEAD_DIM)
            t = t * lax.rsqrt(ms + RMS_EPS) * hg_ref[:, lo:hi]
        o_ref[:, lo:hi] = t


def _inproj(xf, gain, w, head_gain, layer):
    n, d = xf.shape
    tm = INPROJ_TM
    return pl.pallas_call(
        _inproj_kernel,
        grid=(n // tm,),
        in_specs=[pl.BlockSpec((tm, d), lambda i: (i, 0)), _layer_spec(gain, layer),
                  _layer_spec(w, layer), _layer_spec(head_gain, layer)],
        out_specs=pl.BlockSpec((tm, PROJ_COLS), lambda i: (i, 0)),
        out_shape=jax.ShapeDtypeStruct((n, PROJ_COLS), F32),
        compiler_params=_params(("parallel",)),
        name="inproj",
    )(xf, gain, w, head_gain)


MOBA_HPS = 8


def _moba_kernel(q_ref, k_ref, v_ref, t0_ref, t1_ref, o_ref, kaug_s, vt_s, km_s, *, seq):
    n = pl.program_id(2)
    blk = MOBA_BLOCK
    nb = seq // blk
    head_cols = lambda hh: slice(hh * HEAD_DIM, (hh + 1) * HEAD_DIM)

    @pl.when(n == 0)
    def _prepare_keys():
        r = lax.broadcasted_iota(jnp.int32, (seq, LANES), 0)
        c = lax.broadcasted_iota(jnp.int32, (seq, LANES), 1)
        onehot = (c - HEAD_DIM == r // blk).astype(F32)
        place = _eye(HEAD_DIM, LANES)
        for hh in range(MOBA_HPS):
            kn = k_ref[:, head_cols(hh)]
            km = kn.reshape(nb, blk, HEAD_DIM).sum(axis=1) * (1.0 / blk)
            km_s[hh] = jnp.concatenate([km, jnp.zeros((16 - nb, HEAD_DIM), F32)], axis=0)
            kaug_s[hh] = (_dot(kn.astype(BF16), place) + onehot).astype(BF16)
            for j in range(nb):
                vj = v_ref[j * blk:(j + 1) * blk, head_cols(hh)].astype(BF16)
                vt_s[hh, j] = _value_tile(vj)

    jidx = lax.broadcasted_iota(jnp.int32, (16, blk), 0)
    past = jidx < n
    key = lax.broadcasted_iota(jnp.int32, (blk, blk), 0)
    qry = lax.broadcasted_iota(jnp.int32, (blk, blk), 1)
    prev = jnp.maximum(n - 1, 0)
    heads = range(MOBA_HPS)
    qts = []
    for pair in range(MOBA_HPS // 2):
        qp = _transpose_bf16(q_ref[:, pair * LANES:(pair + 1) * LANES].astype(BF16), LANES)
        qts += [qp[0:HEAD_DIM].astype(BF16), qp[HEAD_DIM:].astype(BF16)]
    gates = []
    for hh in heads:
        kmh, kml = _split_bf16(km_s[hh])
        gates.append(_dot(kmh, qts[hh]) + _dot(kml, qts[hh]))
    qaugs, state = [], []
    for hh in heads:
        gate = jnp.where(past, gates[hh], NEG)
        sel = (past & _rank_select(gate, jidx, MOBA_TOPK)) | (jidx == n)
        selb = jnp.where(sel, 0.0, NEG).astype(BF16)
        qaugs.append(jnp.concatenate([qts[hh], selb, jnp.zeros((LANES - HEAD_DIM - 16, blk), BF16)],
                                     axis=0))

    def scores(hh, j):
        start = pl.multiple_of(j * blk, blk)
        return _dot(kaug_s[hh, pl.ds(start, blk), :], qaugs[hh]), vt_s[hh, j]

    near = [(scores(hh, n), scores(hh, prev)) for hh in range(MOBA_HPS)]
    for hh, ((s0, vt0), (s1, vt1)) in enumerate(near):
        s0 = jnp.where(key <= qry, s0, NEG) + t0_ref[hh]
        s1 = jnp.where(n >= 1, s1, NEG) + t1_ref[hh]
        state.append(_softmax_joint([(s0, vt0), (s1, vt1)]))

    def far_body(j, carry):
        tiles = [scores(hh, j) for hh in range(MOBA_HPS)]
        return tuple(_softmax_update(carry[hh], tiles[hh][0], [tiles[hh][1]]) for hh in range(MOBA_HPS))

    state = lax.fori_loop(0, jnp.maximum(n - 1, 0), far_body, tuple(state))
    o_ref[...] = jnp.concatenate([_softmax_finish(c) for c in state], axis=0).T.astype(BF16)


def _moba(proj, t0, t1, batch, seq):
    blk = MOBA_BLOCK
    nb = seq // blk
    width = MOBA_HPS * HEAD_DIM
    qc, kc, vc = C_MQ // width, C_MK // width, C_MV // width
    return pl.pallas_call(
        functools.partial(_moba_kernel, seq=seq),
        grid=(batch, MOBA_HEADS // MOBA_HPS, nb),
        in_specs=[pl.BlockSpec((blk, width), lambda b, h, n: (b * nb + n, qc + h)),
                  pl.BlockSpec((seq, width), lambda b, h, n: (b, kc + h)),
                  pl.BlockSpec((seq, width), lambda b, h, n: (b, vc + h)),
                  pl.BlockSpec((MOBA_HPS, blk, blk), lambda b, h, n: (h, 0, 0)),
                  pl.BlockSpec((MOBA_HPS, blk, blk), lambda b, h, n: (h, 0, 0))],
        out_specs=pl.BlockSpec((blk, width), lambda b, h, n: (b * nb + n, h)),
        out_shape=jax.ShapeDtypeStruct((batch * seq, MOBA_HEADS * HEAD_DIM), BF16),
        scratch_shapes=[pltpu.VMEM((MOBA_HPS, seq, LANES), BF16),
                        pltpu.VMEM((MOBA_HPS, nb, VROWS, blk), BF16),
                        pltpu.VMEM((MOBA_HPS, 16, HEAD_DIM), F32)],
        compiler_params=_params(("parallel", "parallel", "arbitrary")),
        name="moba",
    )(proj, proj, proj, t0, t1)


def _compress_kernel(k_ref, v_ref, w1k_ref, w1v_ref, wtk_ref, wtv_ref, w2k_ref, w2v_ref, pk_ref, pv_ref,
                     kg_ref, ko_ref, vo_ref):
    hid = NSA_CMP_HIDDEN
    chunks = k_ref.shape[0] // NSA_CMP_STRIDE

    def compress(t_ref, w1_ref, wt_ref, w2_ref, pos_ref, g):
        a = functools.reduce(jnp.add, [
            _dot(t_ref[pl.ds(l, chunks, stride=NSA_CMP_STRIDE), :].astype(BF16), wt_ref[g, l])
            for l in range(NSA_CMP_STRIDE)])
        pw = _dot(pos_ref[...].astype(BF16), w1_ref[...])
        pos = pw[0:1, :hid] + pw[1:2, hid:]
        nxt = pltpu.roll(a[:, hid:], chunks - 1, 0)
        h = jax.nn.gelu(a[:, :hid] + nxt + pos)
        return _dot(h.astype(BF16), w2_ref[...])

    for g in range(NSA_KV_GROUPS):
        kc = compress(k_ref, w1k_ref, wtk_ref, w2k_ref, pk_ref, g)
        ko_ref[0, g] = _rms(kc, kg_ref[...]).astype(BF16)
        vc = compress(v_ref, w1v_ref, wtv_ref, w2v_ref, pv_ref, g).astype(BF16)
        vo_ref[0, g] = _transpose_bf16(vc, HEAD_DIM).astype(BF16)


def _compress(proj, w1k, w1v, wtk, wtv, w2k, w2v, pk, pv, kg, layer, batch, seq):
    groups = NSA_KV_GROUPS
    chunks = seq // NSA_CMP_STRIDE
    tok = lambda col: pl.BlockSpec((seq, LANES), lambda b: (b, col // LANES))
    return pl.pallas_call(
        _compress_kernel,
        grid=(batch,),
        in_specs=[tok(C_KC), tok(C_VC)] + [_layer_spec(a, layer)
                                           for a in (w1k, w1v, wtk, wtv, w2k, w2v, pk, pv, kg)],
        out_specs=[pl.BlockSpec((1, groups, chunks, HEAD_DIM), lambda b: (b, 0, 0, 0)),
                   pl.BlockSpec((1, groups, HEAD_DIM, chunks), lambda b: (b, 0, 0, 0))],
        out_shape=[jax.ShapeDtypeStruct((batch, groups, chunks, HEAD_DIM), BF16),
                   jax.ShapeDtypeStruct((batch, groups, HEAD_DIM, chunks), BF16)],
        compiler_params=_params(("parallel",)),
        name="nsa_compress",
    )(proj, proj, w1k, w1v, wtk, wtv, w2k, w2v, pk, pv, kg)


def _nsa_kernel(q_ref, kc_ref, vc_ref, ksl_ref, vsl_ref, kwn_ref, vwn_ref, gt_ref,
                wc_ref, t0_ref, t1_ref, o_ref, kslaug_s, vslt_s, kwn_s, vwnt_s, sc_s, *, seq):
    step = pl.program_id(1)
    tq = NSA_TQ
    hpg = NSA_HPG
    sb = NSA_SEL_BLOCK
    nsel = seq // sb
    ncmp = seq // NSA_CMP_STRIDE
    nwin = NSA_WINDOW // tq
    groups = range(NSA_KV_GROUPS)
    items = [(t, g) for t in range(NSA_TPS) for g in groups]
    qis = [step * NSA_TPS + t for t in range(NSA_TPS)]

    @pl.when(step == 0)
    def _prepare_keys():
        r = lax.broadcasted_iota(jnp.int32, (seq, LANES), 0)
        c = lax.broadcasted_iota(jnp.int32, (seq, LANES), 1)
        onehot = (c - HEAD_DIM == r // sb).astype(F32)
        place = _eye(HEAD_DIM, LANES)
        sc_s[:, 0:CMP_PAD, :] = jnp.zeros((NSA_TPS * NSA_KV_GROUPS, CMP_PAD, hpg * tq), F32)
        for g in groups:
            sl = slice(g * HEAD_DIM, (g + 1) * HEAD_DIM)
            kslaug_s[g] = (_dot(ksl_ref[:, sl].astype(BF16), place) + onehot).astype(BF16)
            kwn_s[g] = kwn_ref[:, sl].astype(BF16)
            for j in range(seq // tq):
                rows = slice(j * tq, (j + 1) * tq)
                vslt_s[g, j] = _value_tile(vsl_ref[rows, sl].astype(BF16))
                vwnt_s[g, j] = _value_tile(vwn_ref[rows, sl].astype(BF16))

    lanes = lambda parts: jnp.concatenate(parts, axis=1)
    key = lax.broadcasted_iota(jnp.int32, (tq, tq), 0)
    qry = lax.broadcasted_iota(jnp.int32, (tq, tq), 1)
    causal = lanes([key <= qry] * hpg)
    upper = lanes([key > qry] * hpg)
    oj = lax.broadcasted_iota(jnp.int32, (nsel, ncmp), 0) * sb
    on = lax.broadcasted_iota(jnp.int32, (nsel, ncmp), 1) * NSA_CMP_STRIDE
    overlap = ((on < oj + sb) & (on + NSA_CMP_LEN > oj) & (on < seq - NSA_CMP_STRIDE)).astype(BF16)
    jidx = lax.broadcasted_iota(jnp.int32, (nsel, tq), 0)
    cend = lax.broadcasted_iota(jnp.int32, (ncmp, tq), 0) * NSA_CMP_STRIDE + (NSA_CMP_LEN - 1)
    cvis = [lanes([cend <= lax.broadcasted_iota(jnp.int32, (ncmp, tq), 1) + qi * tq] * hpg) for qi in qis]
    own = [(lax.broadcasted_iota(jnp.int32, (nsel, tq), 1) + qi * tq) // sb for qi in qis]
    tile_rows = lambda j: pl.ds(pl.multiple_of(j * tq, tq), tq)
    back = lambda t, k: jnp.maximum(qis[t] - k, 0)
    heads = [[g * hpg + r for r in range(hpg)] for g in groups]
    t0 = [lanes([t0_ref[h] for h in heads[g]]) for g in groups]
    t1 = [lanes([t1_ref[h] for h in heads[g]]) for g in groups]
    slot = lambda t, g: t * NSA_KV_GROUPS + g

    qplain, win_scores = {}, {}
    for t, g in items:
        width = hpg * HEAD_DIM
        qg = _transpose_bf16(q_ref[t * tq:(t + 1) * tq, g * width:(g + 1) * width].astype(BF16), width)
        qplain[t, g] = lanes([qg[r * HEAD_DIM:(r + 1) * HEAD_DIM] for r in range(hpg)]).astype(BF16)
        sc_s[slot(t, g), CMP_PAD:, :] = _dot(kc_ref[0, g], qplain[t, g])
        win_scores[t, g] = [_dot(kwn_s[g, tile_rows(back(t, k)), :], qplain[t, g]) for k in range(nwin + 1)]

    o_cmps, imps = {}, {}
    for t, g in items:
        win = pl.ds(pl.multiple_of(qis[t] * (tq // NSA_CMP_STRIDE), 8), CMP_WIN)
        sc_s[slot(t, g), win, :] += lanes([wc_ref[h] for h in heads[g]])
        s = jnp.where(cvis[t], sc_s[slot(t, g), CMP_PAD:, :], NEG)
        m = jnp.max(s, axis=0, keepdims=True)
        e = jnp.where(cvis[t], jnp.exp2(s - m), 0.0)
        den = jnp.sum(e, axis=0, keepdims=True)
        p = e * (1.0 / jnp.where(den > 0, den, 1.0))
        o_cmps[t, g] = _dot(vc_ref[0, g], p.astype(BF16))
        psum = p[:, 0:tq]
        for r in range(1, hpg):
            psum = psum + p[:, r * tq:(r + 1) * tq]
        ph, plo = _split_bf16(psum)
        imps[t, g] = _dot(overlap, ph) + _dot(overlap, plo)

    o_wins = {}
    for t, g in items:
        qi = qis[t]
        tiles = []
        for k, s in enumerate(win_scores[t, g]):
            if k == 0:
                s = jnp.where(causal, s, NEG) + t0[g]
            elif k == 1:
                s = jnp.where(qi >= 1, s, NEG) + t1[g]
            elif k < nwin:
                s = jnp.where(qi >= k, s, NEG)
            else:
                s = jnp.where(upper & (qi >= k), s, NEG)
            tiles.append((s, vwnt_s[g, back(t, k)]))
        o_wins[t, g] = _softmax_finish(_softmax_joint(tiles))

    qaugs = {}
    for t, g in items:
        imp = jnp.where((jidx == 0) | (jidx == own[t]) | (jidx == own[t] - 1), -NEG, imps[t, g])
        imp = jnp.where(jidx > own[t], NEG, imp)
        sel = _rank_select(imp, jidx, NSA_SEL_TOPN) & (jidx <= own[t])
        selb = jnp.where(sel, 0.0, NEG).astype(BF16)
        qaugs[t, g] = jnp.concatenate([qplain[t, g], lanes([selb] * hpg),
                                       jnp.zeros((LANES - HEAD_DIM - nsel, hpg * tq), BF16)], axis=0)

    slc_near = {(t, g): [_dot(kslaug_s[g, tile_rows(back(t, k)), :], qaugs[t, g]) for k in range(3)]
                for t, g in items}
    state = {}
    for t, g in items:
        qi = qis[t]
        odd = (qi >= 2) & (qi % 2 == 0)
        state[t, g] = _softmax_joint([
            (jnp.where(causal, slc_near[t, g][0], NEG) + t0[g], vslt_s[g, qi]),
            (jnp.where(qi >= 1, slc_near[t, g][1], NEG) + t1[g], vslt_s[g, back(t, 1)]),
            (jnp.where(odd, slc_near[t, g][2], NEG), vslt_s[g, back(t, 2)])])

    slabs = [jnp.maximum(qi - 1, 0) // 2 for qi in qis]
    lo = 0
    for first in range(NSA_TPS):
        live = [(t, g) for t, g in items if t >= first]

        def slc_far(i, carry, live=live):
            slab = pl.ds(pl.multiple_of(i * (2 * tq), 2 * tq), 2 * tq)
            scores = [_dot(kslaug_s[g, slab, :], qaugs[t, g]) for t, g in live]
            return tuple(_softmax_update(c, sc, [vslt_s[g, 2 * i], vslt_s[g, 2 * i + 1]])
                         for c, sc, (t, g) in zip(carry, scores, live))

        new = lax.fori_loop(lo, slabs[first], slc_far, tuple(state[it] for it in live))
        state.update(dict(zip(live, new)))
        lo = slabs[first]

    for t in range(NSA_TPS):
        gates = jax.nn.sigmoid(gt_ref[t * tq:(t + 1) * tq, :]).T
        outs = []
        for g in groups:
            o_slc = _softmax_finish(state[t, g])
            for r in range(hpg):
                h = g * hpg + r
                cols = slice(r * tq, (r + 1) * tq)
                outs.append(gates[h:h + 1, :] * o_cmps[t, g][:, cols]
                            + gates[NSA_HEADS + h:NSA_HEADS + h + 1, :] * o_slc[:, cols]
                            + gates[2 * NSA_HEADS + h:2 * NSA_HEADS + h + 1, :] * o_wins[t, g][:, cols])
        o_ref[t * tq:(t + 1) * tq, :] = jnp.concatenate(outs, axis=0).T.astype(BF16)


def _nsa(proj, kcn, vct, wc, t0, t1, batch, seq):
    tq = NSA_TQ
    nq = seq // tq
    rows = NSA_TPS * tq
    steps = nq // NSA_TPS
    ncmp = seq // NSA_CMP_STRIDE
    width = NSA_HEADS * HEAD_DIM
    groups = NSA_KV_GROUPS
    kv = lambda col: pl.BlockSpec((seq, LANES), lambda b, i: (b, col // LANES))
    const = lambda shape: pl.BlockSpec(shape, lambda b, i: (0,) * len(shape))
    return pl.pallas_call(
        functools.partial(_nsa_kernel, seq=seq),
        grid=(batch, steps),
        in_specs=[pl.BlockSpec((rows, width), lambda b, i: (b * steps + i, C_NSQ // width)),
                  pl.BlockSpec((1, groups, ncmp, HEAD_DIM), lambda b, i: (b, 0, 0, 0)),
                  pl.BlockSpec((1, groups, HEAD_DIM, ncmp), lambda b, i: (b, 0, 0, 0)),
                  kv(C_KSL), kv(C_VSL), kv(C_KWN), kv(C_VWN),
                  pl.BlockSpec((rows, LANES), lambda b, i: (b * steps + i, C_NG // LANES)),
                  const((NSA_HEADS, CMP_WIN, tq)), const((NSA_HEADS, tq, tq)),
                  const((NSA_HEADS, tq, tq))],
        out_specs=pl.BlockSpec((rows, width), lambda b, i: (b * steps + i, 0)),
        out_shape=jax.ShapeDtypeStruct((batch * seq, width), BF16),
        scratch_shapes=[pltpu.VMEM((groups, seq, LANES), BF16),
                        pltpu.VMEM((groups, nq, VROWS, tq), BF16),
                        pltpu.VMEM((groups, seq, HEAD_DIM), BF16),
                        pltpu.VMEM((groups, nq, VROWS, tq), BF16),
                        pltpu.VMEM((NSA_TPS * groups, CMP_PAD + ncmp, NSA_HPG * tq), F32)],
        compiler_params=_params(("parallel", "arbitrary")),
        name="nsa",
    )(proj, kcn, vct, proj, proj, proj, proj, proj, wc, t0, t1)


def _merge_kernel(x_ref, ya_ref, yb_ref, ga_ref, gb_ref, wa_ref, wb_ref, wo_ref, o_ref):
    a = _dot(ya_ref[...], wa_ref[...])
    b = _dot(yb_ref[...], wb_ref[...])
    z = jax.nn.sigmoid(ga_ref[...]) * a + jax.nn.sigmoid(gb_ref[...]) * b
    o_ref[...] = x_ref[...] + _dot(z.astype(BF16), wo_ref[...])


def _merge(xf, ya, yb, proj, wa, wb, wo, layer):
    n, d = xf.shape
    tm = MERGE_TM
    row = lambda w, col=0: pl.BlockSpec((tm, w), lambda i: (i, col))
    return pl.pallas_call(
        _merge_kernel,
        grid=(n // tm,),
        in_specs=[row(d), row(ya.shape[1]), row(yb.shape[1]), row(d, C_GA // d), row(d, C_GB // d),
                  _layer_spec(wa, layer), _layer_spec(wb, layer), _layer_spec(wo, layer)],
        out_specs=row(d),
        out_shape=jax.ShapeDtypeStruct((n, d), F32),
        compiler_params=_params(("parallel",)),
        name="merge",
    )(xf, ya, yb, proj, proj, wa, wb, wo)


FFN_HALO = 16
FFN_TM = 1024
FFN_TF = 256


def _ffn_kernel(x_ref, xh_ref, g_ref, wu_ref, cw_ref, cb_ref, wd_ref, p_ref, wg_ref, wp_ref,
                o_ref, act_s, *, seq, d_ff):
    i = pl.program_id(0)
    tm = x_ref.shape[0]
    x = x_ref[...]
    at_start = (i * tm) % seq == 0
    halo = jnp.where(at_start, 0.0, _rms(xh_ref[...], g_ref[...]))
    hn = jnp.concatenate([halo.astype(BF16), _rms(x, g_ref[...]).astype(BF16)], axis=0)

    def conv(cols):
        u = _dot(hn, wu_ref[:, cols])
        u1 = pltpu.roll(u, 1, 0)[FFN_HALO:]
        u2 = pltpu.roll(u, 2, 0)[FFN_HALO:]
        cw = cw_ref[:, cols]
        return cw[0:1] * u2 + cw[1:2] * u1 + cw[2:3] * u[FFN_HALO:] + cb_ref[:, cols]

    for c in range(d_ff // FFN_TF):
        lo = c * FFN_TF
        act = jax.nn.gelu(conv(slice(lo, lo + FFN_TF))) * conv(slice(d_ff + lo, d_ff + lo + FFN_TF))
        act_s[:, lo:lo + FFN_TF] = act.astype(BF16)

    x = x + _dot(act_s[...], wd_ref[...])
    gate = jax.nn.sigmoid(_dot(x.astype(BF16), wg_ref[...]))
    o_ref[...] = x + gate * _dot(p_ref[...].astype(BF16), wp_ref[...])


def _ffn_ple(xf, gain, w_up, conv_w, conv_b, w_down, pf, wg, wp, layer, seq):
    n, d = xf.shape
    d_ff = w_down.shape[1]
    tm = FFN_TM
    hb = tm // FFN_HALO
    resident = lambda a: _layer_spec(a, layer)
    return pl.pallas_call(
        functools.partial(_ffn_kernel, seq=seq, d_ff=d_ff),
        grid=(n // tm,),
        in_specs=[pl.BlockSpec((tm, d), lambda i: (i, 0)),
                  pl.BlockSpec((FFN_HALO, d), lambda i: (jnp.maximum(i * hb - 1, 0), 0)),
                  resident(gain), resident(w_up), resident(conv_w), resident(conv_b), resident(w_down),
                  pl.BlockSpec((None, tm, pf.shape[2]), lambda i: (layer, i, 0)),
                  resident(wg), resident(wp)],
        out_specs=pl.BlockSpec((tm, d), lambda i: (i, 0)),
        out_shape=jax.ShapeDtypeStruct((n, d), F32),
        scratch_shapes=[pltpu.VMEM((tm, d_ff), BF16)],
        compiler_params=_params(("parallel",)),
        name="conv_ffn_ple",
    )(xf, xf, gain, w_up, conv_w, conv_b, w_down, pf, wg, wp)


def _reorder_in_proj(w):
    attn = 3 * MOBA_HEADS * HEAD_DIM + NSA_HEADS * HEAD_DIM + 6 * NSA_KV_GROUPS * HEAD_DIM
    ng = 3 * NSA_HEADS
    pad = jnp.zeros(w.shape[:-1] + (PROJ_COLS - w.shape[-1],), w.dtype)
    return jnp.concatenate([w[..., attn + ng:], w[..., :attn], w[..., attn:attn + ng], pad], axis=-1)


def _head_gain_rows(moba_q, moba_k, nsa_q, nsa_k):
    rows = jnp.zeros((moba_q.shape[0], 1, PROJ_COLS), F32)
    for col, gain, heads in ((C_MQ, moba_q * QSCALE, MOBA_HEADS), (C_MK, moba_k, MOBA_HEADS),
                             (C_NSQ, nsa_q * QSCALE, NSA_HEADS), (C_KSL, nsa_k[:, 1], NSA_KV_GROUPS),
                             (C_KWN, nsa_k[:, 2], NSA_KV_GROUPS)):
        rows = rows.at[:, 0, col:col + heads * HEAD_DIM].set(jnp.tile(gain, (1, heads)))
    return rows


def _cmp_weights(w1, pos):
    depth = w1.shape[0]
    half = NSA_CMP_STRIDE * HEAD_DIM
    w1cat = jnp.concatenate([w1[:, :half], w1[:, half:]], axis=2).astype(BF16)
    tok = w1cat.reshape(depth, NSA_CMP_STRIDE, HEAD_DIM, -1)
    zero = jnp.zeros_like(tok)
    placed = jnp.stack([jnp.concatenate([tok, zero], axis=2), jnp.concatenate([zero, tok], axis=2)], axis=1)
    posr = jnp.zeros((depth, 8, half), F32).at[:, 0:2].set(pos.reshape(depth, 2, half))
    return w1cat, placed, posr


def kernel(x, p, rel_bias, attn_norm, w_in, moba_q_gain, moba_k_gain, nsa_q_gain, nsa_k_gain,
           cmp_pos_k, cmp_w1_k, cmp_w2_k, cmp_pos_v, cmp_w1_v, cmp_w2_v,
           w_br_moba, w_br_nsa, w_o, ffn_norm, w_up, conv_w, conv_b, w_down, w_ple_gate, w_ple):
    batch, seq, d = x.shape
    n = batch * seq
    depth = w_in.shape[0]
    bf = lambda a: a.astype(BF16)
    w_in_r = _reorder_in_proj(bf(w_in))
    head_gain = _head_gain_rows(moba_q_gain, moba_k_gain, nsa_q_gain, nsa_k_gain)
    attn_gain, ffn_gain = attn_norm[:, None, :], ffn_norm[:, None, :]
    w1k, wtk, pk = _cmp_weights(cmp_w1_k, cmp_pos_k)
    w1v, wtv, pv = _cmp_weights(cmp_w1_v, cmp_pos_v)
    w2k, w2v, kg0 = bf(cmp_w2_k), bf(cmp_w2_v), nsa_k_gain[:, 0:1]
    wa, wb, wo = bf(w_br_moba), bf(w_br_nsa), bf(w_o)
    wu, wd, wg, wp = bf(w_up), bf(w_down), bf(w_ple_gate), bf(w_ple)
    conv_b = conv_b[:, None, :]
    pf = p.reshape(depth, n, -1)

    t0m, t1m, t0n, t1n, wc = _bias_tables(rel_bias)
    xf = x.reshape(n, d)
    for i in range(depth):
        proj = _inproj(xf, attn_gain, w_in_r, head_gain, i)
        ya = _moba(proj, t0m, t1m, batch, seq)
        kcn, vct = _compress(proj, w1k, w1v, wtk, wtv, w2k, w2v, pk, pv, kg0, i, batch, seq)
        yb = _nsa(proj, kcn, vct, wc, t0n, t1n, batch, seq)
        xf = _merge(xf, ya, yb, proj, wa, wb, wo, i)
        xf = _ffn_ple(xf, ffn_gain, wu, conv_w, conv_b, wd, pf, wg, wp, i, seq)
    return xf.reshape(batch, seq, d)
```

```python
import functools
import math

import numpy as np
import jax
import jax.numpy as jnp
from jax import lax
from jax.experimental import pallas as pl
from jax.experimental.pallas import tpu as pltpu

F32 = jnp.float32
BF16 = jnp.bfloat16

HEAD_DIM = 64
MOBA_HEADS = 8
MOBA_BLOCK = 256
MOBA_TOPK = 3
NSA_HEADS = 8
NSA_KV_GROUPS = 2
NSA_HPG = NSA_HEADS // NSA_KV_GROUPS
NSA_CMP_LEN = 32
NSA_CMP_STRIDE = 16
NSA_CMP_HIDDEN = 2 * HEAD_DIM
NSA_SEL_BLOCK = 64
NSA_SEL_TOPN = 16
NSA_WINDOW = 512
REL_BUCKETS = 32
REL_MAX_DIST = 128
RMS_EPS = 1e-6
SCALE = HEAD_DIM ** -0.5
LOG2E = math.log2(math.e)
QSCALE = SCALE * LOG2E
NEG = -1e30

LANES = 128
V7X_VMEM_BYTES = 64 * 1024 * 1024
VMEM_LIMIT = V7X_VMEM_BYTES * 7 // 8

INPROJ_TM = 512
MERGE_TM = 512

C_GA, C_GB, C_MQ, C_MK, C_MV, C_NSQ = 0, 1024, 2048, 2560, 3072, 3584
C_KC, C_VC, C_KSL, C_VSL, C_KWN, C_VWN, C_NG = 4096, 4224, 4352, 4480, 4608, 4736, 4864
PROJ_COLS = 4992

NSA_TQ = 128
NSA_TPS = 4
CMP_WIN = 16
CMP_PAD = 8


def _dot(a, b):
    return jnp.dot(a, b, preferred_element_type=F32)


def _dot_nt(a, b):
    return lax.dot_general(a, b, (((1,), (1,)), ((), ())), preferred_element_type=F32)


def _rms(x, gain):
    return x * lax.rsqrt(jnp.mean(x * x, axis=-1, keepdims=True) + RMS_EPS) * gain


def _split_bf16(x):
    hi = x.astype(BF16)
    return hi, (x - hi.astype(F32)).astype(BF16)


def _eye(rows, cols):
    r = lax.broadcasted_iota(jnp.int32, (rows, cols), 0)
    c = lax.broadcasted_iota(jnp.int32, (rows, cols), 1)
    return (r == c).astype(BF16)


def _transpose_bf16(x, rows):
    return _dot_nt(_eye(rows, x.shape[1]), x)


VROWS = 80


def _value_tile(v):
    row = lax.broadcasted_iota(jnp.int32, (VROWS, v.shape[0]), 0)
    return jnp.where(row == HEAD_DIM, 1.0, _transpose_bf16(v, VROWS)).astype(BF16)


def _softmax_joint(tiles):
    m = functools.reduce(jnp.maximum, [jnp.max(s, axis=0, keepdims=True) for s, _ in tiles])
    acc = functools.reduce(jnp.add, [_dot(vt, jnp.exp2(s - m).astype(BF16)) for s, vt in tiles])
    return m, acc


def _softmax_update(carry, s, vts):
    m, acc = carry
    m_new = jnp.maximum(m, jnp.max(s, axis=0, keepdims=True))
    p = jnp.exp2(s - m_new).astype(BF16)
    rows = s.shape[0] // len(vts)
    pv = functools.reduce(jnp.add, [_dot(vt, p[i * rows:(i + 1) * rows]) for i, vt in enumerate(vts)])
    return m_new, jnp.exp2(m - m_new) * acc + pv


def _softmax_finish(carry):
    _, acc = carry
    return acc[0:HEAD_DIM] * (1.0 / acc[HEAD_DIM:HEAD_DIM + 1])


def _rank_select(score, idx, count):
    beaten = jnp.zeros(score.shape, jnp.int32)
    for i in range(score.shape[0]):
        si = score[i:i + 1, :]
        beaten += ((si > score) | ((si == score) & (i < idx))).astype(jnp.int32)
    return beaten < count


def _params(sem):
    return pltpu.CompilerParams(dimension_semantics=sem, vmem_limit_bytes=VMEM_LIMIT)


def _layer_spec(a, layer):
    return pl.BlockSpec((None,) + a.shape[1:], lambda *_: (layer,) + (0,) * (a.ndim - 1),
                        pipeline_mode=pl.Buffered(1))


def _rel_bucket_np(dist):
    n = np.maximum(dist, 0)
    max_exact = REL_BUCKETS // 2
    nf = np.maximum(n, 1).astype(np.float32)
    large = max_exact + (np.log(nf / np.float32(max_exact)) / np.float32(math.log(REL_MAX_DIST / max_exact))
                         * np.float32(REL_BUCKETS - max_exact)).astype(np.int32)
    return np.where(n < max_exact, n, np.minimum(large, REL_BUCKETS - 1))


def _bucket_starts():
    buckets = _rel_bucket_np(np.arange(4 * REL_MAX_DIST))
    return [int(np.argmax(buckets >= k)) for k in range(REL_BUCKETS)]


BUCKET_START = _bucket_starts()
BIAS_REACH = BUCKET_START[-1]
assert BIAS_REACH <= MOBA_BLOCK and BIAS_REACH <= NSA_TQ - NSA_CMP_LEN + 1 + NSA_CMP_STRIDE


def _tables_kernel(tab_ref, t0m_ref, t1m_ref, t0n_ref, t1n_ref, wc_ref):
    h = pl.program_id(0)

    def bias(dist, head):
        last = tab_ref[head, REL_BUCKETS - 1]
        val = jnp.zeros(dist.shape, F32)
        for k in range(REL_BUCKETS - 2, -1, -1):
            val = jnp.where(dist < BUCKET_START[k + 1], (tab_ref[head, k] - last) * LOG2E, val)
        return val

    def toeplitz(size, offset, head):
        key = lax.broadcasted_iota(jnp.int32, (size, size), 0)
        qry = lax.broadcasted_iota(jnp.int32, (size, size), 1)
        return bias(offset + qry - key, head)

    t0m_ref[0] = toeplitz(MOBA_BLOCK, 0, h)
    t1m_ref[0] = toeplitz(MOBA_BLOCK, MOBA_BLOCK, h)
    t0n_ref[0] = toeplitz(NSA_TQ, 0, MOBA_HEADS + h)
    t1n_ref[0] = toeplitz(NSA_TQ, NSA_TQ, MOBA_HEADS + h)
    a = lax.broadcasted_iota(jnp.int32, (CMP_WIN, NSA_TQ), 0)
    i = lax.broadcasted_iota(jnp.int32, (CMP_WIN, NSA_TQ), 1)
    wc_ref[0] = bias(i + (NSA_TQ - NSA_CMP_LEN + 1) - NSA_CMP_STRIDE * a, MOBA_HEADS + h)


def _bias_tables(rel_bias):
    blk, tq = MOBA_BLOCK, NSA_TQ
    shapes = [(blk, blk), (blk, blk), (tq, tq), (tq, tq), (CMP_WIN, tq)]
    return pl.pallas_call(
        _tables_kernel,
        grid=(MOBA_HEADS,),
        in_specs=[pl.BlockSpec(memory_space=pltpu.SMEM)],
        out_specs=[pl.BlockSpec((1,) + s, lambda h: (h, 0, 0)) for s in shapes],
        out_shape=[jax.ShapeDtypeStruct((MOBA_HEADS,) + s, F32) for s in shapes],
        compiler_params=_params(("arbitrary",)),
        name="bias_tables",
    )(rel_bias)


NORM_SLABS = (list(range(C_MQ, C_MV, LANES)) + list(range(C_NSQ, C_KC, LANES)) + [C_KSL, C_KWN])


def _inproj_kernel(x_ref, g_ref, w_ref, hg_ref, o_ref):
    h = _rms(x_ref[...], g_ref[...]).astype(BF16)
    y = _dot(h, w_ref[...])
    first = lax.broadcasted_iota(jnp.int32, (1, LANES), 1) < HEAD_DIM
    edges = sorted(set([0, PROJ_COLS] + NORM_SLABS + [c + LANES for c in NORM_SLABS]))
    for lo, hi in zip(edges[:-1], edges[1:]):
        t = y[:, lo:hi]
        if lo in NORM_SLABS:
            sq = t * t
            s0 = jnp.sum(jnp.where(first, sq, 0.0), axis=-1, keepdims=True)
            s1 = jnp.sum(jnp.where(first, 0.0, sq), axis=-1, keepdims=True)
            ms = jnp.where(first, s0, s1) * (1.0 / HEAD_DIM)
            t = t * lax.rsqrt(ms + RMS_EPS) * hg_ref[:, lo:hi]
        o_ref[:, lo:hi] = t


def _inproj(xf, gain, w, head_gain, layer):
    n, d = xf.shape
    tm = INPROJ_TM
    return pl.pallas_call(
        _inproj_kernel,
        grid=(n // tm,),
        in_specs=[pl.BlockSpec((tm, d), lambda i: (i, 0)), _layer_spec(gain, layer),
                  _layer_spec(w, layer), _layer_spec(head_gain, layer)],
        out_specs=pl.BlockSpec((tm, PROJ_COLS), lambda i: (i, 0)),
        out_shape=jax.ShapeDtypeStruct((n, PROJ_COLS), F32),
        compiler_params=_params(("parallel",)),
        name="inproj",
    )(xf, gain, w, head_gain)


MOBA_HPS = 8


def _moba_kernel(q_ref, k_ref, v_ref, t0_ref, t1_ref, o_ref, kaug_s, vt_s, km_s, *, seq):
    n = pl.program_id(2)
    blk = MOBA_BLOCK
    nb = seq // blk
    head_cols = lambda hh: slice(hh * HEAD_DIM, (hh + 1) * HEAD_DIM)

    @pl.when(n == 0)
    def _prepare_keys():
        r = lax.broadcasted_iota(jnp.int32, (seq, LANES), 0)
        c = lax.broadcasted_iota(jnp.int32, (seq, LANES), 1)
        onehot = (c - HEAD_DIM == r // blk).astype(F32)
        place = _eye(HEAD_DIM, LANES)
        for hh in range(MOBA_HPS):
            kn = k_ref[:, head_cols(hh)]
            km = kn.reshape(nb, blk, HEAD_DIM).sum(axis=1) * (1.0 / blk)
            km_s[hh] = jnp.concatenate([km, jnp.zeros((16 - nb, HEAD_DIM), F32)], axis=0)
            kaug_s[hh] = (_dot(kn.astype(BF16), place) + onehot).astype(BF16)
            for j in range(nb):
                vj = v_ref[j * blk:(j + 1) * blk, head_cols(hh)].astype(BF16)
                vt_s[hh, j] = _value_tile(vj)

    jidx = lax.broadcasted_iota(jnp.int32, (16, blk), 0)
    past = jidx < n
    key = lax.broadcasted_iota(jnp.int32, (blk, blk), 0)
    qry = lax.broadcasted_iota(jnp.int32, (blk, blk), 1)
    prev = jnp.maximum(n - 1, 0)
    heads = range(MOBA_HPS)
    qts = []
    for pair in range(MOBA_HPS // 2):
        qp = _transpose_bf16(q_ref[:, pair * LANES:(pair + 1) * LANES].astype(BF16), LANES)
        qts += [qp[0:HEAD_DIM].astype(BF16), qp[HEAD_DIM:].astype(BF16)]
    gates = []
    for hh in heads:
        kmh, kml = _split_bf16(km_s[hh])
        gates.append(_dot(kmh, qts[hh]) + _dot(kml, qts[hh]))
    qaugs, state = [], []
    for hh in heads:
        gate = jnp.where(past, gates[hh], NEG)
        sel = (past & _rank_select(gate, jidx, MOBA_TOPK)) | (jidx == n)
        selb = jnp.where(sel, 0.0, NEG).astype(BF16)
        qaugs.append(jnp.concatenate([qts[hh], selb, jnp.zeros((LANES - HEAD_DIM - 16, blk), BF16)],
                                     axis=0))

    def scores(hh, j):
        start = pl.multiple_of(j * blk, blk)
        return _dot(kaug_s[hh, pl.ds(start, blk), :], qaugs[hh]), vt_s[hh, j]

    near = [(scores(hh, n), scores(hh, prev)) for hh in range(MOBA_HPS)]
    for hh, ((s0, vt0), (s1, vt1)) in enumerate(near):
        s0 = jnp.where(key <= qry, s0, NEG) + t0_ref[hh]
        s1 = jnp.where(n >= 1, s1, NEG) + t1_ref[hh]
        state.append(_softmax_joint([(s0, vt0), (s1, vt1)]))

    def far_body(j, carry):
        tiles = [scores(hh, j) for hh in range(MOBA_HPS)]
        return tuple(_softmax_update(carry[hh], tiles[hh][0], [tiles[hh][1]]) for hh in range(MOBA_HPS))

    state = lax.fori_loop(0, jnp.maximum(n - 1, 0), far_body, tuple(state))
    o_ref[...] = jnp.concatenate([_softmax_finish(c) for c in state], axis=0).T.astype(BF16)


def _moba(proj, t0, t1, batch, seq):
    blk = MOBA_BLOCK
    nb = seq // blk
    width = MOBA_HPS * HEAD_DIM
    qc, kc, vc = C_MQ // width, C_MK // width, C_MV // width
    return pl.pallas_call(
        functools.partial(_moba_kernel, seq=seq),
        grid=(batch, MOBA_HEADS // MOBA_HPS, nb),
        in_specs=[pl.BlockSpec((blk, width), lambda b, h, n: (b * nb + n, qc + h)),
                  pl.BlockSpec((seq, width), lambda b, h, n: (b, kc + h)),
                  pl.BlockSpec((seq, width), lambda b, h, n: (b, vc + h)),
                  pl.BlockSpec((MOBA_HPS, blk, blk), lambda b, h, n: (h, 0, 0)),
                  pl.BlockSpec((MOBA_HPS, blk, blk), lambda b, h, n: (h, 0, 0))],
        out_specs=pl.BlockSpec((blk, width), lambda b, h, n: (b * nb + n, h)),
        out_shape=jax.ShapeDtypeStruct((batch * seq, MOBA_HEADS * HEAD_DIM), BF16),
        scratch_shapes=[pltpu.VMEM((MOBA_HPS, seq, LANES), BF16),
                        pltpu.VMEM((MOBA_HPS, nb, VROWS, blk), BF16),
                        pltpu.VMEM((MOBA_HPS, 16, HEAD_DIM), F32)],
        compiler_params=_params(("parallel", "parallel", "arbitrary")),
        name="moba",
    )(proj, proj, proj, t0, t1)


def _compress_kernel(k_ref, v_ref, w1k_ref, w1v_ref, wtk_ref, wtv_ref, w2k_ref, w2v_ref, pk_ref, pv_ref,
                     kg_ref, ko_ref, vo_ref):
    hid = NSA_CMP_HIDDEN
    chunks = k_ref.shape[0] // NSA_CMP_STRIDE

    def compress(t_ref, w1_ref, wt_ref, w2_ref, pos_ref, g):
        a = functools.reduce(jnp.add, [
            _dot(t_ref[pl.ds(l, chunks, stride=NSA_CMP_STRIDE), :].astype(BF16), wt_ref[g, l])
            for l in range(NSA_CMP_STRIDE)])
        pw = _dot(pos_ref[...].astype(BF16), w1_ref[...])
        pos = pw[0:1, :hid] + pw[1:2, hid:]
        nxt = pltpu.roll(a[:, hid:], chunks - 1, 0)
        h = jax.nn.gelu(a[:, :hid] + nxt + pos)
        return _dot(h.astype(BF16), w2_ref[...])

    for g in range(NSA_KV_GROUPS):
        kc = compress(k_ref, w1k_ref, wtk_ref, w2k_ref, pk_ref, g)
        ko_ref[0, g] = _rms(kc, kg_ref[...]).astype(BF16)
        vc = compress(v_ref, w1v_ref, wtv_ref, w2v_ref, pv_ref, g).astype(BF16)
        vo_ref[0, g] = _transpose_bf16(vc, HEAD_DIM).astype(BF16)


def _compress(proj, w1k, w1v, wtk, wtv, w2k, w2v, pk, pv, kg, layer, batch, seq):
    groups = NSA_KV_GROUPS
    chunks = seq // NSA_CMP_STRIDE
    tok = lambda col: pl.BlockSpec((seq, LANES), lambda b: (b, col // LANES))
    return pl.pallas_call(
        _compress_kernel,
        grid=(batch,),
        in_specs=[tok(C_KC), tok(C_VC)] + [_layer_spec(a, layer)
                                           for a in (w1k, w1v, wtk, wtv, w2k, w2v, pk, pv, kg)],
        out_specs=[pl.BlockSpec((1, groups, chunks, HEAD_DIM), lambda b: (b, 0, 0, 0)),
                   pl.BlockSpec((1, groups, HEAD_DIM, chunks), lambda b: (b, 0, 0, 0))],
        out_shape=[jax.ShapeDtypeStruct((batch, groups, chunks, HEAD_DIM), BF16),
                   jax.ShapeDtypeStruct((batch, groups, HEAD_DIM, chunks), BF16)],
        compiler_params=_params(("parallel",)),
        name="nsa_compress",
    )(proj, proj, w1k, w1v, wtk, wtv, w2k, w2v, pk, pv, kg)


def _nsa_kernel(q_ref, kc_ref, vc_ref, ksl_ref, vsl_ref, kwn_ref, vwn_ref, gt_ref,
                wc_ref, t0_ref, t1_ref, o_ref, kslaug_s, vslt_s, kwn_s, vwnt_s, sc_s, *, seq):
    step = pl.program_id(1)
    tq = NSA_TQ
    hpg = NSA_HPG
    sb = NSA_SEL_BLOCK
    nsel = seq // sb
    ncmp = seq // NSA_CMP_STRIDE
    nwin = NSA_WINDOW // tq
    groups = range(NSA_KV_GROUPS)
    items = [(t, g) for t in range(NSA_TPS) for g in groups]
    qis = [step * NSA_TPS + t for t in range(NSA_TPS)]

    @pl.when(step == 0)
    def _prepare_keys():
        r = lax.broadcasted_iota(jnp.int32, (seq, LANES), 0)
        c = lax.broadcasted_iota(jnp.int32, (seq, LANES), 1)
        onehot = (c - HEAD_DIM == r // sb).astype(F32)
        place = _eye(HEAD_DIM, LANES)
        sc_s[:, 0:CMP_PAD, :] = jnp.zeros((NSA_TPS * NSA_KV_GROUPS, CMP_PAD, hpg * tq), F32)
        for g in groups:
            sl = slice(g * HEAD_DIM, (g + 1) * HEAD_DIM)
            kslaug_s[g] = (_dot(ksl_ref[:, sl].astype(BF16), place) + onehot).astype(BF16)
            kwn_s[g] = kwn_ref[:, sl].astype(BF16)
            for j in range(seq // tq):
                rows = slice(j * tq, (j + 1) * tq)
                vslt_s[g, j] = _value_tile(vsl_ref[rows, sl].astype(BF16))
                vwnt_s[g, j] = _value_tile(vwn_ref[rows, sl].astype(BF16))

    lanes = lambda parts: jnp.concatenate(parts, axis=1)
    key = lax.broadcasted_iota(jnp.int32, (tq, tq), 0)
    qry = lax.broadcasted_iota(jnp.int32, (tq, tq), 1)
    causal = lanes([key <= qry] * hpg)
    upper = lanes([key > qry] * hpg)
    oj = lax.broadcasted_iota(jnp.int32, (nsel, ncmp), 0) * sb
    on = lax.broadcasted_iota(jnp.int32, (nsel, ncmp), 1) * NSA_CMP_STRIDE
    overlap = ((on < oj + sb) & (on + NSA_CMP_LEN > oj) & (on < seq - NSA_CMP_STRIDE)).astype(BF16)
    jidx = lax.broadcasted_iota(jnp.int32, (nsel, tq), 0)
    cend = lax.broadcasted_iota(jnp.int32, (ncmp, tq), 0) * NSA_CMP_STRIDE + (NSA_CMP_LEN - 1)
    cvis = [lanes([cend <= lax.broadcasted_iota(jnp.int32, (ncmp, tq), 1) + qi * tq] * hpg) for qi in qis]
    own = [(lax.broadcasted_iota(jnp.int32, (nsel, tq), 1) + qi * tq) // sb for qi in qis]
    tile_rows = lambda j: pl.ds(pl.multiple_of(j * tq, tq), tq)
    back = lambda t, k: jnp.maximum(qis[t] - k, 0)
    heads = [[g * hpg + r for r in range(hpg)] for g in groups]
    t0 = [lanes([t0_ref[h] for h in heads[g]]) for g in groups]
    t1 = [lanes([t1_ref[h] for h in heads[g]]) for g in groups]
    slot = lambda t, g: t * NSA_KV_GROUPS + g

    qplain, win_scores = {}, {}
    for t, g in items:
        width = hpg * HEAD_DIM
        qg = _transpose_bf16(q_ref[t * tq:(t + 1) * tq, g * width:(g + 1) * width].astype(BF16), width)
        qplain[t, g] = lanes([qg[r * HEAD_DIM:(r + 1) * HEAD_DIM] for r in range(hpg)]).astype(BF16)
        sc_s[slot(t, g), CMP_PAD:, :] = _dot(kc_ref[0, g], qplain[t, g])
        win_scores[t, g] = [_dot(kwn_s[g, tile_rows(back(t, k)), :], qplain[t, g]) for k in range(nwin + 1)]

    o_cmps, imps = {}, {}
    for t, g in items:
        win = pl.ds(pl.multiple_of(qis[t] * (tq // NSA_CMP_STRIDE), 8), CMP_WIN)
        sc_s[slot(t, g), win, :] += lanes([wc_ref[h] for h in heads[g]])
        s = jnp.where(cvis[t], sc_s[slot(t, g), CMP_PAD:, :], NEG)
        m = jnp.max(s, axis=0, keepdims=True)
        e = jnp.where(cvis[t], jnp.exp2(s - m), 0.0)
        den = jnp.sum(e, axis=0, keepdims=True)
        p = e * (1.0 / jnp.where(den > 0, den, 1.0))
        o_cmps[t, g] = _dot(vc_ref[0, g], p.astype(BF16))
        psum = p[:, 0:tq]
        for r in range(1, hpg):
            psum = psum + p[:, r * tq:(r + 1) * tq]
        ph, plo = _split_bf16(psum)
        imps[t, g] = _dot(overlap, ph) + _dot(overlap, plo)

    o_wins = {}
    for t, g in items:
        qi = qis[t]
        tiles = []
        for k, s in enumerate(win_scores[t, g]):
            if k == 0:
                s = jnp.where(causal, s, NEG) + t0[g]
            elif k == 1:
                s = jnp.where(qi >= 1, s, NEG) + t1[g]
            elif k < nwin:
                s = jnp.where(qi >= k, s, NEG)
            else:
                s = jnp.where(upper & (qi >= k), s, NEG)
            tiles.append((s, vwnt_s[g, back(t, k)]))
        o_wins[t, g] = _softmax_finish(_softmax_joint(tiles))

    qaugs = {}
    for t, g in items:
        imp = jnp.where((jidx == 0) | (jidx == own[t]) | (jidx == own[t] - 1), -NEG, imps[t, g])
        imp = jnp.where(jidx > own[t], NEG, imp)
        sel = _rank_select(imp, jidx, NSA_SEL_TOPN) & (jidx <= own[t])
        selb = jnp.where(sel, 0.0, NEG).astype(BF16)
        qaugs[t, g] = jnp.concatenate([qplain[t, g], lanes([selb] * hpg),
                                       jnp.zeros((LANES - HEAD_DIM - nsel, hpg * tq), BF16)], axis=0)

    slc_near = {(t, g): [_dot(kslaug_s[g, tile_rows(back(t, k)), :], qaugs[t, g]) for k in range(3)]
                for t, g in items}
    state = {}
    for t, g in items:
        qi = qis[t]
        odd = (qi >= 2) & (qi % 2 == 0)
        state[t, g] = _softmax_joint([
            (jnp.where(causal, slc_near[t, g][0], NEG) + t0[g], vslt_s[g, qi]),
            (jnp.where(qi >= 1, slc_near[t, g][1], NEG) + t1[g], vslt_s[g, back(t, 1)]),
            (jnp.where(odd, slc_near[t, g][2], NEG), vslt_s[g, back(t, 2)])])

    slabs = [jnp.maximum(qi - 1, 0) // 2 for qi in qis]
    lo = 0
    for first in range(NSA_TPS):
        live = [(t, g) for t, g in items if t >= first]

        def slc_far(i, carry, live=live):
            slab = pl.ds(pl.multiple_of(i * (2 * tq), 2 * tq), 2 * tq)
            scores = [_dot(kslaug_s[g, slab, :], qaugs[t, g]) for t, g in live]
            return tuple(_softmax_update(c, sc, [vslt_s[g, 2 * i], vslt_s[g, 2 * i + 1]])
                         for c, sc, (t, g) in zip(carry, scores, live))

        new = lax.fori_loop(lo, slabs[first], slc_far, tuple(state[it] for it in live))
        state.update(dict(zip(live, new)))
        lo = slabs[first]

    for t in range(NSA_TPS):
        gates = jax.nn.sigmoid(gt_ref[t * tq:(t + 1) * tq, :]).T
        outs = []
        for g in groups:
            o_slc = _softmax_finish(state[t, g])
            for r in range(hpg):
                h = g * hpg + r
                cols = slice(r * tq, (r + 1) * tq)
                outs.append(gates[h:h + 1, :] * o_cmps[t, g][:, cols]
                            + gates[NSA_HEADS + h:NSA_HEADS + h + 1, :] * o_slc[:, cols]
                            + gates[2 * NSA_HEADS + h:2 * NSA_HEADS + h + 1, :] * o_wins[t, g][:, cols])
        o_ref[t * tq:(t + 1) * tq, :] = jnp.concatenate(outs, axis=0).T.astype(BF16)


def _nsa(proj, kcn, vct, wc, t0, t1, batch, seq):
    tq = NSA_TQ
    nq = seq // tq
    rows = NSA_TPS * tq
    steps = nq // NSA_TPS
    ncmp = seq // NSA_CMP_STRIDE
    width = NSA_HEADS * HEAD_DIM
    groups = NSA_KV_GROUPS
    kv = lambda col: pl.BlockSpec((seq, LANES), lambda b, i: (b, col // LANES))
    const = lambda shape: pl.BlockSpec(shape, lambda b, i: (0,) * len(shape))
    return pl.pallas_call(
        functools.partial(_nsa_kernel, seq=seq),
        grid=(batch, steps),
        in_specs=[pl.BlockSpec((rows, width), lambda b, i: (b * steps + i, C_NSQ // width)),
                  pl.BlockSpec((1, groups, ncmp, HEAD_DIM), lambda b, i: (b, 0, 0, 0)),
                  pl.BlockSpec((1, groups, HEAD_DIM, ncmp), lambda b, i: (b, 0, 0, 0)),
                  kv(C_KSL), kv(C_VSL), kv(C_KWN), kv(C_VWN),
                  pl.BlockSpec((rows, LANES), lambda b, i: (b * steps + i, C_NG // LANES)),
                  const((NSA_HEADS, CMP_WIN, tq)), const((NSA_HEADS, tq, tq)),
                  const((NSA_HEADS, tq, tq))],
        out_specs=pl.BlockSpec((rows, width), lambda b, i: (b * steps + i, 0)),
        out_shape=jax.ShapeDtypeStruct((batch * seq, width), BF16),
        scratch_shapes=[pltpu.VMEM((groups, seq, LANES), BF16),
                        pltpu.VMEM((groups, nq, VROWS, tq), BF16),
                        pltpu.VMEM((groups, seq, HEAD_DIM), BF16),
                        pltpu.VMEM((groups, nq, VROWS, tq), BF16),
                        pltpu.VMEM((NSA_TPS * groups, CMP_PAD + ncmp, NSA_HPG * tq), F32)],
        compiler_params=_params(("parallel", "arbitrary")),
        name="nsa",
    )(proj, kcn, vct, proj, proj, proj, proj, proj, wc, t0, t1)


def _merge_kernel(x_ref, ya_ref, yb_ref, ga_ref, gb_ref, wa_ref, wb_ref, wo_ref, o_ref):
    a = _dot(ya_ref[...], wa_ref[...])
    b = _dot(yb_ref[...], wb_ref[...])
    z = jax.nn.sigmoid(ga_ref[...]) * a + jax.nn.sigmoid(gb_ref[...]) * b
    o_ref[...] = x_ref[...] + _dot(z.astype(BF16), wo_ref[...])


def _merge(xf, ya, yb, proj, wa, wb, wo, layer):
    n, d = xf.shape
    tm = MERGE_TM
    row = lambda w, col=0: pl.BlockSpec((tm, w), lambda i: (i, col))
    return pl.pallas_call(
        _merge_kernel,
        grid=(n // tm,),
        in_specs=[row(d), row(ya.shape[1]), row(yb.shape[1]), row(d, C_GA // d), row(d, C_GB // d),
                  _layer_spec(wa, layer), _layer_spec(wb, layer), _layer_spec(wo, layer)],
        out_specs=row(d),
        out_shape=jax.ShapeDtypeStruct((n, d), F32),
        compiler_params=_params(("parallel",)),
        name="merge",
    )(xf, ya, yb, proj, proj, wa, wb, wo)


FFN_HALO = 16
FFN_TM = 1024
FFN_TF = 256


def _ffn_kernel(x_ref, xh_ref, g_ref, wu_ref, cw_ref, cb_ref, wd_ref, p_ref, wg_ref, wp_ref,
                o_ref, act_s, *, seq, d_ff):
    i = pl.program_id(0)
    tm = x_ref.shape[0]
    x = x_ref[...]
    at_start = (i * tm) % seq == 0
    halo = jnp.where(at_start, 0.0, _rms(xh_ref[...], g_ref[...]))
    hn = jnp.concatenate([halo.astype(BF16), _rms(x, g_ref[...]).astype(BF16)], axis=0)

    def conv(cols):
        u = _dot(hn, wu_ref[:, cols])
        u1 = pltpu.roll(u, 1, 0)[FFN_HALO:]
        u2 = pltpu.roll(u, 2, 0)[FFN_HALO:]
        cw = cw_ref[:, cols]
        return cw[0:1] * u2 + cw[1:2] * u1 + cw[2:3] * u[FFN_HALO:] + cb_ref[:, cols]

    for c in range(d_ff // FFN_TF):
        lo = c * FFN_TF
        act = jax.nn.gelu(conv(slice(lo, lo + FFN_TF))) * conv(slice(d_ff + lo, d_ff + lo + FFN_TF))
        act_s[:, lo:lo + FFN_TF] = act.astype(BF16)

    x = x + _dot(act_s[...], wd_ref[...])
    gate = jax.nn.sigmoid(_dot(x.astype(BF16), wg_ref[...]))
    o_ref[...] = x + gate * _dot(p_ref[...].astype(BF16), wp_ref[...])


def _ffn_ple(xf, gain, w_up, conv_w, conv_b, w_down, pf, wg, wp, layer, seq):
    n, d = xf.shape
    d_ff = w_down.shape[1]
    tm = FFN_TM
    hb = tm // FFN_HALO
    resident = lambda a: _layer_spec(a, layer)
    return pl.pallas_call(
        functools.partial(_ffn_kernel, seq=seq, d_ff=d_ff),
        grid=(n // tm,),
        in_specs=[pl.BlockSpec((tm, d), lambda i: (i, 0)),
                  pl.BlockSpec((FFN_HALO, d), lambda i: (jnp.maximum(i * hb - 1, 0), 0)),
                  resident(gain), resident(w_up), resident(conv_w), resident(conv_b), resident(w_down),
                  pl.BlockSpec((None, tm, pf.shape[2]), lambda i: (layer, i, 0)),
                  resident(wg), resident(wp)],
        out_specs=pl.BlockSpec((tm, d), lambda i: (i, 0)),
        out_shape=jax.ShapeDtypeStruct((n, d), F32),
        scratch_shapes=[pltpu.VMEM((tm, d_ff), BF16)],
        compiler_params=_params(("parallel",)),
        name="conv_ffn_ple",
    )(xf, xf, gain, w_up, conv_w, conv_b, w_down, pf, wg, wp)


def _reorder_in_proj(w):
    attn = 3 * MOBA_HEADS * HEAD_DIM + NSA_HEADS * HEAD_DIM + 6 * NSA_KV_GROUPS * HEAD_DIM
    ng = 3 * NSA_HEADS
    pad = jnp.zeros(w.shape[:-1] + (PROJ_COLS - w.shape[-1],), w.dtype)
    return jnp.concatenate([w[..., attn + ng:], w[..., :attn], w[..., attn:attn + ng], pad], axis=-1)


def _head_gain_rows(moba_q, moba_k, nsa_q, nsa_k):
    rows = jnp.zeros((moba_q.shape[0], 1, PROJ_COLS), F32)
    for col, gain, heads in ((C_MQ, moba_q * QSCALE, MOBA_HEADS), (C_MK, moba_k, MOBA_HEADS),
                             (C_NSQ, nsa_q * QSCALE, NSA_HEADS), (C_KSL, nsa_k[:, 1], NSA_KV_GROUPS),
                             (C_KWN, nsa_k[:, 2], NSA_KV_GROUPS)):
        rows = rows.at[:, 0, col:col + heads * HEAD_DIM].set(jnp.tile(gain, (1, heads)))
    return rows


def _cmp_weights(w1, pos):
    depth = w1.shape[0]
    half = NSA_CMP_STRIDE * HEAD_DIM
    w1cat = jnp.concatenate([w1[:, :half], w1[:, half:]], axis=2).astype(BF16)
    tok = w1cat.reshape(depth, NSA_CMP_STRIDE, HEAD_DIM, -1)
    zero = jnp.zeros_like(tok)
    placed = jnp.stack([jnp.concatenate([tok, zero], axis=2), jnp.concatenate([zero, tok], axis=2)], axis=1)
    posr = jnp.zeros((depth, 8, half), F32).at[:, 0:2].set(pos.reshape(depth, 2, half))
    return w1cat, placed, posr


def kernel(x, p, rel_bias, attn_norm, w_in, moba_q_gain, moba_k_gain, nsa_q_gain, nsa_k_gain,
           cmp_pos_k, cmp_w1_k, cmp_w2_k, cmp_pos_v, cmp_w1_v, cmp_w2_v,
           w_br_moba, w_br_nsa, w_o, ffn_norm, w_up, conv_w, conv_b, w_down, w_ple_gate, w_ple):
    batch, seq, d = x.shape
    n = batch * seq
    depth = w_in.shape[0]
    bf = lambda a: a.astype(BF16)
    w_in_r = _reorder_in_proj(bf(w_in))
    head_gain = _head_gain_rows(moba_q_gain, moba_k_gain, nsa_q_gain, nsa_k_gain)
    attn_gain, ffn_gain = attn_norm[:, None, :], ffn_norm[:, None, :]
    w1k, wtk, pk = _cmp_weights(cmp_w1_k, cmp_pos_k)
    w1v, wtv, pv = _cmp_weights(cmp_w1_v, cmp_pos_v)
    w2k, w2v, kg0 = bf(cmp_w2_k), bf(cmp_w2_v), nsa_k_gain[:, 0:1]
    wa, wb, wo = bf(w_br_moba), bf(w_br_nsa), bf(w_o)
    wu, wd, wg, wp = bf(w_up), bf(w_down), bf(w_ple_gate), bf(w_ple)
    conv_b = conv_b[:, None, :]
    pf = p.reshape(depth, n, -1)

    t0m, t1m, t0n, t1n, wc = _bias_tables(rel_bias)
    xf = x.reshape(n, d)
    for i in range(depth):
        proj = _inproj(xf, attn_gain, w_in_r, head_gain, i)
        ya = _moba(proj, t0m, t1m, batch, seq)
        kcn, vct = _compress(proj, w1k, w1v, wtk, wtv, w2k, w2v, pk, pv, kg0, i, batch, seq)
        yb = _nsa(proj, kcn, vct, wc, t0n, t1n, batch, seq)
        xf = _merge(xf, ya, yb, proj, wa, wb, wo, i)
        xf = _ffn_ple(xf, ffn_gain, wu, conv_w, conv_b, wd, pf, wg, wp, i, seq)
    return xf.reshape(batch, seq, d)
```

```python
import functools
import math

import numpy as np
import jax
import jax.numpy as jnp
from jax import lax
from jax.experimental import pallas as pl
from jax.experimental.pallas import tpu as pltpu

F32 = jnp.float32
BF16 = jnp.bfloat16

HEAD_DIM = 64
MOBA_HEADS = 8
MOBA_BLOCK = 256
MOBA_TOPK = 3
NSA_HEADS = 8
NSA_KV_GROUPS = 2
NSA_HPG = NSA_HEADS // NSA_KV_GROUPS
NSA_CMP_LEN = 32
NSA_CMP_STRIDE = 16
NSA_CMP_HIDDEN = 2 * HEAD_DIM
NSA_SEL_BLOCK = 64
NSA_SEL_TOPN = 16
NSA_WINDOW = 512
REL_BUCKETS = 32
REL_MAX_DIST = 128
RMS_EPS = 1e-6
SCALE = HEAD_DIM ** -0.5
LOG2E = math.log2(math.e)
QSCALE = SCALE * LOG2E
NEG = -1e30

LANES = 128
V7X_VMEM_BYTES = 64 * 1024 * 1024
VMEM_LIMIT = V7X_VMEM_BYTES * 7 // 8

INPROJ_TM = 512
MERGE_TM = 512

C_GA, C_GB, C_MQ, C_MK, C_MV, C_NSQ = 0, 1024, 2048, 2560, 3072, 3584
C_KC, C_VC, C_KSL, C_VSL, C_KWN, C_VWN, C_NG = 4096, 4224, 4352, 4480, 4608, 4736, 4864
PROJ_COLS = 4992

NSA_TQ = 128
NSA_TPS = 4
CMP_WIN = 16
CMP_PAD = 8


def _dot(a, b):
    return jnp.dot(a, b, preferred_element_type=F32)


def _dot_nt(a, b):
    return lax.dot_general(a, b, (((1,), (1,)), ((), ())), preferred_element_type=F32)


def _rms(x, gain):
    return x * lax.rsqrt(jnp.mean(x * x, axis=-1, keepdims=True) + RMS_EPS) * gain


def _split_bf16(x):
    hi = x.astype(BF16)
    return hi, (x - hi.astype(F32)).astype(BF16)


def _eye(rows, cols):
    r = lax.broadcasted_iota(jnp.int32, (rows, cols), 0)
    c = lax.broadcasted_iota(jnp.int32, (rows, cols), 1)
    return (r == c).astype(BF16)


def _transpose_bf16(x, rows):
    return _dot_nt(_eye(rows, x.shape[1]), x)


VROWS = 80


def _value_tile(v):
    row = lax.broadcasted_iota(jnp.int32, (VROWS, v.shape[0]), 0)
    return jnp.where(row == HEAD_DIM, 1.0, _transpose_bf16(v, VROWS)).astype(BF16)


def _softmax_joint(tiles):
    m = functools.reduce(jnp.maximum, [jnp.max(s, axis=0, keepdims=True) for s, _ in tiles])
    acc = functools.reduce(jnp.add, [_dot(vt, jnp.exp2(s - m).astype(BF16)) for s, vt in tiles])
    return m, acc


def _softmax_update(carry, s, vts):
    m, acc = carry
    m_new = jnp.maximum(m, jnp.max(s, axis=0, keepdims=True))
    p = jnp.exp2(s - m_new).astype(BF16)
    rows = s.shape[0] // len(vts)
    pv = functools.reduce(jnp.add, [_dot(vt, p[i * rows:(i + 1) * rows]) for i, vt in enumerate(vts)])
    return m_new, jnp.exp2(m - m_new) * acc + pv


def _softmax_finish(carry):
    _, acc = carry
    return acc[0:HEAD_DIM] * (1.0 / acc[HEAD_DIM:HEAD_DIM + 1])


def _rank_select(score, idx, count):
    beaten = jnp.zeros(score.shape, jnp.int32)
    for i in range(score.shape[0]):
        si = score[i:i + 1, :]
        beaten += ((si > score) | ((si == score) & (i < idx))).astype(jnp.int32)
    return beaten < count


def _params(sem):
    return pltpu.CompilerParams(dimension_semantics=sem, vmem_limit_bytes=VMEM_LIMIT)


def _layer_spec(a, layer):
    return pl.BlockSpec((None,) + a.shape[1:], lambda *_: (layer,) + (0,) * (a.ndim - 1),
                        pipeline_mode=pl.Buffered(1))


def _rel_bucket_np(dist):
    n = np.maximum(dist, 0)
    max_exact = REL_BUCKETS // 2
    nf = np.maximum(n, 1).astype(np.float32)
    large = max_exact + (np.log(nf / np.float32(max_exact)) / np.float32(math.log(REL_MAX_DIST / max_exact))
                         * np.float32(REL_BUCKETS - max_exact)).astype(np.int32)
    return np.where(n < max_exact, n, np.minimum(large, REL_BUCKETS - 1))


def _bucket_starts():
    buckets = _rel_bucket_np(np.arange(4 * REL_MAX_DIST))
    return [int(np.argmax(buckets >= k)) for k in range(REL_BUCKETS)]


BUCKET_START = _bucket_starts()
BIAS_REACH = BUCKET_START[-1]
assert BIAS_REACH <= MOBA_BLOCK and BIAS_REACH <= NSA_TQ - NSA_CMP_LEN + 1 + NSA_CMP_STRIDE


def _tables_kernel(tab_ref, t0m_ref, t1m_ref, t0n_ref, t1n_ref, wc_ref):
    h = pl.program_id(0)

    def bias(dist, head):
        last = tab_ref[head, REL_BUCKETS - 1]
        val = jnp.zeros(dist.shape, F32)
        for k in range(REL_BUCKETS - 2, -1, -1):
            val = jnp.where(dist < BUCKET_START[k + 1], (tab_ref[head, k] - last) * LOG2E, val)
        return val

    def toeplitz(size, offset, head):
        key = lax.broadcasted_iota(jnp.int32, (size, size), 0)
        qry = lax.broadcasted_iota(jnp.int32, (size, size), 1)
        return bias(offset + qry - key, head)

    t0m_ref[0] = toeplitz(MOBA_BLOCK, 0, h)
    t1m_ref[0] = toeplitz(MOBA_BLOCK, MOBA_BLOCK, h)
    t0n_ref[0] = toeplitz(NSA_TQ, 0, MOBA_HEADS + h)
    t1n_ref[0] = toeplitz(NSA_TQ, NSA_TQ, MOBA_HEADS + h)
    a = lax.broadcasted_iota(jnp.int32, (CMP_WIN, NSA_TQ), 0)
    i = lax.broadcasted_iota(jnp.int32, (CMP_WIN, NSA_TQ), 1)
    wc_ref[0] = bias(i + (NSA_TQ - NSA_CMP_LEN + 1) - NSA_CMP_STRIDE * a, MOBA_HEADS + h)


def _bias_tables(rel_bias):
    blk, tq = MOBA_BLOCK, NSA_TQ
    shapes = [(blk, blk), (blk, blk), (tq, tq), (tq, tq), (CMP_WIN, tq)]
    return pl.pallas_call(
        _tables_kernel,
        grid=(MOBA_HEADS,),
        in_specs=[pl.BlockSpec(memory_space=pltpu.SMEM)],
        out_specs=[pl.BlockSpec((1,) + s, lambda h: (h, 0, 0)) for s in shapes],
        out_shape=[jax.ShapeDtypeStruct((MOBA_HEADS,) + s, F32) for s in shapes],
        compiler_params=_params(("arbitrary",)),
        name="bias_tables",
    )(rel_bias)


NORM_SLABS = (list(range(C_MQ, C_MV, LANES)) + list(range(C_NSQ, C_KC, LANES)) + [C_KSL, C_KWN])


def _inproj_kernel(x_ref, g_ref, w_ref, hg_ref, o_ref):
    h = _rms(x_ref[...], g_ref[...]).astype(BF16)
    y = _dot(h, w_ref[...])
    first = lax.broadcasted_iota(jnp.int32, (1, LANES), 1) < HEAD_DIM
    edges = sorted(set([0, PROJ_COLS] + NORM_SLABS + [c + LANES for c in NORM_SLABS]))
    for lo, hi in zip(edges[:-1], edges[1:]):
        t = y[:, lo:hi]
        if lo in NORM_SLABS:
            sq = t * t
            s0 = jnp.sum(jnp.where(first, sq, 0.0), axis=-1, keepdims=True)
            s1 = jnp.sum(jnp.where(first, 0.0, sq), axis=-1, keepdims=True)
            ms = jnp.where(first, s0, s1) * (1.0 / HEAD_DIM)
            t = t * lax.rsqrt(ms + RMS_EPS) * hg_ref[:, lo:hi]
        o_ref[:, lo:hi] = t


def _inproj(xf, gain, w, head_gain, layer):
    n, d = xf.shape
    tm = INPROJ_TM
    return pl.pallas_call(
        _inproj_kernel,
        grid=(n // tm,),
        in_specs=[pl.BlockSpec((tm, d), lambda i: (i, 0)), _layer_spec(gain, layer),
                  _layer_spec(w, layer), _layer_spec(head_gain, layer)],
        out_specs=pl.BlockSpec((tm, PROJ_COLS), lambda i: (i, 0)),
        out_shape=jax.ShapeDtypeStruct((n, PROJ_COLS), F32),
        compiler_params=_params(("parallel",)),
        name="inproj",
    )(xf, gain, w, head_gain)


MOBA_HPS = 8
MOBA_BPS = 2


def _moba_kernel(q_ref, k_ref, v_ref, t0_ref, t1_ref, o_ref, kaug_s, vt_s, km_s, *, seq):
    step = pl.program_id(2)
    blk = MOBA_BLOCK
    nb = seq // blk
    head_cols = lambda hh: slice(hh * HEAD_DIM, (hh + 1) * HEAD_DIM)
    heads = range(MOBA_HPS)
    items = [(t, hh) for t in range(MOBA_BPS) for hh in heads]
    ns = [step * MOBA_BPS + t for t in range(MOBA_BPS)]

    @pl.when(step == 0)
    def _prepare_keys():
        r = lax.broadcasted_iota(jnp.int32, (seq, LANES), 0)
        c = lax.broadcasted_iota(jnp.int32, (seq, LANES), 1)
        onehot = (c - HEAD_DIM == r // blk).astype(F32)
        place = _eye(HEAD_DIM, LANES)
        for hh in heads:
            kn = k_ref[:, head_cols(hh)]
            km = kn.reshape(nb, blk, HEAD_DIM).sum(axis=1) * (1.0 / blk)
            km_s[hh] = jnp.concatenate([km, jnp.zeros((16 - nb, HEAD_DIM), F32)], axis=0)
            kaug_s[hh] = (_dot(kn.astype(BF16), place) + onehot).astype(BF16)
            for j in range(nb):
                vj = v_ref[j * blk:(j + 1) * blk, head_cols(hh)].astype(BF16)
                vt_s[hh, j] = _value_tile(vj)

    jidx = lax.broadcasted_iota(jnp.int32, (16, blk), 0)
    key = lax.broadcasted_iota(jnp.int32, (blk, blk), 0)
    qry = lax.broadcasted_iota(jnp.int32, (blk, blk), 1)
    qts = {}
    for t in range(MOBA_BPS):
        for pair in range(MOBA_HPS // 2):
            qp = _transpose_bf16(q_ref[t * blk:(t + 1) * blk, pair * LANES:(pair + 1) * LANES].astype(BF16),
                                 LANES)
            qts[t, 2 * pair], qts[t, 2 * pair + 1] = qp[0:HEAD_DIM].astype(BF16), qp[HEAD_DIM:].astype(BF16)
    gates = {}
    for t, hh in items:
        kmh, kml = _split_bf16(km_s[hh])
        gates[t, hh] = _dot(kmh, qts[t, hh]) + _dot(kml, qts[t, hh])
    qaugs = {}
    for t, hh in items:
        past = jidx < ns[t]
        gate = jnp.where(past, gates[t, hh], NEG)
        sel = (past & _rank_select(gate, jidx, MOBA_TOPK)) | (jidx == ns[t])
        selb = jnp.where(sel, 0.0, NEG).astype(BF16)
        qaugs[t, hh] = jnp.concatenate([qts[t, hh], selb, jnp.zeros((LANES - HEAD_DIM - 16, blk), BF16)],
                                       axis=0)

    def scores(t, hh, j):
        start = pl.multiple_of(j * blk, blk)
        return _dot(kaug_s[hh, pl.ds(start, blk), :], qaugs[t, hh]), vt_s[hh, j]

    near = {(t, hh): (scores(t, hh, ns[t]), scores(t, hh, jnp.maximum(ns[t] - 1, 0))) for t, hh in items}
    state = {}
    for t, hh in items:
        (s0, vt0), (s1, vt1) = near[t, hh]
        s0 = jnp.where(key <= qry, s0, NEG) + t0_ref[hh]
        s1 = jnp.where(ns[t] >= 1, s1, NEG) + t1_ref[hh]
        state[t, hh] = _softmax_joint([(s0, vt0), (s1, vt1)])

    older = [jnp.maximum(n - 1, 0) for n in ns]
    lo = 0
    for first in range(MOBA_BPS):
        live = [(t, hh) for t, hh in items if t >= first]

        def far_body(j, carry, live=live):
            tiles = [scores(t, hh, j) for t, hh in live]
            return tuple(_softmax_update(c, s, [vt]) for c, (s, vt) in zip(carry, tiles))

        new = lax.fori_loop(lo, older[first], far_body, tuple(state[it] for it in live))
        state.update(dict(zip(live, new)))
        lo = older[first]

    for t in range(MOBA_BPS):
        o_ref[t * blk:(t + 1) * blk, :] = jnp.concatenate(
            [_softmax_finish(state[t, hh]) for hh in heads], axis=0).T.astype(BF16)


def _moba(proj, t0, t1, batch, seq):
    blk = MOBA_BLOCK
    nb = seq // blk
    width = MOBA_HPS * HEAD_DIM
    rows = MOBA_BPS * blk
    steps = nb // MOBA_BPS
    qc, kc, vc = C_MQ // width, C_MK // width, C_MV // width
    return pl.pallas_call(
        functools.partial(_moba_kernel, seq=seq),
        grid=(batch, MOBA_HEADS // MOBA_HPS, steps),
        in_specs=[pl.BlockSpec((rows, width), lambda b, h, n: (b * steps + n, qc + h)),
                  pl.BlockSpec((seq, width), lambda b, h, n: (b, kc + h)),
                  pl.BlockSpec((seq, width), lambda b, h, n: (b, vc + h)),
                  pl.BlockSpec((MOBA_HPS, blk, blk), lambda b, h, n: (h, 0, 0)),
                  pl.BlockSpec((MOBA_HPS, blk, blk), lambda b, h, n: (h, 0, 0))],
        out_specs=pl.BlockSpec((rows, width), lambda b, h, n: (b * steps + n, h)),
        out_shape=jax.ShapeDtypeStruct((batch * seq, MOBA_HEADS * HEAD_DIM), BF16),
        scratch_shapes=[pltpu.VMEM((MOBA_HPS, seq, LANES), BF16),
                        pltpu.VMEM((MOBA_HPS, nb, VROWS, blk), BF16),
                        pltpu.VMEM((MOBA_HPS, 16, HEAD_DIM), F32)],
        compiler_params=_params(("parallel", "parallel", "arbitrary")),
        name="moba",
    )(proj, proj, proj, t0, t1)


def _compress_kernel(k_ref, v_ref, w1k_ref, w1v_ref, wtk_ref, wtv_ref, w2k_ref, w2v_ref, pk_ref, pv_ref,
                     kg_ref, ko_ref, vo_ref):
    hid = NSA_CMP_HIDDEN
    chunks = k_ref.shape[0] // NSA_CMP_STRIDE

    def compress(t_ref, w1_ref, wt_ref, w2_ref, pos_ref, g):
        a = functools.reduce(jnp.add, [
            _dot(t_ref[pl.ds(l, chunks, stride=NSA_CMP_STRIDE), :].astype(BF16), wt_ref[g, l])
            for l in range(NSA_CMP_STRIDE)])
        pw = _dot(pos_ref[...].astype(BF16), w1_ref[...])
        pos = pw[0:1, :hid] + pw[1:2, hid:]
        nxt = pltpu.roll(a[:, hid:], chunks - 1, 0)
        h = jax.nn.gelu(a[:, :hid] + nxt + pos)
        return _dot(h.astype(BF16), w2_ref[...])

    for g in range(NSA_KV_GROUPS):
        kc = compress(k_ref, w1k_ref, wtk_ref, w2k_ref, pk_ref, g)
        ko_ref[0, g] = _rms(kc, kg_ref[...]).astype(BF16)
        vc = compress(v_ref, w1v_ref, wtv_ref, w2v_ref, pv_ref, g).astype(BF16)
        vo_ref[0, g] = _transpose_bf16(vc, HEAD_DIM).astype(BF16)


def _compress(proj, w1k, w1v, wtk, wtv, w2k, w2v, pk, pv, kg, layer, batch, seq):
    groups = NSA_KV_GROUPS
    chunks = seq // NSA_CMP_STRIDE
    tok = lambda col: pl.BlockSpec((seq, LANES), lambda b: (b, col // LANES))
    return pl.pallas_call(
        _compress_kernel,
        grid=(batch,),
        in_specs=[tok(C_KC), tok(C_VC)] + [_layer_spec(a, layer)
                                           for a in (w1k, w1v, wtk, wtv, w2k, w2v, pk, pv, kg)],
        out_specs=[pl.BlockSpec((1, groups, chunks, HEAD_DIM), lambda b: (b, 0, 0, 0)),
                   pl.BlockSpec((1, groups, HEAD_DIM, chunks), lambda b: (b, 0, 0, 0))],
        out_shape=[jax.ShapeDtypeStruct((batch, groups, chunks, HEAD_DIM), BF16),
                   jax.ShapeDtypeStruct((batch, groups, HEAD_DIM, chunks), BF16)],
        compiler_params=_params(("parallel",)),
        name="nsa_compress",
    )(proj, proj, w1k, w1v, wtk, wtv, w2k, w2v, pk, pv, kg)


def _nsa_kernel(q_ref, kc_ref, vc_ref, ksl_ref, vsl_ref, kwn_ref, vwn_ref, gt_ref,
                wc_ref, t0_ref, t1_ref, o_ref, kslaug_s, vslt_s, kwn_s, vwnt_s, sc_s, *, seq):
    step = pl.program_id(1)
    tq = NSA_TQ
    hpg = NSA_HPG
    sb = NSA_SEL_BLOCK
    nsel = seq // sb
    ncmp = seq // NSA_CMP_STRIDE
    nwin = NSA_WINDOW // tq
    groups = range(NSA_KV_GROUPS)
    items = [(t, g) for t in range(NSA_TPS) for g in groups]
    qis = [step * NSA_TPS + t for t in range(NSA_TPS)]

    @pl.when(step == 0)
    def _prepare_keys():
        r = lax.broadcasted_iota(jnp.int32, (seq, LANES), 0)
        c = lax.broadcasted_iota(jnp.int32, (seq, LANES), 1)
        onehot = (c - HEAD_DIM == r // sb).astype(F32)
        place = _eye(HEAD_DIM, LANES)
        sc_s[:, 0:CMP_PAD, :] = jnp.zeros((NSA_TPS * NSA_KV_GROUPS, CMP_PAD, hpg * tq), F32)
        for g in groups:
            sl = slice(g * HEAD_DIM, (g + 1) * HEAD_DIM)
            kslaug_s[g] = (_dot(ksl_ref[:, sl].astype(BF16), place) + onehot).astype(BF16)
            kwn_s[g] = kwn_ref[:, sl].astype(BF16)
            for j in range(seq // tq):
                rows = slice(j * tq, (j + 1) * tq)
                vslt_s[g, j] = _value_tile(vsl_ref[rows, sl].astype(BF16))
                vwnt_s[g, j] = _value_tile(vwn_ref[rows, sl].astype(BF16))

    lanes = lambda parts: jnp.concatenate(parts, axis=1)
    key = lax.broadcasted_iota(jnp.int32, (tq, tq), 0)
    qry = lax.broadcasted_iota(jnp.int32, (tq, tq), 1)
    causal = lanes([key <= qry] * hpg)
    upper = lanes([key > qry] * hpg)
    oj = lax.broadcasted_iota(jnp.int32, (nsel, ncmp), 0) * sb
    on = lax.broadcasted_iota(jnp.int32, (nsel, ncmp), 1) * NSA_CMP_STRIDE
    overlap = ((on < oj + sb) & (on + NSA_CMP_LEN > oj) & (on < seq - NSA_CMP_STRIDE)).astype(BF16)
    jidx = lax.broadcasted_iota(jnp.int32, (nsel, tq), 0)
    cend = lax.broadcasted_iota(jnp.int32, (ncmp, tq), 0) * NSA_CMP_STRIDE + (NSA_CMP_LEN - 1)
    cvis = [lanes([cend <= lax.broadcasted_iota(jnp.int32, (ncmp, tq), 1) + qi * tq] * hpg) for qi in qis]
    own = [(lax.broadcasted_iota(jnp.int32, (nsel, tq), 1) + qi * tq) // sb for qi in qis]
    tile_rows = lambda j: pl.ds(pl.multiple_of(j * tq, tq), tq)
    back = lambda t, k: jnp.maximum(qis[t] - k, 0)
    heads = [[g * hpg + r for r in range(hpg)] for g in groups]
    t0 = [lanes([t0_ref[h] for h in heads[g]]) for g in groups]
    t1 = [lanes([t1_ref[h] for h in heads[g]]) for g in groups]
    slot = lambda t, g: t * NSA_KV_GROUPS + g

    qplain, win_scores = {}, {}
    for t, g in items:
        width = hpg * HEAD_DIM
        qg = _transpose_bf16(q_ref[t * tq:(t + 1) * tq, g * width:(g + 1) * width].astype(BF16), width)
        qplain[t, g] = lanes([qg[r * HEAD_DIM:(r + 1) * HEAD_DIM] for r in range(hpg)]).astype(BF16)
        sc_s[slot(t, g), CMP_PAD:, :] = _dot(kc_ref[0, g], qplain[t, g])
        win_scores[t, g] = [_dot(kwn_s[g, tile_rows(back(t, k)), :], qplain[t, g]) for k in range(nwin + 1)]

    o_cmps, imps = {}, {}
    for t, g in items:
        win = pl.ds(pl.multiple_of(qis[t] * (tq // NSA_CMP_STRIDE), 8), CMP_WIN)
        sc_s[slot(t, g), win, :] += lanes([wc_ref[h] for h in heads[g]])
        s = jnp.where(cvis[t], sc_s[slot(t, g), CMP_PAD:, :], NEG)
        m = jnp.max(s, axis=0, keepdims=True)
        e = jnp.where(cvis[t], jnp.exp2(s - m), 0.0)
        den = jnp.sum(e, axis=0, keepdims=True)
        p = e * (1.0 / jnp.where(den > 0, den, 1.0))
        o_cmps[t, g] = _dot(vc_ref[0, g], p.astype(BF16))
        psum = p[:, 0:tq]
        for r in range(1, hpg):
            psum = psum + p[:, r * tq:(r + 1) * tq]
        ph, plo = _split_bf16(psum)
        imps[t, g] = _dot(overlap, ph) + _dot(overlap, plo)

    o_wins = {}
    for t, g in items:
        qi = qis[t]
        tiles = []
        for k, s in enumerate(win_scores[t, g]):
            if k == 0:
                s = jnp.where(causal, s, NEG) + t0[g]
            elif k == 1:
                s = jnp.where(qi >= 1, s, NEG) + t1[g]
            elif k < nwin:
                s = jnp.where(qi >= k, s, NEG)
            else:
                s = jnp.where(upper & (qi >= k), s, NEG)
            tiles.append((s, vwnt_s[g, back(t, k)]))
        o_wins[t, g] = _softmax_finish(_softmax_joint(tiles))

    qaugs = {}
    for t, g in items:
        imp = jnp.where((jidx == 0) | (jidx == own[t]) | (jidx == own[t] - 1), -NEG, imps[t, g])
        imp = jnp.where(jidx > own[t], NEG, imp)
        sel = _rank_select(imp, jidx, NSA_SEL_TOPN) & (jidx <= own[t])
        selb = jnp.where(sel, 0.0, NEG).astype(BF16)
        qaugs[t, g] = jnp.concatenate([qplain[t, g], lanes([selb] * hpg),
                                       jnp.zeros((LANES - HEAD_DIM - nsel, hpg * tq), BF16)], axis=0)

    slc_near = {(t, g): [_dot(kslaug_s[g, tile_rows(back(t, k)), :], qaugs[t, g]) for k in range(3)]
                for t, g in items}
    state = {}
    for t, g in items:
        qi = qis[t]
        odd = (qi >= 2) & (qi % 2 == 0)
        state[t, g] = _softmax_joint([
            (jnp.where(causal, slc_near[t, g][0], NEG) + t0[g], vslt_s[g, qi]),
            (jnp.where(qi >= 1, slc_near[t, g][1], NEG) + t1[g], vslt_s[g, back(t, 1)]),
            (jnp.where(odd, slc_near[t, g][2], NEG), vslt_s[g, back(t, 2)])])

    slabs = [jnp.maximum(qi - 1, 0) // 2 for qi in qis]
    lo = 0
    for first in range(NSA_TPS):
        live = [(t, g) for t, g in items if t >= first]

        def slc_far(i, carry, live=live):
            slab = pl.ds(pl.multiple_of(i * (2 * tq), 2 * tq), 2 * tq)
            scores = [_dot(kslaug_s[g, slab, :], qaugs[t, g]) for t, g in live]
            return tuple(_softmax_update(c, sc, [vslt_s[g, 2 * i], vslt_s[g, 2 * i + 1]])
                         for c, sc, (t, g) in zip(carry, scores, live))

        new = lax.fori_loop(lo, slabs[first], slc_far, tuple(state[it] for it in live))
        state.update(dict(zip(live, new)))
        lo = slabs[first]

    for t in range(NSA_TPS):
        gates = jax.nn.sigmoid(gt_ref[t * tq:(t + 1) * tq, :]).T
        outs = []
        for g in groups:
            o_slc = _softmax_finish(state[t, g])
            for r in range(hpg):
                h = g * hpg + r
                cols = slice(r * tq, (r + 1) * tq)
                outs.append(gates[h:h + 1, :] * o_cmps[t, g][:, cols]
                            + gates[NSA_HEADS + h:NSA_HEADS + h + 1, :] * o_slc[:, cols]
                            + gates[2 * NSA_HEADS + h:2 * NSA_HEADS + h + 1, :] * o_wins[t, g][:, cols])
        o_ref[t * tq:(t + 1) * tq, :] = jnp.concatenate(outs, axis=0).T.astype(BF16)


def _nsa(proj, kcn, vct, wc, t0, t1, batch, seq):
    tq = NSA_TQ
    nq = seq // tq
    rows = NSA_TPS * tq
    steps = nq // NSA_TPS
    ncmp = seq // NSA_CMP_STRIDE
    width = NSA_HEADS * HEAD_DIM
    groups = NSA_KV_GROUPS
    kv = lambda col: pl.BlockSpec((seq, LANES), lambda b, i: (b, col // LANES))
    const = lambda shape: pl.BlockSpec(shape, lambda b, i: (0,) * len(shape))
    return pl.pallas_call(
        functools.partial(_nsa_kernel, seq=seq),
        grid=(batch, steps),
        in_specs=[pl.BlockSpec((rows, width), lambda b, i: (b * steps + i, C_NSQ // width)),
                  pl.BlockSpec((1, groups, ncmp, HEAD_DIM), lambda b, i: (b, 0, 0, 0)),
                  pl.BlockSpec((1, groups, HEAD_DIM, ncmp), lambda b, i: (b, 0, 0, 0)),
                  kv(C_KSL), kv(C_VSL), kv(C_KWN), kv(C_VWN),
                  pl.BlockSpec((rows, LANES), lambda b, i: (b * steps + i, C_NG // LANES)),
                  const((NSA_HEADS, CMP_WIN, tq)), const((NSA_HEADS, tq, tq)),
                  const((NSA_HEADS, tq, tq))],
        out_specs=pl.BlockSpec((rows, width), lambda b, i: (b * steps + i, 0)),
        out_shape=jax.ShapeDtypeStruct((batch * seq, width), BF16),
        scratch_shapes=[pltpu.VMEM((groups, seq, LANES), BF16),
                        pltpu.VMEM((groups, nq, VROWS, tq), BF16),
                        pltpu.VMEM((groups, seq, HEAD_DIM), BF16),
                        pltpu.VMEM((groups, nq, VROWS, tq), BF16),
                        pltpu.VMEM((NSA_TPS * groups, CMP_PAD + ncmp, NSA_HPG * tq), F32)],
        compiler_params=_params(("parallel", "arbitrary")),
        name="nsa",
    )(proj, kcn, vct, proj, proj, proj, proj, proj, wc, t0, t1)


def _merge_kernel(x_ref, ya_ref, yb_ref, ga_ref, gb_ref, wa_ref, wb_ref, wo_ref, o_ref):
    a = _dot(ya_ref[...], wa_ref[...])
    b = _dot(yb_ref[...], wb_ref[...])
    z = jax.nn.sigmoid(ga_ref[...]) * a + jax.nn.sigmoid(gb_ref[...]) * b
    o_ref[...] = x_ref[...] + _dot(z.astype(BF16), wo_ref[...])


def _merge(xf, ya, yb, proj, wa, wb, wo, layer):
    n, d = xf.shape
    tm = MERGE_TM
    row = lambda w, col=0: pl.BlockSpec((tm, w), lambda i: (i, col))
    return pl.pallas_call(
        _merge_kernel,
        grid=(n // tm,),
        in_specs=[row(d), row(ya.shape[1]), row(yb.shape[1]), row(d, C_GA // d), row(d, C_GB // d),
                  _layer_spec(wa, layer), _layer_spec(wb, layer), _layer_spec(wo, layer)],
        out_specs=row(d),
        out_shape=jax.ShapeDtypeStruct((n, d), F32),
        compiler_params=_params(("parallel",)),
        name="merge",
    )(xf, ya, yb, proj, proj, wa, wb, wo)


FFN_HALO = 16
FFN_TM = 1024
FFN_TF = 256


def _ffn_kernel(x_ref, xh_ref, g_ref, wu_ref, cw_ref, cb_ref, wd_ref, p_ref, wg_ref, wp_ref,
                o_ref, act_s, *, seq, d_ff):
    i = pl.program_id(0)
    tm = x_ref.shape[0]
    x = x_ref[...]
    at_start = (i * tm) % seq == 0
    halo = jnp.where(at_start, 0.0, _rms(xh_ref[...], g_ref[...]))
    hn = jnp.concatenate([halo.astype(BF16), _rms(x, g_ref[...]).astype(BF16)], axis=0)

    def conv(cols):
        u = _dot(hn, wu_ref[:, cols])
        u1 = pltpu.roll(u, 1, 0)[FFN_HALO:]
        u2 = pltpu.roll(u, 2, 0)[FFN_HALO:]
        cw = cw_ref[:, cols]
        return cw[0:1] * u2 + cw[1:2] * u1 + cw[2:3] * u[FFN_HALO:] + cb_ref[:, cols]

    for c in range(d_ff // FFN_TF):
        lo = c * FFN_TF
        act = jax.nn.gelu(conv(slice(lo, lo + FFN_TF))) * conv(slice(d_ff + lo, d_ff + lo + FFN_TF))
        act_s[:, lo:lo + FFN_TF] = act.astype(BF16)

    x = x + _dot(act_s[...], wd_ref[...])
    gate = jax.nn.sigmoid(_dot(x.astype(BF16), wg_ref[...]))
    o_ref[...] = x + gate * _dot(p_ref[...].astype(BF16), wp_ref[...])


def _ffn_ple(xf, gain, w_up, conv_w, conv_b, w_down, pf, wg, wp, layer, seq):
    n, d = xf.shape
    d_ff = w_down.shape[1]
    tm = FFN_TM
    hb = tm // FFN_HALO
    resident = lambda a: _layer_spec(a, layer)
    return pl.pallas_call(
        functools.partial(_ffn_kernel, seq=seq, d_ff=d_ff),
        grid=(n // tm,),
        in_specs=[pl.BlockSpec((tm, d), lambda i: (i, 0)),
                  pl.BlockSpec((FFN_HALO, d), lambda i: (jnp.maximum(i * hb - 1, 0), 0)),
                  resident(gain), resident(w_up), resident(conv_w), resident(conv_b), resident(w_down),
                  pl.BlockSpec((None, tm, pf.shape[2]), lambda i: (layer, i, 0)),
                  resident(wg), resident(wp)],
        out_specs=pl.BlockSpec((tm, d), lambda i: (i, 0)),
        out_shape=jax.ShapeDtypeStruct((n, d), F32),
        scratch_shapes=[pltpu.VMEM((tm, d_ff), BF16)],
        compiler_params=_params(("parallel",)),
        name="conv_ffn_ple",
    )(xf, xf, gain, w_up, conv_w, conv_b, w_down, pf, wg, wp)


def _reorder_in_proj(w):
    attn = 3 * MOBA_HEADS * HEAD_DIM + NSA_HEADS * HEAD_DIM + 6 * NSA_KV_GROUPS * HEAD_DIM
    ng = 3 * NSA_HEADS
    pad = jnp.zeros(w.shape[:-1] + (PROJ_COLS - w.shape[-1],), w.dtype)
    return jnp.concatenate([w[..., attn + ng:], w[..., :attn], w[..., attn:attn + ng], pad], axis=-1)


def _head_gain_rows(moba_q, moba_k, nsa_q, nsa_k):
    rows = jnp.zeros((moba_q.shape[0], 1, PROJ_COLS), F32)
    for col, gain, heads in ((C_MQ, moba_q * QSCALE, MOBA_HEADS), (C_MK, moba_k, MOBA_HEADS),
                             (C_NSQ, nsa_q * QSCALE, NSA_HEADS), (C_KSL, nsa_k[:, 1], NSA_KV_GROUPS),
                             (C_KWN, nsa_k[:, 2], NSA_KV_GROUPS)):
        rows = rows.at[:, 0, col:col + heads * HEAD_DIM].set(jnp.tile(gain, (1, heads)))
    return rows


def _cmp_weights(w1, pos):
    depth = w1.shape[0]
    half = NSA_CMP_STRIDE * HEAD_DIM
    w1cat = jnp.concatenate([w1[:, :half], w1[:, half:]], axis=2).astype(BF16)
    tok = w1cat.reshape(depth, NSA_CMP_STRIDE, HEAD_DIM, -1)
    zero = jnp.zeros_like(tok)
    placed = jnp.stack([jnp.concatenate([tok, zero], axis=2), jnp.concatenate([zero, tok], axis=2)], axis=1)
    posr = jnp.zeros((depth, 8, half), F32).at[:, 0:2].set(pos.reshape(depth, 2, half))
    return w1cat, placed, posr


def kernel(x, p, rel_bias, attn_norm, w_in, moba_q_gain, moba_k_gain, nsa_q_gain, nsa_k_gain,
           cmp_pos_k, cmp_w1_k, cmp_w2_k, cmp_pos_v, cmp_w1_v, cmp_w2_v,
           w_br_moba, w_br_nsa, w_o, ffn_norm, w_up, conv_w, conv_b, w_down, w_ple_gate, w_ple):
    batch, seq, d = x.shape
    n = batch * seq
    depth = w_in.shape[0]
    bf = lambda a: a.astype(BF16)
    w_in_r = _reorder_in_proj(bf(w_in))
    head_gain = _head_gain_rows(moba_q_gain, moba_k_gain, nsa_q_gain, nsa_k_gain)
    attn_gain, ffn_gain = attn_norm[:, None, :], ffn_norm[:, None, :]
    w1k, wtk, pk = _cmp_weights(cmp_w1_k, cmp_pos_k)
    w1v, wtv, pv = _cmp_weights(cmp_w1_v, cmp_pos_v)
    w2k, w2v, kg0 = bf(cmp_w2_k), bf(cmp_w2_v), nsa_k_gain[:, 0:1]
    wa, wb, wo = bf(w_br_moba), bf(w_br_nsa), bf(w_o)
    wu, wd, wg, wp = bf(w_up), bf(w_down), bf(w_ple_gate), bf(w_ple)
    conv_b = conv_b[:, None, :]
    pf = p.reshape(depth, n, -1)

    t0m, t1m, t0n, t1n, wc = _bias_tables(rel_bias)
    xf = x.reshape(n, d)
    for i in range(depth):
        proj = _inproj(xf, attn_gain, w_in_r, head_gain, i)
        ya = _moba(proj, t0m, t1m, batch, seq)
        kcn, vct = _compress(proj, w1k, w1v, wtk, wtv, w2k, w2v, pk, pv, kg0, i, batch, seq)
        yb = _nsa(proj, kcn, vct, wc, t0n, t1n, batch, seq)
        xf = _merge(xf, ya, yb, proj, wa, wb, wo, i)
        xf = _ffn_ple(xf, ffn_gain, wu, conv_w, conv_b, wd, pf, wg, wp, i, seq)
    return xf.reshape(batch, seq, d)
```

```python
import functools
import math

import numpy as np
import jax
import jax.numpy as jnp
from jax import lax
from jax.experimental import pallas as pl
from jax.experimental.pallas import tpu as pltpu

F32 = jnp.float32
BF16 = jnp.bfloat16

HEAD_DIM = 64
MOBA_HEADS = 8
MOBA_BLOCK = 256
MOBA_TOPK = 3
NSA_HEADS = 8
NSA_KV_GROUPS = 2
NSA_HPG = NSA_HEADS // NSA_KV_GROUPS
NSA_CMP_LEN = 32
NSA_CMP_STRIDE = 16
NSA_CMP_HIDDEN = 2 * HEAD_DIM
NSA_SEL_BLOCK = 64
NSA_SEL_TOPN = 16
NSA_WINDOW = 512
REL_BUCKETS = 32
REL_MAX_DIST = 128
RMS_EPS = 1e-6
SCALE = HEAD_DIM ** -0.5
LOG2E = math.log2(math.e)
QSCALE = SCALE * LOG2E
NEG = -1e30

LANES = 128
V7X_VMEM_BYTES = 64 * 1024 * 1024
VMEM_LIMIT = V7X_VMEM_BYTES * 7 // 8

INPROJ_TM = 512
MERGE_TM = 512

C_GA, C_GB, C_MQ, C_MK, C_MV, C_NSQ = 0, 1024, 2048, 2560, 3072, 3584
C_KC, C_VC, C_KSL, C_VSL, C_KWN, C_VWN, C_NG = 4096, 4224, 4352, 4480, 4608, 4736, 4864
PROJ_COLS = 4992

NSA_TQ = 128
NSA_TPS = 4
CMP_WIN = 16
CMP_PAD = 8


def _dot(a, b):
    return jnp.dot(a, b, preferred_element_type=F32)


def _dot_nt(a, b):
    return lax.dot_general(a, b, (((1,), (1,)), ((), ())), preferred_element_type=F32)


def _rms(x, gain):
    return x * lax.rsqrt(jnp.mean(x * x, axis=-1, keepdims=True) + RMS_EPS) * gain


def _split_bf16(x):
    hi = x.astype(BF16)
    return hi, (x - hi.astype(F32)).astype(BF16)


def _eye(rows, cols):
    r = lax.broadcasted_iota(jnp.int32, (rows, cols), 0)
    c = lax.broadcasted_iota(jnp.int32, (rows, cols), 1)
    return (r == c).astype(BF16)


def _transpose_bf16(x, rows):
    return _dot_nt(_eye(rows, x.shape[1]), x)


VROWS = 80


def _value_tile(v):
    row = lax.broadcasted_iota(jnp.int32, (VROWS, v.shape[0]), 0)
    return jnp.where(row == HEAD_DIM, 1.0, _transpose_bf16(v, VROWS)).astype(BF16)


def _softmax_joint(tiles):
    m = functools.reduce(jnp.maximum, [jnp.max(s, axis=0, keepdims=True) for s, _ in tiles])
    acc = functools.reduce(jnp.add, [_dot(vt, jnp.exp2(s - m).astype(BF16)) for s, vt in tiles])
    return m, acc


def _softmax_update(carry, s, vts):
    m, acc = carry
    m_new = jnp.maximum(m, jnp.max(s, axis=0, keepdims=True))
    p = jnp.exp2(s - m_new).astype(BF16)
    rows = s.shape[0] // len(vts)
    pv = functools.reduce(jnp.add, [_dot(vt, p[i * rows:(i + 1) * rows]) for i, vt in enumerate(vts)])
    return m_new, jnp.exp2(m - m_new) * acc + pv


def _softmax_finish(carry):
    _, acc = carry
    return acc[0:HEAD_DIM] * (1.0 / acc[HEAD_DIM:HEAD_DIM + 1])


def _rank_select(score, idx, count):
    beaten = jnp.zeros(score.shape, jnp.int32)
    for i in range(score.shape[0]):
        si = score[i:i + 1, :]
        beaten += ((si > score) | ((si == score) & (i < idx))).astype(jnp.int32)
    return beaten < count


def _params(sem):
    return pltpu.CompilerParams(dimension_semantics=sem, vmem_limit_bytes=VMEM_LIMIT)


def _layer_spec(a, layer):
    return pl.BlockSpec((None,) + a.shape[1:], lambda *_: (layer,) + (0,) * (a.ndim - 1),
                        pipeline_mode=pl.Buffered(1))


def _rel_bucket_np(dist):
    n = np.maximum(dist, 0)
    max_exact = REL_BUCKETS // 2
    nf = np.maximum(n, 1).astype(np.float32)
    large = max_exact + (np.log(nf / np.float32(max_exact)) / np.float32(math.log(REL_MAX_DIST / max_exact))
                         * np.float32(REL_BUCKETS - max_exact)).astype(np.int32)
    return np.where(n < max_exact, n, np.minimum(large, REL_BUCKETS - 1))


def _bucket_starts():
    buckets = _rel_bucket_np(np.arange(4 * REL_MAX_DIST))
    return [int(np.argmax(buckets >= k)) for k in range(REL_BUCKETS)]


BUCKET_START = _bucket_starts()
BIAS_REACH = BUCKET_START[-1]
assert BIAS_REACH <= MOBA_BLOCK and BIAS_REACH <= NSA_TQ - NSA_CMP_LEN + 1 + NSA_CMP_STRIDE


def _tables_kernel(tab_ref, t0m_ref, t1m_ref, t0n_ref, t1n_ref, wc_ref):
    h = pl.program_id(0)

    def bias(dist, head):
        last = tab_ref[head, REL_BUCKETS - 1]
        val = jnp.zeros(dist.shape, F32)
        for k in range(REL_BUCKETS - 2, -1, -1):
            val = jnp.where(dist < BUCKET_START[k + 1], (tab_ref[head, k] - last) * LOG2E, val)
        return val

    def toeplitz(size, offset, head):
        key = lax.broadcasted_iota(jnp.int32, (size, size), 0)
        qry = lax.broadcasted_iota(jnp.int32, (size, size), 1)
        return bias(offset + qry - key, head)

    t0m_ref[0] = toeplitz(MOBA_BLOCK, 0, h)
    t1m_ref[0] = toeplitz(MOBA_BLOCK, MOBA_BLOCK, h)
    t0n_ref[0] = toeplitz(NSA_TQ, 0, MOBA_HEADS + h)
    t1n_ref[0] = toeplitz(NSA_TQ, NSA_TQ, MOBA_HEADS + h)
    a = lax.broadcasted_iota(jnp.int32, (CMP_WIN, NSA_TQ), 0)
    i = lax.broadcasted_iota(jnp.int32, (CMP_WIN, NSA_TQ), 1)
    wc_ref[0] = bias(i + (NSA_TQ - NSA_CMP_LEN + 1) - NSA_CMP_STRIDE * a, MOBA_HEADS + h)


def _bias_tables(rel_bias):
    blk, tq = MOBA_BLOCK, NSA_TQ
    shapes = [(blk, blk), (blk, blk), (tq, tq), (tq, tq), (CMP_WIN, tq)]
    return pl.pallas_call(
        _tables_kernel,
        grid=(MOBA_HEADS,),
        in_specs=[pl.BlockSpec(memory_space=pltpu.SMEM)],
        out_specs=[pl.BlockSpec((1,) + s, lambda h: (h, 0, 0)) for s in shapes],
        out_shape=[jax.ShapeDtypeStruct((MOBA_HEADS,) + s, F32) for s in shapes],
        compiler_params=_params(("arbitrary",)),
        name="bias_tables",
    )(rel_bias)


NORM_SLABS = (list(range(C_MQ, C_MV, LANES)) + list(range(C_NSQ, C_KC, LANES)) + [C_KSL, C_KWN])


def _inproj_kernel(x_ref, g_ref, w_ref, hg_ref, o_ref):
    h = _rms(x_ref[...], g_ref[...]).astype(BF16)
    y = _dot(h, w_ref[...])
    first = lax.broadcasted_iota(jnp.int32, (1, LANES), 1) < HEAD_DIM
    edges = sorted(set([0, PROJ_COLS] + NORM_SLABS + [c + LANES for c in NORM_SLABS]))
    for lo, hi in zip(edges[:-1], edges[1:]):
        t = y[:, lo:hi]
        if lo in NORM_SLABS:
            sq = t * t
            s0 = jnp.sum(jnp.where(first, sq, 0.0), axis=-1, keepdims=True)
            s1 = jnp.sum(jnp.where(first, 0.0, sq), axis=-1, keepdims=True)
            ms = jnp.where(first, s0, s1) * (1.0 / HEAD_DIM)
            t = t * lax.rsqrt(ms + RMS_EPS) * hg_ref[:, lo:hi]
        o_ref[:, lo:hi] = t


def _inproj(xf, gain, w, head_gain, layer):
    n, d = xf.shape
    tm = INPROJ_TM
    return pl.pallas_call(
        _inproj_kernel,
        grid=(n // tm,),
        in_specs=[pl.BlockSpec((tm, d), lambda i: (i, 0)), _layer_spec(gain, layer),
                  _layer_spec(w, layer), _layer_spec(head_gain, layer)],
        out_specs=pl.BlockSpec((tm, PROJ_COLS), lambda i: (i, 0)),
        out_shape=jax.ShapeDtypeStruct((n, PROJ_COLS), F32),
        compiler_params=_params(("parallel",)),
        name="inproj",
    )(xf, gain, w, head_gain)


MOBA_HPS = 8
MOBA_BPS = 2


def _moba_kernel(q_ref, k_ref, v_ref, t0_ref, t1_ref, o_ref, kaug_s, vt_s, km_s, *, seq):
    step = pl.program_id(2)
    blk = MOBA_BLOCK
    nb = seq // blk
    head_cols = lambda hh: slice(hh * HEAD_DIM, (hh + 1) * HEAD_DIM)
    heads = range(MOBA_HPS)
    items = [(t, hh) for t in range(MOBA_BPS) for hh in heads]
    ns = [step * MOBA_BPS + t for t in range(MOBA_BPS)]

    @pl.when(step == 0)
    def _prepare_keys():
        r = lax.broadcasted_iota(jnp.int32, (seq, LANES), 0)
        c = lax.broadcasted_iota(jnp.int32, (seq, LANES), 1)
        onehot = (c - HEAD_DIM == r // blk).astype(F32)
        place = _eye(HEAD_DIM, LANES)
        for hh in heads:
            kn = k_ref[:, head_cols(hh)]
            km = kn.reshape(nb, blk, HEAD_DIM).sum(axis=1) * (1.0 / blk)
            km_s[hh] = jnp.concatenate([km, jnp.zeros((16 - nb, HEAD_DIM), F32)], axis=0)
            kaug_s[hh] = (_dot(kn.astype(BF16), place) + onehot).astype(BF16)
            for j in range(nb):
                vj = v_ref[j * blk:(j + 1) * blk, head_cols(hh)].astype(BF16)
                vt_s[hh, j] = _value_tile(vj)

    jidx = lax.broadcasted_iota(jnp.int32, (16, blk), 0)
    key = lax.broadcasted_iota(jnp.int32, (blk, blk), 0)
    qry = lax.broadcasted_iota(jnp.int32, (blk, blk), 1)
    qts = {}
    for t in range(MOBA_BPS):
        for pair in range(MOBA_HPS // 2):
            qp = _transpose_bf16(q_ref[t * blk:(t + 1) * blk, pair * LANES:(pair + 1) * LANES].astype(BF16),
                                 LANES)
            qts[t, 2 * pair], qts[t, 2 * pair + 1] = qp[0:HEAD_DIM].astype(BF16), qp[HEAD_DIM:].astype(BF16)
    gates = {}
    for t, hh in items:
        kmh, kml = _split_bf16(km_s[hh])
        gates[t, hh] = _dot(kmh, qts[t, hh]) + _dot(kml, qts[t, hh])
    qaugs = {}
    for t, hh in items:
        past = jidx < ns[t]
        gate = jnp.where(past, gates[t, hh], NEG)
        sel = (past & _rank_select(gate, jidx, MOBA_TOPK)) | (jidx == ns[t])
        selb = jnp.where(sel, 0.0, NEG).astype(BF16)
        qaugs[t, hh] = jnp.concatenate([qts[t, hh], selb, jnp.zeros((LANES - HEAD_DIM - 16, blk), BF16)],
                                       axis=0)

    def scores(t, hh, j):
        start = pl.multiple_of(j * blk, blk)
        return _dot(kaug_s[hh, pl.ds(start, blk), :], qaugs[t, hh]), vt_s[hh, j]

    near = {(t, hh): (scores(t, hh, ns[t]), scores(t, hh, jnp.maximum(ns[t] - 1, 0))) for t, hh in items}
    state = {}
    for t, hh in items:
        (s0, vt0), (s1, vt1) = near[t, hh]
        s0 = jnp.where(key <= qry, s0, NEG) + t0_ref[hh]
        s1 = jnp.where(ns[t] >= 1, s1, NEG) + t1_ref[hh]
        state[t, hh] = _softmax_joint([(s0, vt0), (s1, vt1)])

    older = [jnp.maximum(n - 1, 0) for n in ns]
    lo = 0
    for first in range(MOBA_BPS):
        live = [(t, hh) for t, hh in items if t >= first]

        def far_body(j, carry, live=live):
            tiles = [scores(t, hh, j) for t, hh in live]
            return tuple(_softmax_update(c, s, [vt]) for c, (s, vt) in zip(carry, tiles))

        new = lax.fori_loop(lo, older[first], far_body, tuple(state[it] for it in live))
        state.update(dict(zip(live, new)))
        lo = older[first]

    for t in range(MOBA_BPS):
        o_ref[t * blk:(t + 1) * blk, :] = jnp.concatenate(
            [_softmax_finish(state[t, hh]) for hh in heads], axis=0).T.astype(BF16)


def _moba(proj, t0, t1, batch, seq):
    blk = MOBA_BLOCK
    nb = seq // blk
    width = MOBA_HPS * HEAD_DIM
    rows = MOBA_BPS * blk
    steps = nb // MOBA_BPS
    qc, kc, vc = C_MQ // width, C_MK // width, C_MV // width
    return pl.pallas_call(
        functools.partial(_moba_kernel, seq=seq),
        grid=(batch, MOBA_HEADS // MOBA_HPS, steps),
        in_specs=[pl.BlockSpec((rows, width), lambda b, h, n: (b * steps + n, qc + h)),
                  pl.BlockSpec((seq, width), lambda b, h, n: (b, kc + h)),
                  pl.BlockSpec((seq, width), lambda b, h, n: (b, vc + h)),
                  pl.BlockSpec((MOBA_HPS, blk, blk), lambda b, h, n: (h, 0, 0)),
                  pl.BlockSpec((MOBA_HPS, blk, blk), lambda b, h, n: (h, 0, 0))],
        out_specs=pl.BlockSpec((rows, width), lambda b, h, n: (b * steps + n, h)),
        out_shape=jax.ShapeDtypeStruct((batch * seq, MOBA_HEADS * HEAD_DIM), BF16),
        scratch_shapes=[pltpu.VMEM((MOBA_HPS, seq, LANES), BF16),
                        pltpu.VMEM((MOBA_HPS, nb, VROWS, blk), BF16),
                        pltpu.VMEM((MOBA_HPS, 16, HEAD_DIM), F32)],
        compiler_params=_params(("parallel", "parallel", "arbitrary")),
        name="moba",
    )(proj, proj, proj, t0, t1)


def _compress_kernel(k_ref, v_ref, w1k_ref, w1v_ref, wtk_ref, wtv_ref, w2k_ref, w2v_ref, pk_ref, pv_ref,
                     kg_ref, ko_ref, vo_ref):
    hid = NSA_CMP_HIDDEN
    chunks = k_ref.shape[0] // NSA_CMP_STRIDE

    def chunk_rows(t_ref):
        return jnp.concatenate([t_ref[pl.ds(l, chunks, stride=NSA_CMP_STRIDE), :].astype(BF16)
                                for l in range(NSA_CMP_STRIDE)], axis=1)

    def compress(tok, w1_ref, wt_ref, w2_ref, pos_ref, g):
        a = _dot(tok, wt_ref[g])
        pw = _dot(pos_ref[...].astype(BF16), w1_ref[...])
        pos = pw[0:1, :hid] + pw[1:2, hid:]
        nxt = pltpu.roll(a[:, hid:], chunks - 1, 0)
        h = jax.nn.gelu(a[:, :hid] + nxt + pos)
        return _dot(h.astype(BF16), w2_ref[...])

    ktok, vtok = chunk_rows(k_ref), chunk_rows(v_ref)
    for g in range(NSA_KV_GROUPS):
        kc = compress(ktok, w1k_ref, wtk_ref, w2k_ref, pk_ref, g)
        ko_ref[0, g] = _rms(kc, kg_ref[...]).astype(BF16)
        vc = compress(vtok, w1v_ref, wtv_ref, w2v_ref, pv_ref, g).astype(BF16)
        vo_ref[0, g] = _transpose_bf16(vc, HEAD_DIM).astype(BF16)


def _compress(proj, w1k, w1v, wtk, wtv, w2k, w2v, pk, pv, kg, layer, batch, seq):
    groups = NSA_KV_GROUPS
    chunks = seq // NSA_CMP_STRIDE
    tok = lambda col: pl.BlockSpec((seq, LANES), lambda b: (b, col // LANES))
    return pl.pallas_call(
        _compress_kernel,
        grid=(batch,),
        in_specs=[tok(C_KC), tok(C_VC)] + [_layer_spec(a, layer)
                                           for a in (w1k, w1v, wtk, wtv, w2k, w2v, pk, pv, kg)],
        out_specs=[pl.BlockSpec((1, groups, chunks, HEAD_DIM), lambda b: (b, 0, 0, 0)),
                   pl.BlockSpec((1, groups, HEAD_DIM, chunks), lambda b: (b, 0, 0, 0))],
        out_shape=[jax.ShapeDtypeStruct((batch, groups, chunks, HEAD_DIM), BF16),
                   jax.ShapeDtypeStruct((batch, groups, HEAD_DIM, chunks), BF16)],
        compiler_params=_params(("parallel",)),
        name="nsa_compress",
    )(proj, proj, w1k, w1v, wtk, wtv, w2k, w2v, pk, pv, kg)


def _nsa_kernel(q_ref, kc_ref, vc_ref, ksl_ref, vsl_ref, kwn_ref, vwn_ref, gt_ref,
                wc_ref, t0_ref, t1_ref, o_ref, kslaug_s, vslt_s, kwn_s, vwnt_s, sc_s, *, seq):
    step = pl.program_id(1)
    tq = NSA_TQ
    hpg = NSA_HPG
    sb = NSA_SEL_BLOCK
    nsel = seq // sb
    ncmp = seq // NSA_CMP_STRIDE
    nwin = NSA_WINDOW // tq
    groups = range(NSA_KV_GROUPS)
    items = [(t, g) for t in range(NSA_TPS) for g in groups]
    qis = [step * NSA_TPS + t for t in range(NSA_TPS)]

    @pl.when(step == 0)
    def _prepare_keys():
        r = lax.broadcasted_iota(jnp.int32, (seq, LANES), 0)
        c = lax.broadcasted_iota(jnp.int32, (seq, LANES), 1)
        onehot = (c - HEAD_DIM == r // sb).astype(F32)
        place = _eye(HEAD_DIM, LANES)
        sc_s[:, 0:CMP_PAD, :] = jnp.zeros((NSA_TPS * NSA_KV_GROUPS, CMP_PAD, hpg * tq), F32)
        for g in groups:
            sl = slice(g * HEAD_DIM, (g + 1) * HEAD_DIM)
            kslaug_s[g] = (_dot(ksl_ref[:, sl].astype(BF16), place) + onehot).astype(BF16)
            kwn_s[g] = kwn_ref[:, sl].astype(BF16)
            for j in range(seq // tq):
                rows = slice(j * tq, (j + 1) * tq)
                vslt_s[g, j] = _value_tile(vsl_ref[rows, sl].astype(BF16))
                vwnt_s[g, j] = _value_tile(vwn_ref[rows, sl].astype(BF16))

    lanes = lambda parts: jnp.concatenate(parts, axis=1)
    key = lax.broadcasted_iota(jnp.int32, (tq, tq), 0)
    qry = lax.broadcasted_iota(jnp.int32, (tq, tq), 1)
    causal = lanes([key <= qry] * hpg)
    upper = lanes([key > qry] * hpg)
    oj = lax.broadcasted_iota(jnp.int32, (nsel, ncmp), 0) * sb
    on = lax.broadcasted_iota(jnp.int32, (nsel, ncmp), 1) * NSA_CMP_STRIDE
    overlap = ((on < oj + sb) & (on + NSA_CMP_LEN > oj) & (on < seq - NSA_CMP_STRIDE)).astype(BF16)
    jidx = lax.broadcasted_iota(jnp.int32, (nsel, tq), 0)
    cend = lax.broadcasted_iota(jnp.int32, (ncmp, tq), 0) * NSA_CMP_STRIDE + (NSA_CMP_LEN - 1)
    cvis = [lanes([cend <= lax.broadcasted_iota(jnp.int32, (ncmp, tq), 1) + qi * tq] * hpg) for qi in qis]
    own = [(lax.broadcasted_iota(jnp.int32, (nsel, tq), 1) + qi * tq) // sb for qi in qis]
    tile_rows = lambda j: pl.ds(pl.multiple_of(j * tq, tq), tq)
    back = lambda t, k: jnp.maximum(qis[t] - k, 0)
    heads = [[g * hpg + r for r in range(hpg)] for g in groups]
    t0 = [lanes([t0_ref[h] for h in heads[g]]) for g in groups]
    t1 = [lanes([t1_ref[h] for h in heads[g]]) for g in groups]
    slot = lambda t, g: t * NSA_KV_GROUPS + g

    qplain, win_scores = {}, {}
    for t, g in items:
        width = hpg * HEAD_DIM
        qg = _transpose_bf16(q_ref[t * tq:(t + 1) * tq, g * width:(g + 1) * width].astype(BF16), width)
        qplain[t, g] = lanes([qg[r * HEAD_DIM:(r + 1) * HEAD_DIM] for r in range(hpg)]).astype(BF16)
        sc_s[slot(t, g), CMP_PAD:, :] = _dot(kc_ref[0, g], qplain[t, g])
        win_scores[t, g] = [_dot(kwn_s[g, tile_rows(back(t, k)), :], qplain[t, g]) for k in range(nwin + 1)]

    o_cmps, imps = {}, {}
    for t, g in items:
        win = pl.ds(pl.multiple_of(qis[t] * (tq // NSA_CMP_STRIDE), 8), CMP_WIN)
        sc_s[slot(t, g), win, :] += lanes([wc_ref[h] for h in heads[g]])
        s = jnp.where(cvis[t], sc_s[slot(t, g), CMP_PAD:, :], NEG)
        m = jnp.max(s, axis=0, keepdims=True)
        e = jnp.where(cvis[t], jnp.exp2(s - m), 0.0)
        den = jnp.sum(e, axis=0, keepdims=True)
        p = e * (1.0 / jnp.where(den > 0, den, 1.0))
        o_cmps[t, g] = _dot(vc_ref[0, g], p.astype(BF16))
        psum = p[:, 0:tq]
        for r in range(1, hpg):
            psum = psum + p[:, r * tq:(r + 1) * tq]
        ph, plo = _split_bf16(psum)
        imps[t, g] = _dot(overlap, ph) + _dot(overlap, plo)

    o_wins = {}
    for t, g in items:
        qi = qis[t]
        tiles = []
        for k, s in enumerate(win_scores[t, g]):
            if k == 0:
                s = jnp.where(causal, s, NEG) + t0[g]
            elif k == 1:
                s = jnp.where(qi >= 1, s, NEG) + t1[g]
            elif k < nwin:
                s = jnp.where(qi >= k, s, NEG)
            else:
                s = jnp.where(upper & (qi >= k), s, NEG)
            tiles.append((s, vwnt_s[g, back(t, k)]))
        o_wins[t, g] = _softmax_finish(_softmax_joint(tiles))

    qaugs = {}
    for t, g in items:
        imp = jnp.where((jidx == 0) | (jidx == own[t]) | (jidx == own[t] - 1), -NEG, imps[t, g])
        imp = jnp.where(jidx > own[t], NEG, imp)
        sel = _rank_select(imp, jidx, NSA_SEL_TOPN) & (jidx <= own[t])
        selb = jnp.where(sel, 0.0, NEG).astype(BF16)
        qaugs[t, g] = jnp.concatenate([qplain[t, g], lanes([selb] * hpg),
                                       jnp.zeros((LANES - HEAD_DIM - nsel, hpg * tq), BF16)], axis=0)

    slc_near = {(t, g): [_dot(kslaug_s[g, tile_rows(back(t, k)), :], qaugs[t, g]) for k in range(3)]
                for t, g in items}
    state = {}
    for t, g in items:
        qi = qis[t]
        odd = (qi >= 2) & (qi % 2 == 0)
        state[t, g] = _softmax_joint([
            (jnp.where(causal, slc_near[t, g][0], NEG) + t0[g], vslt_s[g, qi]),
            (jnp.where(qi >= 1, slc_near[t, g][1], NEG) + t1[g], vslt_s[g, back(t, 1)]),
            (jnp.where(odd, slc_near[t, g][2], NEG), vslt_s[g, back(t, 2)])])

    slabs = [jnp.maximum(qi - 1, 0) // 2 for qi in qis]
    lo = 0
    for first in range(NSA_TPS):
        live = [(t, g) for t, g in items if t >= first]

        def slc_far(i, carry, live=live):
            slab = pl.ds(pl.multiple_of(i * (2 * tq), 2 * tq), 2 * tq)
            scores = [_dot(kslaug_s[g, slab, :], qaugs[t, g]) for t, g in live]
            return tuple(_softmax_update(c, sc, [vslt_s[g, 2 * i], vslt_s[g, 2 * i + 1]])
                         for c, sc, (t, g) in zip(carry, scores, live))

        new = lax.fori_loop(lo, slabs[first], slc_far, tuple(state[it] for it in live))
        state.update(dict(zip(live, new)))
        lo = slabs[first]

    for t in range(NSA_TPS):
        gates = jax.nn.sigmoid(gt_ref[t * tq:(t + 1) * tq, :]).T
        outs = []
        for g in groups:
            o_slc = _softmax_finish(state[t, g])
            for r in range(hpg):
                h = g * hpg + r
                cols = slice(r * tq, (r + 1) * tq)
                outs.append(gates[h:h + 1, :] * o_cmps[t, g][:, cols]
                            + gates[NSA_HEADS + h:NSA_HEADS + h + 1, :] * o_slc[:, cols]
                            + gates[2 * NSA_HEADS + h:2 * NSA_HEADS + h + 1, :] * o_wins[t, g][:, cols])
        o_ref[t * tq:(t + 1) * tq, :] = jnp.concatenate(outs, axis=0).T.astype(BF16)


def _nsa(proj, kcn, vct, wc, t0, t1, batch, seq):
    tq = NSA_TQ
    nq = seq // tq
    rows = NSA_TPS * tq
    steps = nq // NSA_TPS
    ncmp = seq // NSA_CMP_STRIDE
    width = NSA_HEADS * HEAD_DIM
    groups = NSA_KV_GROUPS
    kv = lambda col: pl.BlockSpec((seq, LANES), lambda b, i: (b, col // LANES))
    const = lambda shape: pl.BlockSpec(shape, lambda b, i: (0,) * len(shape))
    return pl.pallas_call(
        functools.partial(_nsa_kernel, seq=seq),
        grid=(batch, steps),
        in_specs=[pl.BlockSpec((rows, width), lambda b, i: (b * steps + i, C_NSQ // width)),
                  pl.BlockSpec((1, groups, ncmp, HEAD_DIM), lambda b, i: (b, 0, 0, 0)),
                  pl.BlockSpec((1, groups, HEAD_DIM, ncmp), lambda b, i: (b, 0, 0, 0)),
                  kv(C_KSL), kv(C_VSL), kv(C_KWN), kv(C_VWN),
                  pl.BlockSpec((rows, LANES), lambda b, i: (b * steps + i, C_NG // LANES)),
                  const((NSA_HEADS, CMP_WIN, tq)), const((NSA_HEADS, tq, tq)),
                  const((NSA_HEADS, tq, tq))],
        out_specs=pl.BlockSpec((rows, width), lambda b, i: (b * steps + i, 0)),
        out_shape=jax.ShapeDtypeStruct((batch * seq, width), BF16),
        scratch_shapes=[pltpu.VMEM((groups, seq, LANES), BF16),
                        pltpu.VMEM((groups, nq, VROWS, tq), BF16),
                        pltpu.VMEM((groups, seq, HEAD_DIM), BF16),
                        pltpu.VMEM((groups, nq, VROWS, tq), BF16),
                        pltpu.VMEM((NSA_TPS * groups, CMP_PAD + ncmp, NSA_HPG * tq), F32)],
        compiler_params=_params(("parallel", "arbitrary")),
        name="nsa",
    )(proj, kcn, vct, proj, proj, proj, proj, proj, wc, t0, t1)


def _merge_kernel(x_ref, ya_ref, yb_ref, ga_ref, gb_ref, wa_ref, wb_ref, wo_ref, o_ref):
    a = _dot(ya_ref[...], wa_ref[...])
    b = _dot(yb_ref[...], wb_ref[...])
    z = jax.nn.sigmoid(ga_ref[...]) * a + jax.nn.sigmoid(gb_ref[...]) * b
    o_ref[...] = x_ref[...] + _dot(z.astype(BF16), wo_ref[...])


def _merge(xf, ya, yb, proj, wa, wb, wo, layer):
    n, d = xf.shape
    tm = MERGE_TM
    row = lambda w, col=0: pl.BlockSpec((tm, w), lambda i: (i, col))
    return pl.pallas_call(
        _merge_kernel,
        grid=(n // tm,),
        in_specs=[row(d), row(ya.shape[1]), row(yb.shape[1]), row(d, C_GA // d), row(d, C_GB // d),
                  _layer_spec(wa, layer), _layer_spec(wb, layer), _layer_spec(wo, layer)],
        out_specs=row(d),
        out_shape=jax.ShapeDtypeStruct((n, d), F32),
        compiler_params=_params(("parallel",)),
        name="merge",
    )(xf, ya, yb, proj, proj, wa, wb, wo)


FFN_HALO = 16
FFN_TM = 1024
FFN_TF = 256


def _ffn_kernel(x_ref, xh_ref, g_ref, wu_ref, cw_ref, cb_ref, wd_ref, p_ref, wg_ref, wp_ref,
                o_ref, act_s, *, seq, d_ff):
    i = pl.program_id(0)
    tm = x_ref.shape[0]
    x = x_ref[...]
    at_start = (i * tm) % seq == 0
    halo = jnp.where(at_start, 0.0, _rms(xh_ref[...], g_ref[...]))
    hn = jnp.concatenate([halo.astype(BF16), _rms(x, g_ref[...]).astype(BF16)], axis=0)

    def conv(cols):
        u = _dot(hn, wu_ref[:, cols])
        u1 = pltpu.roll(u, 1, 0)[FFN_HALO:]
        u2 = pltpu.roll(u, 2, 0)[FFN_HALO:]
        cw = cw_ref[:, cols]
        return cw[0:1] * u2 + cw[1:2] * u1 + cw[2:3] * u[FFN_HALO:] + cb_ref[:, cols]

    for c in range(d_ff // FFN_TF):
        lo = c * FFN_TF
        act = jax.nn.gelu(conv(slice(lo, lo + FFN_TF))) * conv(slice(d_ff + lo, d_ff + lo + FFN_TF))
        act_s[:, lo:lo + FFN_TF] = act.astype(BF16)

    x = x + _dot(act_s[...], wd_ref[...])
    gate = jax.nn.sigmoid(_dot(x.astype(BF16), wg_ref[...]))
    o_ref[...] = x + gate * _dot(p_ref[...].astype(BF16), wp_ref[...])


def _ffn_ple(xf, gain, w_up, conv_w, conv_b, w_down, pf, wg, wp, layer, seq):
    n, d = xf.shape
    d_ff = w_down.shape[1]
    tm = FFN_TM
    hb = tm // FFN_HALO
    resident = lambda a: _layer_spec(a, layer)
    return pl.pallas_call(
        functools.partial(_ffn_kernel, seq=seq, d_ff=d_ff),
        grid=(n // tm,),
        in_specs=[pl.BlockSpec((tm, d), lambda i: (i, 0)),
                  pl.BlockSpec((FFN_HALO, d), lambda i: (jnp.maximum(i * hb - 1, 0), 0)),
                  resident(gain), resident(w_up), resident(conv_w), resident(conv_b), resident(w_down),
                  pl.BlockSpec((None, tm, pf.shape[2]), lambda i: (layer, i, 0)),
                  resident(wg), resident(wp)],
        out_specs=pl.BlockSpec((tm, d), lambda i: (i, 0)),
        out_shape=jax.ShapeDtypeStruct((n, d), F32),
        scratch_shapes=[pltpu.VMEM((tm, d_ff), BF16)],
        compiler_params=_params(("parallel",)),
        name="conv_ffn_ple",
    )(xf, xf, gain, w_up, conv_w, conv_b, w_down, pf, wg, wp)


def _reorder_kernel(w_ref, o_ref):
    attn = 3 * MOBA_HEADS * HEAD_DIM + NSA_HEADS * HEAD_DIM + 6 * NSA_KV_GROUPS * HEAD_DIM
    ng = 3 * NSA_HEADS
    w = w_ref[...]
    pad = jnp.zeros((w.shape[0], PROJ_COLS - w.shape[1]), w.dtype)
    o_ref[...] = jnp.concatenate([w[:, attn + ng:], w[:, :attn], w[:, attn:attn + ng], pad],
                                 axis=1).astype(BF16)


def _reorder_in_proj(w):
    depth, d, cols = w.shape
    tr = 256
    return pl.pallas_call(
        _reorder_kernel,
        grid=(depth, d // tr),
        in_specs=[pl.BlockSpec((None, tr, cols), lambda l, i: (l, i, 0))],
        out_specs=pl.BlockSpec((None, tr, PROJ_COLS), lambda l, i: (l, i, 0)),
        out_shape=jax.ShapeDtypeStruct((depth, d, PROJ_COLS), BF16),
        compiler_params=_params(("parallel", "parallel")),
        name="reorder_in_proj",
    )(w)


def _head_gain_rows(moba_q, moba_k, nsa_q, nsa_k):
    rows = jnp.zeros((moba_q.shape[0], 1, PROJ_COLS), F32)
    for col, gain, heads in ((C_MQ, moba_q * QSCALE, MOBA_HEADS), (C_MK, moba_k, MOBA_HEADS),
                             (C_NSQ, nsa_q * QSCALE, NSA_HEADS), (C_KSL, nsa_k[:, 1], NSA_KV_GROUPS),
                             (C_KWN, nsa_k[:, 2], NSA_KV_GROUPS)):
        rows = rows.at[:, 0, col:col + heads * HEAD_DIM].set(jnp.tile(gain, (1, heads)))
    return rows


def _cmp_weights(w1, pos):
    depth = w1.shape[0]
    half = NSA_CMP_STRIDE * HEAD_DIM
    w1cat = jnp.concatenate([w1[:, :half], w1[:, half:]], axis=2).astype(BF16)
    tok = w1cat.reshape(depth, NSA_CMP_STRIDE, HEAD_DIM, -1)
    zero = jnp.zeros_like(tok)
    placed = jnp.stack([jnp.concatenate([tok, zero], axis=2), jnp.concatenate([zero, tok], axis=2)], axis=1)
    placed = placed.reshape(depth, NSA_KV_GROUPS, NSA_CMP_STRIDE * LANES, -1)
    posr = jnp.zeros((depth, 8, half), F32).at[:, 0:2].set(pos.reshape(depth, 2, half))
    return w1cat, placed, posr


def kernel(x, p, rel_bias, attn_norm, w_in, moba_q_gain, moba_k_gain, nsa_q_gain, nsa_k_gain,
           cmp_pos_k, cmp_w1_k, cmp_w2_k, cmp_pos_v, cmp_w1_v, cmp_w2_v,
           w_br_moba, w_br_nsa, w_o, ffn_norm, w_up, conv_w, conv_b, w_down, w_ple_gate, w_ple):
    batch, seq, d = x.shape
    n = batch * seq
    depth = w_in.shape[0]
    bf = lambda a: a.astype(BF16)
    w_in_r = _reorder_in_proj(w_in)
    head_gain = _head_gain_rows(moba_q_gain, moba_k_gain, nsa_q_gain, nsa_k_gain)
    attn_gain, ffn_gain = attn_norm[:, None, :], ffn_norm[:, None, :]
    w1k, wtk, pk = _cmp_weights(cmp_w1_k, cmp_pos_k)
    w1v, wtv, pv = _cmp_weights(cmp_w1_v, cmp_pos_v)
    w2k, w2v, kg0 = bf(cmp_w2_k), bf(cmp_w2_v), nsa_k_gain[:, 0:1]
    wa, wb, wo = bf(w_br_moba), bf(w_br_nsa), bf(w_o)
    wu, wd, wg, wp = bf(w_up), bf(w_down), bf(w_ple_gate), bf(w_ple)
    conv_b = conv_b[:, None, :]
    pf = p.reshape(depth, n, -1)

    t0m, t1m, t0n, t1n, wc = _bias_tables(rel_bias)
    xf = x.reshape(n, d)
    for i in range(depth):
        proj = _inproj(xf, attn_gain, w_in_r, head_gain, i)
        ya = _moba(proj, t0m, t1m, batch, seq)
        kcn, vct = _compress(proj, w1k, w1v, wtk, wtv, w2k, w2v, pk, pv, kg0, i, batch, seq)
        yb = _nsa(proj, kcn, vct, wc, t0n, t1n, batch, seq)
        xf = _merge(xf, ya, yb, proj, wa, wb, wo, i)
        xf = _ffn_ple(xf, ffn_gain, wu, conv_w, conv_b, wd, pf, wg, wp, i, seq)
    return xf.reshape(batch, seq, d)
```

```python
import functools
import math

import numpy as np
import jax
import jax.numpy as jnp
from jax import lax
from jax.experimental import pallas as pl
from jax.experimental.pallas import tpu as pltpu

F32 = jnp.float32
BF16 = jnp.bfloat16

HEAD_DIM = 64
MOBA_HEADS = 8
MOBA_BLOCK = 256
MOBA_TOPK = 3
NSA_HEADS = 8
NSA_KV_GROUPS = 2
NSA_HPG = NSA_HEADS // NSA_KV_GROUPS
NSA_CMP_LEN = 32
NSA_CMP_STRIDE = 16
NSA_CMP_HIDDEN = 2 * HEAD_DIM
NSA_SEL_BLOCK = 64
NSA_SEL_TOPN = 16
NSA_WINDOW = 512
REL_BUCKETS = 32
REL_MAX_DIST = 128
RMS_EPS = 1e-6
SCALE = HEAD_DIM ** -0.5
LOG2E = math.log2(math.e)
QSCALE = SCALE * LOG2E
NEG = -1e30

LANES = 128
V7X_VMEM_BYTES = 64 * 1024 * 1024
VMEM_LIMIT = V7X_VMEM_BYTES * 7 // 8

INPROJ_TM = 512
MERGE_TM = 512

C_GA, C_GB, C_MQ, C_MK, C_MV, C_NSQ = 0, 1024, 2048, 2560, 3072, 3584
C_KC, C_VC, C_KSL, C_VSL, C_KWN, C_VWN, C_NG = 4096, 4224, 4352, 4480, 4608, 4736, 4864
PROJ_COLS = 4992

NSA_TQ = 128
NSA_TPS = 4
CMP_WIN = 16
CMP_PAD = 8


def _dot(a, b):
    return jnp.dot(a, b, preferred_element_type=F32)


def _dot_nt(a, b):
    return lax.dot_general(a, b, (((1,), (1,)), ((), ())), preferred_element_type=F32)


def _rms(x, gain):
    return x * lax.rsqrt(jnp.mean(x * x, axis=-1, keepdims=True) + RMS_EPS) * gain


def _split_bf16(x):
    hi = x.astype(BF16)
    return hi, (x - hi.astype(F32)).astype(BF16)


def _eye(rows, cols):
    r = lax.broadcasted_iota(jnp.int32, (rows, cols), 0)
    c = lax.broadcasted_iota(jnp.int32, (rows, cols), 1)
    return (r == c).astype(BF16)


def _transpose_bf16(x, rows):
    return _dot_nt(_eye(rows, x.shape[1]), x)


VROWS = 80


def _value_tile(v):
    row = lax.broadcasted_iota(jnp.int32, (VROWS, v.shape[0]), 0)
    return jnp.where(row == HEAD_DIM, 1.0, _transpose_bf16(v, VROWS)).astype(BF16)


def _softmax_joint(tiles):
    m = functools.reduce(jnp.maximum, [jnp.max(s, axis=0, keepdims=True) for s, _ in tiles])
    acc = functools.reduce(jnp.add, [_dot(vt, jnp.exp2(s - m).astype(BF16)) for s, vt in tiles])
    return m, acc


def _softmax_update(carry, s, vts):
    m, acc = carry
    m_new = jnp.maximum(m, jnp.max(s, axis=0, keepdims=True))
    p = jnp.exp2(s - m_new).astype(BF16)
    rows = s.shape[0] // len(vts)
    pv = functools.reduce(jnp.add, [_dot(vt, p[i * rows:(i + 1) * rows]) for i, vt in enumerate(vts)])
    return m_new, jnp.exp2(m - m_new) * acc + pv


def _softmax_finish(carry):
    _, acc = carry
    return acc[0:HEAD_DIM] * (1.0 / acc[HEAD_DIM:HEAD_DIM + 1])


def _rank_select(score, idx, count):
    beaten = jnp.zeros(score.shape, jnp.int32)
    for i in range(score.shape[0]):
        si = score[i:i + 1, :]
        beaten += ((si > score) | ((si == score) & (i < idx))).astype(jnp.int32)
    return beaten < count


def _params(sem):
    return pltpu.CompilerParams(dimension_semantics=sem, vmem_limit_bytes=VMEM_LIMIT)


def _layer_spec(a, layer):
    return pl.BlockSpec((None,) + a.shape[1:], lambda *_: (layer,) + (0,) * (a.ndim - 1),
                        pipeline_mode=pl.Buffered(1))


def _rel_bucket_np(dist):
    n = np.maximum(dist, 0)
    max_exact = REL_BUCKETS // 2
    nf = np.maximum(n, 1).astype(np.float32)
    large = max_exact + (np.log(nf / np.float32(max_exact)) / np.float32(math.log(REL_MAX_DIST / max_exact))
                         * np.float32(REL_BUCKETS - max_exact)).astype(np.int32)
    return np.where(n < max_exact, n, np.minimum(large, REL_BUCKETS - 1))


def _bucket_starts():
    buckets = _rel_bucket_np(np.arange(4 * REL_MAX_DIST))
    return [int(np.argmax(buckets >= k)) for k in range(REL_BUCKETS)]


BUCKET_START = _bucket_starts()
BIAS_REACH = BUCKET_START[-1]
assert BIAS_REACH <= MOBA_BLOCK and BIAS_REACH <= NSA_TQ - NSA_CMP_LEN + 1 + NSA_CMP_STRIDE


def _tables_kernel(tab_ref, t0m_ref, t1m_ref, t0n_ref, t1n_ref, wc_ref):
    h = pl.program_id(0)

    def bias(dist, head):
        last = tab_ref[head, REL_BUCKETS - 1]
        val = jnp.zeros(dist.shape, F32)
        for k in range(REL_BUCKETS - 2, -1, -1):
            val = jnp.where(dist < BUCKET_START[k + 1], (tab_ref[head, k] - last) * LOG2E, val)
        return val

    def toeplitz(size, offset, head):
        key = lax.broadcasted_iota(jnp.int32, (size, size), 0)
        qry = lax.broadcasted_iota(jnp.int32, (size, size), 1)
        return bias(offset + qry - key, head)

    t0m_ref[0] = toeplitz(MOBA_BLOCK, 0, h)
    t1m_ref[0] = toeplitz(MOBA_BLOCK, MOBA_BLOCK, h)
    t0n_ref[0] = toeplitz(NSA_TQ, 0, MOBA_HEADS + h)
    t1n_ref[0] = toeplitz(NSA_TQ, NSA_TQ, MOBA_HEADS + h)
    a = lax.broadcasted_iota(jnp.int32, (CMP_WIN, NSA_TQ), 0)
    i = lax.broadcasted_iota(jnp.int32, (CMP_WIN, NSA_TQ), 1)
    wc_ref[0] = bias(i + (NSA_TQ - NSA_CMP_LEN + 1) - NSA_CMP_STRIDE * a, MOBA_HEADS + h)


def _bias_tables(rel_bias):
    blk, tq = MOBA_BLOCK, NSA_TQ
    shapes = [(blk, blk), (blk, blk), (tq, tq), (tq, tq), (CMP_WIN, tq)]
    return pl.pallas_call(
        _tables_kernel,
        grid=(MOBA_HEADS,),
        in_specs=[pl.BlockSpec(memory_space=pltpu.SMEM)],
        out_specs=[pl.BlockSpec((1,) + s, lambda h: (h, 0, 0)) for s in shapes],
        out_shape=[jax.ShapeDtypeStruct((MOBA_HEADS,) + s, F32) for s in shapes],
        compiler_params=_params(("arbitrary",)),
        name="bias_tables",
    )(rel_bias)


NORM_SLABS = (list(range(C_MQ, C_MV, LANES)) + list(range(C_NSQ, C_KC, LANES)) + [C_KSL, C_KWN])


def _inproj_kernel(x_ref, g_ref, w_ref, hg_ref, o_ref):
    h = _rms(x_ref[...], g_ref[...]).astype(BF16)
    y = _dot(h, w_ref[...])
    first = lax.broadcasted_iota(jnp.int32, (1, LANES), 1) < HEAD_DIM
    edges = sorted(set([0, PROJ_COLS] + NORM_SLABS + [c + LANES for c in NORM_SLABS]))
    for lo, hi in zip(edges[:-1], edges[1:]):
        t = y[:, lo:hi]
        if lo in NORM_SLABS:
            sq = t * t
            s0 = jnp.sum(jnp.where(first, sq, 0.0), axis=-1, keepdims=True)
            s1 = jnp.sum(jnp.where(first, 0.0, sq), axis=-1, keepdims=True)
            ms = jnp.where(first, s0, s1) * (1.0 / HEAD_DIM)
            t = t * lax.rsqrt(ms + RMS_EPS) * hg_ref[:, lo:hi]
        o_ref[:, lo:hi] = t


def _inproj(xf, gain, w, head_gain, layer):
    n, d = xf.shape
    tm = INPROJ_TM
    return pl.pallas_call(
        _inproj_kernel,
        grid=(n // tm,),
        in_specs=[pl.BlockSpec((tm, d), lambda i: (i, 0)), _layer_spec(gain, layer),
                  _layer_spec(w, layer), _layer_spec(head_gain, layer)],
        out_specs=pl.BlockSpec((tm, PROJ_COLS), lambda i: (i, 0)),
        out_shape=jax.ShapeDtypeStruct((n, PROJ_COLS), F32),
        compiler_params=_params(("parallel",)),
        name="inproj",
    )(xf, gain, w, head_gain)


MOBA_HPS = 8
MOBA_BPS = 2


def _moba_kernel(q_ref, k_ref, v_ref, t0_ref, t1_ref, o_ref, kaug_s, vt_s, km_s, *, seq):
    step = pl.program_id(2)
    blk = MOBA_BLOCK
    nb = seq // blk
    head_cols = lambda hh: slice(hh * HEAD_DIM, (hh + 1) * HEAD_DIM)
    heads = range(MOBA_HPS)
    items = [(t, hh) for t in range(MOBA_BPS) for hh in heads]
    ns = [step * MOBA_BPS + t for t in range(MOBA_BPS)]

    @pl.when(step == 0)
    def _prepare_keys():
        r = lax.broadcasted_iota(jnp.int32, (seq, LANES), 0)
        c = lax.broadcasted_iota(jnp.int32, (seq, LANES), 1)
        onehot = (c - HEAD_DIM == r // blk).astype(F32)
        place = _eye(HEAD_DIM, LANES)
        for hh in heads:
            kn = k_ref[:, head_cols(hh)]
            km = kn.reshape(nb, blk, HEAD_DIM).sum(axis=1) * (1.0 / blk)
            km_s[hh] = jnp.concatenate([km, jnp.zeros((16 - nb, HEAD_DIM), F32)], axis=0)
            kaug_s[hh] = (_dot(kn.astype(BF16), place) + onehot).astype(BF16)
            for j in range(nb):
                vj = v_ref[j * blk:(j + 1) * blk, head_cols(hh)].astype(BF16)
                vt_s[hh, j] = _value_tile(vj)

    jidx = lax.broadcasted_iota(jnp.int32, (16, blk), 0)
    key = lax.broadcasted_iota(jnp.int32, (blk, blk), 0)
    qry = lax.broadcasted_iota(jnp.int32, (blk, blk), 1)
    qts = {}
    for t in range(MOBA_BPS):
        for pair in range(MOBA_HPS // 2):
            qp = _transpose_bf16(q_ref[t * blk:(t + 1) * blk, pair * LANES:(pair + 1) * LANES].astype(BF16),
                                 LANES)
            qts[t, 2 * pair], qts[t, 2 * pair + 1] = qp[0:HEAD_DIM].astype(BF16), qp[HEAD_DIM:].astype(BF16)
    gates = {}
    for t, hh in items:
        kmh, kml = _split_bf16(km_s[hh])
        gates[t, hh] = _dot(kmh, qts[t, hh]) + _dot(kml, qts[t, hh])
    qaugs = {}
    for t, hh in items:
        past = jidx < ns[t]
        gate = jnp.where(past, gates[t, hh], NEG)
        sel = (past & _rank_select(gate, jidx, MOBA_TOPK)) | (jidx == ns[t])
        selb = jnp.where(sel, 0.0, NEG).astype(BF16)
        qaugs[t, hh] = jnp.concatenate([qts[t, hh], selb, jnp.zeros((LANES - HEAD_DIM - 16, blk), BF16)],
                                       axis=0)

    def scores(t, hh, j):
        start = pl.multiple_of(j * blk, blk)
        return _dot(kaug_s[hh, pl.ds(start, blk), :], qaugs[t, hh]), vt_s[hh, j]

    near = {(t, hh): (scores(t, hh, ns[t]), scores(t, hh, jnp.maximum(ns[t] - 1, 0))) for t, hh in items}
    state = {}
    for t, hh in items:
        (s0, vt0), (s1, vt1) = near[t, hh]
        s0 = jnp.where(key <= qry, s0, NEG) + t0_ref[hh]
        s1 = jnp.where(ns[t] >= 1, s1, NEG) + t1_ref[hh]
        state[t, hh] = _softmax_joint([(s0, vt0), (s1, vt1)])

    older = [jnp.maximum(n - 1, 0) for n in ns]
    lo = 0
    for first in range(MOBA_BPS):
        live = [(t, hh) for t, hh in items if t >= first]

        def far_body(j, carry, live=live):
            tiles = [scores(t, hh, j) for t, hh in live]
            return tuple(_softmax_update(c, s, [vt]) for c, (s, vt) in zip(carry, tiles))

        new = lax.fori_loop(lo, older[first], far_body, tuple(state[it] for it in live))
        state.update(dict(zip(live, new)))
        lo = older[first]

    for t in range(MOBA_BPS):
        o_ref[t * blk:(t + 1) * blk, :] = jnp.concatenate(
            [_softmax_finish(state[t, hh]) for hh in heads], axis=0).T.astype(BF16)


def _moba(proj, t0, t1, batch, seq):
    blk = MOBA_BLOCK
    nb = seq // blk
    width = MOBA_HPS * HEAD_DIM
    rows = MOBA_BPS * blk
    steps = nb // MOBA_BPS
    qc, kc, vc = C_MQ // width, C_MK // width, C_MV // width
    return pl.pallas_call(
        functools.partial(_moba_kernel, seq=seq),
        grid=(batch, MOBA_HEADS // MOBA_HPS, steps),
        in_specs=[pl.BlockSpec((rows, width), lambda b, h, n: (b * steps + n, qc + h)),
                  pl.BlockSpec((seq, width), lambda b, h, n: (b, kc + h)),
                  pl.BlockSpec((seq, width), lambda b, h, n: (b, vc + h)),
                  pl.BlockSpec((MOBA_HPS, blk, blk), lambda b, h, n: (h, 0, 0)),
                  pl.BlockSpec((MOBA_HPS, blk, blk), lambda b, h, n: (h, 0, 0))],
        out_specs=pl.BlockSpec((rows, width), lambda b, h, n: (b * steps + n, h)),
        out_shape=jax.ShapeDtypeStruct((batch * seq, MOBA_HEADS * HEAD_DIM), BF16),
        scratch_shapes=[pltpu.VMEM((MOBA_HPS, seq, LANES), BF16),
                        pltpu.VMEM((MOBA_HPS, nb, VROWS, blk), BF16),
                        pltpu.VMEM((MOBA_HPS, 16, HEAD_DIM), F32)],
        compiler_params=_params(("parallel", "parallel", "arbitrary")),
        name="moba",
    )(proj, proj, proj, t0, t1)


def _compress_kernel(k_ref, v_ref, w1k_ref, w1v_ref, wtk_ref, wtv_ref, w2k_ref, w2v_ref, pk_ref, pv_ref,
                     kg_ref, ko_ref, vo_ref):
    hid = NSA_CMP_HIDDEN
    chunks = k_ref.shape[0] // NSA_CMP_STRIDE

    def chunk_rows(t_ref):
        return jnp.concatenate([t_ref[pl.ds(l, chunks, stride=NSA_CMP_STRIDE), :].astype(BF16)
                                for l in range(NSA_CMP_STRIDE)], axis=1)

    def compress(tok, w1_ref, wt_ref, w2_ref, pos_ref, g):
        a = _dot(tok, wt_ref[g])
        pw = _dot(pos_ref[...].astype(BF16), w1_ref[...])
        pos = pw[0:1, :hid] + pw[1:2, hid:]
        nxt = pltpu.roll(a[:, hid:], chunks - 1, 0)
        h = jax.nn.gelu(a[:, :hid] + nxt + pos)
        return _dot(h.astype(BF16), w2_ref[...])

    ktok, vtok = chunk_rows(k_ref), chunk_rows(v_ref)
    for g in range(NSA_KV_GROUPS):
        kc = compress(ktok, w1k_ref, wtk_ref, w2k_ref, pk_ref, g)
        ko_ref[0, g] = _rms(kc, kg_ref[...]).astype(BF16)
        vc = compress(vtok, w1v_ref, wtv_ref, w2v_ref, pv_ref, g).astype(BF16)
        vo_ref[0, g] = _transpose_bf16(vc, HEAD_DIM).astype(BF16)


def _compress(proj, w1k, w1v, wtk, wtv, w2k, w2v, pk, pv, kg, layer, batch, seq):
    groups = NSA_KV_GROUPS
    chunks = seq // NSA_CMP_STRIDE
    tok = lambda col: pl.BlockSpec((seq, LANES), lambda b: (b, col // LANES))
    return pl.pallas_call(
        _compress_kernel,
        grid=(batch,),
        in_specs=[tok(C_KC), tok(C_VC)] + [_layer_spec(a, layer)
                                           for a in (w1k, w1v, wtk, wtv, w2k, w2v, pk, pv, kg)],
        out_specs=[pl.BlockSpec((1, groups, chunks, HEAD_DIM), lambda b: (b, 0, 0, 0)),
                   pl.BlockSpec((1, groups, HEAD_DIM, chunks), lambda b: (b, 0, 0, 0))],
        out_shape=[jax.ShapeDtypeStruct((batch, groups, chunks, HEAD_DIM), BF16),
                   jax.ShapeDtypeStruct((batch, groups, HEAD_DIM, chunks), BF16)],
        compiler_params=_params(("parallel",)),
        name="nsa_compress",
    )(proj, proj, w1k, w1v, wtk, wtv, w2k, w2v, pk, pv, kg)


def _nsa_kernel(q_ref, kc_ref, vc_ref, ksl_ref, vsl_ref, kwn_ref, vwn_ref, gt_ref,
                wc_ref, t0_ref, t1_ref, o_ref, kslaug_s, vslt_s, kwn_s, vwnt_s, sc_s, *, seq):
    step = pl.program_id(1)
    tq = NSA_TQ
    hpg = NSA_HPG
    sb = NSA_SEL_BLOCK
    nsel = seq // sb
    ncmp = seq // NSA_CMP_STRIDE
    nwin = NSA_WINDOW // tq
    groups = range(NSA_KV_GROUPS)
    items = [(t, g) for t in range(NSA_TPS) for g in groups]
    qis = [step * NSA_TPS + t for t in range(NSA_TPS)]

    @pl.when(step == 0)
    def _prepare_keys():
        r = lax.broadcasted_iota(jnp.int32, (seq, LANES), 0)
        c = lax.broadcasted_iota(jnp.int32, (seq, LANES), 1)
        onehot = (c - HEAD_DIM == r // sb).astype(F32)
        place = _eye(HEAD_DIM, LANES)
        sc_s[:, 0:CMP_PAD, :] = jnp.zeros((NSA_TPS * NSA_KV_GROUPS, CMP_PAD, hpg * tq), F32)
        for g in groups:
            sl = slice(g * HEAD_DIM, (g + 1) * HEAD_DIM)
            kslaug_s[g] = (_dot(ksl_ref[:, sl].astype(BF16), place) + onehot).astype(BF16)
            kwn_s[g] = kwn_ref[:, sl].astype(BF16)
            for j in range(seq // tq):
                rows = slice(j * tq, (j + 1) * tq)
                vslt_s[g, j] = _value_tile(vsl_ref[rows, sl].astype(BF16))
                vwnt_s[g, j] = _value_tile(vwn_ref[rows, sl].astype(BF16))

    lanes = lambda parts: jnp.concatenate(parts, axis=1)
    key = lax.broadcasted_iota(jnp.int32, (tq, tq), 0)
    qry = lax.broadcasted_iota(jnp.int32, (tq, tq), 1)
    causal = lanes([key <= qry] * hpg)
    upper = lanes([key > qry] * hpg)
    oj = lax.broadcasted_iota(jnp.int32, (nsel, ncmp), 0) * sb
    on = lax.broadcasted_iota(jnp.int32, (nsel, ncmp), 1) * NSA_CMP_STRIDE
    overlap = ((on < oj + sb) & (on + NSA_CMP_LEN > oj) & (on < seq - NSA_CMP_STRIDE)).astype(BF16)
    jidx = lax.broadcasted_iota(jnp.int32, (nsel, tq), 0)
    cend = lax.broadcasted_iota(jnp.int32, (ncmp, tq), 0) * NSA_CMP_STRIDE + (NSA_CMP_LEN - 1)
    cvis = [lanes([cend <= lax.broadcasted_iota(jnp.int32, (ncmp, tq), 1) + qi * tq] * hpg) for qi in qis]
    own = [(lax.broadcasted_iota(jnp.int32, (nsel, tq), 1) + qi * tq) // sb for qi in qis]
    tile_rows = lambda j: pl.ds(pl.multiple_of(j * tq, tq), tq)
    back = lambda t, k: jnp.maximum(qis[t] - k, 0)
    heads = [[g * hpg + r for r in range(hpg)] for g in groups]
    t0 = [lanes([t0_ref[h] for h in heads[g]]) for g in groups]
    t1 = [lanes([t1_ref[h] for h in heads[g]]) for g in groups]
    slot = lambda t, g: t * NSA_KV_GROUPS + g

    qplain, win_scores = {}, {}
    for t, g in items:
        width = hpg * HEAD_DIM
        qg = _transpose_bf16(q_ref[t * tq:(t + 1) * tq, g * width:(g + 1) * width].astype(BF16), width)
        qplain[t, g] = lanes([qg[r * HEAD_DIM:(r + 1) * HEAD_DIM] for r in range(hpg)]).astype(BF16)
        sc_s[slot(t, g), CMP_PAD:, :] = _dot(kc_ref[0, g], qplain[t, g])
        win_scores[t, g] = [_dot(kwn_s[g, tile_rows(back(t, k)), :], qplain[t, g]) for k in range(nwin + 1)]

    o_cmps, imps = {}, {}
    for t, g in items:
        win = pl.ds(pl.multiple_of(qis[t] * (tq // NSA_CMP_STRIDE), 8), CMP_WIN)
        sc_s[slot(t, g), win, :] += lanes([wc_ref[h] for h in heads[g]])
        s = jnp.where(cvis[t], sc_s[slot(t, g), CMP_PAD:, :], NEG)
        m = jnp.max(s, axis=0, keepdims=True)
        e = jnp.where(cvis[t], jnp.exp2(s - m), 0.0)
        den = jnp.sum(e, axis=0, keepdims=True)
        p = e * (1.0 / jnp.where(den > 0, den, 1.0))
        o_cmps[t, g] = _dot(vc_ref[0, g], p.astype(BF16))
        psum = p[:, 0:tq]
        for r in range(1, hpg):
            psum = psum + p[:, r * tq:(r + 1) * tq]
        ph, plo = _split_bf16(psum)
        imps[t, g] = _dot(overlap, ph) + _dot(overlap, plo)

    o_wins = {}
    for t, g in items:
        qi = qis[t]
        tiles = []
        for k, s in enumerate(win_scores[t, g]):
            if k == 0:
                s = jnp.where(causal, s, NEG) + t0[g]
            elif k == 1:
                s = jnp.where(qi >= 1, s, NEG) + t1[g]
            elif k < nwin:
                s = jnp.where(qi >= k, s, NEG)
            else:
                s = jnp.where(upper & (qi >= k), s, NEG)
            tiles.append((s, vwnt_s[g, back(t, k)]))
        o_wins[t, g] = _softmax_finish(_softmax_joint(tiles))

    qaugs = {}
    for t, g in items:
        imp = jnp.where((jidx == 0) | (jidx == own[t]) | (jidx == own[t] - 1), -NEG, imps[t, g])
        imp = jnp.where(jidx > own[t], NEG, imp)
        sel = _rank_select(imp, jidx, NSA_SEL_TOPN) & (jidx <= own[t])
        selb = jnp.where(sel, 0.0, NEG).astype(BF16)
        qaugs[t, g] = jnp.concatenate([qplain[t, g], lanes([selb] * hpg),
                                       jnp.zeros((LANES - HEAD_DIM - nsel, hpg * tq), BF16)], axis=0)

    assert NSA_TPS % 2 == 0
    near_tiles = lambda t: 3 if t % 2 == 0 else 2
    slc_near = {(t, g): [_dot(kslaug_s[g, tile_rows(back(t, k)), :], qaugs[t, g])
                         for k in range(near_tiles(t))] for t, g in items}
    state = {}
    for t, g in items:
        qi = qis[t]
        tiles = [(jnp.where(causal, slc_near[t, g][0], NEG) + t0[g], vslt_s[g, qi]),
                 (jnp.where(qi >= 1, slc_near[t, g][1], NEG) + t1[g], vslt_s[g, back(t, 1)])]
        if near_tiles(t) == 3:
            tiles.append((jnp.where(qi >= 2, slc_near[t, g][2], NEG), vslt_s[g, back(t, 2)]))
        state[t, g] = _softmax_joint(tiles)

    slabs = [jnp.maximum(qi - 1, 0) // 2 for qi in qis]
    lo = 0
    for first in range(NSA_TPS):
        live = [(t, g) for t, g in items if t >= first]

        def slc_far(i, carry, live=live):
            slab = pl.ds(pl.multiple_of(i * (2 * tq), 2 * tq), 2 * tq)
            scores = [_dot(kslaug_s[g, slab, :], qaugs[t, g]) for t, g in live]
            return tuple(_softmax_update(c, sc, [vslt_s[g, 2 * i], vslt_s[g, 2 * i + 1]])
                         for c, sc, (t, g) in zip(carry, scores, live))

        new = lax.fori_loop(lo, slabs[first], slc_far, tuple(state[it] for it in live))
        state.update(dict(zip(live, new)))
        lo = slabs[first]

    for t in range(NSA_TPS):
        gates = jax.nn.sigmoid(gt_ref[t * tq:(t + 1) * tq, :]).T
        outs = []
        for g in groups:
            o_slc = _softmax_finish(state[t, g])
            for r in range(hpg):
                h = g * hpg + r
                cols = slice(r * tq, (r + 1) * tq)
                outs.append(gates[h:h + 1, :] * o_cmps[t, g][:, cols]
                            + gates[NSA_HEADS + h:NSA_HEADS + h + 1, :] * o_slc[:, cols]
                            + gates[2 * NSA_HEADS + h:2 * NSA_HEADS + h + 1, :] * o_wins[t, g][:, cols])
        o_ref[t * tq:(t + 1) * tq, :] = jnp.concatenate(outs, axis=0).T.astype(BF16)


def _nsa(proj, kcn, vct, wc, t0, t1, batch, seq):
    tq = NSA_TQ
    nq = seq // tq
    rows = NSA_TPS * tq
    steps = nq // NSA_TPS
    ncmp = seq // NSA_CMP_STRIDE
    width = NSA_HEADS * HEAD_DIM
    groups = NSA_KV_GROUPS
    kv = lambda col: pl.BlockSpec((seq, LANES), lambda b, i: (b, col // LANES))
    const = lambda shape: pl.BlockSpec(shape, lambda b, i: (0,) * len(shape))
    return pl.pallas_call(
        functools.partial(_nsa_kernel, seq=seq),
        grid=(batch, steps),
        in_specs=[pl.BlockSpec((rows, width), lambda b, i: (b * steps + i, C_NSQ // width)),
                  pl.BlockSpec((1, groups, ncmp, HEAD_DIM), lambda b, i: (b, 0, 0, 0)),
                  pl.BlockSpec((1, groups, HEAD_DIM, ncmp), lambda b, i: (b, 0, 0, 0)),
                  kv(C_KSL), kv(C_VSL), kv(C_KWN), kv(C_VWN),
                  pl.BlockSpec((rows, LANES), lambda b, i: (b * steps + i, C_NG // LANES)),
                  const((NSA_HEADS, CMP_WIN, tq)), const((NSA_HEADS, tq, tq)),
                  const((NSA_HEADS, tq, tq))],
        out_specs=pl.BlockSpec((rows, width), lambda b, i: (b * steps + i, 0)),
        out_shape=jax.ShapeDtypeStruct((batch * seq, width), BF16),
        scratch_shapes=[pltpu.VMEM((groups, seq, LANES), BF16),
                        pltpu.VMEM((groups, nq, VROWS, tq), BF16),
                        pltpu.VMEM((groups, seq, HEAD_DIM), BF16),
                        pltpu.VMEM((groups, nq, VROWS, tq), BF16),
                        pltpu.VMEM((NSA_TPS * groups, CMP_PAD + ncmp, NSA_HPG * tq), F32)],
        compiler_params=_params(("parallel", "arbitrary")),
        name="nsa",
    )(proj, kcn, vct, proj, proj, proj, proj, proj, wc, t0, t1)


def _merge_kernel(x_ref, ya_ref, yb_ref, ga_ref, gb_ref, wa_ref, wb_ref, wo_ref, o_ref):
    a = _dot(ya_ref[...], wa_ref[...])
    b = _dot(yb_ref[...], wb_ref[...])
    z = jax.nn.sigmoid(ga_ref[...]) * a + jax.nn.sigmoid(gb_ref[...]) * b
    o_ref[...] = x_ref[...] + _dot(z.astype(BF16), wo_ref[...])


def _merge(xf, ya, yb, proj, wa, wb, wo, layer):
    n, d = xf.shape
    tm = MERGE_TM
    row = lambda w, col=0: pl.BlockSpec((tm, w), lambda i: (i, col))
    return pl.pallas_call(
        _merge_kernel,
        grid=(n // tm,),
        in_specs=[row(d), row(ya.shape[1]), row(yb.shape[1]), row(d, C_GA // d), row(d, C_GB // d),
                  _layer_spec(wa, layer), _layer_spec(wb, layer), _layer_spec(wo, layer)],
        out_specs=row(d),
        out_shape=jax.ShapeDtypeStruct((n, d), F32),
        compiler_params=_params(("parallel",)),
        name="merge",
    )(xf, ya, yb, proj, proj, wa, wb, wo)


FFN_HALO = 16
FFN_TM = 1024
FFN_TF = 256


def _ffn_kernel(x_ref, xh_ref, g_ref, wu_ref, cw_ref, cb_ref, wd_ref, p_ref, wg_ref, wp_ref,
                o_ref, act_s, *, seq, d_ff):
    i = pl.program_id(0)
    tm = x_ref.shape[0]
    x = x_ref[...]
    at_start = (i * tm) % seq == 0
    halo = jnp.where(at_start, 0.0, _rms(xh_ref[...], g_ref[...]))
    hn = jnp.concatenate([halo.astype(BF16), _rms(x, g_ref[...]).astype(BF16)], axis=0)

    def conv(cols):
        u = _dot(hn, wu_ref[:, cols])
        u1 = pltpu.roll(u, 1, 0)[FFN_HALO:]
        u2 = pltpu.roll(u, 2, 0)[FFN_HALO:]
        cw = cw_ref[:, cols]
        return cw[0:1] * u2 + cw[1:2] * u1 + cw[2:3] * u[FFN_HALO:] + cb_ref[:, cols]

    for c in range(d_ff // FFN_TF):
        lo = c * FFN_TF
        act = jax.nn.gelu(conv(slice(lo, lo + FFN_TF))) * conv(slice(d_ff + lo, d_ff + lo + FFN_TF))
        act_s[:, lo:lo + FFN_TF] = act.astype(BF16)

    x = x + _dot(act_s[...], wd_ref[...])
    gate = jax.nn.sigmoid(_dot(x.astype(BF16), wg_ref[...]))
    o_ref[...] = x + gate * _dot(p_ref[...].astype(BF16), wp_ref[...])


def _ffn_ple(xf, gain, w_up, conv_w, conv_b, w_down, pf, wg, wp, layer, seq):
    n, d = xf.shape
    d_ff = w_down.shape[1]
    tm = FFN_TM
    hb = tm // FFN_HALO
    resident = lambda a: _layer_spec(a, layer)
    return pl.pallas_call(
        functools.partial(_ffn_kernel, seq=seq, d_ff=d_ff),
        grid=(n // tm,),
        in_specs=[pl.BlockSpec((tm, d), lambda i: (i, 0)),
                  pl.BlockSpec((FFN_HALO, d), lambda i: (jnp.maximum(i * hb - 1, 0), 0)),
                  resident(gain), resident(w_up), resident(conv_w), resident(conv_b), resident(w_down),
                  pl.BlockSpec((None, tm, pf.shape[2]), lambda i: (layer, i, 0)),
                  resident(wg), resident(wp)],
        out_specs=pl.BlockSpec((tm, d), lambda i: (i, 0)),
        out_shape=jax.ShapeDtypeStruct((n, d), F32),
        scratch_shapes=[pltpu.VMEM((tm, d_ff), BF16)],
        compiler_params=_params(("parallel",)),
        name="conv_ffn_ple",
    )(xf, xf, gain, w_up, conv_w, conv_b, w_down, pf, wg, wp)


def _reorder_kernel(w_ref, o_ref):
    attn = 3 * MOBA_HEADS * HEAD_DIM + NSA_HEADS * HEAD_DIM + 6 * NSA_KV_GROUPS * HEAD_DIM
    ng = 3 * NSA_HEADS
    w = w_ref[...]
    pad = jnp.zeros((w.shape[0], PROJ_COLS - w.shape[1]), w.dtype)
    o_ref[...] = jnp.concatenate([w[:, attn + ng:], w[:, :attn], w[:, attn:attn + ng], pad],
                                 axis=1).astype(BF16)


def _reorder_in_proj(w):
    depth, d, cols = w.shape
    tr = 256
    return pl.pallas_call(
        _reorder_kernel,
        grid=(depth, d // tr),
        in_specs=[pl.BlockSpec((None, tr, cols), lambda l, i: (l, i, 0))],
        out_specs=pl.BlockSpec((None, tr, PROJ_COLS), lambda l, i: (l, i, 0)),
        out_shape=jax.ShapeDtypeStruct((depth, d, PROJ_COLS), BF16),
        compiler_params=_params(("parallel", "parallel")),
        name="reorder_in_proj",
    )(w)


def _head_gain_rows(moba_q, moba_k, nsa_q, nsa_k):
    rows = jnp.zeros((moba_q.shape[0], 1, PROJ_COLS), F32)
    for col, gain, heads in ((C_MQ, moba_q * QSCALE, MOBA_HEADS), (C_MK, moba_k, MOBA_HEADS),
                             (C_NSQ, nsa_q * QSCALE, NSA_HEADS), (C_KSL, nsa_k[:, 1], NSA_KV_GROUPS),
                             (C_KWN, nsa_k[:, 2], NSA_KV_GROUPS)):
        rows = rows.at[:, 0, col:col + heads * HEAD_DIM].set(jnp.tile(gain, (1, heads)))
    return rows


def _cmp_weights(w1, pos):
    depth = w1.shape[0]
    half = NSA_CMP_STRIDE * HEAD_DIM
    w1cat = jnp.concatenate([w1[:, :half], w1[:, half:]], axis=2).astype(BF16)
    tok = w1cat.reshape(depth, NSA_CMP_STRIDE, HEAD_DIM, -1)
    zero = jnp.zeros_like(tok)
    placed = jnp.stack([jnp.concatenate([tok, zero], axis=2), jnp.concatenate([zero, tok], axis=2)], axis=1)
    placed = placed.reshape(depth, NSA_KV_GROUPS, NSA_CMP_STRIDE * LANES, -1)
    posr = jnp.zeros((depth, 8, half), F32).at[:, 0:2].set(pos.reshape(depth, 2, half))
    return w1cat, placed, posr


def kernel(x, p, rel_bias, attn_norm, w_in, moba_q_gain, moba_k_gain, nsa_q_gain, nsa_k_gain,
           cmp_pos_k, cmp_w1_k, cmp_w2_k, cmp_pos_v, cmp_w1_v, cmp_w2_v,
           w_br_moba, w_br_nsa, w_o, ffn_norm, w_up, conv_w, conv_b, w_down, w_ple_gate, w_ple):
    batch, seq, d = x.shape
    n = batch * seq
    depth = w_in.shape[0]
    bf = lambda a: a.astype(BF16)
    w_in_r = _reorder_in_proj(w_in)
    head_gain = _head_gain_rows(moba_q_gain, moba_k_gain, nsa_q_gain, nsa_k_gain)
    attn_gain, ffn_gain = attn_norm[:, None, :], ffn_norm[:, None, :]
    w1k, wtk, pk = _cmp_weights(cmp_w1_k, cmp_pos_k)
    w1v, wtv, pv = _cmp_weights(cmp_w1_v, cmp_pos_v)
    w2k, w2v, kg0 = bf(cmp_w2_k), bf(cmp_w2_v), nsa_k_gain[:, 0:1]
    wa, wb, wo = bf(w_br_moba), bf(w_br_nsa), bf(w_o)
    wu, wd, wg, wp = bf(w_up), bf(w_down), bf(w_ple_gate), bf(w_ple)
    conv_b = conv_b[:, None, :]
    pf = p.reshape(depth, n, -1)

    t0m, t1m, t0n, t1n, wc = _bias_tables(rel_bias)
    xf = x.reshape(n, d)
    for i in range(depth):
        proj = _inproj(xf, attn_gain, w_in_r, head_gain, i)
        ya = _moba(proj, t0m, t1m, batch, seq)
        kcn, vct = _compress(proj, w1k, w1v, wtk, wtv, w2k, w2v, pk, pv, kg0, i, batch, seq)
        yb = _nsa(proj, kcn, vct, wc, t0n, t1n, batch, seq)
        xf = _merge(xf, ya, yb, proj, wa, wb, wo, i)
        xf = _ffn_ple(xf, ffn_gain, wu, conv_w, conv_b, wd, pf, wg, wp, i, seq)
    return xf.reshape(batch, seq, d)
```

```python
import functools
import math

import numpy as np
import jax
import jax.numpy as jnp
from jax import lax
from jax.experimental import pallas as pl
from jax.experimental.pallas import tpu as pltpu

F32 = jnp.float32
BF16 = jnp.bfloat16

HEAD_DIM = 64
MOBA_HEADS = 8
MOBA_BLOCK = 256
MOBA_TOPK = 3
NSA_HEADS = 8
NSA_KV_GROUPS = 2
NSA_HPG = NSA_HEADS // NSA_KV_GROUPS
NSA_CMP_LEN = 32
NSA_CMP_STRIDE = 16
NSA_CMP_HIDDEN = 2 * HEAD_DIM
NSA_SEL_BLOCK = 64
NSA_SEL_TOPN = 16
NSA_WINDOW = 512
REL_BUCKETS = 32
REL_MAX_DIST = 128
RMS_EPS = 1e-6
SCALE = HEAD_DIM ** -0.5
LOG2E = math.log2(math.e)
QSCALE = SCALE * LOG2E
NEG = -1e30

LANES = 128
V7X_VMEM_BYTES = 64 * 1024 * 1024
VMEM_LIMIT = V7X_VMEM_BYTES * 7 // 8

INPROJ_TM = 512
MERGE_TM = 512

C_GA, C_GB, C_MQ, C_MK, C_MV, C_NSQ = 0, 1024, 2048, 2560, 3072, 3584
C_KC, C_VC, C_KSL, C_VSL, C_KWN, C_VWN, C_NG = 4096, 4224, 4352, 4480, 4608, 4736, 4864
PROJ_COLS = 4992

NSA_TQ = 128
NSA_TPS = 4
CMP_WIN = 16
CMP_PAD = 8


def _dot(a, b):
    return jnp.dot(a, b, preferred_element_type=F32)


def _dot_nt(a, b):
    return lax.dot_general(a, b, (((1,), (1,)), ((), ())), preferred_element_type=F32)


def _rms(x, gain):
    return x * lax.rsqrt(jnp.mean(x * x, axis=-1, keepdims=True) + RMS_EPS) * gain


def _split_bf16(x):
    hi = x.astype(BF16)
    return hi, (x - hi.astype(F32)).astype(BF16)


def _eye(rows, cols):
    r = lax.broadcasted_iota(jnp.int32, (rows, cols), 0)
    c = lax.broadcasted_iota(jnp.int32, (rows, cols), 1)
    return (r == c).astype(BF16)


def _transpose_bf16(x, rows):
    return _dot_nt(_eye(rows, x.shape[1]), x)


VROWS = 80


def _value_tile(v):
    row = lax.broadcasted_iota(jnp.int32, (VROWS, v.shape[0]), 0)
    return jnp.where(row == HEAD_DIM, 1.0, _transpose_bf16(v, VROWS)).astype(BF16)


def _softmax_joint(tiles):
    m = functools.reduce(jnp.maximum, [jnp.max(s, axis=0, keepdims=True) for s, _ in tiles])
    acc = functools.reduce(jnp.add, [_dot(vt, jnp.exp2(s - m).astype(BF16)) for s, vt in tiles])
    return m, acc


def _softmax_update(carry, s, vts):
    m, acc = carry
    m_new = jnp.maximum(m, jnp.max(s, axis=0, keepdims=True))
    p = jnp.exp2(s - m_new).astype(BF16)
    rows = s.shape[0] // len(vts)
    pv = functools.reduce(jnp.add, [_dot(vt, p[i * rows:(i + 1) * rows]) for i, vt in enumerate(vts)])
    return m_new, jnp.exp2(m - m_new) * acc + pv


def _softmax_finish(carry):
    _, acc = carry
    return acc[0:HEAD_DIM] * (1.0 / acc[HEAD_DIM:HEAD_DIM + 1])


def _rank_select(score, idx, count):
    beaten = jnp.zeros(score.shape, jnp.int32)
    for i in range(score.shape[0]):
        si = score[i:i + 1, :]
        beaten += ((si > score) | ((si == score) & (i < idx))).astype(jnp.int32)
    return beaten < count


def _params(sem):
    return pltpu.CompilerParams(dimension_semantics=sem, vmem_limit_bytes=VMEM_LIMIT)


def _layer_spec(a, layer):
    return pl.BlockSpec((None,) + a.shape[1:], lambda *_: (layer,) + (0,) * (a.ndim - 1),
                        pipeline_mode=pl.Buffered(1))


def _rel_bucket_np(dist):
    n = np.maximum(dist, 0)
    max_exact = REL_BUCKETS // 2
    nf = np.maximum(n, 1).astype(np.float32)
    large = max_exact + (np.log(nf / np.float32(max_exact)) / np.float32(math.log(REL_MAX_DIST / max_exact))
                         * np.float32(REL_BUCKETS - max_exact)).astype(np.int32)
    return np.where(n < max_exact, n, np.minimum(large, REL_BUCKETS - 1))


def _bucket_starts():
    buckets = _rel_bucket_np(np.arange(4 * REL_MAX_DIST))
    return [int(np.argmax(buckets >= k)) for k in range(REL_BUCKETS)]


BUCKET_START = _bucket_starts()
BIAS_REACH = BUCKET_START[-1]
assert BIAS_REACH <= MOBA_BLOCK and BIAS_REACH <= NSA_TQ - NSA_CMP_LEN + 1 + NSA_CMP_STRIDE


def _tables_kernel(tab_ref, t0m_ref, t1m_ref, t0n_ref, t1n_ref, wc_ref):
    h = pl.program_id(0)

    def bias(dist, head):
        last = tab_ref[head, REL_BUCKETS - 1]
        val = jnp.zeros(dist.shape, F32)
        for k in range(REL_BUCKETS - 2, -1, -1):
            val = jnp.where(dist < BUCKET_START[k + 1], (tab_ref[head, k] - last) * LOG2E, val)
        return val

    def toeplitz(size, offset, head):
        key = lax.broadcasted_iota(jnp.int32, (size, size), 0)
        qry = lax.broadcasted_iota(jnp.int32, (size, size), 1)
        return bias(offset + qry - key, head)

    t0m_ref[0] = toeplitz(MOBA_BLOCK, 0, h)
    t1m_ref[0] = toeplitz(MOBA_BLOCK, MOBA_BLOCK, h)
    t0n_ref[0] = toeplitz(NSA_TQ, 0, MOBA_HEADS + h)
    t1n_ref[0] = toeplitz(NSA_TQ, NSA_TQ, MOBA_HEADS + h)
    a = lax.broadcasted_iota(jnp.int32, (CMP_WIN, NSA_TQ), 0)
    i = lax.broadcasted_iota(jnp.int32, (CMP_WIN, NSA_TQ), 1)
    wc_ref[0] = bias(i + (NSA_TQ - NSA_CMP_LEN + 1) - NSA_CMP_STRIDE * a, MOBA_HEADS + h)


def _bias_tables(rel_bias):
    blk, tq = MOBA_BLOCK, NSA_TQ
    shapes = [(blk, blk), (blk, blk), (tq, tq), (tq, tq), (CMP_WIN, tq)]
    return pl.pallas_call(
        _tables_kernel,
        grid=(MOBA_HEADS,),
        in_specs=[pl.BlockSpec(memory_space=pltpu.SMEM)],
        out_specs=[pl.BlockSpec((1,) + s, lambda h: (h, 0, 0)) for s in shapes],
        out_shape=[jax.ShapeDtypeStruct((MOBA_HEADS,) + s, F32) for s in shapes],
        compiler_params=_params(("arbitrary",)),
        name="bias_tables",
    )(rel_bias)


NORM_SLABS = (list(range(C_MQ, C_MV, LANES)) + list(range(C_NSQ, C_KC, LANES)) + [C_KSL, C_KWN])


def _inproj_kernel(x_ref, g_ref, w_ref, hg_ref, o_ref):
    h = _rms(x_ref[...], g_ref[...]).astype(BF16)
    y = _dot(h, w_ref[...])
    first = lax.broadcasted_iota(jnp.int32, (1, LANES), 1) < HEAD_DIM
    edges = sorted(set([0, PROJ_COLS] + NORM_SLABS + [c + LANES for c in NORM_SLABS]))
    for lo, hi in zip(edges[:-1], edges[1:]):
        t = y[:, lo:hi]
        if lo in NORM_SLABS:
            sq = t * t
            s0 = jnp.sum(jnp.where(first, sq, 0.0), axis=-1, keepdims=True)
            s1 = jnp.sum(jnp.where(first, 0.0, sq), axis=-1, keepdims=True)
            ms = jnp.where(first, s0, s1) * (1.0 / HEAD_DIM)
            t = t * lax.rsqrt(ms + RMS_EPS) * hg_ref[:, lo:hi]
        o_ref[:, lo:hi] = t


def _inproj(xf, gain, w, head_gain, layer):
    n, d = xf.shape
    tm = INPROJ_TM
    return pl.pallas_call(
        _inproj_kernel,
        grid=(n // tm,),
        in_specs=[pl.BlockSpec((tm, d), lambda i: (i, 0)), _layer_spec(gain, layer),
                  _layer_spec(w, layer), _layer_spec(head_gain, layer)],
        out_specs=pl.BlockSpec((tm, PROJ_COLS), lambda i: (i, 0)),
        out_shape=jax.ShapeDtypeStruct((n, PROJ_COLS), F32),
        compiler_params=_params(("parallel",)),
        name="inproj",
    )(xf, gain, w, head_gain)


MOBA_HPS = 8
MOBA_BPS = 2


def _moba_kernel(q_ref, k_ref, v_ref, t0_ref, t1_ref, o_ref, kaug_s, vt_s, km_s, *, seq):
    step = pl.program_id(2)
    blk = MOBA_BLOCK
    nb = seq // blk
    heads = range(MOBA_HPS)
    pairs = range(MOBA_HPS // 2)
    pair_cols = lambda p: slice(p * LANES, (p + 1) * LANES)
    items = [(t, hh) for t in range(MOBA_BPS) for hh in heads]
    ns = [step * MOBA_BPS + t for t in range(MOBA_BPS)]

    @pl.when(step == 0)
    def _prepare_keys():
        block = lax.broadcasted_iota(jnp.int32, (seq, LANES), 0) // blk
        lane = lax.broadcasted_iota(jnp.int32, (seq, LANES), 1)
        code_hi = (lane - HEAD_DIM == block).astype(BF16)
        code_lo = (lane == block).astype(BF16)
        ones_row = (lax.broadcasted_iota(jnp.int32, (VROWS - HEAD_DIM, blk), 0) == 0).astype(F32)
        for p in pairs:
            kn = k_ref[:, pair_cols(p)]
            km = kn.reshape(nb, blk, LANES).sum(axis=1) * (1.0 / blk)
            km_s[p] = jnp.concatenate([km, jnp.zeros((16 - nb, LANES), F32)], axis=0)
            kaug_s[2 * p] = jnp.where(lane < HEAD_DIM, kn.astype(BF16), code_hi)
            kaug_s[2 * p + 1] = jnp.where(lane >= HEAD_DIM, kn.astype(BF16), code_lo)
            for j in range(nb):
                vt = _transpose_bf16(v_ref[j * blk:(j + 1) * blk, pair_cols(p)].astype(BF16), LANES)
                vt_s[2 * p, j] = jnp.concatenate([vt[0:HEAD_DIM], ones_row], axis=0).astype(BF16)
                vt_s[2 * p + 1, j] = jnp.concatenate([vt[HEAD_DIM:], ones_row], axis=0).astype(BF16)

    jidx = lax.broadcasted_iota(jnp.int32, (16, blk), 0)
    key = lax.broadcasted_iota(jnp.int32, (blk, blk), 0)
    qry = lax.broadcasted_iota(jnp.int32, (blk, blk), 1)
    qps = {}
    for t in range(MOBA_BPS):
        for p in pairs:
            qps[t, p] = _transpose_bf16(q_ref[t * blk:(t + 1) * blk, pair_cols(p)].astype(BF16),
                                        LANES).astype(BF16)
    even_lanes = lax.broadcasted_iota(jnp.int32, (16, LANES), 1) < HEAD_DIM
    gates = {}
    for t, hh in items:
        km = jnp.where(even_lanes == (hh % 2 == 0), km_s[hh // 2], 0.0)
        kmh, kml = _split_bf16(km)
        gates[t, hh] = _dot(kmh, qps[t, hh // 2]) + _dot(kml, qps[t, hh // 2])
    qaugs = {}
    filler = jnp.zeros((LANES - HEAD_DIM - 16, blk), BF16)
    for t, hh in items:
        past = jidx < ns[t]
        gate = jnp.where(past, gates[t, hh], NEG)
        sel = (past & _rank_select(gate, jidx, MOBA_TOPK)) | (jidx == ns[t])
        selb = jnp.where(sel, 0.0, NEG).astype(BF16)
        qp = qps[t, hh // 2]
        qaugs[t, hh] = (jnp.concatenate([qp[0:HEAD_DIM], selb, filler], axis=0) if hh % 2 == 0 else
                        jnp.concatenate([selb, filler, qp[HEAD_DIM:]], axis=0))

    def scores(t, hh, j):
        start = pl.multiple_of(j * blk, blk)
        return _dot(kaug_s[hh, pl.ds(start, blk), :], qaugs[t, hh]), vt_s[hh, j]

    near = {(t, hh): (scores(t, hh, ns[t]), scores(t, hh, jnp.maximum(ns[t] - 1, 0))) for t, hh in items}
    state = {}
    for t, hh in items:
        (s0, vt0), (s1, vt1) = near[t, hh]
        s0 = jnp.where(key <= qry, s0, NEG) + t0_ref[hh]
        s1 = jnp.where(ns[t] >= 1, s1, NEG) + t1_ref[hh]
        state[t, hh] = _softmax_joint([(s0, vt0), (s1, vt1)])

    older = [jnp.maximum(n - 1, 0) for n in ns]
    lo = 0
    for first in range(MOBA_BPS):
        live = [(t, hh) for t, hh in items if t >= first]

        def far_body(j, carry, live=live):
            tiles = [scores(t, hh, j) for t, hh in live]
            return tuple(_softmax_update(c, s, [vt]) for c, (s, vt) in zip(carry, tiles))

        new = lax.fori_loop(lo, older[first], far_body, tuple(state[it] for it in live))
        state.update(dict(zip(live, new)))
        lo = older[first]

    for t in range(MOBA_BPS):
        o_ref[t * blk:(t + 1) * blk, :] = jnp.concatenate(
            [_softmax_finish(state[t, hh]) for hh in heads], axis=0).T.astype(BF16)


def _moba(proj, t0, t1, batch, seq):
    blk = MOBA_BLOCK
    nb = seq // blk
    width = MOBA_HPS * HEAD_DIM
    rows = MOBA_BPS * blk
    steps = nb // MOBA_BPS
    qc, kc, vc = C_MQ // width, C_MK // width, C_MV // width
    return pl.pallas_call(
        functools.partial(_moba_kernel, seq=seq),
        grid=(batch, MOBA_HEADS // MOBA_HPS, steps),
        in_specs=[pl.BlockSpec((rows, width), lambda b, h, n: (b * steps + n, qc + h)),
                  pl.BlockSpec((seq, width), lambda b, h, n: (b, kc + h)),
                  pl.BlockSpec((seq, width), lambda b, h, n: (b, vc + h)),
                  pl.BlockSpec((MOBA_HPS, blk, blk), lambda b, h, n: (h, 0, 0)),
                  pl.BlockSpec((MOBA_HPS, blk, blk), lambda b, h, n: (h, 0, 0))],
        out_specs=pl.BlockSpec((rows, width), lambda b, h, n: (b * steps + n, h)),
        out_shape=jax.ShapeDtypeStruct((batch * seq, MOBA_HEADS * HEAD_DIM), BF16),
        scratch_shapes=[pltpu.VMEM((MOBA_HPS, seq, LANES), BF16),
                        pltpu.VMEM((MOBA_HPS, nb, VROWS, blk), BF16),
                        pltpu.VMEM((MOBA_HPS // 2, 16, LANES), F32)],
        compiler_params=_params(("parallel", "parallel", "arbitrary")),
        name="moba",
    )(proj, proj, proj, t0, t1)


def _compress_kernel(k_ref, v_ref, w1k_ref, w1v_ref, wtk_ref, wtv_ref, w2k_ref, w2v_ref, pk_ref, pv_ref,
                     kg_ref, ko_ref, vo_ref):
    hid = NSA_CMP_HIDDEN
    chunks = k_ref.shape[0] // NSA_CMP_STRIDE

    def chunk_rows(t_ref):
        return jnp.concatenate([t_ref[pl.ds(l, chunks, stride=NSA_CMP_STRIDE), :].astype(BF16)
                                for l in range(NSA_CMP_STRIDE)], axis=1)

    def compress(tok, w1_ref, wt_ref, w2_ref, pos_ref, g):
        a = _dot(tok, wt_ref[g])
        pw = _dot(pos_ref[...].astype(BF16), w1_ref[...])
        pos = pw[0:1, :hid] + pw[1:2, hid:]
        nxt = pltpu.roll(a[:, hid:], chunks - 1, 0)
        h = jax.nn.gelu(a[:, :hid] + nxt + pos)
        return _dot(h.astype(BF16), w2_ref[...])

    ktok, vtok = chunk_rows(k_ref), chunk_rows(v_ref)
    for g in range(NSA_KV_GROUPS):
        kc = compress(ktok, w1k_ref, wtk_ref, w2k_ref, pk_ref, g)
        ko_ref[0, g] = _rms(kc, kg_ref[...]).astype(BF16)
        vc = compress(vtok, w1v_ref, wtv_ref, w2v_ref, pv_ref, g).astype(BF16)
        vo_ref[0, g] = _transpose_bf16(vc, HEAD_DIM).astype(BF16)


def _compress(proj, w1k, w1v, wtk, wtv, w2k, w2v, pk, pv, kg, layer, batch, seq):
    groups = NSA_KV_GROUPS
    chunks = seq // NSA_CMP_STRIDE
    tok = lambda col: pl.BlockSpec((seq, LANES), lambda b: (b, col // LANES))
    return pl.pallas_call(
        _compress_kernel,
        grid=(batch,),
        in_specs=[tok(C_KC), tok(C_VC)] + [_layer_spec(a, layer)
                                           for a in (w1k, w1v, wtk, wtv, w2k, w2v, pk, pv, kg)],
        out_specs=[pl.BlockSpec((1, groups, chunks, HEAD_DIM), lambda b: (b, 0, 0, 0)),
                   pl.BlockSpec((1, groups, HEAD_DIM, chunks), lambda b: (b, 0, 0, 0))],
        out_shape=[jax.ShapeDtypeStruct((batch, groups, chunks, HEAD_DIM), BF16),
                   jax.ShapeDtypeStruct((batch, groups, HEAD_DIM, chunks), BF16)],
        compiler_params=_params(("parallel",)),
        name="nsa_compress",
    )(proj, proj, w1k, w1v, wtk, wtv, w2k, w2v, pk, pv, kg)


def _nsa_kernel(q_ref, kc_ref, vc_ref, ksl_ref, vsl_ref, kwn_ref, vwn_ref, gt_ref,
                wc_ref, t0_ref, t1_ref, o_ref, kslaug_s, vslt_s, kwn_s, vwnt_s, sc_s, *, seq):
    step = pl.program_id(1)
    tq = NSA_TQ
    hpg = NSA_HPG
    sb = NSA_SEL_BLOCK
    nsel = seq // sb
    ncmp = seq // NSA_CMP_STRIDE
    nwin = NSA_WINDOW // tq
    groups = range(NSA_KV_GROUPS)
    items = [(t, g) for t in range(NSA_TPS) for g in groups]
    qis = [step * NSA_TPS + t for t in range(NSA_TPS)]

    @pl.when(step == 0)
    def _prepare_keys():
        r = lax.broadcasted_iota(jnp.int32, (seq, LANES), 0)
        c = lax.broadcasted_iota(jnp.int32, (seq, LANES), 1)
        onehot = (c - HEAD_DIM == r // sb).astype(F32)
        place = _eye(HEAD_DIM, LANES)
        sc_s[:, 0:CMP_PAD, :] = jnp.zeros((NSA_TPS * NSA_KV_GROUPS, CMP_PAD, hpg * tq), F32)
        for g in groups:
            sl = slice(g * HEAD_DIM, (g + 1) * HEAD_DIM)
            kslaug_s[g] = (_dot(ksl_ref[:, sl].astype(BF16), place) + onehot).astype(BF16)
            kwn_s[g] = kwn_ref[:, sl].astype(BF16)
            for j in range(seq // tq):
                rows = slice(j * tq, (j + 1) * tq)
                vslt_s[g, j] = _value_tile(vsl_ref[rows, sl].astype(BF16))
                vwnt_s[g, j] = _value_tile(vwn_ref[rows, sl].astype(BF16))

    lanes = lambda parts: jnp.concatenate(parts, axis=1)
    key = lax.broadcasted_iota(jnp.int32, (tq, tq), 0)
    qry = lax.broadcasted_iota(jnp.int32, (tq, tq), 1)
    causal = lanes([key <= qry] * hpg)
    upper = lanes([key > qry] * hpg)
    oj = lax.broadcasted_iota(jnp.int32, (nsel, ncmp), 0) * sb
    on = lax.broadcasted_iota(jnp.int32, (nsel, ncmp), 1) * NSA_CMP_STRIDE
    overlap = ((on < oj + sb) & (on + NSA_CMP_LEN > oj) & (on < seq - NSA_CMP_STRIDE)).astype(BF16)
    jidx = lax.broadcasted_iota(jnp.int32, (nsel, tq), 0)
    cend = lax.broadcasted_iota(jnp.int32, (ncmp, tq), 0) * NSA_CMP_STRIDE + (NSA_CMP_LEN - 1)
    cvis = [lanes([cend <= lax.broadcasted_iota(jnp.int32, (ncmp, tq), 1) + qi * tq] * hpg) for qi in qis]
    own = [(lax.broadcasted_iota(jnp.int32, (nsel, tq), 1) + qi * tq) // sb for qi in qis]
    tile_rows = lambda j: pl.ds(pl.multiple_of(j * tq, tq), tq)
    back = lambda t, k: jnp.maximum(qis[t] - k, 0)
    heads = [[g * hpg + r for r in range(hpg)] for g in groups]
    t0 = [lanes([t0_ref[h] for h in heads[g]]) for g in groups]
    t1 = [lanes([t1_ref[h] for h in heads[g]]) for g in groups]
    slot = lambda t, g: t * NSA_KV_GROUPS + g

    qplain, win_scores = {}, {}
    for t, g in items:
        width = hpg * HEAD_DIM
        qg = _transpose_bf16(q_ref[t * tq:(t + 1) * tq, g * width:(g + 1) * width].astype(BF16), width)
        qplain[t, g] = lanes([qg[r * HEAD_DIM:(r + 1) * HEAD_DIM] for r in range(hpg)]).astype(BF16)
        sc_s[slot(t, g), CMP_PAD:, :] = _dot(kc_ref[0, g], qplain[t, g])
        win_scores[t, g] = [_dot(kwn_s[g, tile_rows(back(t, k)), :], qplain[t, g]) for k in range(nwin + 1)]

    o_cmps, imps = {}, {}
    for t, g in items:
        win = pl.ds(pl.multiple_of(qis[t] * (tq // NSA_CMP_STRIDE), 8), CMP_WIN)
        sc_s[slot(t, g), win, :] += lanes([wc_ref[h] for h in heads[g]])
        s = jnp.where(cvis[t], sc_s[slot(t, g), CMP_PAD:, :], NEG)
        m = jnp.max(s, axis=0, keepdims=True)
        e = jnp.where(cvis[t], jnp.exp2(s - m), 0.0)
        den = jnp.sum(e, axis=0, keepdims=True)
        p = e * (1.0 / jnp.where(den > 0, den, 1.0))
        o_cmps[t, g] = _dot(vc_ref[0, g], p.astype(BF16))
        psum = p[:, 0:tq]
        for r in range(1, hpg):
            psum = psum + p[:, r * tq:(r + 1) * tq]
        ph, plo = _split_bf16(psum)
        imps[t, g] = _dot(overlap, ph) + _dot(overlap, plo)

    o_wins = {}
    for t, g in items:
        qi = qis[t]
        tiles = []
        for k, s in enumerate(win_scores[t, g]):
            if k == 0:
                s = jnp.where(causal, s, NEG) + t0[g]
            elif k == 1:
                s = jnp.where(qi >= 1, s, NEG) + t1[g]
            elif k < nwin:
                s = jnp.where(qi >= k, s, NEG)
            else:
                s = jnp.where(upper & (qi >= k), s, NEG)
            tiles.append((s, vwnt_s[g, back(t, k)]))
        o_wins[t, g] = _softmax_finish(_softmax_joint(tiles))

    qaugs = {}
    for t, g in items:
        imp = jnp.where((jidx == 0) | (jidx == own[t]) | (jidx == own[t] - 1), -NEG, imps[t, g])
        imp = jnp.where(jidx > own[t], NEG, imp)
        sel = _rank_select(imp, jidx, NSA_SEL_TOPN) & (jidx <= own[t])
        selb = jnp.where(sel, 0.0, NEG).astype(BF16)
        qaugs[t, g] = jnp.concatenate([qplain[t, g], lanes([selb] * hpg),
                                       jnp.zeros((LANES - HEAD_DIM - nsel, hpg * tq), BF16)], axis=0)

    assert NSA_TPS % 2 == 0
    near_tiles = lambda t: 3 if t % 2 == 0 else 2
    slc_near = {(t, g): [_dot(kslaug_s[g, tile_rows(back(t, k)), :], qaugs[t, g])
                         for k in range(near_tiles(t))] for t, g in items}
    state = {}
    for t, g in items:
        qi = qis[t]
        tiles = [(jnp.where(causal, slc_near[t, g][0], NEG) + t0[g], vslt_s[g, qi]),
                 (jnp.where(qi >= 1, slc_near[t, g][1], NEG) + t1[g], vslt_s[g, back(t, 1)])]
        if near_tiles(t) == 3:
            tiles.append((jnp.where(qi >= 2, slc_near[t, g][2], NEG), vslt_s[g, back(t, 2)]))
        state[t, g] = _softmax_joint(tiles)

    slabs = [jnp.maximum(qi - 1, 0) // 2 for qi in qis]
    lo = 0
    for first in range(NSA_TPS):
        live = [(t, g) for t, g in items if t >= first]

        def slc_far(i, carry, live=live):
            slab = pl.ds(pl.multiple_of(i * (2 * tq), 2 * tq), 2 * tq)
            scores = [_dot(kslaug_s[g, slab, :], qaugs[t, g]) for t, g in live]
            return tuple(_softmax_update(c, sc, [vslt_s[g, 2 * i], vslt_s[g, 2 * i + 1]])
                         for c, sc, (t, g) in zip(carry, scores, live))

        new = lax.fori_loop(lo, slabs[first], slc_far, tuple(state[it] for it in live))
        state.update(dict(zip(live, new)))
        lo = slabs[first]

    for t in range(NSA_TPS):
        gates = jax.nn.sigmoid(gt_ref[t * tq:(t + 1) * tq, :]).T
        outs = []
        for g in groups:
            o_slc = _softmax_finish(state[t, g])
            for r in range(hpg):
                h = g * hpg + r
                cols = slice(r * tq, (r + 1) * tq)
                outs.append(gates[h:h + 1, :] * o_cmps[t, g][:, cols]
                            + gates[NSA_HEADS + h:NSA_HEADS + h + 1, :] * o_slc[:, cols]
                            + gates[2 * NSA_HEADS + h:2 * NSA_HEADS + h + 1, :] * o_wins[t, g][:, cols])
        o_ref[t * tq:(t + 1) * tq, :] = jnp.concatenate(outs, axis=0).T.astype(BF16)


def _nsa(proj, kcn, vct, wc, t0, t1, batch, seq):
    tq = NSA_TQ
    nq = seq // tq
    rows = NSA_TPS * tq
    steps = nq // NSA_TPS
    ncmp = seq // NSA_CMP_STRIDE
    width = NSA_HEADS * HEAD_DIM
    groups = NSA_KV_GROUPS
    kv = lambda col: pl.BlockSpec((seq, LANES), lambda b, i: (b, col // LANES))
    const = lambda shape: pl.BlockSpec(shape, lambda b, i: (0,) * len(shape))
    return pl.pallas_call(
        functools.partial(_nsa_kernel, seq=seq),
        grid=(batch, steps),
        in_specs=[pl.BlockSpec((rows, width), lambda b, i: (b * steps + i, C_NSQ // width)),
                  pl.BlockSpec((1, groups, ncmp, HEAD_DIM), lambda b, i: (b, 0, 0, 0)),
                  pl.BlockSpec((1, groups, HEAD_DIM, ncmp), lambda b, i: (b, 0, 0, 0)),
                  kv(C_KSL), kv(C_VSL), kv(C_KWN), kv(C_VWN),
                  pl.BlockSpec((rows, LANES), lambda b, i: (b * steps + i, C_NG // LANES)),
                  const((NSA_HEADS, CMP_WIN, tq)), const((NSA_HEADS, tq, tq)),
                  const((NSA_HEADS, tq, tq))],
        out_specs=pl.BlockSpec((rows, width), lambda b, i: (b * steps + i, 0)),
        out_shape=jax.ShapeDtypeStruct((batch * seq, width), BF16),
        scratch_shapes=[pltpu.VMEM((groups, seq, LANES), BF16),
                        pltpu.VMEM((groups, nq, VROWS, tq), BF16),
                        pltpu.VMEM((groups, seq, HEAD_DIM), BF16),
                        pltpu.VMEM((groups, nq, VROWS, tq), BF16),
                        pltpu.VMEM((NSA_TPS * groups, CMP_PAD + ncmp, NSA_HPG * tq), F32)],
        compiler_params=_params(("parallel", "arbitrary")),
        name="nsa",
    )(proj, kcn, vct, proj, proj, proj, proj, proj, wc, t0, t1)


def _merge_kernel(x_ref, ya_ref, yb_ref, ga_ref, gb_ref, wa_ref, wb_ref, wo_ref, o_ref):
    a = _dot(ya_ref[...], wa_ref[...])
    b = _dot(yb_ref[...], wb_ref[...])
    z = jax.nn.sigmoid(ga_ref[...]) * a + jax.nn.sigmoid(gb_ref[...]) * b
    o_ref[...] = x_ref[...] + _dot(z.astype(BF16), wo_ref[...])


def _merge(xf, ya, yb, proj, wa, wb, wo, layer):
    n, d = xf.shape
    tm = MERGE_TM
    row = lambda w, col=0: pl.BlockSpec((tm, w), lambda i: (i, col))
    return pl.pallas_call(
        _merge_kernel,
        grid=(n // tm,),
        in_specs=[row(d), row(ya.shape[1]), row(yb.shape[1]), row(d, C_GA // d), row(d, C_GB // d),
                  _layer_spec(wa, layer), _layer_spec(wb, layer), _layer_spec(wo, layer)],
        out_specs=row(d),
        out_shape=jax.ShapeDtypeStruct((n, d), F32),
        compiler_params=_params(("parallel",)),
        name="merge",
    )(xf, ya, yb, proj, proj, wa, wb, wo)


FFN_HALO = 16
FFN_TM = 1024
FFN_TF = 256


def _ffn_kernel(x_ref, xh_ref, g_ref, wu_ref, cw_ref, cb_ref, wd_ref, p_ref, wg_ref, wp_ref,
                o_ref, act_s, *, seq, d_ff):
    i = pl.program_id(0)
    tm = x_ref.shape[0]
    x = x_ref[...]
    at_start = (i * tm) % seq == 0
    halo = jnp.where(at_start, 0.0, _rms(xh_ref[...], g_ref[...]))
    hn = jnp.concatenate([halo.astype(BF16), _rms(x, g_ref[...]).astype(BF16)], axis=0)

    def conv(cols):
        u = _dot(hn, wu_ref[:, cols])
        u1 = pltpu.roll(u, 1, 0)[FFN_HALO:]
        u2 = pltpu.roll(u, 2, 0)[FFN_HALO:]
        cw = cw_ref[:, cols]
        return cw[0:1] * u2 + cw[1:2] * u1 + cw[2:3] * u[FFN_HALO:] + cb_ref[:, cols]

    for c in range(d_ff // FFN_TF):
        lo = c * FFN_TF
        act = jax.nn.gelu(conv(slice(lo, lo + FFN_TF))) * conv(slice(d_ff + lo, d_ff + lo + FFN_TF))
        act_s[:, lo:lo + FFN_TF] = act.astype(BF16)

    x = x + _dot(act_s[...], wd_ref[...])
    gate = jax.nn.sigmoid(_dot(x.astype(BF16), wg_ref[...]))
    o_ref[...] = x + gate * _dot(p_ref[...].astype(BF16), wp_ref[...])


def _ffn_ple(xf, gain, w_up, conv_w, conv_b, w_down, pf, wg, wp, layer, seq):
    n, d = xf.shape
    d_ff = w_down.shape[1]
    tm = FFN_TM
    hb = tm // FFN_HALO
    resident = lambda a: _layer_spec(a, layer)
    return pl.pallas_call(
        functools.partial(_ffn_kernel, seq=seq, d_ff=d_ff),
        grid=(n // tm,),
        in_specs=[pl.BlockSpec((tm, d), lambda i: (i, 0)),
                  pl.BlockSpec((FFN_HALO, d), lambda i: (jnp.maximum(i * hb - 1, 0), 0)),
                  resident(gain), resident(w_up), resident(conv_w), resident(conv_b), resident(w_down),
                  pl.BlockSpec((None, tm, pf.shape[2]), lambda i: (layer, i, 0)),
                  resident(wg), resident(wp)],
        out_specs=pl.BlockSpec((tm, d), lambda i: (i, 0)),
        out_shape=jax.ShapeDtypeStruct((n, d), F32),
        scratch_shapes=[pltpu.VMEM((tm, d_ff), BF16)],
        compiler_params=_params(("parallel",)),
        name="conv_ffn_ple",
    )(xf, xf, gain, w_up, conv_w, conv_b, w_down, pf, wg, wp)


def _reorder_kernel(w_ref, o_ref):
    attn = 3 * MOBA_HEADS * HEAD_DIM + NSA_HEADS * HEAD_DIM + 6 * NSA_KV_GROUPS * HEAD_DIM
    ng = 3 * NSA_HEADS
    w = w_ref[...]
    pad = jnp.zeros((w.shape[0], PROJ_COLS - w.shape[1]), w.dtype)
    o_ref[...] = jnp.concatenate([w[:, attn + ng:], w[:, :attn], w[:, attn:attn + ng], pad],
                                 axis=1).astype(BF16)


def _reorder_in_proj(w):
    depth, d, cols = w.shape
    tr = 256
    return pl.pallas_call(
        _reorder_kernel,
        grid=(depth, d // tr),
        in_specs=[pl.BlockSpec((None, tr, cols), lambda l, i: (l, i, 0))],
        out_specs=pl.BlockSpec((None, tr, PROJ_COLS), lambda l, i: (l, i, 0)),
        out_shape=jax.ShapeDtypeStruct((depth, d, PROJ_COLS), BF16),
        compiler_params=_params(("parallel", "parallel")),
        name="reorder_in_proj",
    )(w)


def _head_gain_rows(moba_q, moba_k, nsa_q, nsa_k):
    rows = jnp.zeros((moba_q.shape[0], 1, PROJ_COLS), F32)
    for col, gain, heads in ((C_MQ, moba_q * QSCALE, MOBA_HEADS), (C_MK, moba_k, MOBA_HEADS),
                             (C_NSQ, nsa_q * QSCALE, NSA_HEADS), (C_KSL, nsa_k[:, 1], NSA_KV_GROUPS),
                             (C_KWN, nsa_k[:, 2], NSA_KV_GROUPS)):
        rows = rows.at[:, 0, col:col + heads * HEAD_DIM].set(jnp.tile(gain, (1, heads)))
    return rows


def _cmp_weights(w1, pos):
    depth = w1.shape[0]
    half = NSA_CMP_STRIDE * HEAD_DIM
    w1cat = jnp.concatenate([w1[:, :half], w1[:, half:]], axis=2).astype(BF16)
    tok = w1cat.reshape(depth, NSA_CMP_STRIDE, HEAD_DIM, -1)
    zero = jnp.zeros_like(tok)
    placed = jnp.stack([jnp.concatenate([tok, zero], axis=2), jnp.concatenate([zero, tok], axis=2)], axis=1)
    placed = placed.reshape(depth, NSA_KV_GROUPS, NSA_CMP_STRIDE * LANES, -1)
    posr = jnp.zeros((depth, 8, half), F32).at[:, 0:2].set(pos.reshape(depth, 2, half))
    return w1cat, placed, posr


def kernel(x, p, rel_bias, attn_norm, w_in, moba_q_gain, moba_k_gain, nsa_q_gain, nsa_k_gain,
           cmp_pos_k, cmp_w1_k, cmp_w2_k, cmp_pos_v, cmp_w1_v, cmp_w2_v,
           w_br_moba, w_br_nsa, w_o, ffn_norm, w_up, conv_w, conv_b, w_down, w_ple_gate, w_ple):
    batch, seq, d = x.shape
    n = batch * seq
    depth = w_in.shape[0]
    bf = lambda a: a.astype(BF16)
    w_in_r = _reorder_in_proj(w_in)
    head_gain = _head_gain_rows(moba_q_gain, moba_k_gain, nsa_q_gain, nsa_k_gain)
    attn_gain, ffn_gain = attn_norm[:, None, :], ffn_norm[:, None, :]
    w1k, wtk, pk = _cmp_weights(cmp_w1_k, cmp_pos_k)
    w1v, wtv, pv = _cmp_weights(cmp_w1_v, cmp_pos_v)
    w2k, w2v, kg0 = bf(cmp_w2_k), bf(cmp_w2_v), nsa_k_gain[:, 0:1]
    wa, wb, wo = bf(w_br_moba), bf(w_br_nsa), bf(w_o)
    wu, wd, wg, wp = bf(w_up), bf(w_down), bf(w_ple_gate), bf(w_ple)
    conv_b = conv_b[:, None, :]
    pf = p.reshape(depth, n, -1)

    t0m, t1m, t0n, t1n, wc = _bias_tables(rel_bias)
    xf = x.reshape(n, d)
    for i in range(depth):
        proj = _inproj(xf, attn_gain, w_in_r, head_gain, i)
        ya = _moba(proj, t0m, t1m, batch, seq)
        kcn, vct = _compress(proj, w1k, w1v, wtk, wtv, w2k, w2v, pk, pv, kg0, i, batch, seq)
        yb = _nsa(proj, kcn, vct, wc, t0n, t1n, batch, seq)
        xf = _merge(xf, ya, yb, proj, wa, wb, wo, i)
        xf = _ffn_ple(xf, ffn_gain, wu, conv_w, conv_b, wd, pf, wg, wp, i, seq)
    return xf.reshape(batch, seq, d)
```

```python
import functools
import math

import numpy as np
import jax
import jax.numpy as jnp
from jax import lax
from jax.experimental import pallas as pl
from jax.experimental.pallas import tpu as pltpu

F32 = jnp.float32
BF16 = jnp.bfloat16

HEAD_DIM = 64
MOBA_HEADS = 8
MOBA_BLOCK = 256
MOBA_TOPK = 3
NSA_HEADS = 8
NSA_KV_GROUPS = 2
NSA_HPG = NSA_HEADS // NSA_KV_GROUPS
NSA_CMP_LEN = 32
NSA_CMP_STRIDE = 16
NSA_CMP_HIDDEN = 2 * HEAD_DIM
NSA_SEL_BLOCK = 64
NSA_SEL_TOPN = 16
NSA_WINDOW = 512
REL_BUCKETS = 32
REL_MAX_DIST = 128
RMS_EPS = 1e-6
SCALE = HEAD_DIM ** -0.5
LOG2E = math.log2(math.e)
QSCALE = SCALE * LOG2E
NEG = -1e30

LANES = 128
V7X_VMEM_BYTES = 64 * 1024 * 1024
VMEM_LIMIT = V7X_VMEM_BYTES * 7 // 8

INPROJ_TM = 512
MERGE_TM = 512

C_GA, C_GB, C_MQ, C_MK, C_MV, C_NSQ = 0, 1024, 2048, 2560, 3072, 3584
C_KC, C_VC, C_KSL, C_VSL, C_KWN, C_VWN, C_NG = 4096, 4224, 4352, 4480, 4608, 4736, 4864
PROJ_COLS = 4992

NSA_TQ = 128
NSA_TPS = 4
CMP_WIN = 16
CMP_PAD = 8


def _dot(a, b):
    return jnp.dot(a, b, preferred_element_type=F32)


def _dot_nt(a, b):
    return lax.dot_general(a, b, (((1,), (1,)), ((), ())), preferred_element_type=F32)


def _rms(x, gain):
    return x * lax.rsqrt(jnp.mean(x * x, axis=-1, keepdims=True) + RMS_EPS) * gain


def _split_bf16(x):
    hi = x.astype(BF16)
    return hi, (x - hi.astype(F32)).astype(BF16)


def _eye(rows, cols):
    r = lax.broadcasted_iota(jnp.int32, (rows, cols), 0)
    c = lax.broadcasted_iota(jnp.int32, (rows, cols), 1)
    return (r == c).astype(BF16)


def _transpose_bf16(x, rows):
    return _dot_nt(_eye(rows, x.shape[1]), x)


VROWS = 80


def _value_tile(v):
    row = lax.broadcasted_iota(jnp.int32, (VROWS, v.shape[0]), 0)
    return jnp.where(row == HEAD_DIM, 1.0, _transpose_bf16(v, VROWS)).astype(BF16)


def _softmax_joint(tiles):
    m = functools.reduce(jnp.maximum, [jnp.max(s, axis=0, keepdims=True) for s, _ in tiles])
    acc = functools.reduce(jnp.add, [_dot(vt, jnp.exp2(s - m).astype(BF16)) for s, vt in tiles])
    return m, acc


def _softmax_update(carry, s, vts):
    m, acc = carry
    m_new = jnp.maximum(m, jnp.max(s, axis=0, keepdims=True))
    p = jnp.exp2(s - m_new).astype(BF16)
    rows = s.shape[0] // len(vts)
    pv = functools.reduce(jnp.add, [_dot(vt, p[i * rows:(i + 1) * rows]) for i, vt in enumerate(vts)])
    return m_new, jnp.exp2(m - m_new) * acc + pv


def _softmax_finish(carry):
    _, acc = carry
    return acc[0:HEAD_DIM] * (1.0 / acc[HEAD_DIM:HEAD_DIM + 1])


def _rank_select(score, idx, count):
    beaten = jnp.zeros(score.shape, jnp.int32)
    for i in range(score.shape[0]):
        si = score[i:i + 1, :]
        beaten += ((si > score) | ((si == score) & (i < idx))).astype(jnp.int32)
    return beaten < count


def _params(sem):
    return pltpu.CompilerParams(dimension_semantics=sem, vmem_limit_bytes=VMEM_LIMIT)


def _layer_spec(a, layer):
    return pl.BlockSpec((None,) + a.shape[1:], lambda *_: (layer,) + (0,) * (a.ndim - 1),
                        pipeline_mode=pl.Buffered(1))


def _rel_bucket_np(dist):
    n = np.maximum(dist, 0)
    max_exact = REL_BUCKETS // 2
    nf = np.maximum(n, 1).astype(np.float32)
    large = max_exact + (np.log(nf / np.float32(max_exact)) / np.float32(math.log(REL_MAX_DIST / max_exact))
                         * np.float32(REL_BUCKETS - max_exact)).astype(np.int32)
    return np.where(n < max_exact, n, np.minimum(large, REL_BUCKETS - 1))


def _bucket_starts():
    buckets = _rel_bucket_np(np.arange(4 * REL_MAX_DIST))
    return [int(np.argmax(buckets >= k)) for k in range(REL_BUCKETS)]


BUCKET_START = _bucket_starts()
BIAS_REACH = BUCKET_START[-1]
assert BIAS_REACH <= MOBA_BLOCK and BIAS_REACH <= NSA_TQ - NSA_CMP_LEN + 1 + NSA_CMP_STRIDE


def _tables_kernel(tab_ref, t0m_ref, t1m_ref, t0n_ref, t1n_ref, wc_ref):
    h = pl.program_id(0)

    def bias(dist, head):
        last = tab_ref[head, REL_BUCKETS - 1]
        val = jnp.zeros(dist.shape, F32)
        for k in range(REL_BUCKETS - 2, -1, -1):
            val = jnp.where(dist < BUCKET_START[k + 1], (tab_ref[head, k] - last) * LOG2E, val)
        return val

    def toeplitz(size, offset, head):
        key = lax.broadcasted_iota(jnp.int32, (size, size), 0)
        qry = lax.broadcasted_iota(jnp.int32, (size, size), 1)
        return bias(offset + qry - key, head)

    t0m_ref[0] = toeplitz(MOBA_BLOCK, 0, h)
    t1m_ref[0] = toeplitz(MOBA_BLOCK, MOBA_BLOCK, h)
    t0n_ref[0] = toeplitz(NSA_TQ, 0, MOBA_HEADS + h)
    t1n_ref[0] = toeplitz(NSA_TQ, NSA_TQ, MOBA_HEADS + h)
    a = lax.broadcasted_iota(jnp.int32, (CMP_WIN, NSA_TQ), 0)
    i = lax.broadcasted_iota(jnp.int32, (CMP_WIN, NSA_TQ), 1)
    wc_ref[0] = bias(i + (NSA_TQ - NSA_CMP_LEN + 1) - NSA_CMP_STRIDE * a, MOBA_HEADS + h)


def _bias_tables(rel_bias):
    blk, tq = MOBA_BLOCK, NSA_TQ
    shapes = [(blk, blk), (blk, blk), (tq, tq), (tq, tq), (CMP_WIN, tq)]
    return pl.pallas_call(
        _tables_kernel,
        grid=(MOBA_HEADS,),
        in_specs=[pl.BlockSpec(memory_space=pltpu.SMEM)],
        out_specs=[pl.BlockSpec((1,) + s, lambda h: (h, 0, 0)) for s in shapes],
        out_shape=[jax.ShapeDtypeStruct((MOBA_HEADS,) + s, F32) for s in shapes],
        compiler_params=_params(("arbitrary",)),
        name="bias_tables",
    )(rel_bias)


NORM_SLABS = (list(range(C_MQ, C_MV, LANES)) + list(range(C_NSQ, C_KC, LANES)) + [C_KSL, C_KWN])


def _inproj_kernel(x_ref, g_ref, w_ref, hg_ref, o_ref):
    h = _rms(x_ref[...], g_ref[...]).astype(BF16)
    y = _dot_nt(h, w_ref[...])
    first = lax.broadcasted_iota(jnp.int32, (1, LANES), 1) < HEAD_DIM
    edges = sorted(set([0, PROJ_COLS] + NORM_SLABS + [c + LANES for c in NORM_SLABS]))
    for lo, hi in zip(edges[:-1], edges[1:]):
        t = y[:, lo:hi]
        if lo in NORM_SLABS:
            sq = t * t
            s0 = jnp.sum(jnp.where(first, sq, 0.0), axis=-1, keepdims=True)
            s1 = jnp.sum(jnp.where(first, 0.0, sq), axis=-1, keepdims=True)
            ms = jnp.where(first, s0, s1) * (1.0 / HEAD_DIM)
            t = t * lax.rsqrt(ms + RMS_EPS) * hg_ref[:, lo:hi]
        o_ref[:, lo:hi] = t


def _inproj(xf, gain, w, head_gain, layer):
    n, d = xf.shape
    tm = INPROJ_TM
    return pl.pallas_call(
        _inproj_kernel,
        grid=(n // tm,),
        in_specs=[pl.BlockSpec((tm, d), lambda i: (i, 0)), _layer_spec(gain, layer),
                  _layer_spec(w, layer), _layer_spec(head_gain, layer)],
        out_specs=pl.BlockSpec((tm, PROJ_COLS), lambda i: (i, 0)),
        out_shape=jax.ShapeDtypeStruct((n, PROJ_COLS), F32),
        compiler_params=_params(("parallel",)),
        name="inproj",
    )(xf, gain, w, head_gain)


MOBA_HPS = 8
MOBA_BPS = 2


def _moba_kernel(q_ref, k_ref, v_ref, t0_ref, t1_ref, o_ref, kaug_s, vt_s, km_s, *, seq):
    step = pl.program_id(2)
    blk = MOBA_BLOCK
    nb = seq // blk
    heads = range(MOBA_HPS)
    pairs = range(MOBA_HPS // 2)
    pair_cols = lambda p: slice(p * LANES, (p + 1) * LANES)
    items = [(t, hh) for t in range(MOBA_BPS) for hh in heads]
    ns = [step * MOBA_BPS + t for t in range(MOBA_BPS)]

    @pl.when(step == 0)
    def _prepare_keys():
        block = lax.broadcasted_iota(jnp.int32, (seq, LANES), 0) // blk
        lane = lax.broadcasted_iota(jnp.int32, (seq, LANES), 1)
        code_hi = (lane - HEAD_DIM == block).astype(BF16)
        code_lo = (lane == block).astype(BF16)
        ones_row = (lax.broadcasted_iota(jnp.int32, (VROWS - HEAD_DIM, blk), 0) == 0).astype(F32)
        for p in pairs:
            kn = k_ref[:, pair_cols(p)]
            km = kn.reshape(nb, blk, LANES).sum(axis=1) * (1.0 / blk)
            km_s[p] = jnp.concatenate([km, jnp.zeros((16 - nb, LANES), F32)], axis=0)
            kaug_s[2 * p] = jnp.where(lane < HEAD_DIM, kn.astype(BF16), code_hi)
            kaug_s[2 * p + 1] = jnp.where(lane >= HEAD_DIM, kn.astype(BF16), code_lo)
            for j in range(nb):
                vt = _transpose_bf16(v_ref[j * blk:(j + 1) * blk, pair_cols(p)].astype(BF16), LANES)
                vt_s[2 * p, j] = jnp.concatenate([vt[0:HEAD_DIM], ones_row], axis=0).astype(BF16)
                vt_s[2 * p + 1, j] = jnp.concatenate([vt[HEAD_DIM:], ones_row], axis=0).astype(BF16)

    jidx = lax.broadcasted_iota(jnp.int32, (16, blk), 0)
    key = lax.broadcasted_iota(jnp.int32, (blk, blk), 0)
    qry = lax.broadcasted_iota(jnp.int32, (blk, blk), 1)
    qps = {}
    for t in range(MOBA_BPS):
        for p in pairs:
            qps[t, p] = _transpose_bf16(q_ref[t * blk:(t + 1) * blk, pair_cols(p)].astype(BF16),
                                        LANES).astype(BF16)
    even_lanes = lax.broadcasted_iota(jnp.int32, (16, LANES), 1) < HEAD_DIM
    gates = {}
    for t, hh in items:
        km = jnp.where(even_lanes == (hh % 2 == 0), km_s[hh // 2], 0.0)
        kmh, kml = _split_bf16(km)
        gates[t, hh] = _dot(kmh, qps[t, hh // 2]) + _dot(kml, qps[t, hh // 2])
    qaugs = {}
    filler = jnp.zeros((LANES - HEAD_DIM - 16, blk), BF16)
    for t, hh in items:
        past = jidx < ns[t]
        gate = jnp.where(past, gates[t, hh], NEG)
        sel = (past & _rank_select(gate, jidx, MOBA_TOPK)) | (jidx == ns[t])
        selb = jnp.where(sel, 0.0, NEG).astype(BF16)
        qp = qps[t, hh // 2]
        qaugs[t, hh] = (jnp.concatenate([qp[0:HEAD_DIM], selb, filler], axis=0) if hh % 2 == 0 else
                        jnp.concatenate([selb, filler, qp[HEAD_DIM:]], axis=0))

    def scores(t, hh, j):
        start = pl.multiple_of(j * blk, blk)
        return _dot(kaug_s[hh, pl.ds(start, blk), :], qaugs[t, hh]), vt_s[hh, j]

    near = {(t, hh): (scores(t, hh, ns[t]), scores(t, hh, jnp.maximum(ns[t] - 1, 0))) for t, hh in items}
    state = {}
    for t, hh in items:
        (s0, vt0), (s1, vt1) = near[t, hh]
        s0 = jnp.where(key <= qry, s0, NEG) + t0_ref[hh]
        s1 = jnp.where(ns[t] >= 1, s1, NEG) + t1_ref[hh]
        state[t, hh] = _softmax_joint([(s0, vt0), (s1, vt1)])

    older = [jnp.maximum(n - 1, 0) for n in ns]
    lo = 0
    for first in range(MOBA_BPS):
        live = [(t, hh) for t, hh in items if t >= first]

        def far_body(j, carry, live=live):
            tiles = [scores(t, hh, j) for t, hh in live]
            return tuple(_softmax_update(c, s, [vt]) for c, (s, vt) in zip(carry, tiles))

        new = lax.fori_loop(lo, older[first], far_body, tuple(state[it] for it in live))
        state.update(dict(zip(live, new)))
        lo = older[first]

    for t in range(MOBA_BPS):
        o_ref[t * blk:(t + 1) * blk, :] = jnp.concatenate(
            [_softmax_finish(state[t, hh]) for hh in heads], axis=0).T.astype(BF16)


def _moba(proj, t0, t1, batch, seq):
    blk = MOBA_BLOCK
    nb = seq // blk
    width = MOBA_HPS * HEAD_DIM
    rows = MOBA_BPS * blk
    steps = nb // MOBA_BPS
    qc, kc, vc = C_MQ // width, C_MK // width, C_MV // width
    return pl.pallas_call(
        functools.partial(_moba_kernel, seq=seq),
        grid=(batch, MOBA_HEADS // MOBA_HPS, steps),
        in_specs=[pl.BlockSpec((rows, width), lambda b, h, n: (b * steps + n, qc + h)),
                  pl.BlockSpec((seq, width), lambda b, h, n: (b, kc + h)),
                  pl.BlockSpec((seq, width), lambda b, h, n: (b, vc + h)),
                  pl.BlockSpec((MOBA_HPS, blk, blk), lambda b, h, n: (h, 0, 0)),
                  pl.BlockSpec((MOBA_HPS, blk, blk), lambda b, h, n: (h, 0, 0))],
        out_specs=pl.BlockSpec((rows, width), lambda b, h, n: (b * steps + n, h)),
        out_shape=jax.ShapeDtypeStruct((batch * seq, MOBA_HEADS * HEAD_DIM), BF16),
        scratch_shapes=[pltpu.VMEM((MOBA_HPS, seq, LANES), BF16),
                        pltpu.VMEM((MOBA_HPS, nb, VROWS, blk), BF16),
                        pltpu.VMEM((MOBA_HPS // 2, 16, LANES), F32)],
        compiler_params=_params(("parallel", "parallel", "arbitrary")),
        name="moba",
    )(proj, proj, proj, t0, t1)


def _compress_kernel(k_ref, v_ref, w1k_ref, w1v_ref, wtk_ref, wtv_ref, w2k_ref, w2v_ref, pk_ref, pv_ref,
                     kg_ref, ko_ref, vo_ref):
    hid = NSA_CMP_HIDDEN
    chunks = k_ref.shape[0] // NSA_CMP_STRIDE

    def chunk_rows(t_ref):
        return jnp.concatenate([t_ref[pl.ds(l, chunks, stride=NSA_CMP_STRIDE), :].astype(BF16)
                                for l in range(NSA_CMP_STRIDE)], axis=1)

    def compress(tok, w1_ref, wt_ref, w2_ref, pos_ref, g):
        a = _dot(tok, wt_ref[g])
        pw = _dot(pos_ref[...].astype(BF16), w1_ref[...])
        pos = pw[0:1, :hid] + pw[1:2, hid:]
        nxt = pltpu.roll(a[:, hid:], chunks - 1, 0)
        h = jax.nn.gelu(a[:, :hid] + nxt + pos)
        return _dot(h.astype(BF16), w2_ref[...])

    ktok, vtok = chunk_rows(k_ref), chunk_rows(v_ref)
    for g in range(NSA_KV_GROUPS):
        kc = compress(ktok, w1k_ref, wtk_ref, w2k_ref, pk_ref, g)
        ko_ref[0, g] = _rms(kc, kg_ref[...]).astype(BF16)
        vc = compress(vtok, w1v_ref, wtv_ref, w2v_ref, pv_ref, g).astype(BF16)
        vo_ref[0, g] = _transpose_bf16(vc, HEAD_DIM).astype(BF16)


def _compress(proj, w1k, w1v, wtk, wtv, w2k, w2v, pk, pv, kg, layer, batch, seq):
    groups = NSA_KV_GROUPS
    chunks = seq // NSA_CMP_STRIDE
    tok = lambda col: pl.BlockSpec((seq, LANES), lambda b: (b, col // LANES))
    return pl.pallas_call(
        _compress_kernel,
        grid=(batch,),
        in_specs=[tok(C_KC), tok(C_VC)] + [_layer_spec(a, layer)
                                           for a in (w1k, w1v, wtk, wtv, w2k, w2v, pk, pv, kg)],
        out_specs=[pl.BlockSpec((1, groups, chunks, HEAD_DIM), lambda b: (b, 0, 0, 0)),
                   pl.BlockSpec((1, groups, HEAD_DIM, chunks), lambda b: (b, 0, 0, 0))],
        out_shape=[jax.ShapeDtypeStruct((batch, groups, chunks, HEAD_DIM), BF16),
                   jax.ShapeDtypeStruct((batch, groups, HEAD_DIM, chunks), BF16)],
        compiler_params=_params(("parallel",)),
        name="nsa_compress",
    )(proj, proj, w1k, w1v, wtk, wtv, w2k, w2v, pk, pv, kg)


def _nsa_kernel(q_ref, kc_ref, vc_ref, ksl_ref, vsl_ref, kwn_ref, vwn_ref, gt_ref,
                wc_ref, t0_ref, t1_ref, o_ref, kslaug_s, vslt_s, kwn_s, vwnt_s, sc_s, *, seq):
    step = pl.program_id(1)
    tq = NSA_TQ
    hpg = NSA_HPG
    sb = NSA_SEL_BLOCK
    nsel = seq // sb
    ncmp = seq // NSA_CMP_STRIDE
    nwin = NSA_WINDOW // tq
    groups = range(NSA_KV_GROUPS)
    items = [(t, g) for t in range(NSA_TPS) for g in groups]
    qis = [step * NSA_TPS + t for t in range(NSA_TPS)]

    @pl.when(step == 0)
    def _prepare_keys():
        r = lax.broadcasted_iota(jnp.int32, (seq, LANES), 0)
        c = lax.broadcasted_iota(jnp.int32, (seq, LANES), 1)
        onehot = (c - HEAD_DIM == r // sb).astype(F32)
        place = _eye(HEAD_DIM, LANES)
        sc_s[:, 0:CMP_PAD, :] = jnp.zeros((NSA_TPS * NSA_KV_GROUPS, CMP_PAD, hpg * tq), F32)
        for g in groups:
            sl = slice(g * HEAD_DIM, (g + 1) * HEAD_DIM)
            kslaug_s[g] = (_dot(ksl_ref[:, sl].astype(BF16), place) + onehot).astype(BF16)
            kwn_s[g] = kwn_ref[:, sl].astype(BF16)
            for j in range(seq // tq):
                rows = slice(j * tq, (j + 1) * tq)
                vslt_s[g, j] = _value_tile(vsl_ref[rows, sl].astype(BF16))
                vwnt_s[g, j] = _value_tile(vwn_ref[rows, sl].astype(BF16))

    lanes = lambda parts: jnp.concatenate(parts, axis=1)
    key = lax.broadcasted_iota(jnp.int32, (tq, tq), 0)
    qry = lax.broadcasted_iota(jnp.int32, (tq, tq), 1)
    causal = lanes([key <= qry] * hpg)
    upper = lanes([key > qry] * hpg)
    oj = lax.broadcasted_iota(jnp.int32, (nsel, ncmp), 0) * sb
    on = lax.broadcasted_iota(jnp.int32, (nsel, ncmp), 1) * NSA_CMP_STRIDE
    overlap = ((on < oj + sb) & (on + NSA_CMP_LEN > oj) & (on < seq - NSA_CMP_STRIDE)).astype(BF16)
    jidx = lax.broadcasted_iota(jnp.int32, (nsel, tq), 0)
    cend = lax.broadcasted_iota(jnp.int32, (ncmp, tq), 0) * NSA_CMP_STRIDE + (NSA_CMP_LEN - 1)
    cvis = [lanes([cend <= lax.broadcasted_iota(jnp.int32, (ncmp, tq), 1) + qi * tq] * hpg) for qi in qis]
    own = [(lax.broadcasted_iota(jnp.int32, (nsel, tq), 1) + qi * tq) // sb for qi in qis]
    tile_rows = lambda j: pl.ds(pl.multiple_of(j * tq, tq), tq)
    back = lambda t, k: jnp.maximum(qis[t] - k, 0)
    heads = [[g * hpg + r for r in range(hpg)] for g in groups]
    t0 = [lanes([t0_ref[h] for h in heads[g]]) for g in groups]
    t1 = [lanes([t1_ref[h] for h in heads[g]]) for g in groups]
    slot = lambda t, g: t * NSA_KV_GROUPS + g

    qplain, win_scores = {}, {}
    for t, g in items:
        width = hpg * HEAD_DIM
        qg = _transpose_bf16(q_ref[t * tq:(t + 1) * tq, g * width:(g + 1) * width].astype(BF16), width)
        qplain[t, g] = lanes([qg[r * HEAD_DIM:(r + 1) * HEAD_DIM] for r in range(hpg)]).astype(BF16)
        sc_s[slot(t, g), CMP_PAD:, :] = _dot(kc_ref[0, g], qplain[t, g])
        win_scores[t, g] = [_dot(kwn_s[g, tile_rows(back(t, k)), :], qplain[t, g]) for k in range(nwin + 1)]

    o_cmps, imps = {}, {}
    for t, g in items:
        win = pl.ds(pl.multiple_of(qis[t] * (tq // NSA_CMP_STRIDE), 8), CMP_WIN)
        sc_s[slot(t, g), win, :] += lanes([wc_ref[h] for h in heads[g]])
        s = jnp.where(cvis[t], sc_s[slot(t, g), CMP_PAD:, :], NEG)
        m = jnp.max(s, axis=0, keepdims=True)
        e = jnp.where(cvis[t], jnp.exp2(s - m), 0.0)
        den = jnp.sum(e, axis=0, keepdims=True)
        p = e * (1.0 / jnp.where(den > 0, den, 1.0))
        o_cmps[t, g] = _dot(vc_ref[0, g], p.astype(BF16))
        psum = p[:, 0:tq]
        for r in range(1, hpg):
            psum = psum + p[:, r * tq:(r + 1) * tq]
        ph, plo = _split_bf16(psum)
        imps[t, g] = _dot(overlap, ph) + _dot(overlap, plo)

    o_wins = {}
    for t, g in items:
        qi = qis[t]
        tiles = []
        for k, s in enumerate(win_scores[t, g]):
            if k == 0:
                s = jnp.where(causal, s, NEG) + t0[g]
            elif k == 1:
                s = jnp.where(qi >= 1, s, NEG) + t1[g]
            elif k < nwin:
                s = jnp.where(qi >= k, s, NEG)
            else:
                s = jnp.where(upper & (qi >= k), s, NEG)
            tiles.append((s, vwnt_s[g, back(t, k)]))
        o_wins[t, g] = _softmax_finish(_softmax_joint(tiles))

    qaugs = {}
    for t, g in items:
        imp = jnp.where((jidx == 0) | (jidx == own[t]) | (jidx == own[t] - 1), -NEG, imps[t, g])
        imp = jnp.where(jidx > own[t], NEG, imp)
        sel = _rank_select(imp, jidx, NSA_SEL_TOPN) & (jidx <= own[t])
        selb = jnp.where(sel, 0.0, NEG).astype(BF16)
        qaugs[t, g] = jnp.concatenate([qplain[t, g], lanes([selb] * hpg),
                                       jnp.zeros((LANES - HEAD_DIM - nsel, hpg * tq), BF16)], axis=0)

    assert NSA_TPS % 2 == 0
    near_tiles = lambda t: 3 if t % 2 == 0 else 2
    slc_near = {(t, g): [_dot(kslaug_s[g, tile_rows(back(t, k)), :], qaugs[t, g])
                         for k in range(near_tiles(t))] for t, g in items}
    state = {}
    for t, g in items:
        qi = qis[t]
        tiles = [(jnp.where(causal, slc_near[t, g][0], NEG) + t0[g], vslt_s[g, qi]),
                 (jnp.where(qi >= 1, slc_near[t, g][1], NEG) + t1[g], vslt_s[g, back(t, 1)])]
        if near_tiles(t) == 3:
            tiles.append((jnp.where(qi >= 2, slc_near[t, g][2], NEG), vslt_s[g, back(t, 2)]))
        state[t, g] = _softmax_joint(tiles)

    slabs = [jnp.maximum(qi - 1, 0) // 2 for qi in qis]
    lo = 0
    for first in range(NSA_TPS):
        live = [(t, g) for t, g in items if t >= first]

        def slc_far(i, carry, live=live):
            slab = pl.ds(pl.multiple_of(i * (2 * tq), 2 * tq), 2 * tq)
            scores = [_dot(kslaug_s[g, slab, :], qaugs[t, g]) for t, g in live]
            return tuple(_softmax_update(c, sc, [vslt_s[g, 2 * i], vslt_s[g, 2 * i + 1]])
                         for c, sc, (t, g) in zip(carry, scores, live))

        new = lax.fori_loop(lo, slabs[first], slc_far, tuple(state[it] for it in live))
        state.update(dict(zip(live, new)))
        lo = slabs[first]

    for t in range(NSA_TPS):
        gates = jax.nn.sigmoid(gt_ref[t * tq:(t + 1) * tq, :]).T
        outs = []
        for g in groups:
            o_slc = _softmax_finish(state[t, g])
            for r in range(hpg):
                h = g * hpg + r
                cols = slice(r * tq, (r + 1) * tq)
                outs.append(gates[h:h + 1, :] * o_cmps[t, g][:, cols]
                            + gates[NSA_HEADS + h:NSA_HEADS + h + 1, :] * o_slc[:, cols]
                            + gates[2 * NSA_HEADS + h:2 * NSA_HEADS + h + 1, :] * o_wins[t, g][:, cols])
        o_ref[t * tq:(t + 1) * tq, :] = jnp.concatenate(outs, axis=0).T.astype(BF16)


def _nsa(proj, kcn, vct, wc, t0, t1, batch, seq):
    tq = NSA_TQ
    nq = seq // tq
    rows = NSA_TPS * tq
    steps = nq // NSA_TPS
    ncmp = seq // NSA_CMP_STRIDE
    width = NSA_HEADS * HEAD_DIM
    groups = NSA_KV_GROUPS
    kv = lambda col: pl.BlockSpec((seq, LANES), lambda b, i: (b, col // LANES))
    const = lambda shape: pl.BlockSpec(shape, lambda b, i: (0,) * len(shape))
    return pl.pallas_call(
        functools.partial(_nsa_kernel, seq=seq),
        grid=(batch, steps),
        in_specs=[pl.BlockSpec((rows, width), lambda b, i: (b * steps + i, C_NSQ // width)),
                  pl.BlockSpec((1, groups, ncmp, HEAD_DIM), lambda b, i: (b, 0, 0, 0)),
                  pl.BlockSpec((1, groups, HEAD_DIM, ncmp), lambda b, i: (b, 0, 0, 0)),
                  kv(C_KSL), kv(C_VSL), kv(C_KWN), kv(C_VWN),
                  pl.BlockSpec((rows, LANES), lambda b, i: (b * steps + i, C_NG // LANES)),
                  const((NSA_HEADS, CMP_WIN, tq)), const((NSA_HEADS, tq, tq)),
                  const((NSA_HEADS, tq, tq))],
        out_specs=pl.BlockSpec((rows, width), lambda b, i: (b * steps + i, 0)),
        out_shape=jax.ShapeDtypeStruct((batch * seq, width), BF16),
        scratch_shapes=[pltpu.VMEM((groups, seq, LANES), BF16),
                        pltpu.VMEM((groups, nq, VROWS, tq), BF16),
                        pltpu.VMEM((groups, seq, HEAD_DIM), BF16),
                        pltpu.VMEM((groups, nq, VROWS, tq), BF16),
                        pltpu.VMEM((NSA_TPS * groups, CMP_PAD + ncmp, NSA_HPG * tq), F32)],
        compiler_params=_params(("parallel", "arbitrary")),
        name="nsa",
    )(proj, kcn, vct, proj, proj, proj, proj, proj, wc, t0, t1)


def _merge_kernel(x_ref, ya_ref, yb_ref, ga_ref, gb_ref, wa_ref, wb_ref, wo_ref, o_ref):
    a = _dot(ya_ref[...], wa_ref[...])
    b = _dot(yb_ref[...], wb_ref[...])
    z = jax.nn.sigmoid(ga_ref[...]) * a + jax.nn.sigmoid(gb_ref[...]) * b
    o_ref[...] = x_ref[...] + _dot(z.astype(BF16), wo_ref[...])


def _merge(xf, ya, yb, proj, wa, wb, wo, layer):
    n, d = xf.shape
    tm = MERGE_TM
    row = lambda w, col=0: pl.BlockSpec((tm, w), lambda i: (i, col))
    return pl.pallas_call(
        _merge_kernel,
        grid=(n // tm,),
        in_specs=[row(d), row(ya.shape[1]), row(yb.shape[1]), row(d, C_GA // d), row(d, C_GB // d),
                  _layer_spec(wa, layer), _layer_spec(wb, layer), _layer_spec(wo, layer)],
        out_specs=row(d),
        out_shape=jax.ShapeDtypeStruct((n, d), F32),
        compiler_params=_params(("parallel",)),
        name="merge",
    )(xf, ya, yb, proj, proj, wa, wb, wo)


FFN_HALO = 16
FFN_TM = 1024
FFN_TF = 256


def _ffn_kernel(x_ref, xh_ref, g_ref, wu_ref, cw_ref, cb_ref, wd_ref, p_ref, wg_ref, wp_ref,
                o_ref, act_s, *, seq, d_ff):
    i = pl.program_id(0)
    tm = x_ref.shape[0]
    x = x_ref[...]
    at_start = (i * tm) % seq == 0
    halo = jnp.where(at_start, 0.0, _rms(xh_ref[...], g_ref[...]))
    hn = jnp.concatenate([halo.astype(BF16), _rms(x, g_ref[...]).astype(BF16)], axis=0)

    def conv(cols):
        u = _dot(hn, wu_ref[:, cols])
        u1 = pltpu.roll(u, 1, 0)[FFN_HALO:]
        u2 = pltpu.roll(u, 2, 0)[FFN_HALO:]
        cw = cw_ref[:, cols]
        return cw[0:1] * u2 + cw[1:2] * u1 + cw[2:3] * u[FFN_HALO:] + cb_ref[:, cols]

    for c in range(d_ff // FFN_TF):
        lo = c * FFN_TF
        act = jax.nn.gelu(conv(slice(lo, lo + FFN_TF))) * conv(slice(d_ff + lo, d_ff + lo + FFN_TF))
        act_s[:, lo:lo + FFN_TF] = act.astype(BF16)

    x = x + _dot(act_s[...], wd_ref[...])
    gate = jax.nn.sigmoid(_dot(x.astype(BF16), wg_ref[...]))
    o_ref[...] = x + gate * _dot(p_ref[...].astype(BF16), wp_ref[...])


def _ffn_ple(xf, gain, w_up, conv_w, conv_b, w_down, pf, wg, wp, layer, seq):
    n, d = xf.shape
    d_ff = w_down.shape[1]
    tm = FFN_TM
    hb = tm // FFN_HALO
    resident = lambda a: _layer_spec(a, layer)
    return pl.pallas_call(
        functools.partial(_ffn_kernel, seq=seq, d_ff=d_ff),
        grid=(n // tm,),
        in_specs=[pl.BlockSpec((tm, d), lambda i: (i, 0)),
                  pl.BlockSpec((FFN_HALO, d), lambda i: (jnp.maximum(i * hb - 1, 0), 0)),
                  resident(gain), resident(w_up), resident(conv_w), resident(conv_b), resident(w_down),
                  pl.BlockSpec((None, tm, pf.shape[2]), lambda i: (layer, i, 0)),
                  resident(wg), resident(wp)],
        out_specs=pl.BlockSpec((tm, d), lambda i: (i, 0)),
        out_shape=jax.ShapeDtypeStruct((n, d), F32),
        scratch_shapes=[pltpu.VMEM((tm, d_ff), BF16)],
        compiler_params=_params(("parallel",)),
        name="conv_ffn_ple",
    )(xf, xf, gain, w_up, conv_w, conv_b, w_down, pf, wg, wp)


def _reorder_kernel(w_ref, o_ref):
    attn = 3 * MOBA_HEADS * HEAD_DIM + NSA_HEADS * HEAD_DIM + 6 * NSA_KV_GROUPS * HEAD_DIM
    ng = 3 * NSA_HEADS
    w = w_ref[...]
    pad = jnp.zeros((PROJ_COLS - w.shape[0], w.shape[1]), w.dtype)
    o_ref[...] = jnp.concatenate([w[attn + ng:], w[:attn], w[attn:attn + ng], pad], axis=0).astype(BF16)


def _reorder_in_proj(w):
    depth, d, cols = w.shape
    tc = 256
    return pl.pallas_call(
        _reorder_kernel,
        grid=(depth, d // tc),
        in_specs=[pl.BlockSpec((None, cols, tc), lambda l, i: (l, 0, i))],
        out_specs=pl.BlockSpec((None, PROJ_COLS, tc), lambda l, i: (l, 0, i)),
        out_shape=jax.ShapeDtypeStruct((depth, PROJ_COLS, d), BF16),
        compiler_params=_params(("parallel", "parallel")),
        name="reorder_in_proj",
    )(jnp.swapaxes(w, 1, 2))


def _head_gain_rows(moba_q, moba_k, nsa_q, nsa_k):
    rows = jnp.zeros((moba_q.shape[0], 1, PROJ_COLS), F32)
    for col, gain, heads in ((C_MQ, moba_q * QSCALE, MOBA_HEADS), (C_MK, moba_k, MOBA_HEADS),
                             (C_NSQ, nsa_q * QSCALE, NSA_HEADS), (C_KSL, nsa_k[:, 1], NSA_KV_GROUPS),
                             (C_KWN, nsa_k[:, 2], NSA_KV_GROUPS)):
        rows = rows.at[:, 0, col:col + heads * HEAD_DIM].set(jnp.tile(gain, (1, heads)))
    return rows


def _cmp_weights(w1, pos):
    depth = w1.shape[0]
    half = NSA_CMP_STRIDE * HEAD_DIM
    w1cat = jnp.concatenate([w1[:, :half], w1[:, half:]], axis=2).astype(BF16)
    tok = w1cat.reshape(depth, NSA_CMP_STRIDE, HEAD_DIM, -1)
    zero = jnp.zeros_like(tok)
    placed = jnp.stack([jnp.concatenate([tok, zero], axis=2), jnp.concatenate([zero, tok], axis=2)], axis=1)
    placed = placed.reshape(depth, NSA_KV_GROUPS, NSA_CMP_STRIDE * LANES, -1)
    posr = jnp.zeros((depth, 8, half), F32).at[:, 0:2].set(pos.reshape(depth, 2, half))
    return w1cat, placed, posr


def kernel(x, p, rel_bias, attn_norm, w_in, moba_q_gain, moba_k_gain, nsa_q_gain, nsa_k_gain,
           cmp_pos_k, cmp_w1_k, cmp_w2_k, cmp_pos_v, cmp_w1_v, cmp_w2_v,
           w_br_moba, w_br_nsa, w_o, ffn_norm, w_up, conv_w, conv_b, w_down, w_ple_gate, w_ple):
    batch, seq, d = x.shape
    n = batch * seq
    depth = w_in.shape[0]
    bf = lambda a: a.astype(BF16)
    w_in_r = _reorder_in_proj(w_in)
    head_gain = _head_gain_rows(moba_q_gain, moba_k_gain, nsa_q_gain, nsa_k_gain)
    attn_gain, ffn_gain = attn_norm[:, None, :], ffn_norm[:, None, :]
    w1k, wtk, pk = _cmp_weights(cmp_w1_k, cmp_pos_k)
    w1v, wtv, pv = _cmp_weights(cmp_w1_v, cmp_pos_v)
    w2k, w2v, kg0 = bf(cmp_w2_k), bf(cmp_w2_v), nsa_k_gain[:, 0:1]
    wa, wb, wo = bf(w_br_moba), bf(w_br_nsa), bf(w_o)
    wu, wd, wg, wp = bf(w_up), bf(w_down), bf(w_ple_gate), bf(w_ple)
    conv_b = conv_b[:, None, :]
    pf = p.reshape(depth, n, -1)

    t0m, t1m, t0n, t1n, wc = _bias_tables(rel_bias)
    xf = x.reshape(n, d)
    for i in range(depth):
        proj = _inproj(xf, attn_gain, w_in_r, head_gain, i)
        ya = _moba(proj, t0m, t1m, batch, seq)
        kcn, vct = _compress(proj, w1k, w1v, wtk, wtv, w2k, w2v, pk, pv, kg0, i, batch, seq)
        yb = _nsa(proj, kcn, vct, wc, t0n, t1n, batch, seq)
        xf = _merge(xf, ya, yb, proj, wa, wb, wo, i)
        xf = _ffn_ple(xf, ffn_gain, wu, conv_w, conv_b, wd, pf, wg, wp, i, seq)
    return xf.reshape(batch, seq, d)
```

```python
import functools
import math

import numpy as np
import jax
import jax.numpy as jnp
from jax import lax
from jax.experimental import pallas as pl
from jax.experimental.pallas import tpu as pltpu

F32 = jnp.float32
BF16 = jnp.bfloat16

HEAD_DIM = 64
MOBA_HEADS = 8
MOBA_BLOCK = 256
MOBA_TOPK = 3
NSA_HEADS = 8
NSA_KV_GROUPS = 2
NSA_HPG = NSA_HEADS // NSA_KV_GROUPS
NSA_CMP_LEN = 32
NSA_CMP_STRIDE = 16
NSA_CMP_HIDDEN = 2 * HEAD_DIM
NSA_SEL_BLOCK = 64
NSA_SEL_TOPN = 16
NSA_WINDOW = 512
REL_BUCKETS = 32
REL_MAX_DIST = 128
RMS_EPS = 1e-6
SCALE = HEAD_DIM ** -0.5
LOG2E = math.log2(math.e)
QSCALE = SCALE * LOG2E
NEG = -1e30

LANES = 128
V7X_VMEM_BYTES = 64 * 1024 * 1024
VMEM_LIMIT = V7X_VMEM_BYTES * 7 // 8

INPROJ_TM = 512

C_GA, C_GB, C_MQ, C_MK, C_MV, C_NSQ = 0, 1024, 2048, 2560, 3072, 3584
C_KC, C_VC, C_KSL, C_VSL, C_KWN, C_VWN, C_NG = 4096, 4224, 4352, 4480, 4608, 4736, 4864
PROJ_COLS = 4992

NSA_TQ = 128
NSA_TPS = 4
CMP_WIN = 16
CMP_PAD = 8


def _dot(a, b):
    return jnp.dot(a, b, preferred_element_type=F32)


def _dot_nt(a, b):
    return lax.dot_general(a, b, (((1,), (1,)), ((), ())), preferred_element_type=F32)


def _rms(x, gain):
    return x * lax.rsqrt(jnp.mean(x * x, axis=-1, keepdims=True) + RMS_EPS) * gain


def _split_bf16(x):
    hi = x.astype(BF16)
    return hi, (x - hi.astype(F32)).astype(BF16)


def _eye(rows, cols):
    r = lax.broadcasted_iota(jnp.int32, (rows, cols), 0)
    c = lax.broadcasted_iota(jnp.int32, (rows, cols), 1)
    return (r == c).astype(BF16)


def _transpose_bf16(x, rows):
    return _dot_nt(_eye(rows, x.shape[1]), x)


VROWS = 80


def _value_tile(v):
    row = lax.broadcasted_iota(jnp.int32, (VROWS, v.shape[0]), 0)
    return jnp.where(row == HEAD_DIM, 1.0, _transpose_bf16(v, VROWS)).astype(BF16)


def _softmax_joint(tiles):
    m = functools.reduce(jnp.maximum, [jnp.max(s, axis=0, keepdims=True) for s, _ in tiles])
    acc = functools.reduce(jnp.add, [_dot(vt, jnp.exp2(s - m).astype(BF16)) for s, vt in tiles])
    return m, acc


def _softmax_update(carry, s, vts):
    m, acc = carry
    m_new = jnp.maximum(m, jnp.max(s, axis=0, keepdims=True))
    p = jnp.exp2(s - m_new).astype(BF16)
    rows = s.shape[0] // len(vts)
    pv = functools.reduce(jnp.add, [_dot(vt, p[i * rows:(i + 1) * rows]) for i, vt in enumerate(vts)])
    return m_new, jnp.exp2(m - m_new) * acc + pv


def _softmax_finish(carry):
    _, acc = carry
    return acc[0:HEAD_DIM] * (1.0 / acc[HEAD_DIM:HEAD_DIM + 1])


def _rank_select(score, idx, count):
    beaten = jnp.zeros(score.shape, jnp.int32)
    for i in range(score.shape[0]):
        si = score[i:i + 1, :]
        beaten += ((si > score) | ((si == score) & (i < idx))).astype(jnp.int32)
    return beaten < count


def _params(sem):
    return pltpu.CompilerParams(dimension_semantics=sem, vmem_limit_bytes=VMEM_LIMIT)


def _layer_spec(a, layer):
    return pl.BlockSpec((None,) + a.shape[1:], lambda *_: (layer,) + (0,) * (a.ndim - 1),
                        pipeline_mode=pl.Buffered(1))


def _rel_bucket_np(dist):
    n = np.maximum(dist, 0)
    max_exact = REL_BUCKETS // 2
    nf = np.maximum(n, 1).astype(np.float32)
    large = max_exact + (np.log(nf / np.float32(max_exact)) / np.float32(math.log(REL_MAX_DIST / max_exact))
                         * np.float32(REL_BUCKETS - max_exact)).astype(np.int32)
    return np.where(n < max_exact, n, np.minimum(large, REL_BUCKETS - 1))


def _bucket_starts():
    buckets = _rel_bucket_np(np.arange(4 * REL_MAX_DIST))
    return [int(np.argmax(buckets >= k)) for k in range(REL_BUCKETS)]


BUCKET_START = _bucket_starts()
BIAS_REACH = BUCKET_START[-1]
assert BIAS_REACH <= MOBA_BLOCK and BIAS_REACH <= NSA_TQ - NSA_CMP_LEN + 1 + NSA_CMP_STRIDE


def _tables_kernel(tab_ref, t0m_ref, t1m_ref, t0n_ref, t1n_ref, wc_ref):
    h = pl.program_id(0)

    def bias(dist, head):
        last = tab_ref[head, REL_BUCKETS - 1]
        val = jnp.zeros(dist.shape, F32)
        for k in range(REL_BUCKETS - 2, -1, -1):
            val = jnp.where(dist < BUCKET_START[k + 1], (tab_ref[head, k] - last) * LOG2E, val)
        return val

    def toeplitz(size, offset, head):
        key = lax.broadcasted_iota(jnp.int32, (size, size), 0)
        qry = lax.broadcasted_iota(jnp.int32, (size, size), 1)
        return bias(offset + qry - key, head)

    t0m_ref[0] = toeplitz(MOBA_BLOCK, 0, h)
    t1m_ref[0] = toeplitz(MOBA_BLOCK, MOBA_BLOCK, h)
    t0n_ref[0] = toeplitz(NSA_TQ, 0, MOBA_HEADS + h)
    t1n_ref[0] = toeplitz(NSA_TQ, NSA_TQ, MOBA_HEADS + h)
    a = lax.broadcasted_iota(jnp.int32, (CMP_WIN, NSA_TQ), 0)
    i = lax.broadcasted_iota(jnp.int32, (CMP_WIN, NSA_TQ), 1)
    wc_ref[0] = bias(i + (NSA_TQ - NSA_CMP_LEN + 1) - NSA_CMP_STRIDE * a, MOBA_HEADS + h)


def _bias_tables(rel_bias):
    blk, tq = MOBA_BLOCK, NSA_TQ
    shapes = [(blk, blk), (blk, blk), (tq, tq), (tq, tq), (CMP_WIN, tq)]
    return pl.pallas_call(
        _tables_kernel,
        grid=(MOBA_HEADS,),
        in_specs=[pl.BlockSpec(memory_space=pltpu.SMEM)],
        out_specs=[pl.BlockSpec((1,) + s, lambda h: (h, 0, 0)) for s in shapes],
        out_shape=[jax.ShapeDtypeStruct((MOBA_HEADS,) + s, F32) for s in shapes],
        compiler_params=_params(("arbitrary",)),
        name="bias_tables",
    )(rel_bias)


NORM_SLABS = (list(range(C_MQ, C_MV, LANES)) + list(range(C_NSQ, C_KC, LANES)) + [C_KSL, C_KWN])


def _inproj_kernel(x_ref, g_ref, w_ref, hg_ref, o_ref):
    h = _rms(x_ref[...], g_ref[...]).astype(BF16)
    y = _dot_nt(h, w_ref[...])
    first = lax.broadcasted_iota(jnp.int32, (1, LANES), 1) < HEAD_DIM
    edges = sorted(set([0, PROJ_COLS] + NORM_SLABS + [c + LANES for c in NORM_SLABS]))
    for lo, hi in zip(edges[:-1], edges[1:]):
        t = y[:, lo:hi]
        if lo in NORM_SLABS:
            sq = t * t
            s0 = jnp.sum(jnp.where(first, sq, 0.0), axis=-1, keepdims=True)
            s1 = jnp.sum(jnp.where(first, 0.0, sq), axis=-1, keepdims=True)
            ms = jnp.where(first, s0, s1) * (1.0 / HEAD_DIM)
            t = t * lax.rsqrt(ms + RMS_EPS) * hg_ref[:, lo:hi]
        o_ref[:, lo:hi] = t


def _inproj(xf, gain, w, head_gain, layer):
    n, d = xf.shape
    tm = INPROJ_TM
    return pl.pallas_call(
        _inproj_kernel,
        grid=(n // tm,),
        in_specs=[pl.BlockSpec((tm, d), lambda i: (i, 0)), _layer_spec(gain, layer),
                  _layer_spec(w, layer), _layer_spec(head_gain, layer)],
        out_specs=pl.BlockSpec((tm, PROJ_COLS), lambda i: (i, 0)),
        out_shape=jax.ShapeDtypeStruct((n, PROJ_COLS), F32),
        compiler_params=_params(("parallel",)),
        name="inproj",
    )(xf, gain, w, head_gain)


MOBA_HPS = 8
MOBA_BPS = 2


def _moba_kernel(q_ref, k_ref, v_ref, t0_ref, t1_ref, o_ref, kaug_s, vt_s, km_s, *, seq):
    step = pl.program_id(2)
    blk = MOBA_BLOCK
    nb = seq // blk
    heads = range(MOBA_HPS)
    pairs = range(MOBA_HPS // 2)
    pair_cols = lambda p: slice(p * LANES, (p + 1) * LANES)
    items = [(t, hh) for t in range(MOBA_BPS) for hh in heads]
    ns = [step * MOBA_BPS + t for t in range(MOBA_BPS)]

    @pl.when(step == 0)
    def _prepare_keys():
        block = lax.broadcasted_iota(jnp.int32, (seq, LANES), 0) // blk
        lane = lax.broadcasted_iota(jnp.int32, (seq, LANES), 1)
        code_hi = (lane - HEAD_DIM == block).astype(BF16)
        code_lo = (lane == block).astype(BF16)
        ones_row = (lax.broadcasted_iota(jnp.int32, (VROWS - HEAD_DIM, blk), 0) == 0).astype(F32)
        for p in pairs:
            kn = k_ref[:, pair_cols(p)]
            km = kn.reshape(nb, blk, LANES).sum(axis=1) * (1.0 / blk)
            km_s[p] = jnp.concatenate([km, jnp.zeros((16 - nb, LANES), F32)], axis=0)
            kaug_s[2 * p] = jnp.where(lane < HEAD_DIM, kn.astype(BF16), code_hi)
            kaug_s[2 * p + 1] = jnp.where(lane >= HEAD_DIM, kn.astype(BF16), code_lo)
            for j in range(nb):
                vt = _transpose_bf16(v_ref[j * blk:(j + 1) * blk, pair_cols(p)].astype(BF16), LANES)
                vt_s[2 * p, j] = jnp.concatenate([vt[0:HEAD_DIM], ones_row], axis=0).astype(BF16)
                vt_s[2 * p + 1, j] = jnp.concatenate([vt[HEAD_DIM:], ones_row], axis=0).astype(BF16)

    jidx = lax.broadcasted_iota(jnp.int32, (16, blk), 0)
    key = lax.broadcasted_iota(jnp.int32, (blk, blk), 0)
    qry = lax.broadcasted_iota(jnp.int32, (blk, blk), 1)
    qps = {}
    for t in range(MOBA_BPS):
        for p in pairs:
            qps[t, p] = _transpose_bf16(q_ref[t * blk:(t + 1) * blk, pair_cols(p)].astype(BF16),
                                        LANES).astype(BF16)
    even_lanes = lax.broadcasted_iota(jnp.int32, (16, LANES), 1) < HEAD_DIM
    gates = {}
    for t, hh in items:
        km = jnp.where(even_lanes == (hh % 2 == 0), km_s[hh // 2], 0.0)
        kmh, kml = _split_bf16(km)
        gates[t, hh] = _dot(kmh, qps[t, hh // 2]) + _dot(kml, qps[t, hh // 2])
    qaugs = {}
    filler = jnp.zeros((LANES - HEAD_DIM - 16, blk), BF16)
    for t, hh in items:
        past = jidx < ns[t]
        gate = jnp.where(past, gates[t, hh], NEG)
        sel = (past & _rank_select(gate, jidx, MOBA_TOPK)) | (jidx == ns[t])
        selb = jnp.where(sel, 0.0, NEG).astype(BF16)
        qp = qps[t, hh // 2]
        qaugs[t, hh] = (jnp.concatenate([qp[0:HEAD_DIM], selb, filler], axis=0) if hh % 2 == 0 else
                        jnp.concatenate([selb, filler, qp[HEAD_DIM:]], axis=0))

    def scores(t, hh, j):
        start = pl.multiple_of(j * blk, blk)
        return _dot(kaug_s[hh, pl.ds(start, blk), :], qaugs[t, hh]), vt_s[hh, j]

    near = {(t, hh): (scores(t, hh, ns[t]), scores(t, hh, jnp.maximum(ns[t] - 1, 0))) for t, hh in items}
    state = {}
    for t, hh in items:
        (s0, vt0), (s1, vt1) = near[t, hh]
        s0 = jnp.where(key <= qry, s0, NEG) + t0_ref[hh]
        s1 = jnp.where(ns[t] >= 1, s1, NEG) + t1_ref[hh]
        state[t, hh] = _softmax_joint([(s0, vt0), (s1, vt1)])

    older = [jnp.maximum(n - 1, 0) for n in ns]
    lo = 0
    for first in range(MOBA_BPS):
        live = [(t, hh) for t, hh in items if t >= first]

        def far_body(j, carry, live=live):
            tiles = [scores(t, hh, j) for t, hh in live]
            return tuple(_softmax_update(c, s, [vt]) for c, (s, vt) in zip(carry, tiles))

        new = lax.fori_loop(lo, older[first], far_body, tuple(state[it] for it in live))
        state.update(dict(zip(live, new)))
        lo = older[first]

    for t in range(MOBA_BPS):
        o_ref[t * blk:(t + 1) * blk, :] = jnp.concatenate(
            [_softmax_finish(state[t, hh]) for hh in heads], axis=0).T.astype(BF16)


def _moba(proj, t0, t1, batch, seq):
    blk = MOBA_BLOCK
    nb = seq // blk
    width = MOBA_HPS * HEAD_DIM
    rows = MOBA_BPS * blk
    steps = nb // MOBA_BPS
    qc, kc, vc = C_MQ // width, C_MK // width, C_MV // width
    return pl.pallas_call(
        functools.partial(_moba_kernel, seq=seq),
        grid=(batch, MOBA_HEADS // MOBA_HPS, steps),
        in_specs=[pl.BlockSpec((rows, width), lambda b, h, n: (b * steps + n, qc + h)),
                  pl.BlockSpec((seq, width), lambda b, h, n: (b, kc + h)),
                  pl.BlockSpec((seq, width), lambda b, h, n: (b, vc + h)),
                  pl.BlockSpec((MOBA_HPS, blk, blk), lambda b, h, n: (h, 0, 0)),
                  pl.BlockSpec((MOBA_HPS, blk, blk), lambda b, h, n: (h, 0, 0))],
        out_specs=pl.BlockSpec((rows, width), lambda b, h, n: (b * steps + n, h)),
        out_shape=jax.ShapeDtypeStruct((batch * seq, MOBA_HEADS * HEAD_DIM), BF16),
        scratch_shapes=[pltpu.VMEM((MOBA_HPS, seq, LANES), BF16),
                        pltpu.VMEM((MOBA_HPS, nb, VROWS, blk), BF16),
                        pltpu.VMEM((MOBA_HPS // 2, 16, LANES), F32)],
        compiler_params=_params(("parallel", "parallel", "arbitrary")),
        name="moba",
    )(proj, proj, proj, t0, t1)


def _compress_kernel(k_ref, v_ref, w1k_ref, w1v_ref, wtk_ref, wtv_ref, w2k_ref, w2v_ref, pk_ref, pv_ref,
                     kg_ref, ko_ref, vo_ref):
    hid = NSA_CMP_HIDDEN
    chunks = k_ref.shape[0] // NSA_CMP_STRIDE

    def chunk_rows(t_ref):
        return jnp.concatenate([t_ref[pl.ds(l, chunks, stride=NSA_CMP_STRIDE), :].astype(BF16)
                                for l in range(NSA_CMP_STRIDE)], axis=1)

    def compress(tok, w1_ref, wt_ref, w2_ref, pos_ref, g):
        a = _dot(tok, wt_ref[g])
        pw = _dot(pos_ref[...].astype(BF16), w1_ref[...])
        pos = pw[0:1, :hid] + pw[1:2, hid:]
        nxt = pltpu.roll(a[:, hid:], chunks - 1, 0)
        h = jax.nn.gelu(a[:, :hid] + nxt + pos)
        return _dot(h.astype(BF16), w2_ref[...])

    ktok, vtok = chunk_rows(k_ref), chunk_rows(v_ref)
    for g in range(NSA_KV_GROUPS):
        kc = compress(ktok, w1k_ref, wtk_ref, w2k_ref, pk_ref, g)
        ko_ref[0, g] = _rms(kc, kg_ref[...]).astype(BF16)
        vc = compress(vtok, w1v_ref, wtv_ref, w2v_ref, pv_ref, g).astype(BF16)
        vo_ref[0, g] = _transpose_bf16(vc, HEAD_DIM).astype(BF16)


def _compress(proj, w1k, w1v, wtk, wtv, w2k, w2v, pk, pv, kg, layer, batch, seq):
    groups = NSA_KV_GROUPS
    chunks = seq // NSA_CMP_STRIDE
    tok = lambda col: pl.BlockSpec((seq, LANES), lambda b: (b, col // LANES))
    return pl.pallas_call(
        _compress_kernel,
        grid=(batch,),
        in_specs=[tok(C_KC), tok(C_VC)] + [_layer_spec(a, layer)
                                           for a in (w1k, w1v, wtk, wtv, w2k, w2v, pk, pv, kg)],
        out_specs=[pl.BlockSpec((1, groups, chunks, HEAD_DIM), lambda b: (b, 0, 0, 0)),
                   pl.BlockSpec((1, groups, HEAD_DIM, chunks), lambda b: (b, 0, 0, 0))],
        out_shape=[jax.ShapeDtypeStruct((batch, groups, chunks, HEAD_DIM), BF16),
                   jax.ShapeDtypeStruct((batch, groups, HEAD_DIM, chunks), BF16)],
        compiler_params=_params(("parallel",)),
        name="nsa_compress",
    )(proj, proj, w1k, w1v, wtk, wtv, w2k, w2v, pk, pv, kg)


def _nsa_kernel(q_ref, kc_ref, vc_ref, ksl_ref, vsl_ref, kwn_ref, vwn_ref, gt_ref,
                wc_ref, t0_ref, t1_ref, x_ref, ya_ref, ga_ref, gb_ref, wa_ref, wb_ref, wo_ref,
                o_ref, kslaug_s, vslt_s, kwn_s, vwnt_s, sc_s, *, seq):
    step = pl.program_id(1)
    tq = NSA_TQ
    hpg = NSA_HPG
    sb = NSA_SEL_BLOCK
    nsel = seq // sb
    ncmp = seq // NSA_CMP_STRIDE
    nwin = NSA_WINDOW // tq
    groups = range(NSA_KV_GROUPS)
    items = [(t, g) for t in range(NSA_TPS) for g in groups]
    qis = [step * NSA_TPS + t for t in range(NSA_TPS)]

    @pl.when(step == 0)
    def _prepare_keys():
        r = lax.broadcasted_iota(jnp.int32, (seq, LANES), 0)
        c = lax.broadcasted_iota(jnp.int32, (seq, LANES), 1)
        onehot = (c - HEAD_DIM == r // sb).astype(F32)
        place = _eye(HEAD_DIM, LANES)
        sc_s[:, 0:CMP_PAD, :] = jnp.zeros((NSA_TPS * NSA_KV_GROUPS, CMP_PAD, hpg * tq), F32)
        for g in groups:
            sl = slice(g * HEAD_DIM, (g + 1) * HEAD_DIM)
            kslaug_s[g] = (_dot(ksl_ref[:, sl].astype(BF16), place) + onehot).astype(BF16)
            kwn_s[g] = kwn_ref[:, sl].astype(BF16)
            for j in range(seq // tq):
                rows = slice(j * tq, (j + 1) * tq)
                vslt_s[g, j] = _value_tile(vsl_ref[rows, sl].astype(BF16))
                vwnt_s[g, j] = _value_tile(vwn_ref[rows, sl].astype(BF16))

    lanes = lambda parts: jnp.concatenate(parts, axis=1)
    key = lax.broadcasted_iota(jnp.int32, (tq, tq), 0)
    qry = lax.broadcasted_iota(jnp.int32, (tq, tq), 1)
    causal = lanes([key <= qry] * hpg)
    upper = lanes([key > qry] * hpg)
    oj = lax.broadcasted_iota(jnp.int32, (nsel, ncmp), 0) * sb
    on = lax.broadcasted_iota(jnp.int32, (nsel, ncmp), 1) * NSA_CMP_STRIDE
    overlap = ((on < oj + sb) & (on + NSA_CMP_LEN > oj) & (on < seq - NSA_CMP_STRIDE)).astype(BF16)
    jidx = lax.broadcasted_iota(jnp.int32, (nsel, tq), 0)
    cend = lax.broadcasted_iota(jnp.int32, (ncmp, tq), 0) * NSA_CMP_STRIDE + (NSA_CMP_LEN - 1)
    cvis = [lanes([cend <= lax.broadcasted_iota(jnp.int32, (ncmp, tq), 1) + qi * tq] * hpg) for qi in qis]
    own = [(lax.broadcasted_iota(jnp.int32, (nsel, tq), 1) + qi * tq) // sb for qi in qis]
    tile_rows = lambda j: pl.ds(pl.multiple_of(j * tq, tq), tq)
    back = lambda t, k: jnp.maximum(qis[t] - k, 0)
    heads = [[g * hpg + r for r in range(hpg)] for g in groups]
    t0 = [lanes([t0_ref[h] for h in heads[g]]) for g in groups]
    t1 = [lanes([t1_ref[h] for h in heads[g]]) for g in groups]
    slot = lambda t, g: t * NSA_KV_GROUPS + g

    qplain, win_scores = {}, {}
    for t, g in items:
        width = hpg * HEAD_DIM
        qg = _transpose_bf16(q_ref[t * tq:(t + 1) * tq, g * width:(g + 1) * width].astype(BF16), width)
        qplain[t, g] = lanes([qg[r * HEAD_DIM:(r + 1) * HEAD_DIM] for r in range(hpg)]).astype(BF16)
        sc_s[slot(t, g), CMP_PAD:, :] = _dot(kc_ref[0, g], qplain[t, g])
        win_scores[t, g] = [_dot(kwn_s[g, tile_rows(back(t, k)), :], qplain[t, g]) for k in range(nwin + 1)]

    merged_a = jax.nn.sigmoid(ga_ref[...]) * _dot(ya_ref[...], wa_ref[...])

    o_cmps, imps = {}, {}
    for t, g in items:
        win = pl.ds(pl.multiple_of(qis[t] * (tq // NSA_CMP_STRIDE), 8), CMP_WIN)
        sc_s[slot(t, g), win, :] += lanes([wc_ref[h] for h in heads[g]])
        s = jnp.where(cvis[t], sc_s[slot(t, g), CMP_PAD:, :], NEG)
        m = jnp.max(s, axis=0, keepdims=True)
        e = jnp.where(cvis[t], jnp.exp2(s - m), 0.0)
        den = jnp.sum(e, axis=0, keepdims=True)
        p = e * (1.0 / jnp.where(den > 0, den, 1.0))
        o_cmps[t, g] = _dot(vc_ref[0, g], p.astype(BF16))
        psum = p[:, 0:tq]
        for r in range(1, hpg):
            psum = psum + p[:, r * tq:(r + 1) * tq]
        ph, plo = _split_bf16(psum)
        imps[t, g] = _dot(overlap, ph) + _dot(overlap, plo)

    o_wins = {}
    for t, g in items:
        qi = qis[t]
        tiles = []
        for k, s in enumerate(win_scores[t, g]):
            if k == 0:
                s = jnp.where(causal, s, NEG) + t0[g]
            elif k == 1:
                s = jnp.where(qi >= 1, s, NEG) + t1[g]
            elif k < nwin:
                s = jnp.where(qi >= k, s, NEG)
            else:
                s = jnp.where(upper & (qi >= k), s, NEG)
            tiles.append((s, vwnt_s[g, back(t, k)]))
        o_wins[t, g] = _softmax_finish(_softmax_joint(tiles))

    qaugs = {}
    for t, g in items:
        imp = jnp.where((jidx == 0) | (jidx == own[t]) | (jidx == own[t] - 1), -NEG, imps[t, g])
        imp = jnp.where(jidx > own[t], NEG, imp)
        sel = _rank_select(imp, jidx, NSA_SEL_TOPN) & (jidx <= own[t])
        selb = jnp.where(sel, 0.0, NEG).astype(BF16)
        qaugs[t, g] = jnp.concatenate([qplain[t, g], lanes([selb] * hpg),
                                       jnp.zeros((LANES - HEAD_DIM - nsel, hpg * tq), BF16)], axis=0)

    assert NSA_TPS % 2 == 0
    near_tiles = lambda t: 3 if t % 2 == 0 else 2
    slc_near = {(t, g): [_dot(kslaug_s[g, tile_rows(back(t, k)), :], qaugs[t, g])
                         for k in range(near_tiles(t))] for t, g in items}
    state = {}
    for t, g in items:
        qi = qis[t]
        tiles = [(jnp.where(causal, slc_near[t, g][0], NEG) + t0[g], vslt_s[g, qi]),
                 (jnp.where(qi >= 1, slc_near[t, g][1], NEG) + t1[g], vslt_s[g, back(t, 1)])]
        if near_tiles(t) == 3:
            tiles.append((jnp.where(qi >= 2, slc_near[t, g][2], NEG), vslt_s[g, back(t, 2)]))
        state[t, g] = _softmax_joint(tiles)

    slabs = [jnp.maximum(qi - 1, 0) // 2 for qi in qis]
    lo = 0
    for first in range(NSA_TPS):
        live = [(t, g) for t, g in items if t >= first]

        def slc_far(i, carry, live=live):
            slab = pl.ds(pl.multiple_of(i * (2 * tq), 2 * tq), 2 * tq)
            scores = [_dot(kslaug_s[g, slab, :], qaugs[t, g]) for t, g in live]
            return tuple(_softmax_update(c, sc, [vslt_s[g, 2 * i], vslt_s[g, 2 * i + 1]])
                         for c, sc, (t, g) in zip(carry, scores, live))

        new = lax.fori_loop(lo, slabs[first], slc_far, tuple(state[it] for it in live))
        state.update(dict(zip(live, new)))
        lo = slabs[first]

    yb = []
    for t in range(NSA_TPS):
        gates = jax.nn.sigmoid(gt_ref[t * tq:(t + 1) * tq, :]).T
        outs = []
        for g in groups:
            o_slc = _softmax_finish(state[t, g])
            for r in range(hpg):
                h = g * hpg + r
                cols = slice(r * tq, (r + 1) * tq)
                outs.append(gates[h:h + 1, :] * o_cmps[t, g][:, cols]
                            + gates[NSA_HEADS + h:NSA_HEADS + h + 1, :] * o_slc[:, cols]
                            + gates[2 * NSA_HEADS + h:2 * NSA_HEADS + h + 1, :] * o_wins[t, g][:, cols])
        yb.append(jnp.concatenate(outs, axis=0).T.astype(BF16))
    yb = jnp.concatenate(yb, axis=0)
    z = merged_a + jax.nn.sigmoid(gb_ref[...]) * _dot(yb, wb_ref[...])
    o_ref[...] = x_ref[...] + _dot(z.astype(BF16), wo_ref[...])


def _nsa_merge(proj, kcn, vct, wc, t0, t1, xf, ya, wa, wb, wo, layer, batch, seq):
    tq = NSA_TQ
    nq = seq // tq
    rows = NSA_TPS * tq
    steps = nq // NSA_TPS
    ncmp = seq // NSA_CMP_STRIDE
    width = NSA_HEADS * HEAD_DIM
    groups = NSA_KV_GROUPS
    kv = lambda col: pl.BlockSpec((seq, LANES), lambda b, i: (b, col // LANES))
    const = lambda shape: pl.BlockSpec(shape, lambda b, i: (0,) * len(shape))
    d = xf.shape[1]
    row = lambda w, col=0: pl.BlockSpec((rows, w), lambda b, i: (b * steps + i, col))
    return pl.pallas_call(
        functools.partial(_nsa_kernel, seq=seq),
        grid=(batch, steps),
        in_specs=[pl.BlockSpec((rows, width), lambda b, i: (b * steps + i, C_NSQ // width)),
                  pl.BlockSpec((1, groups, ncmp, HEAD_DIM), lambda b, i: (b, 0, 0, 0)),
                  pl.BlockSpec((1, groups, HEAD_DIM, ncmp), lambda b, i: (b, 0, 0, 0)),
                  kv(C_KSL), kv(C_VSL), kv(C_KWN), kv(C_VWN),
                  pl.BlockSpec((rows, LANES), lambda b, i: (b * steps + i, C_NG // LANES)),
                  const((NSA_HEADS, CMP_WIN, tq)), const((NSA_HEADS, tq, tq)),
                  const((NSA_HEADS, tq, tq)),
                  row(d), row(ya.shape[1]), row(d, C_GA // d), row(d, C_GB // d),
                  _layer_spec(wa, layer), _layer_spec(wb, layer), _layer_spec(wo, layer)],
        out_specs=row(d),
        out_shape=jax.ShapeDtypeStruct((batch * seq, d), F32),
        scratch_shapes=[pltpu.VMEM((groups, seq, LANES), BF16),
                        pltpu.VMEM((groups, nq, VROWS, tq), BF16),
                        pltpu.VMEM((groups, seq, HEAD_DIM), BF16),
                        pltpu.VMEM((groups, nq, VROWS, tq), BF16),
                        pltpu.VMEM((NSA_TPS * groups, CMP_PAD + ncmp, NSA_HPG * tq), F32)],
        compiler_params=_params(("parallel", "arbitrary")),
        name="nsa_merge",
    )(proj, kcn, vct, proj, proj, proj, proj, proj, wc, t0, t1, xf, ya, proj, proj, wa, wb, wo)


FFN_HALO = 16
FFN_TM = 1024
FFN_TF = 256


def _ffn_kernel(x_ref, xh_ref, g_ref, wu_ref, cw_ref, cb_ref, wd_ref, p_ref, wg_ref, wp_ref,
                o_ref, act_s, *, seq, d_ff):
    i = pl.program_id(0)
    tm = x_ref.shape[0]
    x = x_ref[...]
    at_start = (i * tm) % seq == 0
    halo = jnp.where(at_start, 0.0, _rms(xh_ref[...], g_ref[...]))
    hn = jnp.concatenate([halo.astype(BF16), _rms(x, g_ref[...]).astype(BF16)], axis=0)

    def conv(cols):
        u = _dot(hn, wu_ref[:, cols])
        u1 = pltpu.roll(u, 1, 0)[FFN_HALO:]
        u2 = pltpu.roll(u, 2, 0)[FFN_HALO:]
        cw = cw_ref[:, cols]
        return cw[0:1] * u2 + cw[1:2] * u1 + cw[2:3] * u[FFN_HALO:] + cb_ref[:, cols]

    for c in range(d_ff // FFN_TF):
        lo = c * FFN_TF
        act = jax.nn.gelu(conv(slice(lo, lo + FFN_TF))) * conv(slice(d_ff + lo, d_ff + lo + FFN_TF))
        act_s[:, lo:lo + FFN_TF] = act.astype(BF16)

    x = x + _dot(act_s[...], wd_ref[...])
    gate = jax.nn.sigmoid(_dot(x.astype(BF16), wg_ref[...]))
    o_ref[...] = x + gate * _dot(p_ref[...].astype(BF16), wp_ref[...])


def _ffn_ple(xf, gain, w_up, conv_w, conv_b, w_down, pf, wg, wp, layer, seq):
    n, d = xf.shape
    d_ff = w_down.shape[1]
    tm = FFN_TM
    hb = tm // FFN_HALO
    resident = lambda a: _layer_spec(a, layer)
    return pl.pallas_call(
        functools.partial(_ffn_kernel, seq=seq, d_ff=d_ff),
        grid=(n // tm,),
        in_specs=[pl.BlockSpec((tm, d), lambda i: (i, 0)),
                  pl.BlockSpec((FFN_HALO, d), lambda i: (jnp.maximum(i * hb - 1, 0), 0)),
                  resident(gain), resident(w_up), resident(conv_w), resident(conv_b), resident(w_down),
                  pl.BlockSpec((None, tm, pf.shape[2]), lambda i: (layer, i, 0)),
                  resident(wg), resident(wp)],
        out_specs=pl.BlockSpec((tm, d), lambda i: (i, 0)),
        out_shape=jax.ShapeDtypeStruct((n, d), F32),
        scratch_shapes=[pltpu.VMEM((tm, d_ff), BF16)],
        compiler_params=_params(("parallel",)),
        name="conv_ffn_ple",
    )(xf, xf, gain, w_up, conv_w, conv_b, w_down, pf, wg, wp)


def _reorder_kernel(w_ref, o_ref):
    attn = 3 * MOBA_HEADS * HEAD_DIM + NSA_HEADS * HEAD_DIM + 6 * NSA_KV_GROUPS * HEAD_DIM
    ng = 3 * NSA_HEADS
    w = w_ref[...]
    pad = jnp.zeros((PROJ_COLS - w.shape[0], w.shape[1]), w.dtype)
    o_ref[...] = jnp.concatenate([w[attn + ng:], w[:attn], w[attn:attn + ng], pad], axis=0).astype(BF16)


def _reorder_in_proj(w):
    depth, d, cols = w.shape
    tc = 256
    return pl.pallas_call(
        _reorder_kernel,
        grid=(depth, d // tc),
        in_specs=[pl.BlockSpec((None, cols, tc), lambda l, i: (l, 0, i))],
        out_specs=pl.BlockSpec((None, PROJ_COLS, tc), lambda l, i: (l, 0, i)),
        out_shape=jax.ShapeDtypeStruct((depth, PROJ_COLS, d), BF16),
        compiler_params=_params(("parallel", "parallel")),
        name="reorder_in_proj",
    )(jnp.swapaxes(w, 1, 2))


def _head_gain_rows(moba_q, moba_k, nsa_q, nsa_k):
    rows = jnp.zeros((moba_q.shape[0], 1, PROJ_COLS), F32)
    for col, gain, heads in ((C_MQ, moba_q * QSCALE, MOBA_HEADS), (C_MK, moba_k, MOBA_HEADS),
                             (C_NSQ, nsa_q * QSCALE, NSA_HEADS), (C_KSL, nsa_k[:, 1], NSA_KV_GROUPS),
                             (C_KWN, nsa_k[:, 2], NSA_KV_GROUPS)):
        rows = rows.at[:, 0, col:col + heads * HEAD_DIM].set(jnp.tile(gain, (1, heads)))
    return rows


def _cmp_weights(w1, pos):
    depth = w1.shape[0]
    half = NSA_CMP_STRIDE * HEAD_DIM
    w1cat = jnp.concatenate([w1[:, :half], w1[:, half:]], axis=2).astype(BF16)
    tok = w1cat.reshape(depth, NSA_CMP_STRIDE, HEAD_DIM, -1)
    zero = jnp.zeros_like(tok)
    placed = jnp.stack([jnp.concatenate([tok, zero], axis=2), jnp.concatenate([zero, tok], axis=2)], axis=1)
    placed = placed.reshape(depth, NSA_KV_GROUPS, NSA_CMP_STRIDE * LANES, -1)
    posr = jnp.zeros((depth, 8, half), F32).at[:, 0:2].set(pos.reshape(depth, 2, half))
    return w1cat, placed, posr


def kernel(x, p, rel_bias, attn_norm, w_in, moba_q_gain, moba_k_gain, nsa_q_gain, nsa_k_gain,
           cmp_pos_k, cmp_w1_k, cmp_w2_k, cmp_pos_v, cmp_w1_v, cmp_w2_v,
           w_br_moba, w_br_nsa, w_o, ffn_norm, w_up, conv_w, conv_b, w_down, w_ple_gate, w_ple):
    batch, seq, d = x.shape
    n = batch * seq
    depth = w_in.shape[0]
    bf = lambda a: a.astype(BF16)
    w_in_r = _reorder_in_proj(w_in)
    head_gain = _head_gain_rows(moba_q_gain, moba_k_gain, nsa_q_gain, nsa_k_gain)
    attn_gain, ffn_gain = attn_norm[:, None, :], ffn_norm[:, None, :]
    w1k, wtk, pk = _cmp_weights(cmp_w1_k, cmp_pos_k)
    w1v, wtv, pv = _cmp_weights(cmp_w1_v, cmp_pos_v)
    w2k, w2v, kg0 = bf(cmp_w2_k), bf(cmp_w2_v), nsa_k_gain[:, 0:1]
    wa, wb, wo = bf(w_br_moba), bf(w_br_nsa), bf(w_o)
    wu, wd, wg, wp = bf(w_up), bf(w_down), bf(w_ple_gate), bf(w_ple)
    conv_b = conv_b[:, None, :]
    pf = p.reshape(depth, n, -1)

    t0m, t1m, t0n, t1n, wc = _bias_tables(rel_bias)
    xf = x.reshape(n, d)
    for i in range(depth):
        proj = _inproj(xf, attn_gain, w_in_r, head_gain, i)
        ya = _moba(proj, t0m, t1m, batch, seq)
        kcn, vct = _compress(proj, w1k, w1v, wtk, wtv, w2k, w2v, pk, pv, kg0, i, batch, seq)
        xf = _nsa_merge(proj, kcn, vct, wc, t0n, t1n, xf, ya, wa, wb, wo, i, batch, seq)
        xf = _ffn_ple(xf, ffn_gain, wu, conv_w, conv_b, wd, pf, wg, wp, i, seq)
    return xf.reshape(batch, seq, d)
```

```python
import functools
import math

import numpy as np
import jax
import jax.numpy as jnp
from jax import lax
from jax.experimental import pallas as pl
from jax.experimental.pallas import tpu as pltpu

F32 = jnp.float32
BF16 = jnp.bfloat16

HEAD_DIM = 64
MOBA_HEADS = 8
MOBA_BLOCK = 256
MOBA_TOPK = 3
NSA_HEADS = 8
NSA_KV_GROUPS = 2
NSA_HPG = NSA_HEADS // NSA_KV_GROUPS
NSA_CMP_LEN = 32
NSA_CMP_STRIDE = 16
NSA_CMP_HIDDEN = 2 * HEAD_DIM
NSA_SEL_BLOCK = 64
NSA_SEL_TOPN = 16
NSA_WINDOW = 512
REL_BUCKETS = 32
REL_MAX_DIST = 128
RMS_EPS = 1e-6
SCALE = HEAD_DIM ** -0.5
LOG2E = math.log2(math.e)
QSCALE = SCALE * LOG2E
NEG = -1e30

LANES = 128
V7X_VMEM_BYTES = 64 * 1024 * 1024
VMEM_LIMIT = V7X_VMEM_BYTES * 15 // 16

INPROJ_TM = 512

C_GA, C_GB, C_MQ, C_MK, C_MV, C_NSQ = 0, 1024, 2048, 2560, 3072, 3584
C_KC, C_VC, C_KSL, C_VSL, C_KWN, C_VWN, C_NG = 4096, 4224, 4352, 4480, 4608, 4736, 4864
PROJ_COLS = 4992

NSA_TQ = 128
NSA_TPS = 4
CMP_WIN = 16
CMP_PAD = 8


def _dot(a, b):
    return jnp.dot(a, b, preferred_element_type=F32)


def _dot_nt(a, b):
    return lax.dot_general(a, b, (((1,), (1,)), ((), ())), preferred_element_type=F32)


def _rms(x, gain):
    return x * lax.rsqrt(jnp.mean(x * x, axis=-1, keepdims=True) + RMS_EPS) * gain


def _split_bf16(x):
    hi = x.astype(BF16)
    return hi, (x - hi.astype(F32)).astype(BF16)


def _eye(rows, cols):
    r = lax.broadcasted_iota(jnp.int32, (rows, cols), 0)
    c = lax.broadcasted_iota(jnp.int32, (rows, cols), 1)
    return (r == c).astype(BF16)


def _transpose_bf16(x, rows):
    return _dot_nt(_eye(rows, x.shape[1]), x)


VROWS = 80


def _value_tile(v):
    row = lax.broadcasted_iota(jnp.int32, (VROWS, v.shape[0]), 0)
    return jnp.where(row == HEAD_DIM, 1.0, _transpose_bf16(v, VROWS)).astype(BF16)


def _softmax_joint(tiles):
    m = functools.reduce(jnp.maximum, [jnp.max(s, axis=0, keepdims=True) for s, _ in tiles])
    acc = functools.reduce(jnp.add, [_dot(vt, jnp.exp2(s - m).astype(BF16)) for s, vt in tiles])
    return m, acc


def _softmax_update(carry, s, vts):
    m, acc = carry
    m_new = jnp.maximum(m, jnp.max(s, axis=0, keepdims=True))
    p = jnp.exp2(s - m_new).astype(BF16)
    rows = s.shape[0] // len(vts)
    pv = functools.reduce(jnp.add, [_dot(vt, p[i * rows:(i + 1) * rows]) for i, vt in enumerate(vts)])
    return m_new, jnp.exp2(m - m_new) * acc + pv


def _softmax_finish(carry):
    _, acc = carry
    return acc[0:HEAD_DIM] * (1.0 / acc[HEAD_DIM:HEAD_DIM + 1])


def _rank_select(score, idx, count):
    beaten = jnp.zeros(score.shape, jnp.int32)
    for i in range(score.shape[0]):
        si = score[i:i + 1, :]
        beaten += ((si > score) | ((si == score) & (i < idx))).astype(jnp.int32)
    return beaten < count


def _params(sem):
    return pltpu.CompilerParams(dimension_semantics=sem, vmem_limit_bytes=VMEM_LIMIT)


def _layer_spec(a, layer):
    return pl.BlockSpec((None,) + a.shape[1:], lambda *_: (layer,) + (0,) * (a.ndim - 1),
                        pipeline_mode=pl.Buffered(1))


def _rel_bucket_np(dist):
    n = np.maximum(dist, 0)
    max_exact = REL_BUCKETS // 2
    nf = np.maximum(n, 1).astype(np.float32)
    large = max_exact + (np.log(nf / np.float32(max_exact)) / np.float32(math.log(REL_MAX_DIST / max_exact))
                         * np.float32(REL_BUCKETS - max_exact)).astype(np.int32)
    return np.where(n < max_exact, n, np.minimum(large, REL_BUCKETS - 1))


def _bucket_starts():
    buckets = _rel_bucket_np(np.arange(4 * REL_MAX_DIST))
    return [int(np.argmax(buckets >= k)) for k in range(REL_BUCKETS)]


BUCKET_START = _bucket_starts()
BIAS_REACH = BUCKET_START[-1]
assert BIAS_REACH <= MOBA_BLOCK and BIAS_REACH <= NSA_TQ - NSA_CMP_LEN + 1 + NSA_CMP_STRIDE


def _tables_kernel(tab_ref, t0m_ref, t1m_ref, t0n_ref, t1n_ref, wc_ref):
    h = pl.program_id(0)

    def bias(dist, head):
        last = tab_ref[head, REL_BUCKETS - 1]
        val = jnp.zeros(dist.shape, F32)
        for k in range(REL_BUCKETS - 2, -1, -1):
            val = jnp.where(dist < BUCKET_START[k + 1], (tab_ref[head, k] - last) * LOG2E, val)
        return val

    def toeplitz(size, offset, head):
        key = lax.broadcasted_iota(jnp.int32, (size, size), 0)
        qry = lax.broadcasted_iota(jnp.int32, (size, size), 1)
        return bias(offset + qry - key, head)

    t0m_ref[0] = toeplitz(MOBA_BLOCK, 0, h)
    t1m_ref[0] = toeplitz(MOBA_BLOCK, MOBA_BLOCK, h)
    t0n_ref[0] = toeplitz(NSA_TQ, 0, MOBA_HEADS + h)
    t1n_ref[0] = toeplitz(NSA_TQ, NSA_TQ, MOBA_HEADS + h)
    a = lax.broadcasted_iota(jnp.int32, (CMP_WIN, NSA_TQ), 0)
    i = lax.broadcasted_iota(jnp.int32, (CMP_WIN, NSA_TQ), 1)
    wc_ref[0] = bias(i + (NSA_TQ - NSA_CMP_LEN + 1) - NSA_CMP_STRIDE * a, MOBA_HEADS + h)


def _bias_tables(rel_bias):
    blk, tq = MOBA_BLOCK, NSA_TQ
    shapes = [(blk, blk), (blk, blk), (tq, tq), (tq, tq), (CMP_WIN, tq)]
    return pl.pallas_call(
        _tables_kernel,
        grid=(MOBA_HEADS,),
        in_specs=[pl.BlockSpec(memory_space=pltpu.SMEM)],
        out_specs=[pl.BlockSpec((1,) + s, lambda h: (h, 0, 0)) for s in shapes],
        out_shape=[jax.ShapeDtypeStruct((MOBA_HEADS,) + s, F32) for s in shapes],
        compiler_params=_params(("arbitrary",)),
        name="bias_tables",
    )(rel_bias)


NORM_SLABS = (list(range(C_MQ, C_MV, LANES)) + list(range(C_NSQ, C_KC, LANES)) + [C_KSL, C_KWN])


def _inproj_kernel(x_ref, g_ref, w_ref, hg_ref, o_ref):
    h = _rms(x_ref[...], g_ref[...]).astype(BF16)
    y = _dot_nt(h, w_ref[...])
    first = lax.broadcasted_iota(jnp.int32, (1, LANES), 1) < HEAD_DIM
    edges = sorted(set([0, PROJ_COLS] + NORM_SLABS + [c + LANES for c in NORM_SLABS]))
    for lo, hi in zip(edges[:-1], edges[1:]):
        t = y[:, lo:hi]
        if lo in NORM_SLABS:
            sq = t * t
            s0 = jnp.sum(jnp.where(first, sq, 0.0), axis=-1, keepdims=True)
            s1 = jnp.sum(jnp.where(first, 0.0, sq), axis=-1, keepdims=True)
            ms = jnp.where(first, s0, s1) * (1.0 / HEAD_DIM)
            t = t * lax.rsqrt(ms + RMS_EPS) * hg_ref[:, lo:hi]
        o_ref[:, lo:hi] = t


def _inproj(xf, gain, w, head_gain, layer):
    n, d = xf.shape
    tm = INPROJ_TM
    return pl.pallas_call(
        _inproj_kernel,
        grid=(n // tm,),
        in_specs=[pl.BlockSpec((tm, d), lambda i: (i, 0)), _layer_spec(gain, layer),
                  _layer_spec(w, layer), _layer_spec(head_gain, layer)],
        out_specs=pl.BlockSpec((tm, PROJ_COLS), lambda i: (i, 0)),
        out_shape=jax.ShapeDtypeStruct((n, PROJ_COLS), F32),
        compiler_params=_params(("parallel",)),
        name="inproj",
    )(xf, gain, w, head_gain)


MOBA_HPS = 8
MOBA_BPS = 2


def _moba_kernel(q_ref, k_ref, v_ref, t0_ref, t1_ref, o_ref, kaug_s, vt_s, km_s, *, seq):
    step = pl.program_id(2)
    blk = MOBA_BLOCK
    nb = seq // blk
    heads = range(MOBA_HPS)
    pairs = range(MOBA_HPS // 2)
    pair_cols = lambda p: slice(p * LANES, (p + 1) * LANES)
    items = [(t, hh) for t in range(MOBA_BPS) for hh in heads]
    ns = [step * MOBA_BPS + t for t in range(MOBA_BPS)]

    @pl.when(step == 0)
    def _prepare_keys():
        block = lax.broadcasted_iota(jnp.int32, (seq, LANES), 0) // blk
        lane = lax.broadcasted_iota(jnp.int32, (seq, LANES), 1)
        code_hi = (lane - HEAD_DIM == block).astype(BF16)
        code_lo = (lane == block).astype(BF16)
        ones_row = (lax.broadcasted_iota(jnp.int32, (VROWS - HEAD_DIM, blk), 0) == 0).astype(F32)
        for p in pairs:
            kn = k_ref[:, pair_cols(p)]
            km = kn.reshape(nb, blk, LANES).sum(axis=1) * (1.0 / blk)
            km_s[p] = jnp.concatenate([km, jnp.zeros((16 - nb, LANES), F32)], axis=0)
            kaug_s[2 * p] = jnp.where(lane < HEAD_DIM, kn.astype(BF16), code_hi)
            kaug_s[2 * p + 1] = jnp.where(lane >= HEAD_DIM, kn.astype(BF16), code_lo)
            for j in range(nb):
                vt = _transpose_bf16(v_ref[j * blk:(j + 1) * blk, pair_cols(p)].astype(BF16), LANES)
                vt_s[2 * p, j] = jnp.concatenate([vt[0:HEAD_DIM], ones_row], axis=0).astype(BF16)
                vt_s[2 * p + 1, j] = jnp.concatenate([vt[HEAD_DIM:], ones_row], axis=0).astype(BF16)

    jidx = lax.broadcasted_iota(jnp.int32, (16, blk), 0)
    key = lax.broadcasted_iota(jnp.int32, (blk, blk), 0)
    qry = lax.broadcasted_iota(jnp.int32, (blk, blk), 1)
    qps = {}
    for t in range(MOBA_BPS):
        for p in pairs:
            qps[t, p] = _transpose_bf16(q_ref[t * blk:(t + 1) * blk, pair_cols(p)].astype(BF16),
                                        LANES).astype(BF16)
    even_lanes = lax.broadcasted_iota(jnp.int32, (16, LANES), 1) < HEAD_DIM
    gates = {}
    for t, hh in items:
        km = jnp.where(even_lanes == (hh % 2 == 0), km_s[hh // 2], 0.0)
        kmh, kml = _split_bf16(km)
        gates[t, hh] = _dot(kmh, qps[t, hh // 2]) + _dot(kml, qps[t, hh // 2])
    qaugs = {}
    filler = jnp.zeros((LANES - HEAD_DIM - 16, blk), BF16)
    for t, hh in items:
        past = jidx < ns[t]
        gate = jnp.where(past, gates[t, hh], NEG)
        sel = (past & _rank_select(gate, jidx, MOBA_TOPK)) | (jidx == ns[t])
        selb = jnp.where(sel, 0.0, NEG).astype(BF16)
        qp = qps[t, hh // 2]
        qaugs[t, hh] = (jnp.concatenate([qp[0:HEAD_DIM], selb, filler], axis=0) if hh % 2 == 0 else
                        jnp.concatenate([selb, filler, qp[HEAD_DIM:]], axis=0))

    def scores(t, hh, j):
        start = pl.multiple_of(j * blk, blk)
        return _dot(kaug_s[hh, pl.ds(start, blk), :], qaugs[t, hh]), vt_s[hh, j]

    near = {(t, hh): (scores(t, hh, ns[t]), scores(t, hh, jnp.maximum(ns[t] - 1, 0))) for t, hh in items}
    state = {}
    for t, hh in items:
        (s0, vt0), (s1, vt1) = near[t, hh]
        s0 = jnp.where(key <= qry, s0, NEG) + t0_ref[hh]
        s1 = jnp.where(ns[t] >= 1, s1, NEG) + t1_ref[hh]
        state[t, hh] = _softmax_joint([(s0, vt0), (s1, vt1)])

    older = [jnp.maximum(n - 1, 0) for n in ns]
    lo = 0
    for first in range(MOBA_BPS):
        live = [(t, hh) for t, hh in items if t >= first]

        def far_body(j, carry, live=live):
            tiles = [scores(t, hh, j) for t, hh in live]
            return tuple(_softmax_update(c, s, [vt]) for c, (s, vt) in zip(carry, tiles))

        new = lax.fori_loop(lo, older[first], far_body, tuple(state[it] for it in live))
        state.update(dict(zip(live, new)))
        lo = older[first]

    for t in range(MOBA_BPS):
        o_ref[t * blk:(t + 1) * blk, :] = jnp.concatenate(
            [_softmax_finish(state[t, hh]) for hh in heads], axis=0).T.astype(BF16)


def _moba(proj, t0, t1, batch, seq):
    blk = MOBA_BLOCK
    nb = seq // blk
    width = MOBA_HPS * HEAD_DIM
    rows = MOBA_BPS * blk
    steps = nb // MOBA_BPS
    qc, kc, vc = C_MQ // width, C_MK // width, C_MV // width
    return pl.pallas_call(
        functools.partial(_moba_kernel, seq=seq),
        grid=(batch, MOBA_HEADS // MOBA_HPS, steps),
        in_specs=[pl.BlockSpec((rows, width), lambda b, h, n: (b * steps + n, qc + h)),
                  pl.BlockSpec((seq, width), lambda b, h, n: (b, kc + h)),
                  pl.BlockSpec((seq, width), lambda b, h, n: (b, vc + h)),
                  pl.BlockSpec((MOBA_HPS, blk, blk), lambda b, h, n: (h, 0, 0)),
                  pl.BlockSpec((MOBA_HPS, blk, blk), lambda b, h, n: (h, 0, 0))],
        out_specs=pl.BlockSpec((rows, width), lambda b, h, n: (b * steps + n, h)),
        out_shape=jax.ShapeDtypeStruct((batch * seq, MOBA_HEADS * HEAD_DIM), BF16),
        scratch_shapes=[pltpu.VMEM((MOBA_HPS, seq, LANES), BF16),
                        pltpu.VMEM((MOBA_HPS, nb, VROWS, blk), BF16),
                        pltpu.VMEM((MOBA_HPS // 2, 16, LANES), F32)],
        compiler_params=_params(("parallel", "parallel", "arbitrary")),
        name="moba",
    )(proj, proj, proj, t0, t1)


def _compress_kernel(k_ref, v_ref, w1k_ref, w1v_ref, wtk_ref, wtv_ref, w2k_ref, w2v_ref, pk_ref, pv_ref,
                     kg_ref, ko_ref, vo_ref):
    hid = NSA_CMP_HIDDEN
    chunks = k_ref.shape[0] // NSA_CMP_STRIDE

    def chunk_rows(t_ref):
        return jnp.concatenate([t_ref[pl.ds(l, chunks, stride=NSA_CMP_STRIDE), :].astype(BF16)
                                for l in range(NSA_CMP_STRIDE)], axis=1)

    def compress(tok, w1_ref, wt_ref, w2_ref, pos_ref, g):
        a = _dot(tok, wt_ref[g])
        pw = _dot(pos_ref[...].astype(BF16), w1_ref[...])
        pos = pw[0:1, :hid] + pw[1:2, hid:]
        nxt = pltpu.roll(a[:, hid:], chunks - 1, 0)
        h = jax.nn.gelu(a[:, :hid] + nxt + pos)
        return _dot(h.astype(BF16), w2_ref[...])

    ktok, vtok = chunk_rows(k_ref), chunk_rows(v_ref)
    for g in range(NSA_KV_GROUPS):
        kc = compress(ktok, w1k_ref, wtk_ref, w2k_ref, pk_ref, g)
        ko_ref[0, g] = _rms(kc, kg_ref[...]).astype(BF16)
        vc = compress(vtok, w1v_ref, wtv_ref, w2v_ref, pv_ref, g).astype(BF16)
        vo_ref[0, g] = _transpose_bf16(vc, HEAD_DIM).astype(BF16)


def _compress(proj, w1k, w1v, wtk, wtv, w2k, w2v, pk, pv, kg, layer, batch, seq):
    groups = NSA_KV_GROUPS
    chunks = seq // NSA_CMP_STRIDE
    tok = lambda col: pl.BlockSpec((seq, LANES), lambda b: (b, col // LANES))
    return pl.pallas_call(
        _compress_kernel,
        grid=(batch,),
        in_specs=[tok(C_KC), tok(C_VC)] + [_layer_spec(a, layer)
                                           for a in (w1k, w1v, wtk, wtv, w2k, w2v, pk, pv, kg)],
        out_specs=[pl.BlockSpec((1, groups, chunks, HEAD_DIM), lambda b: (b, 0, 0, 0)),
                   pl.BlockSpec((1, groups, HEAD_DIM, chunks), lambda b: (b, 0, 0, 0))],
        out_shape=[jax.ShapeDtypeStruct((batch, groups, chunks, HEAD_DIM), BF16),
                   jax.ShapeDtypeStruct((batch, groups, HEAD_DIM, chunks), BF16)],
        compiler_params=_params(("parallel",)),
        name="nsa_compress",
    )(proj, proj, w1k, w1v, wtk, wtv, w2k, w2v, pk, pv, kg)


def _nsa_kernel(q_ref, kc_ref, vc_ref, ksl_ref, vsl_ref, kwn_ref, vwn_ref, gt_ref,
                wc_ref, t0_ref, t1_ref, x_ref, ya_ref, ga_ref, gb_ref, wa_ref, wb_ref, wo_ref,
                o_ref, kslaug_s, vslt_s, kwn_s, vwnt_s, sc_s, yb_s, *, seq, total):
    flat = pl.program_id(0)
    steps = (seq // NSA_TQ) // NSA_TPS
    step = jnp.minimum(flat, total - 1) % steps
    tq = NSA_TQ
    hpg = NSA_HPG
    sb = NSA_SEL_BLOCK
    nsel = seq // sb
    ncmp = seq // NSA_CMP_STRIDE
    nwin = NSA_WINDOW // tq
    groups = range(NSA_KV_GROUPS)
    items = [(t, g) for t in range(NSA_TPS) for g in groups]
    qis = [step * NSA_TPS + t for t in range(NSA_TPS)]

    @pl.when(flat == 0)
    def _no_previous_slab():
        yb_s[...] = jnp.zeros_like(yb_s)

    @pl.when((step == 0) & (flat < total))
    def _prepare_keys():
        r = lax.broadcasted_iota(jnp.int32, (seq, LANES), 0)
        c = lax.broadcasted_iota(jnp.int32, (seq, LANES), 1)
        onehot = (c - HEAD_DIM == r // sb).astype(F32)
        place = _eye(HEAD_DIM, LANES)
        sc_s[:, 0:CMP_PAD, :] = jnp.zeros((NSA_TPS * NSA_KV_GROUPS, CMP_PAD, hpg * tq), F32)
        for g in groups:
            sl = slice(g * HEAD_DIM, (g + 1) * HEAD_DIM)
            kslaug_s[g] = (_dot(ksl_ref[:, sl].astype(BF16), place) + onehot).astype(BF16)
            kwn_s[g] = kwn_ref[:, sl].astype(BF16)
            for j in range(seq // tq):
                rows = slice(j * tq, (j + 1) * tq)
                vslt_s[g, j] = _value_tile(vsl_ref[rows, sl].astype(BF16))
                vwnt_s[g, j] = _value_tile(vwn_ref[rows, sl].astype(BF16))

    lanes = lambda parts: jnp.concatenate(parts, axis=1)
    key = lax.broadcasted_iota(jnp.int32, (tq, tq), 0)
    qry = lax.broadcasted_iota(jnp.int32, (tq, tq), 1)
    causal = lanes([key <= qry] * hpg)
    upper = lanes([key > qry] * hpg)
    oj = lax.broadcasted_iota(jnp.int32, (nsel, ncmp), 0) * sb
    on = lax.broadcasted_iota(jnp.int32, (nsel, ncmp), 1) * NSA_CMP_STRIDE
    overlap = ((on < oj + sb) & (on + NSA_CMP_LEN > oj) & (on < seq - NSA_CMP_STRIDE)).astype(BF16)
    jidx = lax.broadcasted_iota(jnp.int32, (nsel, tq), 0)
    cend = lax.broadcasted_iota(jnp.int32, (ncmp, tq), 0) * NSA_CMP_STRIDE + (NSA_CMP_LEN - 1)
    cvis = [lanes([cend <= lax.broadcasted_iota(jnp.int32, (ncmp, tq), 1) + qi * tq] * hpg) for qi in qis]
    own = [(lax.broadcasted_iota(jnp.int32, (nsel, tq), 1) + qi * tq) // sb for qi in qis]
    tile_rows = lambda j: pl.ds(pl.multiple_of(j * tq, tq), tq)
    back = lambda t, k: jnp.maximum(qis[t] - k, 0)
    heads = [[g * hpg + r for r in range(hpg)] for g in groups]
    t0 = [lanes([t0_ref[h] for h in heads[g]]) for g in groups]
    t1 = [lanes([t1_ref[h] for h in heads[g]]) for g in groups]
    slot = lambda t, g: t * NSA_KV_GROUPS + g

    qplain, win_scores = {}, {}
    for t, g in items:
        width = hpg * HEAD_DIM
        qg = _transpose_bf16(q_ref[t * tq:(t + 1) * tq, g * width:(g + 1) * width].astype(BF16), width)
        qplain[t, g] = lanes([qg[r * HEAD_DIM:(r + 1) * HEAD_DIM] for r in range(hpg)]).astype(BF16)
        sc_s[slot(t, g), CMP_PAD:, :] = _dot(kc_ref[0, g], qplain[t, g])
        win_scores[t, g] = [_dot(kwn_s[g, tile_rows(back(t, k)), :], qplain[t, g]) for k in range(nwin + 1)]

    z = (jax.nn.sigmoid(ga_ref[...]) * _dot(ya_ref[...], wa_ref[...])
         + jax.nn.sigmoid(gb_ref[...]) * _dot(yb_s[...], wb_ref[...]))
    o_ref[...] = x_ref[...] + _dot(z.astype(BF16), wo_ref[...])

    o_cmps, imps = {}, {}
    for t, g in items:
        win = pl.ds(pl.multiple_of(qis[t] * (tq // NSA_CMP_STRIDE), 8), CMP_WIN)
        sc_s[slot(t, g), win, :] += lanes([wc_ref[h] for h in heads[g]])
        s = jnp.where(cvis[t], sc_s[slot(t, g), CMP_PAD:, :], NEG)
        m = jnp.max(s, axis=0, keepdims=True)
        e = jnp.where(cvis[t], jnp.exp2(s - m), 0.0)
        den = jnp.sum(e, axis=0, keepdims=True)
        p = e * (1.0 / jnp.where(den > 0, den, 1.0))
        o_cmps[t, g] = _dot(vc_ref[0, g], p.astype(BF16))
        psum = p[:, 0:tq]
        for r in range(1, hpg):
            psum = psum + p[:, r * tq:(r + 1) * tq]
        ph, plo = _split_bf16(psum)
        imps[t, g] = _dot(overlap, ph) + _dot(overlap, plo)

    o_wins = {}
    for t, g in items:
        qi = qis[t]
        tiles = []
        for k, s in enumerate(win_scores[t, g]):
            if k == 0:
                s = jnp.where(causal, s, NEG) + t0[g]
            elif k == 1:
                s = jnp.where(qi >= 1, s, NEG) + t1[g]
            elif k < nwin:
                s = jnp.where(qi >= k, s, NEG)
            else:
                s = jnp.where(upper & (qi >= k), s, NEG)
            tiles.append((s, vwnt_s[g, back(t, k)]))
        o_wins[t, g] = _softmax_finish(_softmax_joint(tiles))

    qaugs = {}
    for t, g in items:
        imp = jnp.where((jidx == 0) | (jidx == own[t]) | (jidx == own[t] - 1), -NEG, imps[t, g])
        imp = jnp.where(jidx > own[t], NEG, imp)
        sel = _rank_select(imp, jidx, NSA_SEL_TOPN) & (jidx <= own[t])
        selb = jnp.where(sel, 0.0, NEG).astype(BF16)
        qaugs[t, g] = jnp.concatenate([qplain[t, g], lanes([selb] * hpg),
                                       jnp.zeros((LANES - HEAD_DIM - nsel, hpg * tq), BF16)], axis=0)

    assert NSA_TPS % 2 == 0
    near_tiles = lambda t: 3 if t % 2 == 0 else 2
    slc_near = {(t, g): [_dot(kslaug_s[g, tile_rows(back(t, k)), :], qaugs[t, g])
                         for k in range(near_tiles(t))] for t, g in items}
    state = {}
    for t, g in items:
        qi = qis[t]
        tiles = [(jnp.where(causal, slc_near[t, g][0], NEG) + t0[g], vslt_s[g, qi]),
                 (jnp.where(qi >= 1, slc_near[t, g][1], NEG) + t1[g], vslt_s[g, back(t, 1)])]
        if near_tiles(t) == 3:
            tiles.append((jnp.where(qi >= 2, slc_near[t, g][2], NEG), vslt_s[g, back(t, 2)]))
        state[t, g] = _softmax_joint(tiles)

    slabs = [jnp.maximum(qi - 1, 0) // 2 for qi in qis]
    lo = 0
    for first in range(NSA_TPS):
        live = [(t, g) for t, g in items if t >= first]

        def slc_far(i, carry, live=live):
            slab = pl.ds(pl.multiple_of(i * (2 * tq), 2 * tq), 2 * tq)
            scores = [_dot(kslaug_s[g, slab, :], qaugs[t, g]) for t, g in live]
            return tuple(_softmax_update(c, sc, [vslt_s[g, 2 * i], vslt_s[g, 2 * i + 1]])
                         for c, sc, (t, g) in zip(carry, scores, live))

        new = lax.fori_loop(lo, slabs[first], slc_far, tuple(state[it] for it in live))
        state.update(dict(zip(live, new)))
        lo = slabs[first]

    yb = []
    for t in range(NSA_TPS):
        gates = jax.nn.sigmoid(gt_ref[t * tq:(t + 1) * tq, :]).T
        outs = []
        for g in groups:
            o_slc = _softmax_finish(state[t, g])
            for r in range(hpg):
                h = g * hpg + r
                cols = slice(r * tq, (r + 1) * tq)
                outs.append(gates[h:h + 1, :] * o_cmps[t, g][:, cols]
                            + gates[NSA_HEADS + h:NSA_HEADS + h + 1, :] * o_slc[:, cols]
                            + gates[2 * NSA_HEADS + h:2 * NSA_HEADS + h + 1, :] * o_wins[t, g][:, cols])
        yb.append(jnp.concatenate(outs, axis=0).T.astype(BF16))
    yb_s[...] = jnp.concatenate(yb, axis=0)


def _nsa_merge(proj, kcn, vct, wc, t0, t1, xf, ya, wa, wb, wo, layer, batch, seq):
    tq = NSA_TQ
    nq = seq // tq
    rows = NSA_TPS * tq
    steps = nq // NSA_TPS
    total = batch * steps
    ncmp = seq // NSA_CMP_STRIDE
    width = NSA_HEADS * HEAD_DIM
    groups = NSA_KV_GROUPS
    d = xf.shape[1]
    cur = lambda s: jnp.minimum(s, total - 1)
    prv = lambda s: jnp.maximum(s - 1, 0)
    kv = lambda col: pl.BlockSpec((seq, LANES), lambda s: (cur(s) // steps, col // LANES))
    const = lambda shape: pl.BlockSpec(shape, lambda s: (0,) * len(shape))
    lag = lambda w, col=0: pl.BlockSpec((rows, w), lambda s: (prv(s), col))
    return pl.pallas_call(
        functools.partial(_nsa_kernel, seq=seq, total=total),
        grid=(total + 1,),
        in_specs=[pl.BlockSpec((rows, width), lambda s: (cur(s), C_NSQ // width)),
                  pl.BlockSpec((1, groups, ncmp, HEAD_DIM), lambda s: (cur(s) // steps, 0, 0, 0)),
                  pl.BlockSpec((1, groups, HEAD_DIM, ncmp), lambda s: (cur(s) // steps, 0, 0, 0)),
                  kv(C_KSL), kv(C_VSL), kv(C_KWN), kv(C_VWN),
                  pl.BlockSpec((rows, LANES), lambda s: (cur(s), C_NG // LANES)),
                  const((NSA_HEADS, CMP_WIN, tq)), const((NSA_HEADS, tq, tq)),
                  const((NSA_HEADS, tq, tq)),
                  lag(d), lag(ya.shape[1]), lag(d, C_GA // d), lag(d, C_GB // d),
                  _layer_spec(wa, layer), _layer_spec(wb, layer), _layer_spec(wo, layer)],
        out_specs=lag(d),
        out_shape=jax.ShapeDtypeStruct((batch * seq, d), F32),
        scratch_shapes=[pltpu.VMEM((groups, seq, LANES), BF16),
                        pltpu.VMEM((groups, nq, VROWS, tq), BF16),
                        pltpu.VMEM((groups, seq, HEAD_DIM), BF16),
                        pltpu.VMEM((groups, nq, VROWS, tq), BF16),
                        pltpu.VMEM((NSA_TPS * groups, CMP_PAD + ncmp, NSA_HPG * tq), F32),
                        pltpu.VMEM((rows, width), BF16)],
        compiler_params=_params(("arbitrary",)),
        name="nsa_merge",
    )(proj, kcn, vct, proj, proj, proj, proj, proj, wc, t0, t1, xf, ya, proj, proj, wa, wb, wo)


FFN_HALO = 16
FFN_TM = 1024
FFN_TF = 256


def _ffn_kernel(x_ref, xh_ref, g_ref, wu_ref, cw_ref, cb_ref, wd_ref, p_ref, wg_ref, wp_ref,
                o_ref, act_s, *, seq, d_ff):
    i = pl.program_id(0)
    tm = x_ref.shape[0]
    x = x_ref[...]
    at_start = (i * tm) % seq == 0
    halo = jnp.where(at_start, 0.0, _rms(xh_ref[...], g_ref[...]))
    hn = jnp.concatenate([halo.astype(BF16), _rms(x, g_ref[...]).astype(BF16)], axis=0)

    def conv(cols):
        u = _dot(hn, wu_ref[:, cols])
        u1 = pltpu.roll(u, 1, 0)[FFN_HALO:]
        u2 = pltpu.roll(u, 2, 0)[FFN_HALO:]
        cw = cw_ref[:, cols]
        return cw[0:1] * u2 + cw[1:2] * u1 + cw[2:3] * u[FFN_HALO:] + cb_ref[:, cols]

    for c in range(d_ff // FFN_TF):
        lo = c * FFN_TF
        act = jax.nn.gelu(conv(slice(lo, lo + FFN_TF))) * conv(slice(d_ff + lo, d_ff + lo + FFN_TF))
        act_s[:, lo:lo + FFN_TF] = act.astype(BF16)

    x = x + _dot(act_s[...], wd_ref[...])
    gate = jax.nn.sigmoid(_dot(x.astype(BF16), wg_ref[...]))
    o_ref[...] = x + gate * _dot(p_ref[...].astype(BF16), wp_ref[...])


def _ffn_ple(xf, gain, w_up, conv_w, conv_b, w_down, pf, wg, wp, layer, seq):
    n, d = xf.shape
    d_ff = w_down.shape[1]
    tm = FFN_TM
    hb = tm // FFN_HALO
    resident = lambda a: _layer_spec(a, layer)
    return pl.pallas_call(
        functools.partial(_ffn_kernel, seq=seq, d_ff=d_ff),
        grid=(n // tm,),
        in_specs=[pl.BlockSpec((tm, d), lambda i: (i, 0)),
                  pl.BlockSpec((FFN_HALO, d), lambda i: (jnp.maximum(i * hb - 1, 0), 0)),
                  resident(gain), resident(w_up), resident(conv_w), resident(conv_b), resident(w_down),
                  pl.BlockSpec((None, tm, pf.shape[2]), lambda i: (layer, i, 0)),
                  resident(wg), resident(wp)],
        out_specs=pl.BlockSpec((tm, d), lambda i: (i, 0)),
        out_shape=jax.ShapeDtypeStruct((n, d), F32),
        scratch_shapes=[pltpu.VMEM((tm, d_ff), BF16)],
        compiler_params=_params(("parallel",)),
        name="conv_ffn_ple",
    )(xf, xf, gain, w_up, conv_w, conv_b, w_down, pf, wg, wp)


def _reorder_kernel(w_ref, o_ref):
    attn = 3 * MOBA_HEADS * HEAD_DIM + NSA_HEADS * HEAD_DIM + 6 * NSA_KV_GROUPS * HEAD_DIM
    ng = 3 * NSA_HEADS
    w = w_ref[...]
    pad = jnp.zeros((PROJ_COLS - w.shape[0], w.shape[1]), w.dtype)
    o_ref[...] = jnp.concatenate([w[attn + ng:], w[:attn], w[attn:attn + ng], pad], axis=0).astype(BF16)


def _reorder_in_proj(w):
    depth, d, cols = w.shape
    tc = 256
    return pl.pallas_call(
        _reorder_kernel,
        grid=(depth, d // tc),
        in_specs=[pl.BlockSpec((None, cols, tc), lambda l, i: (l, 0, i))],
        out_specs=pl.BlockSpec((None, PROJ_COLS, tc), lambda l, i: (l, 0, i)),
        out_shape=jax.ShapeDtypeStruct((depth, PROJ_COLS, d), BF16),
        compiler_params=_params(("parallel", "parallel")),
        name="reorder_in_proj",
    )(jnp.swapaxes(w, 1, 2))


def _head_gain_rows(moba_q, moba_k, nsa_q, nsa_k):
    rows = jnp.zeros((moba_q.shape[0], 1, PROJ_COLS), F32)
    for col, gain, heads in ((C_MQ, moba_q * QSCALE, MOBA_HEADS), (C_MK, moba_k, MOBA_HEADS),
                             (C_NSQ, nsa_q * QSCALE, NSA_HEADS), (C_KSL, nsa_k[:, 1], NSA_KV_GROUPS),
                             (C_KWN, nsa_k[:, 2], NSA_KV_GROUPS)):
        rows = rows.at[:, 0, col:col + heads * HEAD_DIM].set(jnp.tile(gain, (1, heads)))
    return rows


def _cmp_weights(w1, pos):
    depth = w1.shape[0]
    half = NSA_CMP_STRIDE * HEAD_DIM
    w1cat = jnp.concatenate([w1[:, :half], w1[:, half:]], axis=2).astype(BF16)
    tok = w1cat.reshape(depth, NSA_CMP_STRIDE, HEAD_DIM, -1)
    zero = jnp.zeros_like(tok)
    placed = jnp.stack([jnp.concatenate([tok, zero], axis=2), jnp.concatenate([zero, tok], axis=2)], axis=1)
    placed = placed.reshape(depth, NSA_KV_GROUPS, NSA_CMP_STRIDE * LANES, -1)
    posr = jnp.zeros((depth, 8, half), F32).at[:, 0:2].set(pos.reshape(depth, 2, half))
    return w1cat, placed, posr


def kernel(x, p, rel_bias, attn_norm, w_in, moba_q_gain, moba_k_gain, nsa_q_gain, nsa_k_gain,
           cmp_pos_k, cmp_w1_k, cmp_w2_k, cmp_pos_v, cmp_w1_v, cmp_w2_v,
           w_br_moba, w_br_nsa, w_o, ffn_norm, w_up, conv_w, conv_b, w_down, w_ple_gate, w_ple):
    batch, seq, d = x.shape
    n = batch * seq
    depth = w_in.shape[0]
    bf = lambda a: a.astype(BF16)
    w_in_r = _reorder_in_proj(w_in)
    head_gain = _head_gain_rows(moba_q_gain, moba_k_gain, nsa_q_gain, nsa_k_gain)
    attn_gain, ffn_gain = attn_norm[:, None, :], ffn_norm[:, None, :]
    w1k, wtk, pk = _cmp_weights(cmp_w1_k, cmp_pos_k)
    w1v, wtv, pv = _cmp_weights(cmp_w1_v, cmp_pos_v)
    w2k, w2v, kg0 = bf(cmp_w2_k), bf(cmp_w2_v), nsa_k_gain[:, 0:1]
    wa, wb, wo = bf(w_br_moba), bf(w_br_nsa), bf(w_o)
    wu, wd, wg, wp = bf(w_up), bf(w_down), bf(w_ple_gate), bf(w_ple)
    conv_b = conv_b[:, None, :]
    pf = p.reshape(depth, n, -1)

    t0m, t1m, t0n, t1n, wc = _bias_tables(rel_bias)
    xf = x.reshape(n, d)
    for i in range(depth):
        proj = _inproj(xf, attn_gain, w_in_r, head_gain, i)
        ya = _moba(proj, t0m, t1m, batch, seq)
        kcn, vct = _compress(proj, w1k, w1v, wtk, wtv, w2k, w2v, pk, pv, kg0, i, batch, seq)
        xf = _nsa_merge(proj, kcn, vct, wc, t0n, t1n, xf, ya, wa, wb, wo, i, batch, seq)
        xf = _ffn_ple(xf, ffn_gain, wu, conv_w, conv_b, wd, pf, wg, wp, i, seq)
    return xf.reshape(batch, seq, d)
```

```python
import functools
import math

import numpy as np
import jax
import jax.numpy as jnp
from jax import lax
from jax.experimental import pallas as pl
from jax.experimental.pallas import tpu as pltpu

F32 = jnp.float32
BF16 = jnp.bfloat16

HEAD_DIM = 64
MOBA_HEADS = 8
MOBA_BLOCK = 256
MOBA_TOPK = 3
NSA_HEADS = 8
NSA_KV_GROUPS = 2
NSA_HPG = NSA_HEADS // NSA_KV_GROUPS
NSA_CMP_LEN = 32
NSA_CMP_STRIDE = 16
NSA_CMP_HIDDEN = 2 * HEAD_DIM
NSA_SEL_BLOCK = 64
NSA_SEL_TOPN = 16
NSA_WINDOW = 512
REL_BUCKETS = 32
REL_MAX_DIST = 128
RMS_EPS = 1e-6
SCALE = HEAD_DIM ** -0.5
LOG2E = math.log2(math.e)
QSCALE = SCALE * LOG2E
NEG = -1e30

LANES = 128
V7X_VMEM_BYTES = 64 * 1024 * 1024
VMEM_LIMIT = V7X_VMEM_BYTES * 15 // 16

INPROJ_TM = 512

C_GA, C_GB, C_MQ, C_MK, C_MV, C_NSQ = 0, 1024, 2048, 2560, 3072, 3584
C_KC, C_VC, C_KSL, C_VSL, C_KWN, C_VWN, C_NG = 4096, 4224, 4352, 4480, 4608, 4736, 4864
PROJ_COLS = 4992

NSA_TQ = 128
NSA_TPS = 4
CMP_WIN = 16
CMP_PAD = 8


def _dot(a, b):
    return jnp.dot(a, b, preferred_element_type=F32)


def _dot_nt(a, b):
    return lax.dot_general(a, b, (((1,), (1,)), ((), ())), preferred_element_type=F32)


def _rms(x, gain):
    return x * lax.rsqrt(jnp.mean(x * x, axis=-1, keepdims=True) + RMS_EPS) * gain


def _split_bf16(x):
    hi = x.astype(BF16)
    return hi, (x - hi.astype(F32)).astype(BF16)


def _eye(rows, cols):
    r = lax.broadcasted_iota(jnp.int32, (rows, cols), 0)
    c = lax.broadcasted_iota(jnp.int32, (rows, cols), 1)
    return (r == c).astype(BF16)


def _transpose_bf16(x, rows):
    return _dot_nt(_eye(rows, x.shape[1]), x)


VROWS = 80


def _value_tile(v):
    row = lax.broadcasted_iota(jnp.int32, (VROWS, v.shape[0]), 0)
    return jnp.where(row == HEAD_DIM, 1.0, _transpose_bf16(v, VROWS)).astype(BF16)


def _softmax_joint(tiles):
    m = functools.reduce(jnp.maximum, [jnp.max(s, axis=0, keepdims=True) for s, _ in tiles])
    acc = functools.reduce(jnp.add, [_dot(vt, jnp.exp2(s - m).astype(BF16)) for s, vt in tiles])
    return m, acc


def _softmax_update(carry, s, vts):
    m, acc = carry
    m_new = jnp.maximum(m, jnp.max(s, axis=0, keepdims=True))
    p = jnp.exp2(s - m_new).astype(BF16)
    rows = s.shape[0] // len(vts)
    pv = functools.reduce(jnp.add, [_dot(vt, p[i * rows:(i + 1) * rows]) for i, vt in enumerate(vts)])
    return m_new, jnp.exp2(m - m_new) * acc + pv


def _softmax_finish(carry):
    _, acc = carry
    return acc[0:HEAD_DIM] * (1.0 / acc[HEAD_DIM:HEAD_DIM + 1])


def _rank_select(score, idx, count):
    beaten = jnp.zeros(score.shape, jnp.int32)
    for i in range(score.shape[0]):
        si = score[i:i + 1, :]
        beaten += ((si > score) | ((si == score) & (i < idx))).astype(jnp.int32)
    return beaten < count


def _params(sem):
    return pltpu.CompilerParams(dimension_semantics=sem, vmem_limit_bytes=VMEM_LIMIT)


def _layer_spec(a, layer):
    return pl.BlockSpec((None,) + a.shape[1:], lambda *_: (layer,) + (0,) * (a.ndim - 1),
                        pipeline_mode=pl.Buffered(1))


def _rel_bucket_np(dist):
    n = np.maximum(dist, 0)
    max_exact = REL_BUCKETS // 2
    nf = np.maximum(n, 1).astype(np.float32)
    large = max_exact + (np.log(nf / np.float32(max_exact)) / np.float32(math.log(REL_MAX_DIST / max_exact))
                         * np.float32(REL_BUCKETS - max_exact)).astype(np.int32)
    return np.where(n < max_exact, n, np.minimum(large, REL_BUCKETS - 1))


def _bucket_starts():
    buckets = _rel_bucket_np(np.arange(4 * REL_MAX_DIST))
    return [int(np.argmax(buckets >= k)) for k in range(REL_BUCKETS)]


BUCKET_START = _bucket_starts()
BIAS_REACH = BUCKET_START[-1]
assert BIAS_REACH <= MOBA_BLOCK and BIAS_REACH <= NSA_TQ - NSA_CMP_LEN + 1 + NSA_CMP_STRIDE


def _tables_kernel(tab_ref, t0m_ref, t1m_ref, t0n_ref, t1n_ref, wc_ref):
    h = pl.program_id(0)

    def bias(dist, head):
        last = tab_ref[head, REL_BUCKETS - 1]
        val = jnp.zeros(dist.shape, F32)
        for k in range(REL_BUCKETS - 2, -1, -1):
            val = jnp.where(dist < BUCKET_START[k + 1], (tab_ref[head, k] - last) * LOG2E, val)
        return val

    def toeplitz(size, offset, head):
        key = lax.broadcasted_iota(jnp.int32, (size, size), 0)
        qry = lax.broadcasted_iota(jnp.int32, (size, size), 1)
        return bias(offset + qry - key, head)

    t0m_ref[0] = toeplitz(MOBA_BLOCK, 0, h)
    t1m_ref[0] = toeplitz(MOBA_BLOCK, MOBA_BLOCK, h)
    t0n_ref[0] = toeplitz(NSA_TQ, 0, MOBA_HEADS + h)
    t1n_ref[0] = toeplitz(NSA_TQ, NSA_TQ, MOBA_HEADS + h)
    a = lax.broadcasted_iota(jnp.int32, (CMP_WIN, NSA_TQ), 0)
    i = lax.broadcasted_iota(jnp.int32, (CMP_WIN, NSA_TQ), 1)
    wc_ref[0] = bias(i + (NSA_TQ - NSA_CMP_LEN + 1) - NSA_CMP_STRIDE * a, MOBA_HEADS + h)


def _bias_tables(rel_bias):
    blk, tq = MOBA_BLOCK, NSA_TQ
    shapes = [(blk, blk), (blk, blk), (tq, tq), (tq, tq), (CMP_WIN, tq)]
    return pl.pallas_call(
        _tables_kernel,
        grid=(MOBA_HEADS,),
        in_specs=[pl.BlockSpec(memory_space=pltpu.SMEM)],
        out_specs=[pl.BlockSpec((1,) + s, lambda h: (h, 0, 0)) for s in shapes],
        out_shape=[jax.ShapeDtypeStruct((MOBA_HEADS,) + s, F32) for s in shapes],
        compiler_params=_params(("arbitrary",)),
        name="bias_tables",
    )(rel_bias)


NORM_SLABS = (list(range(C_MQ, C_MV, LANES)) + list(range(C_NSQ, C_KC, LANES)) + [C_KSL, C_KWN])


def _inproj_kernel(x_ref, g_ref, w_ref, hg_ref, *rest):
    ncast = (len(rest) - 1) // 2
    o_ref = rest[ncast]
    for src, dst in zip(rest[:ncast], rest[ncast + 1:]):
        dst[...] = src[...].astype(BF16)
    h = _rms(x_ref[...], g_ref[...]).astype(BF16)
    y = _dot_nt(h, w_ref[...])
    first = lax.broadcasted_iota(jnp.int32, (1, LANES), 1) < HEAD_DIM
    edges = sorted(set([0, PROJ_COLS] + NORM_SLABS + [c + LANES for c in NORM_SLABS]))
    for lo, hi in zip(edges[:-1], edges[1:]):
        t = y[:, lo:hi]
        if lo in NORM_SLABS:
            sq = t * t
            s0 = jnp.sum(jnp.where(first, sq, 0.0), axis=-1, keepdims=True)
            s1 = jnp.sum(jnp.where(first, 0.0, sq), axis=-1, keepdims=True)
            ms = jnp.where(first, s0, s1) * (1.0 / HEAD_DIM)
            t = t * lax.rsqrt(ms + RMS_EPS) * hg_ref[:, lo:hi]
        o_ref[:, lo:hi] = t


def _inproj(xf, gain, w, head_gain, layer, cast=()):
    n, d = xf.shape
    tm = INPROJ_TM
    steps = n // tm
    slab = lambda a: pl.BlockSpec((a.shape[0] // steps, a.shape[1]), lambda i: (i, 0))
    outs = pl.pallas_call(
        _inproj_kernel,
        grid=(steps,),
        in_specs=[pl.BlockSpec((tm, d), lambda i: (i, 0)), _layer_spec(gain, layer),
                  _layer_spec(w, layer), _layer_spec(head_gain, layer)] + [slab(a) for a in cast],
        out_specs=[pl.BlockSpec((tm, PROJ_COLS), lambda i: (i, 0))] + [slab(a) for a in cast],
        out_shape=[jax.ShapeDtypeStruct((n, PROJ_COLS), F32)]
        + [jax.ShapeDtypeStruct(a.shape, BF16) for a in cast],
        compiler_params=_params(("parallel",)),
        name="inproj",
    )(xf, gain, w, head_gain, *cast)
    return outs[0], outs[1:]


MOBA_HPS = 8
MOBA_BPS = 2


def _moba_kernel(q_ref, k_ref, v_ref, t0_ref, t1_ref, o_ref, kaug_s, vt_s, km_s, *, seq):
    step = pl.program_id(2)
    blk = MOBA_BLOCK
    nb = seq // blk
    heads = range(MOBA_HPS)
    pairs = range(MOBA_HPS // 2)
    pair_cols = lambda p: slice(p * LANES, (p + 1) * LANES)
    items = [(t, hh) for t in range(MOBA_BPS) for hh in heads]
    ns = [step * MOBA_BPS + t for t in range(MOBA_BPS)]

    @pl.when(step == 0)
    def _prepare_keys():
        block = lax.broadcasted_iota(jnp.int32, (seq, LANES), 0) // blk
        lane = lax.broadcasted_iota(jnp.int32, (seq, LANES), 1)
        code_hi = (lane - HEAD_DIM == block).astype(BF16)
        code_lo = (lane == block).astype(BF16)
        ones_row = (lax.broadcasted_iota(jnp.int32, (VROWS - HEAD_DIM, blk), 0) == 0).astype(F32)
        for p in pairs:
            kn = k_ref[:, pair_cols(p)]
            km = kn.reshape(nb, blk, LANES).sum(axis=1) * (1.0 / blk)
            km_s[p] = jnp.concatenate([km, jnp.zeros((16 - nb, LANES), F32)], axis=0)
            kaug_s[2 * p] = jnp.where(lane < HEAD_DIM, kn.astype(BF16), code_hi)
            kaug_s[2 * p + 1] = jnp.where(lane >= HEAD_DIM, kn.astype(BF16), code_lo)
            for j in range(nb):
                vt = _transpose_bf16(v_ref[j * blk:(j + 1) * blk, pair_cols(p)].astype(BF16), LANES)
                vt_s[2 * p, j] = jnp.concatenate([vt[0:HEAD_DIM], ones_row], axis=0).astype(BF16)
                vt_s[2 * p + 1, j] = jnp.concatenate([vt[HEAD_DIM:], ones_row], axis=0).astype(BF16)

    jidx = lax.broadcasted_iota(jnp.int32, (16, blk), 0)
    key = lax.broadcasted_iota(jnp.int32, (blk, blk), 0)
    qry = lax.broadcasted_iota(jnp.int32, (blk, blk), 1)
    qps = {}
    for t in range(MOBA_BPS):
        for p in pairs:
            qps[t, p] = _transpose_bf16(q_ref[t * blk:(t + 1) * blk, pair_cols(p)].astype(BF16),
                                        LANES).astype(BF16)
    even_lanes = lax.broadcasted_iota(jnp.int32, (16, LANES), 1) < HEAD_DIM
    gates = {}
    for t, hh in items:
        km = jnp.where(even_lanes == (hh % 2 == 0), km_s[hh // 2], 0.0)
        kmh, kml = _split_bf16(km)
        gates[t, hh] = _dot(kmh, qps[t, hh // 2]) + _dot(kml, qps[t, hh // 2])
    qaugs = {}
    filler = jnp.zeros((LANES - HEAD_DIM - 16, blk), BF16)
    for t, hh in items:
        past = jidx < ns[t]
        gate = jnp.where(past, gates[t, hh], NEG)
        sel = (past & _rank_select(gate, jidx, MOBA_TOPK)) | (jidx == ns[t])
        selb = jnp.where(sel, 0.0, NEG).astype(BF16)
        qp = qps[t, hh // 2]
        qaugs[t, hh] = (jnp.concatenate([qp[0:HEAD_DIM], selb, filler], axis=0) if hh % 2 == 0 else
                        jnp.concatenate([selb, filler, qp[HEAD_DIM:]], axis=0))

    def scores(t, hh, j):
        start = pl.multiple_of(j * blk, blk)
        return _dot(kaug_s[hh, pl.ds(start, blk), :], qaugs[t, hh]), vt_s[hh, j]

    near = {(t, hh): (scores(t, hh, ns[t]), scores(t, hh, jnp.maximum(ns[t] - 1, 0))) for t, hh in items}
    state = {}
    for t, hh in items:
        (s0, vt0), (s1, vt1) = near[t, hh]
        s0 = jnp.where(key <= qry, s0, NEG) + t0_ref[hh]
        s1 = jnp.where(ns[t] >= 1, s1, NEG) + t1_ref[hh]
        state[t, hh] = _softmax_joint([(s0, vt0), (s1, vt1)])

    older = [jnp.maximum(n - 1, 0) for n in ns]
    lo = 0
    for first in range(MOBA_BPS):
        live = [(t, hh) for t, hh in items if t >= first]

        def far_body(j, carry, live=live):
            tiles = [scores(t, hh, j) for t, hh in live]
            return tuple(_softmax_update(c, s, [vt]) for c, (s, vt) in zip(carry, tiles))

        new = lax.fori_loop(lo, older[first], far_body, tuple(state[it] for it in live))
        state.update(dict(zip(live, new)))
        lo = older[first]

    for t in range(MOBA_BPS):
        o_ref[t * blk:(t + 1) * blk, :] = jnp.concatenate(
            [_softmax_finish(state[t, hh]) for hh in heads], axis=0).T.astype(BF16)


def _moba(proj, t0, t1, batch, seq):
    blk = MOBA_BLOCK
    nb = seq // blk
    width = MOBA_HPS * HEAD_DIM
    rows = MOBA_BPS * blk
    steps = nb // MOBA_BPS
    qc, kc, vc = C_MQ // width, C_MK // width, C_MV // width
    return pl.pallas_call(
        functools.partial(_moba_kernel, seq=seq),
        grid=(batch, MOBA_HEADS // MOBA_HPS, steps),
        in_specs=[pl.BlockSpec((rows, width), lambda b, h, n: (b * steps + n, qc + h)),
                  pl.BlockSpec((seq, width), lambda b, h, n: (b, kc + h)),
                  pl.BlockSpec((seq, width), lambda b, h, n: (b, vc + h)),
                  pl.BlockSpec((MOBA_HPS, blk, blk), lambda b, h, n: (h, 0, 0)),
                  pl.BlockSpec((MOBA_HPS, blk, blk), lambda b, h, n: (h, 0, 0))],
        out_specs=pl.BlockSpec((rows, width), lambda b, h, n: (b * steps + n, h)),
        out_shape=jax.ShapeDtypeStruct((batch * seq, MOBA_HEADS * HEAD_DIM), BF16),
        scratch_shapes=[pltpu.VMEM((MOBA_HPS, seq, LANES), BF16),
                        pltpu.VMEM((MOBA_HPS, nb, VROWS, blk), BF16),
                        pltpu.VMEM((MOBA_HPS // 2, 16, LANES), F32)],
        compiler_params=_params(("parallel", "parallel", "arbitrary")),
        name="moba",
    )(proj, proj, proj, t0, t1)


def _compress_kernel(k_ref, v_ref, w1k_ref, w1v_ref, wtk_ref, wtv_ref, w2k_ref, w2v_ref, pk_ref, pv_ref,
                     kg_ref, ko_ref, vo_ref):
    hid = NSA_CMP_HIDDEN
    chunks = k_ref.shape[0] // NSA_CMP_STRIDE

    def chunk_rows(t_ref):
        return jnp.concatenate([t_ref[pl.ds(l, chunks, stride=NSA_CMP_STRIDE), :].astype(BF16)
                                for l in range(NSA_CMP_STRIDE)], axis=1)

    def compress(tok, w1_ref, wt_ref, w2_ref, pos_ref, g):
        a = _dot(tok, wt_ref[g])
        pw = _dot(pos_ref[...].astype(BF16), w1_ref[...])
        pos = pw[0:1, :hid] + pw[1:2, hid:]
        nxt = pltpu.roll(a[:, hid:], chunks - 1, 0)
        h = jax.nn.gelu(a[:, :hid] + nxt + pos)
        return _dot(h.astype(BF16), w2_ref[...])

    ktok, vtok = chunk_rows(k_ref), chunk_rows(v_ref)
    for g in range(NSA_KV_GROUPS):
        kc = compress(ktok, w1k_ref, wtk_ref, w2k_ref, pk_ref, g)
        ko_ref[0, g] = _rms(kc, kg_ref[...]).astype(BF16)
        vc = compress(vtok, w1v_ref, wtv_ref, w2v_ref, pv_ref, g).astype(BF16)
        vo_ref[0, g] = _transpose_bf16(vc, HEAD_DIM).astype(BF16)


def _compress(proj, w1k, w1v, wtk, wtv, w2k, w2v, pk, pv, kg, layer, batch, seq):
    groups = NSA_KV_GROUPS
    chunks = seq // NSA_CMP_STRIDE
    tok = lambda col: pl.BlockSpec((seq, LANES), lambda b: (b, col // LANES))
    return pl.pallas_call(
        _compress_kernel,
        grid=(batch,),
        in_specs=[tok(C_KC), tok(C_VC)] + [_layer_spec(a, layer)
                                           for a in (w1k, w1v, wtk, wtv, w2k, w2v, pk, pv, kg)],
        out_specs=[pl.BlockSpec((1, groups, chunks, HEAD_DIM), lambda b: (b, 0, 0, 0)),
                   pl.BlockSpec((1, groups, HEAD_DIM, chunks), lambda b: (b, 0, 0, 0))],
        out_shape=[jax.ShapeDtypeStruct((batch, groups, chunks, HEAD_DIM), BF16),
                   jax.ShapeDtypeStruct((batch, groups, HEAD_DIM, chunks), BF16)],
        compiler_params=_params(("parallel",)),
        name="nsa_compress",
    )(proj, proj, w1k, w1v, wtk, wtv, w2k, w2v, pk, pv, kg)


def _nsa_kernel(q_ref, kc_ref, vc_ref, ksl_ref, vsl_ref, kwn_ref, vwn_ref, gt_ref,
                wc_ref, t0_ref, t1_ref, x_ref, ya_ref, ga_ref, gb_ref, wa_ref, wb_ref, wo_ref,
                o_ref, kslaug_s, vslt_s, kwn_s, vwnt_s, sc_s, yb_s, *, seq, total):
    flat = pl.program_id(0)
    steps = (seq // NSA_TQ) // NSA_TPS
    step = jnp.minimum(flat, total - 1) % steps
    tq = NSA_TQ
    hpg = NSA_HPG
    sb = NSA_SEL_BLOCK
    nsel = seq // sb
    ncmp = seq // NSA_CMP_STRIDE
    nwin = NSA_WINDOW // tq
    groups = range(NSA_KV_GROUPS)
    items = [(t, g) for t in range(NSA_TPS) for g in groups]
    qis = [step * NSA_TPS + t for t in range(NSA_TPS)]

    @pl.when(flat == 0)
    def _no_previous_slab():
        yb_s[...] = jnp.zeros_like(yb_s)

    @pl.when((step == 0) & (flat < total))
    def _prepare_keys():
        r = lax.broadcasted_iota(jnp.int32, (seq, LANES), 0)
        c = lax.broadcasted_iota(jnp.int32, (seq, LANES), 1)
        onehot = (c - HEAD_DIM == r // sb).astype(F32)
        place = _eye(HEAD_DIM, LANES)
        sc_s[:, 0:CMP_PAD, :] = jnp.zeros((NSA_TPS * NSA_KV_GROUPS, CMP_PAD, hpg * tq), F32)
        for g in groups:
            sl = slice(g * HEAD_DIM, (g + 1) * HEAD_DIM)
            kslaug_s[g] = (_dot(ksl_ref[:, sl].astype(BF16), place) + onehot).astype(BF16)
            kwn_s[g] = kwn_ref[:, sl].astype(BF16)
            for j in range(seq // tq):
                rows = slice(j * tq, (j + 1) * tq)
                vslt_s[g, j] = _value_tile(vsl_ref[rows, sl].astype(BF16))
                vwnt_s[g, j] = _value_tile(vwn_ref[rows, sl].astype(BF16))

    lanes = lambda parts: jnp.concatenate(parts, axis=1)
    key = lax.broadcasted_iota(jnp.int32, (tq, tq), 0)
    qry = lax.broadcasted_iota(jnp.int32, (tq, tq), 1)
    causal = lanes([key <= qry] * hpg)
    upper = lanes([key > qry] * hpg)
    oj = lax.broadcasted_iota(jnp.int32, (nsel, ncmp), 0) * sb
    on = lax.broadcasted_iota(jnp.int32, (nsel, ncmp), 1) * NSA_CMP_STRIDE
    overlap = ((on < oj + sb) & (on + NSA_CMP_LEN > oj) & (on < seq - NSA_CMP_STRIDE)).astype(BF16)
    jidx = lax.broadcasted_iota(jnp.int32, (nsel, tq), 0)
    cend = lax.broadcasted_iota(jnp.int32, (ncmp, tq), 0) * NSA_CMP_STRIDE + (NSA_CMP_LEN - 1)
    cvis = [lanes([cend <= lax.broadcasted_iota(jnp.int32, (ncmp, tq), 1) + qi * tq] * hpg) for qi in qis]
    own = [(lax.broadcasted_iota(jnp.int32, (nsel, tq), 1) + qi * tq) // sb for qi in qis]
    tile_rows = lambda j: pl.ds(pl.multiple_of(j * tq, tq), tq)
    back = lambda t, k: jnp.maximum(qis[t] - k, 0)
    heads = [[g * hpg + r for r in range(hpg)] for g in groups]
    t0 = [lanes([t0_ref[h] for h in heads[g]]) for g in groups]
    t1 = [lanes([t1_ref[h] for h in heads[g]]) for g in groups]
    slot = lambda t, g: t * NSA_KV_GROUPS + g

    qplain, win_scores = {}, {}
    for t, g in items:
        width = hpg * HEAD_DIM
        qg = _transpose_bf16(q_ref[t * tq:(t + 1) * tq, g * width:(g + 1) * width].astype(BF16), width)
        qplain[t, g] = lanes([qg[r * HEAD_DIM:(r + 1) * HEAD_DIM] for r in range(hpg)]).astype(BF16)
        sc_s[slot(t, g), CMP_PAD:, :] = _dot(kc_ref[0, g], qplain[t, g])
        win_scores[t, g] = [_dot(kwn_s[g, tile_rows(back(t, k)), :], qplain[t, g]) for k in range(nwin + 1)]

    z = (jax.nn.sigmoid(ga_ref[...]) * _dot(ya_ref[...], wa_ref[...])
         + jax.nn.sigmoid(gb_ref[...]) * _dot(yb_s[...], wb_ref[...]))
    o_ref[...] = x_ref[...] + _dot(z.astype(BF16), wo_ref[...])

    o_cmps, imps = {}, {}
    for t, g in items:
        win = pl.ds(pl.multiple_of(qis[t] * (tq // NSA_CMP_STRIDE), 8), CMP_WIN)
        sc_s[slot(t, g), win, :] += lanes([wc_ref[h] for h in heads[g]])
        s = jnp.where(cvis[t], sc_s[slot(t, g), CMP_PAD:, :], NEG)
        m = jnp.max(s, axis=0, keepdims=True)
        e = jnp.where(cvis[t], jnp.exp2(s - m), 0.0)
        den = jnp.sum(e, axis=0, keepdims=True)
        p = e * (1.0 / jnp.where(den > 0, den, 1.0))
        o_cmps[t, g] = _dot(vc_ref[0, g], p.astype(BF16))
        psum = p[:, 0:tq]
        for r in range(1, hpg):
            psum = psum + p[:, r * tq:(r + 1) * tq]
        ph, plo = _split_bf16(psum)
        imps[t, g] = _dot(overlap, ph) + _dot(overlap, plo)

    o_wins = {}
    for t, g in items:
        qi = qis[t]
        tiles = []
        for k, s in enumerate(win_scores[t, g]):
            if k == 0:
                s = jnp.where(causal, s, NEG) + t0[g]
            elif k == 1:
                s = jnp.where(qi >= 1, s, NEG) + t1[g]
            elif k < nwin:
                s = jnp.where(qi >= k, s, NEG)
            else:
                s = jnp.where(upper & (qi >= k), s, NEG)
            tiles.append((s, vwnt_s[g, back(t, k)]))
        o_wins[t, g] = _softmax_finish(_softmax_joint(tiles))

    qaugs = {}
    for t, g in items:
        imp = jnp.where((jidx == 0) | (jidx == own[t]) | (jidx == own[t] - 1), -NEG, imps[t, g])
        imp = jnp.where(jidx > own[t], NEG, imp)
        sel = _rank_select(imp, jidx, NSA_SEL_TOPN) & (jidx <= own[t])
        selb = jnp.where(sel, 0.0, NEG).astype(BF16)
        qaugs[t, g] = jnp.concatenate([qplain[t, g], lanes([selb] * hpg),
                                       jnp.zeros((LANES - HEAD_DIM - nsel, hpg * tq), BF16)], axis=0)

    assert NSA_TPS % 2 == 0
    near_tiles = lambda t: 3 if t % 2 == 0 else 2
    slc_near = {(t, g): [_dot(kslaug_s[g, tile_rows(back(t, k)), :], qaugs[t, g])
                         for k in range(near_tiles(t))] for t, g in items}
    state = {}
    for t, g in items:
        qi = qis[t]
        tiles = [(jnp.where(causal, slc_near[t, g][0], NEG) + t0[g], vslt_s[g, qi]),
                 (jnp.where(qi >= 1, slc_near[t, g][1], NEG) + t1[g], vslt_s[g, back(t, 1)])]
        if near_tiles(t) == 3:
            tiles.append((jnp.where(qi >= 2, slc_near[t, g][2], NEG), vslt_s[g, back(t, 2)]))
        state[t, g] = _softmax_joint(tiles)

    slabs = [jnp.maximum(qi - 1, 0) // 2 for qi in qis]
    lo = 0
    for first in range(NSA_TPS):
        live = [(t, g) for t, g in items if t >= first]

        def slc_far(i, carry, live=live):
            slab = pl.ds(pl.multiple_of(i * (2 * tq), 2 * tq), 2 * tq)
            scores = [_dot(kslaug_s[g, slab, :], qaugs[t, g]) for t, g in live]
            return tuple(_softmax_update(c, sc, [vslt_s[g, 2 * i], vslt_s[g, 2 * i + 1]])
                         for c, sc, (t, g) in zip(carry, scores, live))

        new = lax.fori_loop(lo, slabs[first], slc_far, tuple(state[it] for it in live))
        state.update(dict(zip(live, new)))
        lo = slabs[first]

    yb = []
    for t in range(NSA_TPS):
        gates = jax.nn.sigmoid(gt_ref[t * tq:(t + 1) * tq, :]).T
        outs = []
        for g in groups:
            o_slc = _softmax_finish(state[t, g])
            for r in range(hpg):
                h = g * hpg + r
                cols = slice(r * tq, (r + 1) * tq)
                outs.append(gates[h:h + 1, :] * o_cmps[t, g][:, cols]
                            + gates[NSA_HEADS + h:NSA_HEADS + h + 1, :] * o_slc[:, cols]
                            + gates[2 * NSA_HEADS + h:2 * NSA_HEADS + h + 1, :] * o_wins[t, g][:, cols])
        yb.append(jnp.concatenate(outs, axis=0).T.astype(BF16))
    yb_s[...] = jnp.concatenate(yb, axis=0)


def _nsa_merge(proj, kcn, vct, wc, t0, t1, xf, ya, wa, wb, wo, layer, batch, seq):
    tq = NSA_TQ
    nq = seq // tq
    rows = NSA_TPS * tq
    steps = nq // NSA_TPS
    total = batch * steps
    ncmp = seq // NSA_CMP_STRIDE
    width = NSA_HEADS * HEAD_DIM
    groups = NSA_KV_GROUPS
    d = xf.shape[1]
    cur = lambda s: jnp.minimum(s, total - 1)
    prv = lambda s: jnp.maximum(s - 1, 0)
    kv = lambda col: pl.BlockSpec((seq, LANES), lambda s: (cur(s) // steps, col // LANES))
    const = lambda shape: pl.BlockSpec(shape, lambda s: (0,) * len(shape))
    lag = lambda w, col=0: pl.BlockSpec((rows, w), lambda s: (prv(s), col))
    return pl.pallas_call(
        functools.partial(_nsa_kernel, seq=seq, total=total),
        grid=(total + 1,),
        in_specs=[pl.BlockSpec((rows, width), lambda s: (cur(s), C_NSQ // width)),
                  pl.BlockSpec((1, groups, ncmp, HEAD_DIM), lambda s: (cur(s) // steps, 0, 0, 0)),
                  pl.BlockSpec((1, groups, HEAD_DIM, ncmp), lambda s: (cur(s) // steps, 0, 0, 0)),
                  kv(C_KSL), kv(C_VSL), kv(C_KWN), kv(C_VWN),
                  pl.BlockSpec((rows, LANES), lambda s: (cur(s), C_NG // LANES)),
                  const((NSA_HEADS, CMP_WIN, tq)), const((NSA_HEADS, tq, tq)),
                  const((NSA_HEADS, tq, tq)),
                  lag(d), lag(ya.shape[1]), lag(d, C_GA // d), lag(d, C_GB // d),
                  _layer_spec(wa, layer), _layer_spec(wb, layer), _layer_spec(wo, layer)],
        out_specs=lag(d),
        out_shape=jax.ShapeDtypeStruct((batch * seq, d), F32),
        scratch_shapes=[pltpu.VMEM((groups, seq, LANES), BF16),
                        pltpu.VMEM((groups, nq, VROWS, tq), BF16),
                        pltpu.VMEM((groups, seq, HEAD_DIM), BF16),
                        pltpu.VMEM((groups, nq, VROWS, tq), BF16),
                        pltpu.VMEM((NSA_TPS * groups, CMP_PAD + ncmp, NSA_HPG * tq), F32),
                        pltpu.VMEM((rows, width), BF16)],
        compiler_params=_params(("arbitrary",)),
        name="nsa_merge",
    )(proj, kcn, vct, proj, proj, proj, proj, proj, wc, t0, t1, xf, ya, proj, proj, wa, wb, wo)


FFN_HALO = 16
FFN_TM = 1024
FFN_TF = 256


def _ffn_kernel(x_ref, xh_ref, g_ref, wu_ref, cw_ref, cb_ref, wd_ref, p_ref, wg_ref, wp_ref,
                o_ref, act_s, *, seq, d_ff):
    i = pl.program_id(0)
    tm = x_ref.shape[0]
    x = x_ref[...]
    at_start = (i * tm) % seq == 0
    halo = jnp.where(at_start, 0.0, _rms(xh_ref[...], g_ref[...]))
    hn = jnp.concatenate([halo.astype(BF16), _rms(x, g_ref[...]).astype(BF16)], axis=0)

    def conv(cols):
        u = _dot(hn, wu_ref[:, cols])
        u1 = pltpu.roll(u, 1, 0)[FFN_HALO:]
        u2 = pltpu.roll(u, 2, 0)[FFN_HALO:]
        cw = cw_ref[:, cols]
        return cw[0:1] * u2 + cw[1:2] * u1 + cw[2:3] * u[FFN_HALO:] + cb_ref[:, cols]

    for c in range(d_ff // FFN_TF):
        lo = c * FFN_TF
        act = jax.nn.gelu(conv(slice(lo, lo + FFN_TF))) * conv(slice(d_ff + lo, d_ff + lo + FFN_TF))
        act_s[:, lo:lo + FFN_TF] = act.astype(BF16)

    x = x + _dot(act_s[...], wd_ref[...])
    gate = jax.nn.sigmoid(_dot(x.astype(BF16), wg_ref[...]))
    o_ref[...] = x + gate * _dot(p_ref[...].astype(BF16), wp_ref[...])


def _ffn_ple(xf, gain, w_up, conv_w, conv_b, w_down, pf, wg, wp, layer, seq):
    n, d = xf.shape
    d_ff = w_down.shape[1]
    tm = FFN_TM
    hb = tm // FFN_HALO
    resident = lambda a: _layer_spec(a, layer)
    return pl.pallas_call(
        functools.partial(_ffn_kernel, seq=seq, d_ff=d_ff),
        grid=(n // tm,),
        in_specs=[pl.BlockSpec((tm, d), lambda i: (i, 0)),
                  pl.BlockSpec((FFN_HALO, d), lambda i: (jnp.maximum(i * hb - 1, 0), 0)),
                  resident(gain), resident(w_up), resident(conv_w), resident(conv_b), resident(w_down),
                  pl.BlockSpec((None, tm, pf.shape[2]), lambda i: (layer, i, 0)),
                  resident(wg), resident(wp)],
        out_specs=pl.BlockSpec((tm, d), lambda i: (i, 0)),
        out_shape=jax.ShapeDtypeStruct((n, d), F32),
        scratch_shapes=[pltpu.VMEM((tm, d_ff), BF16)],
        compiler_params=_params(("parallel",)),
        name="conv_ffn_ple",
    )(xf, xf, gain, w_up, conv_w, conv_b, w_down, pf, wg, wp)


def _reorder_kernel(w_ref, o_ref):
    attn = 3 * MOBA_HEADS * HEAD_DIM + NSA_HEADS * HEAD_DIM + 6 * NSA_KV_GROUPS * HEAD_DIM
    ng = 3 * NSA_HEADS
    w = w_ref[...]
    pad = jnp.zeros((PROJ_COLS - w.shape[0], w.shape[1]), w.dtype)
    o_ref[...] = jnp.concatenate([w[attn + ng:], w[:attn], w[attn:attn + ng], pad], axis=0).astype(BF16)


def _reorder_in_proj(w):
    depth, d, cols = w.shape
    tc = 256
    return pl.pallas_call(
        _reorder_kernel,
        grid=(depth, d // tc),
        in_specs=[pl.BlockSpec((None, cols, tc), lambda l, i: (l, 0, i))],
        out_specs=pl.BlockSpec((None, PROJ_COLS, tc), lambda l, i: (l, 0, i)),
        out_shape=jax.ShapeDtypeStruct((depth, PROJ_COLS, d), BF16),
        compiler_params=_params(("parallel", "parallel")),
        name="reorder_in_proj",
    )(jnp.swapaxes(w, 1, 2))


def _head_gain_rows(moba_q, moba_k, nsa_q, nsa_k):
    rows = jnp.zeros((moba_q.shape[0], 1, PROJ_COLS), F32)
    for col, gain, heads in ((C_MQ, moba_q * QSCALE, MOBA_HEADS), (C_MK, moba_k, MOBA_HEADS),
                             (C_NSQ, nsa_q * QSCALE, NSA_HEADS), (C_KSL, nsa_k[:, 1], NSA_KV_GROUPS),
                             (C_KWN, nsa_k[:, 2], NSA_KV_GROUPS)):
        rows = rows.at[:, 0, col:col + heads * HEAD_DIM].set(jnp.tile(gain, (1, heads)))
    return rows


def _cmp_weights(w1, pos):
    depth = w1.shape[0]
    half = NSA_CMP_STRIDE * HEAD_DIM
    w1cat = jnp.concatenate([w1[:, :half], w1[:, half:]], axis=2).astype(BF16)
    tok = w1cat.reshape(depth, NSA_CMP_STRIDE, HEAD_DIM, -1)
    zero = jnp.zeros_like(tok)
    placed = jnp.stack([jnp.concatenate([tok, zero], axis=2), jnp.concatenate([zero, tok], axis=2)], axis=1)
    placed = placed.reshape(depth, NSA_KV_GROUPS, NSA_CMP_STRIDE * LANES, -1)
    posr = jnp.zeros((depth, 8, half), F32).at[:, 0:2].set(pos.reshape(depth, 2, half))
    return w1cat, placed, posr


def kernel(x, p, rel_bias, attn_norm, w_in, moba_q_gain, moba_k_gain, nsa_q_gain, nsa_k_gain,
           cmp_pos_k, cmp_w1_k, cmp_w2_k, cmp_pos_v, cmp_w1_v, cmp_w2_v,
           w_br_moba, w_br_nsa, w_o, ffn_norm, w_up, conv_w, conv_b, w_down, w_ple_gate, w_ple):
    batch, seq, d = x.shape
    n = batch * seq
    depth = w_in.shape[0]
    bf = lambda a: a.astype(BF16)
    w_in_r = _reorder_in_proj(w_in)
    head_gain = _head_gain_rows(moba_q_gain, moba_k_gain, nsa_q_gain, nsa_k_gain)
    attn_gain, ffn_gain = attn_norm[:, None, :], ffn_norm[:, None, :]
    w1k, wtk, pk = _cmp_weights(cmp_w1_k, cmp_pos_k)
    w1v, wtv, pv = _cmp_weights(cmp_w1_v, cmp_pos_v)
    w2k, w2v, kg0 = bf(cmp_w2_k), bf(cmp_w2_v), nsa_k_gain[:, 0:1]
    wa, wb, wo = bf(w_br_moba), bf(w_br_nsa), bf(w_o)
    wg, wp = bf(w_ple_gate), bf(w_ple)
    conv_b = conv_b[:, None, :]
    pf = p.reshape(depth, n, -1)

    t0m, t1m, t0n, t1n, wc = _bias_tables(rel_bias)
    xf = x.reshape(n, d)
    for i in range(depth):
        if i == 0:
            flat = [w_up.reshape(-1, w_up.shape[2]), w_down.reshape(-1, w_down.shape[2])]
            proj, (wu, wd) = _inproj(xf, attn_gain, w_in_r, head_gain, i, flat)
            wu, wd = wu.reshape(w_up.shape), wd.reshape(w_down.shape)
        else:
            proj, _ = _inproj(xf, attn_gain, w_in_r, head_gain, i)
        ya = _moba(proj, t0m, t1m, batch, seq)
        kcn, vct = _compress(proj, w1k, w1v, wtk, wtv, w2k, w2v, pk, pv, kg0, i, batch, seq)
        xf = _nsa_merge(proj, kcn, vct, wc, t0n, t1n, xf, ya, wa, wb, wo, i, batch, seq)
        xf = _ffn_ple(xf, ffn_gain, wu, conv_w, conv_b, wd, pf, wg, wp, i, seq)
    return xf.reshape(batch, seq, d)
```
